```python
import functools
import jax, jax.numpy as jnp
from jax import lax
import numpy as np

D_MODEL = 1024
BATCH = 32
SEQ = 2048
DEPTH = 2
DEC_BATCH = 32
DEC_SEQ = 64
PAST_LEN = 2048

CHUNK = 64
Q_BLOCK = 128
HEAD_DIM = 64
H_A = 6
H_B = 6
H_C = 4
W_A = H_A * HEAD_DIM
W_B = H_B * HEAD_DIM
W_C = H_C * HEAD_DIM
W_MIX = W_A + W_B + W_C
D_DECAY_LORA = 64
D_AAA_LORA = 64
SHIFT_W = 3 * W_A + D_DECAY_LORA + D_AAA_LORA
IN_COLS = SHIFT_W + W_A + 4 * W_B + H_B + 4 * W_C
ALPHA = (2 * DEPTH) ** 0.25
BETA = (8 * DEPTH) ** -0.25
GN_EPS = 64e-5
LN_EPS = 1e-5
NEG_INF = -1e30

kernel_name = 'hybrid_rwkv7_fox_stickbreak_stream_step'


def _layernorm(x, g, b):
    xf = x.astype(jnp.float32)
    mu = jnp.mean(xf, axis=-1, keepdims=True)
    var = jnp.mean(jnp.square(xf - mu), axis=-1, keepdims=True)
    return ((xf - mu) * lax.rsqrt(var + LN_EPS) * g + b).astype(x.dtype)


def _wkv7_scan(s0, r, decay, k, v, kk, a):
    def step(s, inp):
        r_t, w_t, k_t, v_t, kk_t, a_t = inp
        sa = jnp.einsum('bhvk,bhk->bhv', s, kk_t)
        s = (s * w_t[:, :, None, :]
             - sa[..., None] * (kk_t * a_t)[:, :, None, :]
             + v_t[..., None] * k_t[:, :, None, :])
        return s, jnp.einsum('bhvk,bhk->bhv', s, r_t)
    xs = tuple(jnp.moveaxis(z, 1, 0) for z in (r, decay, k, v, kk, a))
    s_fin, y = lax.scan(step, s0, xs)
    return s_fin, jnp.moveaxis(y, 0, 1)


def _sweep_queries(block_fn, q_inputs, q_pos):
    tq = q_pos.shape[0]
    qb = min(Q_BLOCK, tq)
    nb = -(-tq // qb)
    pad = nb * qb - tq

    def to_blocks(arr):
        arr = jnp.pad(arr, [(0, 0), (0, pad)] + [(0, 0)] * (arr.ndim - 2), mode='edge')
        return jnp.moveaxis(arr.reshape(arr.shape[0], nb, qb, *arr.shape[2:]), 1, 0)

    qs = tuple(to_blocks(arr) for arr in q_inputs)
    qp = jnp.pad(q_pos, (0, pad), mode='edge').reshape(nb, qb)
    out = lax.map(lambda xs: block_fn(*xs[0], xs[1]), (qs, qp))
    out = jnp.moveaxis(out, 0, 1)
    return out.reshape(out.shape[0], nb * qb, *out.shape[3:])[:, :tq]


def _fox_block(q, fq, q_pos, k, v, fk, k_pos):
    s = jnp.einsum('bqhd,bkhd->bhqk', q, k).astype(jnp.float32) * (HEAD_DIM ** -0.5)
    s = s + jnp.swapaxes(fq, 1, 2)[..., :, None] - jnp.swapaxes(fk, 1, 2)[..., None, :]
    s = jnp.where(k_pos[None, :] <= q_pos[:, None], s, NEG_INF)
    p = jax.nn.softmax(s, axis=-1)
    return jnp.einsum('bhqk,bkhd->bqhd', p.astype(v.dtype), v)


def _sb_block(q, q_pos, k, v, k_pos):
    z = jnp.einsum('bqhd,bkhd->bhqk', q, k).astype(jnp.float32) * (HEAD_DIM ** -0.5)
    mask = k_pos[None, :] < q_pos[:, None]
    log_1m = jnp.where(mask, jax.nn.log_sigmoid(-z), 0.0)
    after = lax.cumsum(log_1m, axis=3, reverse=True) - log_1m
    w = jnp.where(mask, jnp.exp(jax.nn.log_sigmoid(z) + after), 0.0)
    return jnp.einsum('bhqk,bkhd->bqhd', w.astype(v.dtype), v)


def _layer(x, hist, w_in, mu_shift, w0_decay, w_decay, a0, w_aaa, k_k, k_a, r_k,
           lnx_g, lnx_b, fox_fb, w_out, ln_g, ln_b):
    f32 = jnp.float32
    bsz, t, _ = x.shape
    past = 0 if hist is None else hist[0].shape[1]
    q_pos = past + jnp.arange(t, dtype=jnp.int32)
    k_pos = jnp.arange(past + t, dtype=jnp.int32)

    u = jnp.einsum('btd,dc->btc', x, w_in)
    sizes = (SHIFT_W, W_A, W_B, W_B, W_B, H_B, W_B, W_C, W_C, W_C, W_C)
    (u_shift, g_a, q_b, k_b, v_b, f_b, g_b, q_c, k_c, v_c, g_c) = jnp.split(
        u, np.cumsum(sizes)[:-1].tolist(), axis=-1)

    prev = (jnp.zeros((bsz, 1, SHIFT_W), u.dtype) if hist is None else hist[6].astype(u.dtype))
    z_prev = jnp.concatenate([prev, u_shift[:, :-1]], axis=1)
    zs = u_shift + (z_prev - u_shift) * mu_shift
    r, k, v, w_lo, a_lo = jnp.split(
        zs, [W_A, 2 * W_A, 3 * W_A, 3 * W_A + D_DECAY_LORA], axis=-1)
    w_log = -jax.nn.softplus(-(w0_decay + jnp.tanh(w_lo) @ w_decay).astype(f32)) - 0.5
    decay = jnp.exp(-jnp.exp(w_log))
    a = jax.nn.sigmoid((a0 + a_lo @ w_aaa).astype(f32))

    def heads_a(z):
        return z.reshape(bsz, t, H_A, HEAD_DIM).astype(f32)

    r, k, v, decay, a = (heads_a(z) for z in (r, k, v, decay, a))
    kk = k * k_k.reshape(H_A, HEAD_DIM)
    kk = kk * lax.rsqrt(jnp.sum(kk * kk, axis=-1, keepdims=True) + 1e-12)
    k = k * (1.0 + (a - 1.0) * k_a.reshape(H_A, HEAD_DIM))
    s0 = (jnp.zeros((bsz, H_A, HEAD_DIM, HEAD_DIM), f32) if hist is None else hist[5].astype(f32))
    wkv_new, y = _wkv7_scan(s0, r, decay, k, v, kk, a)
    mean = jnp.mean(y, axis=-1, keepdims=True)
    var = jnp.mean(jnp.square(y - mean), axis=-1, keepdims=True)
    yn = ((y - mean) * lax.rsqrt(var + GN_EPS) * lnx_g.reshape(H_A, HEAD_DIM)
          + lnx_b.reshape(H_A, HEAD_DIM))
    bonus = jnp.sum(r * k * r_k, axis=-1, keepdims=True) * v
    o_a = (yn + bonus).reshape(bsz, t, W_A).astype(x.dtype) * jax.nn.silu(g_a)

    qh_b = q_b.reshape(bsz, t, H_B, HEAD_DIM)
    kh_b = k_b.reshape(bsz, t, H_B, HEAD_DIM)
    vh_b = v_b.reshape(bsz, t, H_B, HEAD_DIM)
    logf = jax.nn.log_sigmoid((f_b + fox_fb).astype(f32))
    if hist is None:
        kb_all, vb_all, lf_all = kh_b, vh_b, logf
    else:
        kb_all = jnp.concatenate([hist[0].astype(kh_b.dtype), kh_b], axis=1)
        vb_all = jnp.concatenate([hist[1].astype(vh_b.dtype), vh_b], axis=1)
        lf_all = jnp.concatenate([hist[2].astype(f32), logf], axis=1)
    cum_f = jnp.cumsum(lf_all, axis=1)
    o = _sweep_queries(
        functools.partial(_fox_block, k=kb_all, v=vb_all, fk=cum_f, k_pos=k_pos),
        (qh_b, cum_f[:, past:]), q_pos)
    o_b = o.reshape(bsz, t, W_B).astype(x.dtype) * jax.nn.silu(g_b)

    qh_c = q_c.reshape(bsz, t, H_C, HEAD_DIM)
    kh_c = k_c.reshape(bsz, t, H_C, HEAD_DIM)
    vh_c = v_c.reshape(bsz, t, H_C, HEAD_DIM)
    if hist is None:
        kc_all, vc_all = kh_c, vh_c
    else:
        kc_all = jnp.concatenate([hist[3].astype(kh_c.dtype), kh_c], axis=1)
        vc_all = jnp.concatenate([hist[4].astype(vh_c.dtype), vh_c], axis=1)
    o = _sweep_queries(
        functools.partial(_sb_block, k=kc_all, v=vc_all, k_pos=k_pos), (qh_c,), q_pos)
    o_c = o.reshape(bsz, t, W_C).astype(x.dtype) * jax.nn.silu(g_c)

    out = jnp.einsum('btc,cd->btd', jnp.concatenate([o_a, o_b, o_c], axis=-1), w_out)
    y_out = _layernorm(ALPHA * x + out, ln_g, ln_b)
    return y_out, (kh_b, vh_b, logf, kh_c, vh_c, wkv_new, u_shift[:, -1:])


def setup_inputs(seed: int = 0) -> dict:
    key = jax.random.key(seed)
    ks = jax.random.split(key, 24)
    f32 = jnp.float32
    L = DEPTH

    def nrm(k, shape, s=1.0):
        return s * jax.random.normal(k, shape, f32)

    def uni(k, shape, lo, hi):
        return jax.random.uniform(k, shape, f32, lo, hi)

    return {
        'x_prompt': nrm(ks[0], (BATCH, SEQ, D_MODEL)),
        'x_sample': nrm(ks[1], (DEC_BATCH, DEC_SEQ, D_MODEL)),
        'cache_fox_k': nrm(ks[2], (L, DEC_BATCH, PAST_LEN, H_B, HEAD_DIM)),
        'cache_fox_v': nrm(ks[3], (L, DEC_BATCH, PAST_LEN, H_B, HEAD_DIM)),
        'cache_fox_logf': jax.nn.log_sigmoid(3.0 + nrm(ks[4], (L, DEC_BATCH, PAST_LEN, H_B))),
        'cache_sb_k': nrm(ks[5], (L, DEC_BATCH, PAST_LEN, H_C, HEAD_DIM)),
        'cache_sb_v': nrm(ks[6], (L, DEC_BATCH, PAST_LEN, H_C, HEAD_DIM)),
        'state_wkv': nrm(ks[7], (L, DEC_BATCH, H_A, HEAD_DIM, HEAD_DIM), 0.1),
        'state_shift': nrm(ks[8], (L, DEC_BATCH, 1, SHIFT_W)),
        'w_in': nrm(ks[9], (L, D_MODEL, IN_COLS), D_MODEL ** -0.5),
        'mu_shift': uni(ks[10], (L, SHIFT_W), 0.0, 1.0),
        'w0_decay': uni(ks[11], (L, W_A), -5.0, 1.0),
        'w_decay': nrm(ks[12], (L, D_DECAY_LORA, W_A), 0.5 * D_DECAY_LORA ** -0.5),
        'a0': nrm(ks[13], (L, W_A), 0.1),
        'w_aaa': nrm(ks[14], (L, D_AAA_LORA, W_A), 0.5 * D_AAA_LORA ** -0.5),
        'k_k': 0.85 + nrm(ks[15], (L, W_A), 0.05),
        'k_a': 1.0 + nrm(ks[16], (L, W_A), 0.05),
        'r_k': nrm(ks[17], (L, H_A, HEAD_DIM), 0.1),
        'lnx_g': 1.0 + nrm(ks[18], (L, W_A), 0.05),
        'lnx_b': nrm(ks[19], (L, W_A), 0.01),
        'fox_fb': uni(ks[20], (L, H_B), 1.0, 5.0),
        'w_out': nrm(ks[21], (L, W_MIX, D_MODEL), BETA * W_MIX ** -0.5),
        'ln_g': 1.0 + nrm(ks[22], (L, D_MODEL), 0.05),
        'ln_b': nrm(ks[23], (L, D_MODEL), 0.01),
    }


def reference(x_prompt, x_sample, cache_fox_k, cache_fox_v, cache_fox_logf, cache_sb_k,
              cache_sb_v, state_wkv, state_shift, w_in, mu_shift, w0_decay, w_decay, a0,
              w_aaa, k_k, k_a, r_k, lnx_g, lnx_b, fox_fb, w_out, ln_g, ln_b):
    yp, ys = x_prompt, x_sample
    new_p, new_s = [], []
    for l in range(DEPTH):
        prm = (w_in[l], mu_shift[l], w0_decay[l], w_decay[l], a0[l], w_aaa[l], k_k[l],
               k_a[l], r_k[l], lnx_g[l], lnx_b[l], fox_fb[l], w_out[l], ln_g[l], ln_b[l])
        yp, st_p = _layer(yp, None, *prm)
        hist = (cache_fox_k[l], cache_fox_v[l], cache_fox_logf[l], cache_sb_k[l],
                cache_sb_v[l], state_wkv[l], state_shift[l])
        ys, st_s = _layer(ys, hist, *prm)
        new_p.append(st_p)
        new_s.append(st_s)

    def stack(sts, i):
        return jnp.stack([st[i] for st in sts])

    p_fox_k, p_fox_v, p_fox_logf = stack(new_p, 0), stack(new_p, 1), stack(new_p, 2)
    p_sb_k, p_sb_v = stack(new_p, 3), stack(new_p, 4)
    p_wkv, p_shift = stack(new_p, 5), stack(new_p, 6)
    s_fox_k, s_fox_v, s_fox_logf = stack(new_s, 0), stack(new_s, 1), stack(new_s, 2)
    s_sb_k, s_sb_v = stack(new_s, 3), stack(new_s, 4)
    s_wkv, s_shift = stack(new_s, 5), stack(new_s, 6)
    return (yp, ys, p_fox_k, p_fox_v, p_fox_logf, p_sb_k, p_sb_v, p_wkv, p_shift,
            s_fox_k, s_fox_v, s_fox_logf, s_sb_k, s_sb_v, s_wkv, s_shift)
```

```python
import functools

import numpy as np
import jax
import jax.numpy as jnp
from jax import lax
from jax.experimental import pallas as pl
from jax.experimental.pallas import tpu as pltpu

F32 = jnp.float32
BF16 = jnp.bfloat16

HEAD_DIM = 64
H_A, H_B, H_C = 6, 6, 4
W_A, W_B, W_C = H_A * HEAD_DIM, H_B * HEAD_DIM, H_C * HEAD_DIM
W_MIX = W_A + W_B + W_C
D_LORA = 64
SHIFT_W = 3 * W_A + 2 * D_LORA
LANES = 128
WKV_CHUNK = 64
GN_EPS = 64e-5
LN_EPS = 1e-5
NEG_INF = -1e30
VMEM_LIMIT = 56 * 1024 * 1024


def _cparams(sem):
    return pltpu.CompilerParams(dimension_semantics=sem, vmem_limit_bytes=VMEM_LIMIT)


def _dot(a, b):
    return jnp.dot(a, b, preferred_element_type=F32)


def _dot_nt(a, b):
    return lax.dot_general(a, b, (((1,), (1,)), ((), ())), preferred_element_type=F32)


def _dot_tn(a, b):
    return lax.dot_general(a, b, (((0,), (0,)), ((), ())), preferred_element_type=F32)


def _split2(x):
    hi = x.astype(BF16)
    lo = (x - hi.astype(F32)).astype(BF16)
    return hi, lo


def _split3(x):
    hi = x.astype(BF16)
    r1 = x - hi.astype(F32)
    mid = r1.astype(BF16)
    lo = (r1 - mid.astype(F32)).astype(BF16)
    return hi, mid, lo


def _softplus(x):
    return jnp.maximum(x, 0.0) + jnp.log1p(jnp.exp(-jnp.abs(x)))


def _sigmoid(x):
    return 1.0 / (1.0 + jnp.exp(-x))


_OFF_SHIFT = 0
_OFF_G = _OFF_SHIFT + SHIFT_W
_OFF_QB = _OFF_G + W_MIX
_OFF_KB = _OFF_QB + W_B
_OFF_VB = _OFF_KB + W_B
_OFF_QC = _OFF_VB + W_B
_OFF_KC = _OFF_QC + W_C
_OFF_VC = _OFF_KC + W_C
_OFF_F = _OFF_VC + W_C
_PROJ_COLS = _OFF_F + LANES


def _proj_kernel(x_ref, w_ref, fb_ref, us_ref, g_ref, qb_ref, kb_ref, vb_ref,
                 qc_ref, kc_ref, vc_ref, lf_ref):
    xb = x_ref[...].astype(BF16)
    for ref, off in ((us_ref, _OFF_SHIFT), (g_ref, _OFF_G), (qb_ref, _OFF_QB),
                     (kb_ref, _OFF_KB), (vb_ref, _OFF_VB), (qc_ref, _OFF_QC),
                     (kc_ref, _OFF_KC), (vc_ref, _OFF_VC)):
        width = ref.shape[-1]
        ref[...] = _dot(xb, w_ref[:, off:off + width])
    f = _dot(xb, w_ref[:, _OFF_F:_OFF_F + LANES]) + fb_ref[...]
    lf_ref[...] = -_softplus(-f)


def _proj(x2d, w_cat, fb_pad, tm):
    n, d = x2d.shape
    widths = (SHIFT_W, W_MIX, W_B, W_B, W_B, W_C, W_C, W_C, LANES)
    return pl.pallas_call(
        _proj_kernel,
        grid=(n // tm,),
        in_specs=[pl.BlockSpec((tm, d), lambda i: (i, 0)),
                  pl.BlockSpec((d, _PROJ_COLS), lambda i: (0, 0)),
                  pl.BlockSpec((1, LANES), lambda i: (0, 0))],
        out_specs=[pl.BlockSpec((tm, w), lambda i: (i, 0)) for w in widths],
        out_shape=[jax.ShapeDtypeStruct((n, w), F32) for w in widths],
        compiler_params=_cparams(("parallel",)),
    )(x2d, w_cat, fb_pad)


def _cumsum_kernel(lf_ref, cum_ref, *, blk):
    tk = lf_ref.shape[1]
    row = lax.broadcasted_iota(jnp.int32, (blk, blk), 0)
    col = lax.broadcasted_iota(jnp.int32, (blk, blk), 1)
    tri = (col <= row).astype(BF16)
    carry = jnp.zeros((1, LANES), F32)
    for s in range(0, tk, blk):
        x = lf_ref[0, s:s + blk, :]
        hi, mid, lo = _split3(x)
        c = _dot(tri, hi) + _dot(tri, mid) + _dot(tri, lo) + carry
        cum_ref[0, s:s + blk, :] = c
        carry = c[blk - 1:blk, :]


def _cumsum(lf, blk=64):
    b, tk, _ = lf.shape
    return pl.pallas_call(
        functools.partial(_cumsum_kernel, blk=blk),
        grid=(b,),
        in_specs=[pl.BlockSpec((1, tk, LANES), lambda i: (i, 0, 0))],
        out_specs=pl.BlockSpec((1, tk, LANES), lambda i: (i, 0, 0)),
        out_shape=jax.ShapeDtypeStruct((b, tk, LANES), F32),
        compiler_params=_cparams(("parallel",)),
    )(lf)


def _seg_sum(x, bd):
    hi, lo = _split2(x)
    return _dot(hi, bd) + _dot(lo, bd)


def _wkv_kernel(us_ref, prev0_ref, s0_ref, mu_ref, w0_ref, wd_ref, a0_ref, wa_ref,
                kkp_ref, ka_ref, rk_ref, lng_ref, lnb_ref,
                oa_ref, sout_ref, s_scr, prev_scr):
    c_idx = pl.program_id(1)
    nc = pl.num_programs(1)
    C = WKV_CHUNK

    @pl.when(c_idx == 0)
    def _():
        s_scr[...] = s0_ref[0]
        prev_scr[...] = prev0_ref[0]

    u = us_ref[0]
    row1 = lax.broadcasted_iota(jnp.int32, (C, 1), 0)
    zprev = jnp.where(row1 == 0, prev_scr[...], pltpu.roll(u, 1, axis=0))
    prev_scr[...] = u[C - 1:C, :]
    zs = u + (zprev - u) * mu_ref[...]
    r = zs[:, 0:W_A]
    k = zs[:, W_A:2 * W_A]
    v = zs[:, 2 * W_A:3 * W_A]
    lora_in = zs[:, 3 * W_A:3 * W_A + LANES]
    lane1 = lax.broadcasted_iota(jnp.int32, (1, LANES), 1)
    lora_t = jnp.where(lane1 < D_LORA, jnp.tanh(lora_in), lora_in).astype(BF16)
    wl = w0_ref[...] + _dot(lora_t, wd_ref[...])
    w_log = -_softplus(-wl) - 0.5
    lw = -jnp.exp(w_log)
    a = _sigmoid(a0_ref[...] + _dot(lora_t, wa_ref[...]))

    lane_w = lax.broadcasted_iota(jnp.int32, (W_A, W_A), 0) // HEAD_DIM
    lane_c = lax.broadcasted_iota(jnp.int32, (W_A, W_A), 1) // HEAD_DIM
    bd = (lane_w == lane_c).astype(BF16)

    kk = k * kkp_ref[...]
    kk = kk * lax.rsqrt(_seg_sum(kk * kk, bd) + 1e-12)
    k2 = k * (1.0 + (a - 1.0) * ka_ref[...])

    rowc = lax.broadcasted_iota(jnp.int32, (C, C), 0)
    colc = lax.broadcasted_iota(jnp.int32, (C, C), 1)
    tri_incl = colc <= rowc
    tri_strict = colc < rowc
    eye = (colc == rowc).astype(F32)
    h1, h2, h3 = _split3(lw)
    tri_b = tri_incl.astype(BF16)
    cs = _dot(tri_b, h1) + _dot(tri_b, h2) + _dot(tri_b, h3)
    cs_last = cs[C - 1:C, :]
    g_in = jnp.exp(cs)
    g_ex = jnp.exp(cs - lw)
    g_inv = jnp.exp(-cs)
    g_end = jnp.exp(cs_last - cs)
    g_all = jnp.exp(cs_last)
    kka = kk * a
    at = kk * g_ex
    rt = r * g_in
    bh = -kka * g_inv
    kh = k2 * g_inv
    b_end = -kka * g_end
    k_end = k2 * g_end

    rowp = lax.broadcasted_iota(jnp.int32, (LANES, LANES), 0) // HEAD_DIM
    colp = lax.broadcasted_iota(jnp.int32, (LANES, LANES), 1) // HEAD_DIM
    bd_pair = rowp == colp
    head_of_lane = lane1 // HEAD_DIM

    y_parts = []
    for p in range(H_A // 2):
        sl = slice(p * LANES, (p + 1) * LANES)
        s_p = s_scr[p]
        l_cat = jnp.concatenate([at[:, sl], rt[:, sl]], axis=0)
        bh_b = bh[:, sl].astype(BF16)
        kh_b = kh[:, sl].astype(BF16)
        v_p = v[:, sl]
        ls = _dot_nt(l_cat.astype(BF16), s_p.astype(BF16))
        rhs = ls[:C]
        mats = []
        for hh in range(2):
            m = head_of_lane == hh
            l_m = jnp.where(m, l_cat, 0.0).astype(BF16)
            ab = _dot_nt(l_m, bh_b)
            ak = _dot_nt(l_m, kh_b)
            a_ab = jnp.where(tri_strict, ab[:C], 0.0)
            a_rb = jnp.where(tri_incl, ab[C:], 0.0)
            a_ak = jnp.where(tri_strict, ak[:C], 0.0)
            a_rk = jnp.where(tri_incl, ak[C:], 0.0)
            n_pow = a_ab
            t_inv = eye + n_pow
            span = 2
            while span < C:
                nb = n_pow.astype(BF16)
                n_pow = _dot(nb, nb)
                t_inv = t_inv + _dot(t_inv.astype(BF16), n_pow.astype(BF16))
                span *= 2
            v_m = jnp.where(m, v_p, 0.0).astype(BF16)
            rhs = rhs + _dot(a_ak.astype(BF16), v_m)
            mats.append((m, t_inv, a_rb, a_rk, v_m))
        u_p = jnp.zeros((C, LANES), F32)
        for m, t_inv, _, _, _ in mats:
            u_p = u_p + _dot(t_inv.astype(BF16), jnp.where(m, rhs, 0.0).astype(BF16))
        y_p = ls[C:]
        for m, _, a_rb, a_rk, v_m in mats:
            y_p = y_p + _dot(a_rb.astype(BF16), jnp.where(m, u_p, 0.0).astype(BF16))
            y_p = y_p + _dot(a_rk.astype(BF16), v_m)
        y_parts.append(y_p)
        uv = jnp.concatenate([u_p, v_p], axis=0).astype(BF16)
        bk = jnp.concatenate([b_end[:, sl], k_end[:, sl]], axis=0).astype(BF16)
        s_new = s_p * g_all[:, sl] + jnp.where(bd_pair, _dot_tn(uv, bk), 0.0)
        s_scr[p] = s_new

    y = jnp.concatenate(y_parts, axis=1)
    inv_n = 1.0 / HEAD_DIM
    mean = _seg_sum(y, bd) * inv_n
    d = y - mean
    var = _seg_sum(d * d, bd) * inv_n
    yn = d * lax.rsqrt(var + GN_EPS) * lng_ref[...] + lnb_ref[...]
    bonus = _seg_sum(r * k2 * rk_ref[...], bd) * v
    oa_ref[0] = yn + bonus

    @pl.when(c_idx == nc - 1)
    def _():
        sout_ref[0] = s_scr[...]


def _wkv(us, prev0, s0_bd, prm):
    b, t, _ = us.shape
    C = WKV_CHUNK
    npair = H_A // 2
    vec = lambda w: pl.BlockSpec((1, w), lambda i, j: (0, 0))
    mat = lambda: pl.BlockSpec((LANES, W_A), lambda i, j: (0, 0))
    return pl.pallas_call(
        _wkv_kernel,
        grid=(b, t // C),
        in_specs=[pl.BlockSpec((1, C, SHIFT_W), lambda i, j: (i, j, 0)),
                  pl.BlockSpec((1, 1, SHIFT_W), lambda i, j: (i, 0, 0)),
                  pl.BlockSpec((1, npair, LANES, LANES), lambda i, j: (i, 0, 0, 0)),
                  vec(SHIFT_W), vec(W_A), mat(), vec(W_A), mat(),
                  vec(W_A), vec(W_A), vec(W_A), vec(W_A), vec(W_A)],
        out_specs=[pl.BlockSpec((1, C, W_A), lambda i, j: (i, j, 0)),
                   pl.BlockSpec((1, npair, LANES, LANES), lambda i, j: (i, 0, 0, 0))],
        out_shape=[jax.ShapeDtypeStruct((b, t, W_A), F32),
                   jax.ShapeDtypeStruct((b, npair, LANES, LANES), F32)],
        scratch_shapes=[pltpu.VMEM((npair, LANES, LANES), F32),
                        pltpu.VMEM((1, SHIFT_W), F32)],
        compiler_params=_cparams(("parallel", "arbitrary")),
    )(us, prev0, s0_bd, prm["mu"], prm["w0"], prm["wd"], prm["a0"], prm["wa"],
      prm["kkp"], prm["ka"], prm["rk"], prm["lnx_g"], prm["lnx_b"])


def _fox_kernel(q_ref, k_ref, v_ref, cq_ref, ck_ref, o_ref, kb_scr, vb_scr, *, tq, tk, past, single):
    if single:
        i = 0
        kb_scr[...] = k_ref[0].astype(BF16)
        vb_scr[...] = v_ref[0].astype(BF16)
    else:
        i = pl.program_id(1)

        @pl.when(i == 0)
        def _():
            kb_scr[...] = k_ref[0].astype(BF16)
            vb_scr[...] = v_ref[0].astype(BF16)

    q = q_ref[0] * (HEAD_DIM ** -0.5)
    cq = cq_ref[0]
    lane1 = lax.broadcasted_iota(jnp.int32, (1, LANES), 1)
    head_of_lane = lane1 // HEAD_DIM
    rowq = lax.broadcasted_iota(jnp.int32, (tq, tq), 0)
    colq = lax.broadcasted_iota(jnp.int32, (tq, tq), 1)
    causal = colq <= rowq
    n_full = past // tk + i * (tq // tk)
    diag0 = past if single else pl.multiple_of(past + i * tq, tq)

    for p in range(H_B // 2):
        sl = slice(p * LANES, (p + 1) * LANES)
        q_p = q[:, sl]
        outs = []
        for hh in range(2):
            h = 2 * p + hh
            q_m = jnp.where(head_of_lane == hh, q_p, 0.0).astype(BF16)
            fq = cq[:, h:h + 1]

            def update(carry, k_blk, v_blk, fk, mask):
                m_i, l_i, acc = carry
                s = _dot_nt(q_m, k_blk) + fq - fk
                if mask is not None:
                    s = jnp.where(mask, s, NEG_INF)
                m_new = jnp.maximum(m_i, jnp.max(s, axis=-1, keepdims=True))
                alpha = jnp.exp(m_i - m_new)
                pr = jnp.exp(s - m_new)
                l_new = alpha * l_i + jnp.sum(pr, axis=-1, keepdims=True)
                acc_new = alpha * acc + _dot(pr.astype(BF16), v_blk)
                return m_new, l_new, acc_new

            def body(j, carry):
                start = pl.multiple_of(j * tk, tk)
                k_blk = kb_scr[pl.ds(start, tk), sl]
                v_blk = vb_scr[pl.ds(start, tk), sl]
                fk = ck_ref[0, h:h + 1, pl.ds(start, tk)]
                return update(carry, k_blk, v_blk, fk, None)

            init = (jnp.full((tq, 1), NEG_INF, F32), jnp.zeros((tq, 1), F32),
                    jnp.zeros((tq, LANES), F32))
            carry = lax.fori_loop(0, n_full, body, init)
            k_blk = kb_scr[pl.ds(diag0, tq), sl]
            v_blk = vb_scr[pl.ds(diag0, tq), sl]
            fk = ck_ref[0, h:h + 1, pl.ds(diag0, tq)]
            _, l_f, acc_f = update(carry, k_blk, v_blk, fk, causal)
            outs.append(acc_f / l_f)
        o_ref[0, :, sl] = jnp.where(head_of_lane == 0, outs[0], outs[1])


def _fox(q, k_all, v_all, cum_q, cum_k_row, *, tq, tk, past):
    b, t, _ = q.shape
    tkk = k_all.shape[1]
    return pl.pallas_call(
        functools.partial(_fox_kernel, tq=tq, tk=tk, past=past, single=(t == tq)),
        grid=(b, t // tq),
        in_specs=[pl.BlockSpec((1, tq, W_B), lambda i, j: (i, j, 0)),
                  pl.BlockSpec((1, tkk, W_B), lambda i, j: (i, 0, 0)),
                  pl.BlockSpec((1, tkk, W_B), lambda i, j: (i, 0, 0)),
                  pl.BlockSpec((1, tq, LANES), lambda i, j: (i, j, 0)),
                  pl.BlockSpec((1, 8, tkk), lambda i, j: (i, 0, 0))],
        out_specs=pl.BlockSpec((1, tq, W_B), lambda i, j: (i, j, 0)),
        out_shape=jax.ShapeDtypeStruct((b, t, W_B), F32),
        scratch_shapes=[pltpu.VMEM((tkk, W_B), BF16), pltpu.VMEM((tkk, W_B), BF16)],
        compiler_params=_cparams(("parallel", "arbitrary")),
    )(q, k_all, v_all, cum_q, cum_k_row)


def _sb_kernel(q_ref, k_ref, v_ref, o_ref, kb_scr, vb_scr, *, tq, tk, past, single):
    if single:
        i = 0
        kb_scr[...] = k_ref[0].astype(BF16)
        vb_scr[...] = v_ref[0].astype(BF16)
    else:
        i = pl.program_id(1)

        @pl.when(i == 0)
        def _():
            kb_scr[...] = k_ref[0].astype(BF16)
            vb_scr[...] = v_ref[0].astype(BF16)

    q = q_ref[0] * (HEAD_DIM ** -0.5)
    lane1 = lax.broadcasted_iota(jnp.int32, (1, LANES), 1)
    head_of_lane = lane1 // HEAD_DIM
    rowq = lax.broadcasted_iota(jnp.int32, (tq, tq), 0)
    colq = lax.broadcasted_iota(jnp.int32, (tq, tq), 1)
    strict = colq < rowq

    def later_matrix(n):
        rr = lax.broadcasted_iota(jnp.int32, (n, n), 0)
        cc = lax.broadcasted_iota(jnp.int32, (n, n), 1)
        return (rr > cc).astype(BF16)

    later_q = later_matrix(tq)
    later_k = later_q if tk == tq else later_matrix(tk)
    n_full = past // tk + i * (tq // tk)
    diag0 = past if single else pl.multiple_of(past + i * tq, tq)

    for p in range(H_C // 2):
        sl = slice(p * LANES, (p + 1) * LANES)
        q_p = q[:, sl]
        outs = []
        for hh in range(2):
            q_m = jnp.where(head_of_lane == hh, q_p, 0.0).astype(BF16)

            def update(carry, k_blk, v_blk, later, mask):
                run, acc = carry
                z = _dot_nt(q_m, k_blk)
                l1m = -_softplus(z)
                if mask is not None:
                    l1m = jnp.where(mask, l1m, 0.0)
                hi, lo = _split2(l1m)
                after = _dot(hi, later) + _dot(lo, later)
                w = jnp.exp(z + l1m + after + run)
                if mask is not None:
                    w = jnp.where(mask, w, 0.0)
                acc_new = acc + _dot(w.astype(BF16), v_blk)
                run_new = run + jnp.sum(l1m, axis=-1, keepdims=True)
                return run_new, acc_new

            k_blk = kb_scr[pl.ds(diag0, tq), sl]
            v_blk = vb_scr[pl.ds(diag0, tq), sl]
            init = (jnp.zeros((tq, 1), F32), jnp.zeros((tq, LANES), F32))
            carry = update(init, k_blk, v_blk, later_q, strict)

            def body(jj, carry):
                j = n_full - 1 - jj
                start = pl.multiple_of(j * tk, tk)
                return update(carry, kb_scr[pl.ds(start, tk), sl], vb_scr[pl.ds(start, tk), sl],
                              later_k, None)

            _, acc_f = lax.fori_loop(0, n_full, body, carry)
            outs.append(acc_f)
        o_ref[0, :, sl] = jnp.where(head_of_lane == 0, outs[0], outs[1])


def _sb(q, k_all, v_all, *, tq, tk, past):
    b, t, _ = q.shape
    tkk = k_all.shape[1]
    return pl.pallas_call(
        functools.partial(_sb_kernel, tq=tq, tk=tk, past=past, single=(t == tq)),
        grid=(b, t // tq),
        in_specs=[pl.BlockSpec((1, tq, W_C), lambda i, j: (i, j, 0)),
                  pl.BlockSpec((1, tkk, W_C), lambda i, j: (i, 0, 0)),
                  pl.BlockSpec((1, tkk, W_C), lambda i, j: (i, 0, 0))],
        out_specs=pl.BlockSpec((1, tq, W_C), lambda i, j: (i, j, 0)),
        out_shape=jax.ShapeDtypeStruct((b, t, W_C), F32),
        scratch_shapes=[pltpu.VMEM((tkk, W_C), BF16), pltpu.VMEM((tkk, W_C), BF16)],
        compiler_params=_cparams(("parallel", "arbitrary")),
    )(q, k_all, v_all)


def _out_kernel(oa_ref, ob_ref, oc_ref, g_ref, x_ref, w_ref, lng_ref, lnb_ref, y_ref, *, alpha):
    g = g_ref[...]
    gate = g * _sigmoid(g)
    acc = _dot((oa_ref[...] * gate[:, 0:W_A]).astype(BF16), w_ref[0:W_A, :])
    acc = acc + _dot((ob_ref[...] * gate[:, W_A:W_A + W_B]).astype(BF16), w_ref[W_A:W_A + W_B, :])
    acc = acc + _dot((oc_ref[...] * gate[:, W_A + W_B:]).astype(BF16), w_ref[W_A + W_B:, :])
    z = alpha * x_ref[...] + acc
    mu = jnp.mean(z, axis=-1, keepdims=True)
    d = z - mu
    var = jnp.mean(d * d, axis=-1, keepdims=True)
    y_ref[...] = d * lax.rsqrt(var + LN_EPS) * lng_ref[...] + lnb_ref[...]


def _out(oa, ob, oc, g, x2d, w_out, ln_g, ln_b, alpha, tm):
    n, d = x2d.shape
    row = lambda w: pl.BlockSpec((tm, w), lambda i: (i, 0))
    return pl.pallas_call(
        functools.partial(_out_kernel, alpha=alpha),
        grid=(n // tm,),
        in_specs=[row(W_A), row(W_B), row(W_C), row(W_MIX), row(d),
                  pl.BlockSpec((W_MIX, d), lambda i: (0, 0)),
                  pl.BlockSpec((1, d), lambda i: (0, 0)),
                  pl.BlockSpec((1, d), lambda i: (0, 0))],
        out_specs=row(d),
        out_shape=jax.ShapeDtypeStruct((n, d), F32),
        compiler_params=_cparams(("parallel",)),
    )(oa, ob, oc, g, x2d, w_out, ln_g, ln_b)


def _pad_lanes(a, width=LANES):
    return jnp.pad(a, [(0, 0)] * (a.ndim - 1) + [(0, width - a.shape[-1])])


def _state_to_pairs(s):
    b = s.shape[0]
    s = s.reshape(b, H_A // 2, 2, HEAD_DIM, HEAD_DIM)
    eye2 = jnp.eye(2, dtype=s.dtype)
    bd = jnp.einsum("bphvk,hg->bphvgk", s, eye2)
    return bd.reshape(b, H_A // 2, LANES, LANES)


def _pairs_to_state(sp):
    b = sp.shape[0]
    s = sp.reshape(b, H_A // 2, 2, HEAD_DIM, 2, HEAD_DIM)
    s = jnp.stack([s[:, :, 0, :, 0, :], s[:, :, 1, :, 1, :]], axis=2)
    return s.reshape(b, H_A, HEAD_DIM, HEAD_DIM)


def _layer_params(l, w_in, mu_shift, w0_decay, w_decay, a0, w_aaa, k_k, k_a, r_k,
                  lnx_g, lnx_b, fox_fb, w_out, ln_g, ln_b):
    w = w_in[l]
    sizes = (SHIFT_W, W_A, W_B, W_B, W_B, H_B, W_B, W_C, W_C, W_C, W_C)
    offs = np.concatenate([[0], np.cumsum(sizes)])
    (w_shift, w_ga, w_qb, w_kb, w_vb, w_fb, w_gb, w_qc, w_kc, w_vc, w_gc) = (
        w[:, int(offs[i]):int(offs[i + 1])] for i in range(len(sizes)))
    w_cat = jnp.concatenate([w_shift, w_ga, w_gb, w_gc, w_qb, w_kb, w_vb, w_qc, w_kc, w_vc,
                             _pad_lanes(w_fb)], axis=1).astype(BF16)
    zeros = jnp.zeros((D_LORA, W_A), F32)
    row = lambda a: a.reshape(1, -1).astype(F32)
    return dict(
        w_cat=w_cat, fb=_pad_lanes(row(fox_fb[l])),
        mu=row(mu_shift[l]), w0=row(w0_decay[l]),
        wd=jnp.concatenate([w_decay[l], zeros], axis=0).astype(BF16),
        a0=row(a0[l]), wa=jnp.concatenate([zeros, w_aaa[l]], axis=0).astype(BF16),
        kkp=row(k_k[l]), ka=row(k_a[l]), rk=row(r_k[l]), lnx_g=row(lnx_g[l]), lnx_b=row(lnx_b[l]),
        w_out=w_out[l].astype(BF16), ln_g=row(ln_g[l]), ln_b=row(ln_b[l]))


def _run_layer(x, hist, prm, alpha):
    b, t, d = x.shape
    n = b * t
    x2d = x.reshape(n, d)
    tm = 256 if n % 256 == 0 else n
    us, g, qb, kb, vb, qc, kc, vc, lf = _proj(x2d, prm["w_cat"], prm["fb"], tm)
    r3 = lambda a: a.reshape(b, t, a.shape[-1])
    us, qb, kb, vb, qc, kc, vc, lf = (r3(a) for a in (us, qb, kb, vb, qc, kc, vc, lf))

    if hist is None:
        past = 0
        prev0 = jnp.zeros((b, 1, SHIFT_W), F32)
        s0 = jnp.zeros((b, H_A // 2, LANES, LANES), F32)
        kb_all, vb_all, lf_all, kc_all, vc_all = kb, vb, lf, kc, vc
        tq = min(256, t)
        tk = tq
    else:
        h_fk, h_fv, h_lf, h_sk, h_sv, h_wkv, h_shift = hist
        past = h_fk.shape[1]
        prev0 = h_shift
        s0 = _state_to_pairs(h_wkv)
        kb_all = jnp.concatenate([h_fk.reshape(b, past, W_B), kb], axis=1)
        vb_all = jnp.concatenate([h_fv.reshape(b, past, W_B), vb], axis=1)
        lf_all = jnp.concatenate([_pad_lanes(h_lf), lf], axis=1)
        kc_all = jnp.concatenate([h_sk.reshape(b, past, W_C), kc], axis=1)
        vc_all = jnp.concatenate([h_sv.reshape(b, past, W_C), vc], axis=1)
        tq = t
        tk = min(256, past)

    oa, s_fin = _wkv(us, prev0, s0, prm)
    cum = _cumsum(lf_all)
    cum_row = jnp.swapaxes(cum[:, :, :8], 1, 2)
    ob = _fox(qb, kb_all, vb_all, cum[:, past:], cum_row, tq=tq, tk=tk, past=past)
    oc = _sb(qc, kc_all, vc_all, tq=tq, tk=tk, past=past)
    y = _out(oa.reshape(n, W_A), ob.reshape(n, W_B), oc.reshape(n, W_C), g, x2d,
             prm["w_out"], prm["ln_g"], prm["ln_b"], alpha, tm)
    new = (kb.reshape(b, t, H_B, HEAD_DIM), vb.reshape(b, t, H_B, HEAD_DIM), lf[:, :, :H_B],
           kc.reshape(b, t, H_C, HEAD_DIM), vc.reshape(b, t, H_C, HEAD_DIM),
           _pairs_to_state(s_fin), us[:, -1:, :])
    return y.reshape(b, t, d), new


def kernel(x_prompt, x_sample, cache_fox_k, cache_fox_v, cache_fox_logf, cache_sb_k, cache_sb_v, state_wkv, state_shift, w_in, mu_shift, w0_decay, w_decay, a0, w_aaa, k_k, k_a, r_k, lnx_g, lnx_b, fox_fb, w_out, ln_g, ln_b):
    depth = w_in.shape[0]
    alpha = (2 * depth) ** 0.25
    yp, ys = x_prompt, x_sample
    new_p, new_s = [], []
    for l in range(depth):
        prm = _layer_params(l, w_in, mu_shift, w0_decay, w_decay, a0, w_aaa, k_k, k_a, r_k,
                            lnx_g, lnx_b, fox_fb, w_out, ln_g, ln_b)
        yp, st_p = _run_layer(yp, None, prm, alpha)
        hist = (cache_fox_k[l], cache_fox_v[l], cache_fox_logf[l], cache_sb_k[l],
                cache_sb_v[l], state_wkv[l], state_shift[l])
        ys, st_s = _run_layer(ys, hist, prm, alpha)
        new_p.append(st_p)
        new_s.append(st_s)
    stack = lambda sts, i: jnp.stack([st[i] for st in sts])
    return ((yp, ys) + tuple(stack(new_p, i) for i in range(7))
            + tuple(stack(new_s, i) for i in range(7)))
```

```python
import functools

import numpy as np
import jax
import jax.numpy as jnp
from jax import lax
from jax.experimental import pallas as pl
from jax.experimental.pallas import tpu as pltpu

F32 = jnp.float32
BF16 = jnp.bfloat16

HEAD_DIM = 64
H_A, H_B, H_C = 6, 6, 4
W_A, W_B, W_C = H_A * HEAD_DIM, H_B * HEAD_DIM, H_C * HEAD_DIM
W_MIX = W_A + W_B + W_C
D_LORA = 64
SHIFT_W = 3 * W_A + 2 * D_LORA
LANES = 128
WKV_CHUNK = 64
WKV_BLOCK = 128
GN_EPS = 64e-5
LN_EPS = 1e-5
NEG_INF = -1e30
VMEM_LIMIT = 56 * 1024 * 1024


def _cparams(sem):
    return pltpu.CompilerParams(dimension_semantics=sem, vmem_limit_bytes=VMEM_LIMIT)


def _dot(a, b):
    return jnp.dot(a, b, preferred_element_type=F32)


def _dot_nt(a, b):
    return lax.dot_general(a, b, (((1,), (1,)), ((), ())), preferred_element_type=F32)


def _dot_tn(a, b):
    return lax.dot_general(a, b, (((0,), (0,)), ((), ())), preferred_element_type=F32)


def _split2(x):
    hi = x.astype(BF16)
    lo = (x - hi.astype(F32)).astype(BF16)
    return hi, lo


def _split3(x):
    hi = x.astype(BF16)
    r1 = x - hi.astype(F32)
    mid = r1.astype(BF16)
    lo = (r1 - mid.astype(F32)).astype(BF16)
    return hi, mid, lo


def _softplus(x):
    return jnp.maximum(x, 0.0) + jnp.log1p(jnp.exp(-jnp.abs(x)))


def _sigmoid(x):
    return 1.0 / (1.0 + jnp.exp(-x))


_OFF_SHIFT = 0
_OFF_G = _OFF_SHIFT + SHIFT_W
_OFF_QB = _OFF_G + W_MIX
_OFF_KB = _OFF_QB + W_B
_OFF_VB = _OFF_KB + W_B
_OFF_QC = _OFF_VB + W_B
_OFF_KC = _OFF_QC + W_C
_OFF_VC = _OFF_KC + W_C
_OFF_F = _OFF_VC + W_C
_PROJ_COLS = _OFF_F + LANES


def _proj_kernel(x_ref, w_ref, fb_ref, us_ref, g_ref, qb_ref, kb_ref, vb_ref,
                 qc_ref, kc_ref, vc_ref, lf_ref):
    xb = x_ref[...].astype(BF16)
    for ref, off in ((us_ref, _OFF_SHIFT), (g_ref, _OFF_G), (qb_ref, _OFF_QB),
                     (kb_ref, _OFF_KB), (vb_ref, _OFF_VB), (qc_ref, _OFF_QC),
                     (kc_ref, _OFF_KC), (vc_ref, _OFF_VC)):
        width = ref.shape[-1]
        ref[...] = _dot(xb, w_ref[:, off:off + width])
    f = _dot(xb, w_ref[:, _OFF_F:_OFF_F + LANES]) + fb_ref[...]
    lf_ref[...] = -_softplus(-f)


def _proj(x2d, w_cat, fb_pad, tm):
    n, d = x2d.shape
    widths = (SHIFT_W, W_MIX, W_B, W_B, W_B, W_C, W_C, W_C, LANES)
    return pl.pallas_call(
        _proj_kernel,
        grid=(n // tm,),
        in_specs=[pl.BlockSpec((tm, d), lambda i: (i, 0)),
                  pl.BlockSpec((d, _PROJ_COLS), lambda i: (0, 0)),
                  pl.BlockSpec((1, LANES), lambda i: (0, 0))],
        out_specs=[pl.BlockSpec((tm, w), lambda i: (i, 0)) for w in widths],
        out_shape=[jax.ShapeDtypeStruct((n, w), F32) for w in widths],
        compiler_params=_cparams(("parallel",)),
    )(x2d, w_cat, fb_pad)


def _cumsum_kernel(lf_ref, cum_ref, *, blk):
    tk = lf_ref.shape[1]
    row = lax.broadcasted_iota(jnp.int32, (blk, blk), 0)
    col = lax.broadcasted_iota(jnp.int32, (blk, blk), 1)
    tri = (col <= row).astype(BF16)
    carry = jnp.zeros((1, LANES), F32)
    for s in range(0, tk, blk):
        x = lf_ref[0, s:s + blk, :]
        hi, mid, lo = _split3(x)
        c = _dot(tri, hi) + _dot(tri, mid) + _dot(tri, lo) + carry
        cum_ref[0, s:s + blk, :] = c
        carry = c[blk - 1:blk, :]


def _cumsum(lf, blk=64):
    b, tk, _ = lf.shape
    return pl.pallas_call(
        functools.partial(_cumsum_kernel, blk=blk),
        grid=(b,),
        in_specs=[pl.BlockSpec((1, tk, LANES), lambda i: (i, 0, 0))],
        out_specs=pl.BlockSpec((1, tk, LANES), lambda i: (i, 0, 0)),
        out_shape=jax.ShapeDtypeStruct((b, tk, LANES), F32),
        compiler_params=_cparams(("parallel",)),
    )(lf)


def _seg_sum(x, bd):
    hi, lo = _split2(x)
    return _dot(hi, bd) + _dot(lo, bd)


def _tri_inverse(n1s, eye):
    def sq(xs):
        xb = [x.astype(BF16) for x in xs]
        return [_dot(x, x) for x in xb]

    def pair(xs, ys):
        return [eye + x + y + _dot(x.astype(BF16), y.astype(BF16)) for x, y in zip(xs, ys)]

    def mul(xs, ys):
        return [_dot(x.astype(BF16), y.astype(BF16)) for x, y in zip(xs, ys)]

    n2s = sq(n1s)
    n4s = sq(n2s)
    p1s = pair(n1s, n2s)
    n8s = sq(n4s)
    n16s = sq(n8s)
    p2s = pair(n4s, n8s)
    n32s = sq(n16s)
    p12s = mul(p1s, p2s)
    p3s = pair(n16s, n32s)
    return mul(p12s, p3s)


def _wkv_kernel(us_ref, prev0_ref, s0_ref, mu_ref, w0_ref, wd_ref, a0_ref, wa_ref,
                kkp_ref, ka_ref, rk_ref, lng_ref, lnb_ref,
                oa_ref, sout_ref,
                s_scr, prev_scr, at_s, rt_s, bh_s, kh_s, be_s, ke_s, v_s, ga_s, y_s):
    c_idx = pl.program_id(1)
    nc = pl.num_programs(1)
    C = WKV_CHUNK
    tt = us_ref.shape[1]
    nb = tt // C

    @pl.when(c_idx == 0)
    def _():
        s_scr[...] = s0_ref[0]
        prev_scr[...] = prev0_ref[0]

    u = us_ref[0]
    row1 = lax.broadcasted_iota(jnp.int32, (tt, 1), 0)
    zprev = jnp.where(row1 == 0, prev_scr[...], pltpu.roll(u, 1, axis=0))
    prev_scr[...] = u[tt - 1:tt, :]
    zs = u + (zprev - u) * mu_ref[...]
    r = zs[:, 0:W_A]
    k = zs[:, W_A:2 * W_A]
    v = zs[:, 2 * W_A:3 * W_A]
    lora_in = zs[:, 3 * W_A:3 * W_A + LANES]
    lane1 = lax.broadcasted_iota(jnp.int32, (1, LANES), 1)
    lora_t = jnp.where(lane1 < D_LORA, jnp.tanh(lora_in), lora_in).astype(BF16)
    wl = w0_ref[...] + _dot(lora_t, wd_ref[...])
    w_log = -_softplus(-wl) - 0.5
    lw = -jnp.exp(w_log)
    a = _sigmoid(a0_ref[...] + _dot(lora_t, wa_ref[...]))

    lane_w = lax.broadcasted_iota(jnp.int32, (W_A, W_A), 0) // HEAD_DIM
    lane_c = lax.broadcasted_iota(jnp.int32, (W_A, W_A), 1) // HEAD_DIM
    bd = (lane_w == lane_c).astype(BF16)

    kk = k * kkp_ref[...]
    kk = kk * lax.rsqrt(_seg_sum(kk * kk, bd) + 1e-12)
    k2 = k * (1.0 + (a - 1.0) * ka_ref[...])

    rowt = lax.broadcasted_iota(jnp.int32, (tt, tt), 0)
    colt = lax.broadcasted_iota(jnp.int32, (tt, tt), 1)
    same_chunk = (rowt // C) == (colt // C)
    tri_b = (same_chunk & (colt <= rowt)).astype(BF16)
    ones_b = same_chunk.astype(BF16)
    h1, h2, h3 = _split3(lw)
    cs = _dot(tri_b, h1) + _dot(tri_b, h2) + _dot(tri_b, h3)
    tot = _dot(ones_b, h1) + _dot(ones_b, h2) + _dot(ones_b, h3)
    g_inv = jnp.exp(-cs)
    g_end = jnp.exp(tot - cs)
    kka = kk * a
    at_s[...] = kk * jnp.exp(cs - lw)
    rt_s[...] = r * jnp.exp(cs)
    bh_s[...] = (-kka * g_inv).astype(BF16)
    kh_s[...] = (k2 * g_inv).astype(BF16)
    be_s[...] = (-kka * g_end).astype(BF16)
    ke_s[...] = (k2 * g_end).astype(BF16)
    v_s[...] = v
    ga_s[...] = jnp.exp(tot)

    rowc = lax.broadcasted_iota(jnp.int32, (C, C), 0)
    colc = lax.broadcasted_iota(jnp.int32, (C, C), 1)
    tri_incl = colc <= rowc
    tri_strict = colc < rowc
    eye = (colc == rowc).astype(F32)
    rowp = lax.broadcasted_iota(jnp.int32, (LANES, LANES), 0) // HEAD_DIM
    colp = lax.broadcasted_iota(jnp.int32, (LANES, LANES), 1) // HEAD_DIM
    bd_pair = rowp == colp
    head_of_lane = lane1 // HEAD_DIM
    npair = H_A // 2

    units = [(c, p) for c in range(nb) for p in range(npair)]
    probs = [(ui, hh) for ui in range(len(units)) for hh in range(2)]
    ld = []
    for c, p in units:
        rows = slice(c * C, (c + 1) * C)
        sl = slice(p * LANES, (p + 1) * LANES)
        ld.append(dict(at=at_s[rows, sl], rt=rt_s[rows, sl], v=v_s[rows, sl], bh=bh_s[rows, sl],
                       kh=kh_s[rows, sl], be=be_s[rows, sl], ke=ke_s[rows, sl]))
    masks = [head_of_lane == hh for hh in range(2)]
    l_ms = [jnp.where(masks[hh], jnp.concatenate([ld[ui]["at"], ld[ui]["rt"]], axis=0), 0.0).astype(BF16)
            for ui, hh in probs]
    abs_ = [_dot_nt(l_m, ld[ui]["bh"]) for l_m, (ui, hh) in zip(l_ms, probs)]
    aks = [_dot_nt(l_m, ld[ui]["kh"]) for l_m, (ui, hh) in zip(l_ms, probs)]
    a_abs = [jnp.where(tri_strict, ab[:C], 0.0) for ab in abs_]
    a_rbs = [jnp.where(tri_incl, ab[C:], 0.0).astype(BF16) for ab in abs_]
    a_aks = [jnp.where(tri_strict, ak[:C], 0.0).astype(BF16) for ak in aks]
    a_rks = [jnp.where(tri_incl, ak[C:], 0.0).astype(BF16) for ak in aks]
    v_ms = [jnp.where(masks[hh], ld[ui]["v"], 0.0).astype(BF16) for ui, hh in probs]
    avs = [_dot(a_ak, v_m) for a_ak, v_m in zip(a_aks, v_ms)]
    yvs = [_dot(a_rk, v_m) for a_rk, v_m in zip(a_rks, v_ms)]
    t_invs = _tri_inverse(a_abs, eye)
    zs_ = [jnp.concatenate([jnp.where(masks[hh], ld[ui]["at"], 0.0), av], axis=1).astype(BF16)
           for av, (ui, hh) in zip(avs, probs)]
    xs = [_dot(t.astype(BF16), z) for t, z in zip(t_invs, zs_)]
    rys = [_dot(a_rb, x.astype(BF16)) for a_rb, x in zip(a_rbs, xs)]
    atps, wvs, rps, y0s = [], [], [], []
    for ui in range(len(units)):
        x0, x1 = xs[2 * ui], xs[2 * ui + 1]
        ry0, ry1 = rys[2 * ui], rys[2 * ui + 1]
        atps.append((x0[:, :LANES] + x1[:, :LANES]).astype(BF16))
        wvs.append(jnp.concatenate([x0[:, LANES:] + x1[:, LANES:], ld[ui]["v"]], axis=0).astype(BF16))
        rps.append((ld[ui]["rt"] + ry0[:, :LANES] + ry1[:, :LANES]).astype(BF16))
        y0s.append(ry0[:, LANES:] + ry1[:, LANES:] + yvs[2 * ui] + yvs[2 * ui + 1])
    pps = [jnp.where(bd_pair, _dot_tn(atp, ld[ui]["be"]), 0.0).astype(BF16)
           for ui, atp in enumerate(atps)]
    qs = [jnp.where(bd_pair, _dot_tn(wv, jnp.concatenate([ld[ui]["be"], ld[ui]["ke"]], axis=0)), 0.0)
          for ui, wv in enumerate(wvs)]

    for c in range(nb):
        rows = slice(c * C, (c + 1) * C)
        s_ps = [s_scr[p] for p in range(npair)]
        s_bs = [s_p.astype(BF16) for s_p in s_ps]
        for p in range(npair):
            ui = c * npair + p
            sl = slice(p * LANES, (p + 1) * LANES)
            y_s[rows, sl] = _dot_nt(rps[ui], s_bs[p]) + y0s[ui]
            s_scr[p] = s_ps[p] * ga_s[c * C:c * C + 1, sl] + _dot(s_bs[p], pps[ui]) + qs[ui]

    y = y_s[...]
    inv_n = 1.0 / HEAD_DIM
    mean = _seg_sum(y, bd) * inv_n
    d = y - mean
    var = _seg_sum(d * d, bd) * inv_n
    yn = d * lax.rsqrt(var + GN_EPS) * lng_ref[...] + lnb_ref[...]
    bonus = _seg_sum(r * k2 * rk_ref[...], bd) * v
    oa_ref[0] = yn + bonus

    @pl.when(c_idx == nc - 1)
    def _():
        sout_ref[0] = s_scr[...]


def _wkv(us, prev0, s0_bd, prm, tt):
    b, t, _ = us.shape
    npair = H_A // 2
    vec = lambda w: pl.BlockSpec((1, w), lambda i, j: (0, 0))
    mat = lambda: pl.BlockSpec((LANES, W_A), lambda i, j: (0, 0))
    blk = lambda dt: pltpu.VMEM((tt, W_A), dt)
    return pl.pallas_call(
        _wkv_kernel,
        grid=(b, t // tt),
        in_specs=[pl.BlockSpec((1, tt, SHIFT_W), lambda i, j: (i, j, 0)),
                  pl.BlockSpec((1, 1, SHIFT_W), lambda i, j: (i, 0, 0)),
                  pl.BlockSpec((1, npair, LANES, LANES), lambda i, j: (i, 0, 0, 0)),
                  vec(SHIFT_W), vec(W_A), mat(), vec(W_A), mat(),
                  vec(W_A), vec(W_A), vec(W_A), vec(W_A), vec(W_A)],
        out_specs=[pl.BlockSpec((1, tt, W_A), lambda i, j: (i, j, 0)),
                   pl.BlockSpec((1, npair, LANES, LANES), lambda i, j: (i, 0, 0, 0))],
        out_shape=[jax.ShapeDtypeStruct((b, t, W_A), F32),
                   jax.ShapeDtypeStruct((b, npair, LANES, LANES), F32)],
        scratch_shapes=[pltpu.VMEM((npair, LANES, LANES), F32),
                        pltpu.VMEM((1, SHIFT_W), F32),
                        blk(F32), blk(F32), blk(BF16), blk(BF16), blk(BF16), blk(BF16),
                        blk(F32), blk(F32), blk(F32)],
        compiler_params=_cparams(("parallel", "arbitrary")),
    )(us, prev0, s0_bd, prm["mu"], prm["w0"], prm["wd"], prm["a0"], prm["wa"],
      prm["kkp"], prm["ka"], prm["rk"], prm["lnx_g"], prm["lnx_b"])


def _fox_kernel(q_ref, k_ref, v_ref, cq_ref, ck_ref, o_ref, kb_scr, vb_scr, *, tq, tk, past, single):
    if single:
        i = 0
        kb_scr[...] = k_ref[0].astype(BF16)
        vb_scr[...] = v_ref[0].astype(BF16)
    else:
        i = pl.program_id(1)

        @pl.when(i == 0)
        def _():
            kb_scr[...] = k_ref[0].astype(BF16)
            vb_scr[...] = v_ref[0].astype(BF16)

    q = q_ref[0] * (HEAD_DIM ** -0.5)
    cq = cq_ref[0]
    lane1 = lax.broadcasted_iota(jnp.int32, (1, LANES), 1)
    head_of_lane = lane1 // HEAD_DIM
    rowq = lax.broadcasted_iota(jnp.int32, (tq, tq), 0)
    colq = lax.broadcasted_iota(jnp.int32, (tq, tq), 1)
    causal = colq <= rowq
    n_full = past // tk + i * (tq // tk)
    diag0 = past if single else pl.multiple_of(past + i * tq, tq)

    for p in range(H_B // 2):
        sl = slice(p * LANES, (p + 1) * LANES)
        q_p = q[:, sl]
        outs = []
        for hh in range(2):
            h = 2 * p + hh
            q_m = jnp.where(head_of_lane == hh, q_p, 0.0).astype(BF16)
            fq = cq[:, h:h + 1]

            def update(carry, k_blk, v_blk, fk, mask):
                m_i, l_i, acc = carry
                s = _dot_nt(q_m, k_blk) + fq - fk
                if mask is not None:
                    s = jnp.where(mask, s, NEG_INF)
                m_new = jnp.maximum(m_i, jnp.max(s, axis=-1, keepdims=True))
                alpha = jnp.exp(m_i - m_new)
                pr = jnp.exp(s - m_new)
                l_new = alpha * l_i + jnp.sum(pr, axis=-1, keepdims=True)
                acc_new = alpha * acc + _dot(pr.astype(BF16), v_blk)
                return m_new, l_new, acc_new

            def body(j, carry):
                start = pl.multiple_of(j * tk, tk)
                k_blk = kb_scr[pl.ds(start, tk), sl]
                v_blk = vb_scr[pl.ds(start, tk), sl]
                fk = ck_ref[0, h:h + 1, pl.ds(start, tk)]
                return update(carry, k_blk, v_blk, fk, None)

            init = (jnp.full((tq, 1), NEG_INF, F32), jnp.zeros((tq, 1), F32),
                    jnp.zeros((tq, LANES), F32))
            carry = lax.fori_loop(0, n_full, body, init)
            k_blk = kb_scr[pl.ds(diag0, tq), sl]
            v_blk = vb_scr[pl.ds(diag0, tq), sl]
            fk = ck_ref[0, h:h + 1, pl.ds(diag0, tq)]
            _, l_f, acc_f = update(carry, k_blk, v_blk, fk, causal)
            outs.append(acc_f / l_f)
        o_ref[0, :, sl] = jnp.where(head_of_lane == 0, outs[0], outs[1])


def _fox(q, k_all, v_all, cum_q, cum_k_row, *, tq, tk, past):
    b, t, _ = q.shape
    tkk = k_all.shape[1]
    return pl.pallas_call(
        functools.partial(_fox_kernel, tq=tq, tk=tk, past=past, single=(t == tq)),
        grid=(b, t // tq),
        in_specs=[pl.BlockSpec((1, tq, W_B), lambda i, j: (i, j, 0)),
                  pl.BlockSpec((1, tkk, W_B), lambda i, j: (i, 0, 0)),
                  pl.BlockSpec((1, tkk, W_B), lambda i, j: (i, 0, 0)),
                  pl.BlockSpec((1, tq, LANES), lambda i, j: (i, j, 0)),
                  pl.BlockSpec((1, 8, tkk), lambda i, j: (i, 0, 0))],
        out_specs=pl.BlockSpec((1, tq, W_B), lambda i, j: (i, j, 0)),
        out_shape=jax.ShapeDtypeStruct((b, t, W_B), F32),
        scratch_shapes=[pltpu.VMEM((tkk, W_B), BF16), pltpu.VMEM((tkk, W_B), BF16)],
        compiler_params=_cparams(("parallel", "arbitrary")),
    )(q, k_all, v_all, cum_q, cum_k_row)


def _sb_kernel(q_ref, k_ref, v_ref, o_ref, kb_scr, vb_scr, *, tq, tk, past, single):
    if single:
        i = 0
        kb_scr[...] = k_ref[0].astype(BF16)
        vb_scr[...] = v_ref[0].astype(BF16)
    else:
        i = pl.program_id(1)

        @pl.when(i == 0)
        def _():
            kb_scr[...] = k_ref[0].astype(BF16)
            vb_scr[...] = v_ref[0].astype(BF16)

    q = q_ref[0] * (HEAD_DIM ** -0.5)
    lane1 = lax.broadcasted_iota(jnp.int32, (1, LANES), 1)
    head_of_lane = lane1 // HEAD_DIM
    rowq = lax.broadcasted_iota(jnp.int32, (tq, tq), 0)
    colq = lax.broadcasted_iota(jnp.int32, (tq, tq), 1)
    strict = colq < rowq

    def later_matrix(n):
        rr = lax.broadcasted_iota(jnp.int32, (n, n), 0)
        cc = lax.broadcasted_iota(jnp.int32, (n, n), 1)
        return (rr > cc).astype(BF16)

    later_q = later_matrix(tq)
    later_k = later_q if tk == tq else later_matrix(tk)
    n_full = past // tk + i * (tq // tk)
    diag0 = past if single else pl.multiple_of(past + i * tq, tq)

    for p in range(H_C // 2):
        sl = slice(p * LANES, (p + 1) * LANES)
        q_p = q[:, sl]
        outs = []
        for hh in range(2):
            q_m = jnp.where(head_of_lane == hh, q_p, 0.0).astype(BF16)

            def update(carry, k_blk, v_blk, later, mask):
                run, acc = carry
                z = _dot_nt(q_m, k_blk)
                l1m = -_softplus(z)
                if mask is not None:
                    l1m = jnp.where(mask, l1m, 0.0)
                hi, lo = _split2(l1m)
                after = _dot(hi, later) + _dot(lo, later)
                w = jnp.exp(z + l1m + after + run)
                if mask is not None:
                    w = jnp.where(mask, w, 0.0)
                acc_new = acc + _dot(w.astype(BF16), v_blk)
                run_new = run + jnp.sum(l1m, axis=-1, keepdims=True)
                return run_new, acc_new

            k_blk = kb_scr[pl.ds(diag0, tq), sl]
            v_blk = vb_scr[pl.ds(diag0, tq), sl]
            init = (jnp.zeros((tq, 1), F32), jnp.zeros((tq, LANES), F32))
            carry = update(init, k_blk, v_blk, later_q, strict)

            def body(jj, carry):
                j = n_full - 1 - jj
                start = pl.multiple_of(j * tk, tk)
                return update(carry, kb_scr[pl.ds(start, tk), sl], vb_scr[pl.ds(start, tk), sl],
                              later_k, None)

            _, acc_f = lax.fori_loop(0, n_full, body, carry)
            outs.append(acc_f)
        o_ref[0, :, sl] = jnp.where(head_of_lane == 0, outs[0], outs[1])


def _sb(q, k_all, v_all, *, tq, tk, past):
    b, t, _ = q.shape
    tkk = k_all.shape[1]
    return pl.pallas_call(
        functools.partial(_sb_kernel, tq=tq, tk=tk, past=past, single=(t == tq)),
        grid=(b, t // tq),
        in_specs=[pl.BlockSpec((1, tq, W_C), lambda i, j: (i, j, 0)),
                  pl.BlockSpec((1, tkk, W_C), lambda i, j: (i, 0, 0)),
                  pl.BlockSpec((1, tkk, W_C), lambda i, j: (i, 0, 0))],
        out_specs=pl.BlockSpec((1, tq, W_C), lambda i, j: (i, j, 0)),
        out_shape=jax.ShapeDtypeStruct((b, t, W_C), F32),
        scratch_shapes=[pltpu.VMEM((tkk, W_C), BF16), pltpu.VMEM((tkk, W_C), BF16)],
        compiler_params=_cparams(("parallel", "arbitrary")),
    )(q, k_all, v_all)


def _out_kernel(oa_ref, ob_ref, oc_ref, g_ref, x_ref, w_ref, lng_ref, lnb_ref, y_ref, *, alpha):
    g = g_ref[...]
    gate = g * _sigmoid(g)
    acc = _dot((oa_ref[...] * gate[:, 0:W_A]).astype(BF16), w_ref[0:W_A, :])
    acc = acc + _dot((ob_ref[...] * gate[:, W_A:W_A + W_B]).astype(BF16), w_ref[W_A:W_A + W_B, :])
    acc = acc + _dot((oc_ref[...] * gate[:, W_A + W_B:]).astype(BF16), w_ref[W_A + W_B:, :])
    z = alpha * x_ref[...] + acc
    mu = jnp.mean(z, axis=-1, keepdims=True)
    d = z - mu
    var = jnp.mean(d * d, axis=-1, keepdims=True)
    y_ref[...] = d * lax.rsqrt(var + LN_EPS) * lng_ref[...] + lnb_ref[...]


def _out(oa, ob, oc, g, x2d, w_out, ln_g, ln_b, alpha, tm):
    n, d = x2d.shape
    row = lambda w: pl.BlockSpec((tm, w), lambda i: (i, 0))
    return pl.pallas_call(
        functools.partial(_out_kernel, alpha=alpha),
        grid=(n // tm,),
        in_specs=[row(W_A), row(W_B), row(W_C), row(W_MIX), row(d),
                  pl.BlockSpec((W_MIX, d), lambda i: (0, 0)),
                  pl.BlockSpec((1, d), lambda i: (0, 0)),
                  pl.BlockSpec((1, d), lambda i: (0, 0))],
        out_specs=row(d),
        out_shape=jax.ShapeDtypeStruct((n, d), F32),
        compiler_params=_cparams(("parallel",)),
    )(oa, ob, oc, g, x2d, w_out, ln_g, ln_b)


def _pad_lanes(a, width=LANES):
    return jnp.pad(a, [(0, 0)] * (a.ndim - 1) + [(0, width - a.shape[-1])])


def _state_to_pairs(s):
    b = s.shape[0]
    s = s.reshape(b, H_A // 2, 2, HEAD_DIM, HEAD_DIM)
    eye2 = jnp.eye(2, dtype=s.dtype)
    bd = jnp.einsum("bphvk,hg->bphvgk", s, eye2)
    return bd.reshape(b, H_A // 2, LANES, LANES)


def _pairs_to_state(sp):
    b = sp.shape[0]
    s = sp.reshape(b, H_A // 2, 2, HEAD_DIM, 2, HEAD_DIM)
    s = jnp.stack([s[:, :, 0, :, 0, :], s[:, :, 1, :, 1, :]], axis=2)
    return s.reshape(b, H_A, HEAD_DIM, HEAD_DIM)


def _layer_params(l, w_in, mu_shift, w0_decay, w_decay, a0, w_aaa, k_k, k_a, r_k,
                  lnx_g, lnx_b, fox_fb, w_out, ln_g, ln_b):
    w = w_in[l]
    sizes = (SHIFT_W, W_A, W_B, W_B, W_B, H_B, W_B, W_C, W_C, W_C, W_C)
    offs = np.concatenate([[0], np.cumsum(sizes)])
    (w_shift, w_ga, w_qb, w_kb, w_vb, w_fb, w_gb, w_qc, w_kc, w_vc, w_gc) = (
        w[:, int(offs[i]):int(offs[i + 1])] for i in range(len(sizes)))
    w_cat = jnp.concatenate([w_shift, w_ga, w_gb, w_gc, w_qb, w_kb, w_vb, w_qc, w_kc, w_vc,
                             _pad_lanes(w_fb)], axis=1).astype(BF16)
    zeros = jnp.zeros((D_LORA, W_A), F32)
    row = lambda a: a.reshape(1, -1).astype(F32)
    return dict(
        w_cat=w_cat, fb=_pad_lanes(row(fox_fb[l])),
        mu=row(mu_shift[l]), w0=row(w0_decay[l]),
        wd=jnp.concatenate([w_decay[l], zeros], axis=0).astype(BF16),
        a0=row(a0[l]), wa=jnp.concatenate([zeros, w_aaa[l]], axis=0).astype(BF16),
        kkp=row(k_k[l]), ka=row(k_a[l]), rk=row(r_k[l]), lnx_g=row(lnx_g[l]), lnx_b=row(lnx_b[l]),
        w_out=w_out[l].astype(BF16), ln_g=row(ln_g[l]), ln_b=row(ln_b[l]))


def _run_layer(x, hist, prm, alpha):
    b, t, d = x.shape
    n = b * t
    x2d = x.reshape(n, d)
    tm = 256 if n % 256 == 0 else n
    us, g, qb, kb, vb, qc, kc, vc, lf = _proj(x2d, prm["w_cat"], prm["fb"], tm)
    r3 = lambda a: a.reshape(b, t, a.shape[-1])
    us, qb, kb, vb, qc, kc, vc, lf = (r3(a) for a in (us, qb, kb, vb, qc, kc, vc, lf))

    if hist is None:
        past = 0
        prev0 = jnp.zeros((b, 1, SHIFT_W), F32)
        s0 = jnp.zeros((b, H_A // 2, LANES, LANES), F32)
        kb_all, vb_all, lf_all, kc_all, vc_all = kb, vb, lf, kc, vc
        tq = min(256, t)
        tk = tq
    else:
        h_fk, h_fv, h_lf, h_sk, h_sv, h_wkv, h_shift = hist
        past = h_fk.shape[1]
        prev0 = h_shift
        s0 = _state_to_pairs(h_wkv)
        kb_all = jnp.concatenate([h_fk.reshape(b, past, W_B), kb], axis=1)
        vb_all = jnp.concatenate([h_fv.reshape(b, past, W_B), vb], axis=1)
        lf_all = jnp.concatenate([_pad_lanes(h_lf), lf], axis=1)
        kc_all = jnp.concatenate([h_sk.reshape(b, past, W_C), kc], axis=1)
        vc_all = jnp.concatenate([h_sv.reshape(b, past, W_C), vc], axis=1)
        tq = t
        tk = min(256, past)

    oa, s_fin = _wkv(us, prev0, s0, prm, min(WKV_BLOCK, t))
    cum = _cumsum(lf_all)
    cum_row = jnp.swapaxes(cum[:, :, :8], 1, 2)
    ob = _fox(qb, kb_all, vb_all, cum[:, past:], cum_row, tq=tq, tk=tk, past=past)
    oc = _sb(qc, kc_all, vc_all, tq=tq, tk=tk, past=past)
    y = _out(oa.reshape(n, W_A), ob.reshape(n, W_B), oc.reshape(n, W_C), g, x2d,
             prm["w_out"], prm["ln_g"], prm["ln_b"], alpha, tm)
    new = (kb.reshape(b, t, H_B, HEAD_DIM), vb.reshape(b, t, H_B, HEAD_DIM), lf[:, :, :H_B],
           kc.reshape(b, t, H_C, HEAD_DIM), vc.reshape(b, t, H_C, HEAD_DIM),
           _pairs_to_state(s_fin), us[:, -1:, :])
    return y.reshape(b, t, d), new


def kernel(x_prompt, x_sample, cache_fox_k, cache_fox_v, cache_fox_logf, cache_sb_k, cache_sb_v, state_wkv, state_shift, w_in, mu_shift, w0_decay, w_decay, a0, w_aaa, k_k, k_a, r_k, lnx_g, lnx_b, fox_fb, w_out, ln_g, ln_b):
    depth = w_in.shape[0]
    alpha = (2 * depth) ** 0.25
    yp, ys = x_prompt, x_sample
    new_p, new_s = [], []
    for l in range(depth):
        prm = _layer_params(l, w_in, mu_shift, w0_decay, w_decay, a0, w_aaa, k_k, k_a, r_k,
                            lnx_g, lnx_b, fox_fb, w_out, ln_g, ln_b)
        yp, st_p = _run_layer(yp, None, prm, alpha)
        hist = (cache_fox_k[l], cache_fox_v[l], cache_fox_logf[l], cache_sb_k[l],
                cache_sb_v[l], state_wkv[l], state_shift[l])
        ys, st_s = _run_layer(ys, hist, prm, alpha)
        new_p.append(st_p)
        new_s.append(st_s)
    stack = lambda sts, i: jnp.stack([st[i] for st in sts])
    return ((yp, ys) + tuple(stack(new_p, i) for i in range(7))
            + tuple(stack(new_s, i) for i in range(7)))
```

```python
import functools

import numpy as np
import jax
import jax.numpy as jnp
from jax import lax
from jax.experimental import pallas as pl
from jax.experimental.pallas import tpu as pltpu

F32 = jnp.float32
BF16 = jnp.bfloat16

HEAD_DIM = 64
H_A, H_B, H_C = 6, 6, 4
W_A, W_B, W_C = H_A * HEAD_DIM, H_B * HEAD_DIM, H_C * HEAD_DIM
W_MIX = W_A + W_B + W_C
D_LORA = 64
SHIFT_W = 3 * W_A + 2 * D_LORA
LANES = 128
WKV_CHUNK = 64
WKV_BLOCK = 128
GN_EPS = 64e-5
LN_EPS = 1e-5
NEG_INF = -1e30
VMEM_LIMIT = 56 * 1024 * 1024


def _cparams(sem):
    return pltpu.CompilerParams(dimension_semantics=sem, vmem_limit_bytes=VMEM_LIMIT)


def _dot(a, b):
    return jnp.dot(a, b, preferred_element_type=F32)


def _dot_nt(a, b):
    return lax.dot_general(a, b, (((1,), (1,)), ((), ())), preferred_element_type=F32)


def _dot_tn(a, b):
    return lax.dot_general(a, b, (((0,), (0,)), ((), ())), preferred_element_type=F32)


def _split2(x):
    hi = x.astype(BF16)
    lo = (x - hi.astype(F32)).astype(BF16)
    return hi, lo


def _split3(x):
    hi = x.astype(BF16)
    r1 = x - hi.astype(F32)
    mid = r1.astype(BF16)
    lo = (r1 - mid.astype(F32)).astype(BF16)
    return hi, mid, lo


def _softplus(x):
    return jnp.maximum(x, 0.0) + jnp.log1p(jnp.exp(-jnp.abs(x)))


def _sigmoid(x):
    return 1.0 / (1.0 + jnp.exp(-x))


_OFF_SHIFT = 0
_OFF_G = _OFF_SHIFT + SHIFT_W
_OFF_QB = _OFF_G + W_MIX
_OFF_KB = _OFF_QB + W_B
_OFF_VB = _OFF_KB + W_B
_OFF_QC = _OFF_VB + W_B
_OFF_KC = _OFF_QC + W_C
_OFF_VC = _OFF_KC + W_C
_OFF_F = _OFF_VC + W_C
_PROJ_COLS = _OFF_F + LANES


def _proj_kernel(x_ref, w_ref, fb_ref, us_ref, g_ref, qb_ref, kb_ref, vb_ref,
                 qc_ref, kc_ref, vc_ref, lf_ref):
    xb = x_ref[...].astype(BF16)
    for ref, off in ((us_ref, _OFF_SHIFT), (g_ref, _OFF_G), (qb_ref, _OFF_QB),
                     (kb_ref, _OFF_KB), (vb_ref, _OFF_VB), (qc_ref, _OFF_QC),
                     (kc_ref, _OFF_KC), (vc_ref, _OFF_VC)):
        width = ref.shape[-1]
        ref[...] = _dot(xb, w_ref[:, off:off + width])
    f = _dot(xb, w_ref[:, _OFF_F:_OFF_F + LANES]) + fb_ref[...]
    lf_ref[...] = -_softplus(-f)


def _proj(x2d, w_cat, fb_pad, tm):
    n, d = x2d.shape
    widths = (SHIFT_W, W_MIX, W_B, W_B, W_B, W_C, W_C, W_C, LANES)
    return pl.pallas_call(
        _proj_kernel,
        grid=(n // tm,),
        in_specs=[pl.BlockSpec((tm, d), lambda i: (i, 0)),
                  pl.BlockSpec((d, _PROJ_COLS), lambda i: (0, 0)),
                  pl.BlockSpec((1, LANES), lambda i: (0, 0))],
        out_specs=[pl.BlockSpec((tm, w), lambda i: (i, 0)) for w in widths],
        out_shape=[jax.ShapeDtypeStruct((n, w), F32) for w in widths],
        compiler_params=_cparams(("parallel",)),
    )(x2d, w_cat, fb_pad)


def _cumsum_kernel(lf_ref, cum_ref, *, blk):
    tk = lf_ref.shape[1]
    row = lax.broadcasted_iota(jnp.int32, (blk, blk), 0)
    col = lax.broadcasted_iota(jnp.int32, (blk, blk), 1)
    tri = (col <= row).astype(BF16)
    carry = jnp.zeros((1, LANES), F32)
    for s in range(0, tk, blk):
        x = lf_ref[0, s:s + blk, :]
        hi, mid, lo = _split3(x)
        c = _dot(tri, hi) + _dot(tri, mid) + _dot(tri, lo) + carry
        cum_ref[0, s:s + blk, :] = c
        carry = c[blk - 1:blk, :]


def _cumsum(lf, blk=64):
    b, tk, _ = lf.shape
    return pl.pallas_call(
        functools.partial(_cumsum_kernel, blk=blk),
        grid=(b,),
        in_specs=[pl.BlockSpec((1, tk, LANES), lambda i: (i, 0, 0))],
        out_specs=pl.BlockSpec((1, tk, LANES), lambda i: (i, 0, 0)),
        out_shape=jax.ShapeDtypeStruct((b, tk, LANES), F32),
        compiler_params=_cparams(("parallel",)),
    )(lf)


def _seg_sum(x, bd):
    hi, lo = _split2(x)
    return _dot(hi, bd) + _dot(lo, bd)


def _tri_inverse(n1s, eye):
    def sq(xs):
        xb = [x.astype(BF16) for x in xs]
        return [_dot(x, x) for x in xb]

    def pair(xs, ys):
        return [eye + x + y + _dot(x.astype(BF16), y.astype(BF16)) for x, y in zip(xs, ys)]

    def mul(xs, ys):
        return [_dot(x.astype(BF16), y.astype(BF16)) for x, y in zip(xs, ys)]

    n2s = sq(n1s)
    n4s = sq(n2s)
    p1s = pair(n1s, n2s)
    n8s = sq(n4s)
    n16s = sq(n8s)
    p2s = pair(n4s, n8s)
    n32s = sq(n16s)
    p12s = mul(p1s, p2s)
    p3s = pair(n16s, n32s)
    return mul(p12s, p3s)


def _wkv_kernel(us_ref, prev0_ref, s0_ref, mu_ref, w0_ref, wd_ref, a0_ref, wa_ref,
                kkp_ref, ka_ref, rk_ref, lng_ref, lnb_ref,
                oa_ref, sout_ref,
                s_scr, prev_scr, at_s, rt_s, bh_s, kh_s, be_s, ke_s, v_s, ga_s, y_s):
    c_idx = pl.program_id(1)
    nc = pl.num_programs(1)
    C = WKV_CHUNK
    tt = us_ref.shape[1]
    nb = tt // C

    @pl.when(c_idx == 0)
    def _():
        s_scr[...] = s0_ref[0]
        prev_scr[...] = prev0_ref[0]

    u = us_ref[0]
    row1 = lax.broadcasted_iota(jnp.int32, (tt, 1), 0)
    zprev = jnp.where(row1 == 0, prev_scr[...], pltpu.roll(u, 1, axis=0))
    prev_scr[...] = u[tt - 1:tt, :]
    zs = u + (zprev - u) * mu_ref[...]
    r = zs[:, 0:W_A]
    k = zs[:, W_A:2 * W_A]
    v = zs[:, 2 * W_A:3 * W_A]
    lora_in = zs[:, 3 * W_A:3 * W_A + LANES]
    lane1 = lax.broadcasted_iota(jnp.int32, (1, LANES), 1)
    lora_t = jnp.where(lane1 < D_LORA, jnp.tanh(lora_in), lora_in).astype(BF16)
    wl = w0_ref[...] + _dot(lora_t, wd_ref[...])
    w_log = -_softplus(-wl) - 0.5
    lw = -jnp.exp(w_log)
    a = _sigmoid(a0_ref[...] + _dot(lora_t, wa_ref[...]))

    lane_w = lax.broadcasted_iota(jnp.int32, (W_A, W_A), 0) // HEAD_DIM
    lane_c = lax.broadcasted_iota(jnp.int32, (W_A, W_A), 1) // HEAD_DIM
    bd = (lane_w == lane_c).astype(BF16)

    kk = k * kkp_ref[...]
    kk = kk * lax.rsqrt(_seg_sum(kk * kk, bd) + 1e-12)
    k2 = k * (1.0 + (a - 1.0) * ka_ref[...])

    rowt = lax.broadcasted_iota(jnp.int32, (tt, tt), 0)
    colt = lax.broadcasted_iota(jnp.int32, (tt, tt), 1)
    same_chunk = (rowt // C) == (colt // C)
    tri_b = (same_chunk & (colt <= rowt)).astype(BF16)
    ones_b = same_chunk.astype(BF16)
    h1, h2, h3 = _split3(lw)
    cs = _dot(tri_b, h1) + _dot(tri_b, h2) + _dot(tri_b, h3)
    tot = _dot(ones_b, h1) + _dot(ones_b, h2) + _dot(ones_b, h3)
    g_inv = jnp.exp(-cs)
    g_end = jnp.exp(tot - cs)
    kka = kk * a
    at_s[...] = kk * jnp.exp(cs - lw)
    rt_s[...] = r * jnp.exp(cs)
    bh_s[...] = (-kka * g_inv).astype(BF16)
    kh_s[...] = (k2 * g_inv).astype(BF16)
    be_s[...] = (-kka * g_end).astype(BF16)
    ke_s[...] = (k2 * g_end).astype(BF16)
    v_s[...] = v
    ga_s[...] = jnp.exp(tot)

    rowc = lax.broadcasted_iota(jnp.int32, (C, C), 0)
    colc = lax.broadcasted_iota(jnp.int32, (C, C), 1)
    tri_incl = colc <= rowc
    tri_strict = colc < rowc
    eye = (colc == rowc).astype(F32)
    rowp = lax.broadcasted_iota(jnp.int32, (LANES, LANES), 0) // HEAD_DIM
    colp = lax.broadcasted_iota(jnp.int32, (LANES, LANES), 1) // HEAD_DIM
    bd_pair = rowp == colp
    head_of_lane = lane1 // HEAD_DIM
    npair = H_A // 2

    units = [(c, p) for c in range(nb) for p in range(npair)]
    probs = [(ui, hh) for ui in range(len(units)) for hh in range(2)]
    ld = []
    for c, p in units:
        rows = slice(c * C, (c + 1) * C)
        sl = slice(p * LANES, (p + 1) * LANES)
        ld.append(dict(at=at_s[rows, sl], rt=rt_s[rows, sl], v=v_s[rows, sl], bh=bh_s[rows, sl],
                       kh=kh_s[rows, sl], be=be_s[rows, sl], ke=ke_s[rows, sl]))
    masks = [head_of_lane == hh for hh in range(2)]
    l_ms = [jnp.where(masks[hh], jnp.concatenate([ld[ui]["at"], ld[ui]["rt"]], axis=0), 0.0).astype(BF16)
            for ui, hh in probs]
    abs_ = [_dot_nt(l_m, ld[ui]["bh"]) for l_m, (ui, hh) in zip(l_ms, probs)]
    aks = [_dot_nt(l_m, ld[ui]["kh"]) for l_m, (ui, hh) in zip(l_ms, probs)]
    a_abs = [jnp.where(tri_strict, ab[:C], 0.0) for ab in abs_]
    a_rbs = [jnp.where(tri_incl, ab[C:], 0.0).astype(BF16) for ab in abs_]
    a_aks = [jnp.where(tri_strict, ak[:C], 0.0).astype(BF16) for ak in aks]
    a_rks = [jnp.where(tri_incl, ak[C:], 0.0).astype(BF16) for ak in aks]
    v_ms = [jnp.where(masks[hh], ld[ui]["v"], 0.0).astype(BF16) for ui, hh in probs]
    avs = [_dot(a_ak, v_m) for a_ak, v_m in zip(a_aks, v_ms)]
    yvs = [_dot(a_rk, v_m) for a_rk, v_m in zip(a_rks, v_ms)]
    t_invs = _tri_inverse(a_abs, eye)
    zs_ = [jnp.concatenate([jnp.where(masks[hh], ld[ui]["at"], 0.0), av], axis=1).astype(BF16)
           for av, (ui, hh) in zip(avs, probs)]
    xs = [_dot(t.astype(BF16), z) for t, z in zip(t_invs, zs_)]
    rys = [_dot(a_rb, x.astype(BF16)) for a_rb, x in zip(a_rbs, xs)]
    atps, wvs, rps, y0s = [], [], [], []
    for ui in range(len(units)):
        x0, x1 = xs[2 * ui], xs[2 * ui + 1]
        ry0, ry1 = rys[2 * ui], rys[2 * ui + 1]
        atps.append((x0[:, :LANES] + x1[:, :LANES]).astype(BF16))
        wvs.append(jnp.concatenate([x0[:, LANES:] + x1[:, LANES:], ld[ui]["v"]], axis=0).astype(BF16))
        rps.append((ld[ui]["rt"] + ry0[:, :LANES] + ry1[:, :LANES]).astype(BF16))
        y0s.append(ry0[:, LANES:] + ry1[:, LANES:] + yvs[2 * ui] + yvs[2 * ui + 1])
    pps = [jnp.where(bd_pair, _dot_tn(atp, ld[ui]["be"]), 0.0).astype(BF16)
           for ui, atp in enumerate(atps)]
    qs = [jnp.where(bd_pair, _dot_tn(wv, jnp.concatenate([ld[ui]["be"], ld[ui]["ke"]], axis=0)), 0.0)
          for ui, wv in enumerate(wvs)]

    for c in range(nb):
        rows = slice(c * C, (c + 1) * C)
        s_ps = [s_scr[p] for p in range(npair)]
        s_bs = [s_p.astype(BF16) for s_p in s_ps]
        for p in range(npair):
            ui = c * npair + p
            sl = slice(p * LANES, (p + 1) * LANES)
            y_s[rows, sl] = _dot_nt(rps[ui], s_bs[p]) + y0s[ui]
            s_scr[p] = s_ps[p] * ga_s[c * C:c * C + 1, sl] + _dot(s_bs[p], pps[ui]) + qs[ui]

    y = y_s[...]
    inv_n = 1.0 / HEAD_DIM
    mean = _seg_sum(y, bd) * inv_n
    d = y - mean
    var = _seg_sum(d * d, bd) * inv_n
    yn = d * lax.rsqrt(var + GN_EPS) * lng_ref[...] + lnb_ref[...]
    bonus = _seg_sum(r * k2 * rk_ref[...], bd) * v
    oa_ref[0] = yn + bonus

    @pl.when(c_idx == nc - 1)
    def _():
        sout_ref[0] = s_scr[...]


def _wkv(us, prev0, s0_bd, prm, tt):
    b, t, _ = us.shape
    npair = H_A // 2
    vec = lambda w: pl.BlockSpec((1, w), lambda i, j: (0, 0))
    mat = lambda: pl.BlockSpec((LANES, W_A), lambda i, j: (0, 0))
    blk = lambda dt: pltpu.VMEM((tt, W_A), dt)
    return pl.pallas_call(
        _wkv_kernel,
        grid=(b, t // tt),
        in_specs=[pl.BlockSpec((1, tt, SHIFT_W), lambda i, j: (i, j, 0)),
                  pl.BlockSpec((1, 1, SHIFT_W), lambda i, j: (i, 0, 0)),
                  pl.BlockSpec((1, npair, LANES, LANES), lambda i, j: (i, 0, 0, 0)),
                  vec(SHIFT_W), vec(W_A), mat(), vec(W_A), mat(),
                  vec(W_A), vec(W_A), vec(W_A), vec(W_A), vec(W_A)],
        out_specs=[pl.BlockSpec((1, tt, W_A), lambda i, j: (i, j, 0)),
                   pl.BlockSpec((1, npair, LANES, LANES), lambda i, j: (i, 0, 0, 0))],
        out_shape=[jax.ShapeDtypeStruct((b, t, W_A), F32),
                   jax.ShapeDtypeStruct((b, npair, LANES, LANES), F32)],
        scratch_shapes=[pltpu.VMEM((npair, LANES, LANES), F32),
                        pltpu.VMEM((1, SHIFT_W), F32),
                        blk(F32), blk(F32), blk(BF16), blk(BF16), blk(BF16), blk(BF16),
                        blk(F32), blk(F32), blk(F32)],
        compiler_params=_cparams(("parallel", "arbitrary")),
    )(us, prev0, s0_bd, prm["mu"], prm["w0"], prm["wd"], prm["a0"], prm["wa"],
      prm["kkp"], prm["ka"], prm["rk"], prm["lnx_g"], prm["lnx_b"])


_AUG = 8
_PREP_ROWS = 256


def _bias_lanes(cum, pair, head, key_side):
    h = 2 * pair + head
    r = lax.broadcasted_iota(jnp.int32, (LANES, LANES), 0)
    c = lax.broadcasted_iota(jnp.int32, (LANES, LANES), 1)
    lane = lax.broadcasted_iota(jnp.int32, (1, LANES), 1)
    base = _AUG * head
    f_off, one_off = (0, 3) if key_side else (3, 0)
    sign = -1.0 if key_side else 1.0
    out = jnp.where((lane >= base + one_off) & (lane < base + one_off + 3), 1.0, 0.0)
    for t, part in enumerate(_split3(cum)):
        sel = jnp.where((r == h) & (c == base + f_off + t), sign, 0.0).astype(BF16)
        out = out + _dot(part, sel)
    return out


def _stage_rows(total):
    return [(s, min(_PREP_ROWS, total - s)) for s in range(0, total, _PREP_ROWS)]


def _transpose_rows(x):
    n = x.shape[0]
    n_pad = -n % LANES
    if n_pad:
        x = jnp.concatenate([x, jnp.zeros((n_pad, x.shape[1]), x.dtype)], axis=0)
    return x.T[:, :n]


def _fox_kernel(q_ref, k_ref, v_ref, cq_ref, ck_ref, o_ref, ka_scr, vt_scr, acc_scr,
                *, tq, tk, past, single):
    npair = H_B // 2
    tkk = k_ref.shape[1]
    lane1 = lax.broadcasted_iota(jnp.int32, (1, LANES), 1)
    head_of_lane = lane1 // HEAD_DIM

    def stage_kv():
        for s, n in _stage_rows(tkk):
            kb = k_ref[0, s:s + n, :].astype(BF16)
            cum = ck_ref[0, s:s + n, :]
            for p in range(npair):
                aug = _bias_lanes(cum, p, 0, True) + _bias_lanes(cum, p, 1, True)
                ka_scr[p, s:s + n, 0:LANES] = kb[:, p * LANES:(p + 1) * LANES]
                ka_scr[p, s:s + n, LANES:2 * LANES] = aug.astype(BF16)
            vt_scr[:, s:s + n] = _transpose_rows(v_ref[0, s:s + n, :]).astype(BF16)

    if single:
        i = 0
        stage_kv()
    else:
        i = pl.program_id(1)
        pl.when(i == 0)(stage_kv)

    q = q_ref[0] * (HEAD_DIM ** -0.5)
    cq = cq_ref[0]
    rhs = []
    for p in range(npair):
        q_p = q[:, p * LANES:(p + 1) * LANES]
        halves = [jnp.concatenate([jnp.where(head_of_lane == hh, q_p, 0.0),
                                   _bias_lanes(cq, p, hh, False)], axis=1) for hh in range(2)]
        rhs.append(jnp.concatenate(halves, axis=0).astype(BF16))

    n_full = past // tk + i * (tq // tk)
    diag0 = past if single else pl.multiple_of(past + i * tq, tq)
    krow = lax.broadcasted_iota(jnp.int32, (tq, 2 * tq), 0)
    qcol = lax.broadcasted_iota(jnp.int32, (tq, 2 * tq), 1)
    causal_t = krow <= jnp.where(qcol >= tq, qcol - tq, qcol)

    acc_scr[...] = jnp.zeros(acc_scr.shape, F32)

    def update(carry, start, size, mask):
        ms, ls = carry
        st = [_dot_nt(ka_scr[p, pl.ds(start, size), :], rhs[p]) for p in range(npair)]
        if mask is not None:
            st = [jnp.where(mask, s, NEG_INF) for s in st]
        m_new = [jnp.maximum(m, jnp.max(s, axis=0, keepdims=True)) for m, s in zip(ms, st)]
        alpha = [jnp.exp(m - mn) for m, mn in zip(ms, m_new)]
        pt = [jnp.exp(s - mn) for s, mn in zip(st, m_new)]
        l_new = [a * l + jnp.sum(x, axis=0, keepdims=True) for a, l, x in zip(alpha, ls, pt)]
        pv = [_dot(vt_scr[p * LANES:(p + 1) * LANES, pl.ds(start, size)], pt[p].astype(BF16))
              for p in range(npair)]
        for p in range(npair):
            acc_scr[p] = acc_scr[p] * alpha[p] + pv[p]
        return tuple(m_new), tuple(l_new)

    def body(j, carry):
        return update(carry, pl.multiple_of(j * tk, tk), tk, None)

    init = (tuple(jnp.full((1, 2 * tq), NEG_INF, F32) for _ in range(npair)),
            tuple(jnp.zeros((1, 2 * tq), F32) for _ in range(npair)))
    carry = lax.fori_loop(0, n_full, body, init)
    _, ls = update(carry, diag0, tq, causal_t)
    for p in range(npair):
        o_t = (acc_scr[p] / ls[p]).T
        o_ref[0, :, p * LANES:(p + 1) * LANES] = jnp.where(head_of_lane == 0, o_t[:tq], o_t[tq:])


def _fox(q, k_all, v_all, cum_q, cum_k, *, tq, tk, past):
    b, t, _ = q.shape
    tkk = k_all.shape[1]
    npair = H_B // 2
    return pl.pallas_call(
        functools.partial(_fox_kernel, tq=tq, tk=tk, past=past, single=(t == tq)),
        grid=(b, t // tq),
        in_specs=[pl.BlockSpec((1, tq, W_B), lambda i, j: (i, j, 0)),
                  pl.BlockSpec((1, tkk, W_B), lambda i, j: (i, 0, 0)),
                  pl.BlockSpec((1, tkk, W_B), lambda i, j: (i, 0, 0)),
                  pl.BlockSpec((1, tq, LANES), lambda i, j: (i, j, 0)),
                  pl.BlockSpec((1, tkk, LANES), lambda i, j: (i, 0, 0))],
        out_specs=pl.BlockSpec((1, tq, W_B), lambda i, j: (i, j, 0)),
        out_shape=jax.ShapeDtypeStruct((b, t, W_B), F32),
        scratch_shapes=[pltpu.VMEM((npair, tkk, 2 * LANES), BF16), pltpu.VMEM((W_B, tkk), BF16),
                        pltpu.VMEM((npair, LANES, 2 * tq), F32)],
        compiler_params=_cparams(("parallel", "arbitrary")),
    )(q, k_all, v_all, cum_q, cum_k)


def _sb_kernel(q_ref, k_ref, v_ref, o_ref, kb_scr, vt_scr, acc_scr, *, tq, tk, past, single):
    npair = H_C // 2
    tkk = k_ref.shape[1]
    lane1 = lax.broadcasted_iota(jnp.int32, (1, LANES), 1)
    head_of_lane = lane1 // HEAD_DIM

    def stage_kv():
        for s, n in _stage_rows(tkk):
            kb_scr[s:s + n, :] = k_ref[0, s:s + n, :].astype(BF16)
            vt_scr[:, s:s + n] = _transpose_rows(v_ref[0, s:s + n, :]).astype(BF16)

    if single:
        i = 0
        stage_kv()
    else:
        i = pl.program_id(1)
        pl.when(i == 0)(stage_kv)

    q = q_ref[0] * (HEAD_DIM ** -0.5)
    rhs = []
    for p in range(npair):
        q_p = q[:, p * LANES:(p + 1) * LANES]
        rhs.append(jnp.concatenate([jnp.where(head_of_lane == hh, q_p, 0.0) for hh in range(2)],
                                   axis=0).astype(BF16))

    def earlier_matrix(n):
        rr = lax.broadcasted_iota(jnp.int32, (n, n), 0)
        cc = lax.broadcasted_iota(jnp.int32, (n, n), 1)
        return (cc > rr).astype(BF16)

    n_full = past // tk + i * (tq // tk)
    diag0 = past if single else pl.multiple_of(past + i * tq, tq)
    krow = lax.broadcasted_iota(jnp.int32, (tq, 2 * tq), 0)
    qcol = lax.broadcasted_iota(jnp.int32, (tq, 2 * tq), 1)
    strict_t = krow < jnp.where(qcol >= tq, qcol - tq, qcol)

    def update(runs, start, size, mask, first):
        zt = [_dot_nt(kb_scr[pl.ds(start, size), p * LANES:(p + 1) * LANES], rhs[p])
              for p in range(npair)]
        l1m = [-_softplus(z) for z in zt]
        if mask is not None:
            l1m = [jnp.where(mask, x, 0.0) for x in l1m]
        splits = [_split2(x) for x in l1m]
        later = earlier_matrix(size)
        after = [_dot(later, hi) + _dot(later, lo) for hi, lo in splits]
        wt = [jnp.exp(z + x + a + r) for z, x, a, r in zip(zt, l1m, after, runs)]
        if mask is not None:
            wt = [jnp.where(mask, w, 0.0) for w in wt]
        pv = [_dot(vt_scr[p * LANES:(p + 1) * LANES, pl.ds(start, size)], wt[p].astype(BF16))
              for p in range(npair)]
        for p in range(npair):
            acc_scr[p] = pv[p] if first else acc_scr[p] + pv[p]
        return tuple(r + jnp.sum(x, axis=0, keepdims=True) for r, x in zip(runs, l1m))

    runs = tuple(jnp.zeros((1, 2 * tq), F32) for _ in range(npair))
    runs = update(runs, diag0, tq, strict_t, True)

    def body(jj, runs):
        j = n_full - 1 - jj
        return update(runs, pl.multiple_of(j * tk, tk), tk, None, False)

    lax.fori_loop(0, n_full, body, runs)
    for p in range(npair):
        o_t = acc_scr[p].T
        o_ref[0, :, p * LANES:(p + 1) * LANES] = jnp.where(head_of_lane == 0, o_t[:tq], o_t[tq:])


def _sb(q, k_all, v_all, *, tq, tk, past):
    b, t, _ = q.shape
    tkk = k_all.shape[1]
    npair = H_C // 2
    return pl.pallas_call(
        functools.partial(_sb_kernel, tq=tq, tk=tk, past=past, single=(t == tq)),
        grid=(b, t // tq),
        in_specs=[pl.BlockSpec((1, tq, W_C), lambda i, j: (i, j, 0)),
                  pl.BlockSpec((1, tkk, W_C), lambda i, j: (i, 0, 0)),
                  pl.BlockSpec((1, tkk, W_C), lambda i, j: (i, 0, 0))],
        out_specs=pl.BlockSpec((1, tq, W_C), lambda i, j: (i, j, 0)),
        out_shape=jax.ShapeDtypeStruct((b, t, W_C), F32),
        scratch_shapes=[pltpu.VMEM((tkk, W_C), BF16), pltpu.VMEM((W_C, tkk), BF16),
                        pltpu.VMEM((npair, LANES, 2 * tq), F32)],
        compiler_params=_cparams(("parallel", "arbitrary")),
    )(q, k_all, v_all)


def _out_kernel(oa_ref, ob_ref, oc_ref, g_ref, x_ref, w_ref, lng_ref, lnb_ref, y_ref, *, alpha):
    g = g_ref[...]
    gate = g * _sigmoid(g)
    acc = _dot((oa_ref[...] * gate[:, 0:W_A]).astype(BF16), w_ref[0:W_A, :])
    acc = acc + _dot((ob_ref[...] * gate[:, W_A:W_A + W_B]).astype(BF16), w_ref[W_A:W_A + W_B, :])
    acc = acc + _dot((oc_ref[...] * gate[:, W_A + W_B:]).astype(BF16), w_ref[W_A + W_B:, :])
    z = alpha * x_ref[...] + acc
    mu = jnp.mean(z, axis=-1, keepdims=True)
    d = z - mu
    var = jnp.mean(d * d, axis=-1, keepdims=True)
    y_ref[...] = d * lax.rsqrt(var + LN_EPS) * lng_ref[...] + lnb_ref[...]


def _out(oa, ob, oc, g, x2d, w_out, ln_g, ln_b, alpha, tm):
    n, d = x2d.shape
    row = lambda w: pl.BlockSpec((tm, w), lambda i: (i, 0))
    return pl.pallas_call(
        functools.partial(_out_kernel, alpha=alpha),
        grid=(n // tm,),
        in_specs=[row(W_A), row(W_B), row(W_C), row(W_MIX), row(d),
                  pl.BlockSpec((W_MIX, d), lambda i: (0, 0)),
                  pl.BlockSpec((1, d), lambda i: (0, 0)),
                  pl.BlockSpec((1, d), lambda i: (0, 0))],
        out_specs=row(d),
        out_shape=jax.ShapeDtypeStruct((n, d), F32),
        compiler_params=_cparams(("parallel",)),
    )(oa, ob, oc, g, x2d, w_out, ln_g, ln_b)


def _pad_lanes(a, width=LANES):
    return jnp.pad(a, [(0, 0)] * (a.ndim - 1) + [(0, width - a.shape[-1])])


def _state_to_pairs(s):
    b = s.shape[0]
    s = s.reshape(b, H_A // 2, 2, HEAD_DIM, HEAD_DIM)
    eye2 = jnp.eye(2, dtype=s.dtype)
    bd = jnp.einsum("bphvk,hg->bphvgk", s, eye2)
    return bd.reshape(b, H_A // 2, LANES, LANES)


def _pairs_to_state(sp):
    b = sp.shape[0]
    s = sp.reshape(b, H_A // 2, 2, HEAD_DIM, 2, HEAD_DIM)
    s = jnp.stack([s[:, :, 0, :, 0, :], s[:, :, 1, :, 1, :]], axis=2)
    return s.reshape(b, H_A, HEAD_DIM, HEAD_DIM)


def _layer_params(l, w_in, mu_shift, w0_decay, w_decay, a0, w_aaa, k_k, k_a, r_k,
                  lnx_g, lnx_b, fox_fb, w_out, ln_g, ln_b):
    w = w_in[l]
    sizes = (SHIFT_W, W_A, W_B, W_B, W_B, H_B, W_B, W_C, W_C, W_C, W_C)
    offs = np.concatenate([[0], np.cumsum(sizes)])
    (w_shift, w_ga, w_qb, w_kb, w_vb, w_fb, w_gb, w_qc, w_kc, w_vc, w_gc) = (
        w[:, int(offs[i]):int(offs[i + 1])] for i in range(len(sizes)))
    w_cat = jnp.concatenate([w_shift, w_ga, w_gb, w_gc, w_qb, w_kb, w_vb, w_qc, w_kc, w_vc,
                             _pad_lanes(w_fb)], axis=1).astype(BF16)
    zeros = jnp.zeros((D_LORA, W_A), F32)
    row = lambda a: a.reshape(1, -1).astype(F32)
    return dict(
        w_cat=w_cat, fb=_pad_lanes(row(fox_fb[l])),
        mu=row(mu_shift[l]), w0=row(w0_decay[l]),
        wd=jnp.concatenate([w_decay[l], zeros], axis=0).astype(BF16),
        a0=row(a0[l]), wa=jnp.concatenate([zeros, w_aaa[l]], axis=0).astype(BF16),
        kkp=row(k_k[l]), ka=row(k_a[l]), rk=row(r_k[l]), lnx_g=row(lnx_g[l]), lnx_b=row(lnx_b[l]),
        w_out=w_out[l].astype(BF16), ln_g=row(ln_g[l]), ln_b=row(ln_b[l]))


def _run_layer(x, hist, prm, alpha):
    b, t, d = x.shape
    n = b * t
    x2d = x.reshape(n, d)
    tm = 256 if n % 256 == 0 else n
    us, g, qb, kb, vb, qc, kc, vc, lf = _proj(x2d, prm["w_cat"], prm["fb"], tm)
    r3 = lambda a: a.reshape(b, t, a.shape[-1])
    us, qb, kb, vb, qc, kc, vc, lf = (r3(a) for a in (us, qb, kb, vb, qc, kc, vc, lf))

    if hist is None:
        past = 0
        prev0 = jnp.zeros((b, 1, SHIFT_W), F32)
        s0 = jnp.zeros((b, H_A // 2, LANES, LANES), F32)
        kb_all, vb_all, lf_all, kc_all, vc_all = kb, vb, lf, kc, vc
        tq = min(256, t)
        tk = tq
    else:
        h_fk, h_fv, h_lf, h_sk, h_sv, h_wkv, h_shift = hist
        past = h_fk.shape[1]
        prev0 = h_shift
        s0 = _state_to_pairs(h_wkv)
        kb_all = jnp.concatenate([h_fk.reshape(b, past, W_B), kb], axis=1)
        vb_all = jnp.concatenate([h_fv.reshape(b, past, W_B), vb], axis=1)
        lf_all = jnp.concatenate([_pad_lanes(h_lf), lf], axis=1)
        kc_all = jnp.concatenate([h_sk.reshape(b, past, W_C), kc], axis=1)
        vc_all = jnp.concatenate([h_sv.reshape(b, past, W_C), vc], axis=1)
        tq = t
        tk = min(256, past)

    oa, s_fin = _wkv(us, prev0, s0, prm, min(WKV_BLOCK, t))
    cum = _cumsum(lf_all)
    ob = _fox(qb, kb_all, vb_all, cum[:, past:], cum, tq=tq, tk=tk, past=past)
    oc = _sb(qc, kc_all, vc_all, tq=tq, tk=tk, past=past)
    y = _out(oa.reshape(n, W_A), ob.reshape(n, W_B), oc.reshape(n, W_C), g, x2d,
             prm["w_out"], prm["ln_g"], prm["ln_b"], alpha, tm)
    new = (kb.reshape(b, t, H_B, HEAD_DIM), vb.reshape(b, t, H_B, HEAD_DIM), lf[:, :, :H_B],
           kc.reshape(b, t, H_C, HEAD_DIM), vc.reshape(b, t, H_C, HEAD_DIM),
           _pairs_to_state(s_fin), us[:, -1:, :])
    return y.reshape(b, t, d), new


def kernel(x_prompt, x_sample, cache_fox_k, cache_fox_v, cache_fox_logf, cache_sb_k, cache_sb_v, state_wkv, state_shift, w_in, mu_shift, w0_decay, w_decay, a0, w_aaa, k_k, k_a, r_k, lnx_g, lnx_b, fox_fb, w_out, ln_g, ln_b):
    depth = w_in.shape[0]
    alpha = (2 * depth) ** 0.25
    yp, ys = x_prompt, x_sample
    new_p, new_s = [], []
    for l in range(depth):
        prm = _layer_params(l, w_in, mu_shift, w0_decay, w_decay, a0, w_aaa, k_k, k_a, r_k,
                            lnx_g, lnx_b, fox_fb, w_out, ln_g, ln_b)
        yp, st_p = _run_layer(yp, None, prm, alpha)
        hist = (cache_fox_k[l], cache_fox_v[l], cache_fox_logf[l], cache_sb_k[l],
                cache_sb_v[l], state_wkv[l], state_shift[l])
        ys, st_s = _run_layer(ys, hist, prm, alpha)
        new_p.append(st_p)
        new_s.append(st_s)
    stack = lambda sts, i: jnp.stack([st[i] for st in sts])
    return ((yp, ys) + tuple(stack(new_p, i) for i in range(7))
            + tuple(stack(new_s, i) for i in range(7)))
```

```python
import functools

import numpy as np
import jax
import jax.numpy as jnp
from jax import lax
from jax.experimental import pallas as pl
from jax.experimental.pallas import tpu as pltpu

F32 = jnp.float32
BF16 = jnp.bfloat16

HEAD_DIM = 64
H_A, H_B, H_C = 6, 6, 4
W_A, W_B, W_C = H_A * HEAD_DIM, H_B * HEAD_DIM, H_C * HEAD_DIM
W_MIX = W_A + W_B + W_C
D_LORA = 64
SHIFT_W = 3 * W_A + 2 * D_LORA
LANES = 128
WKV_CHUNK = 64
WKV_BLOCK = 128
FOX_BLOCK = 512
SB_BLOCK = 256
GN_EPS = 64e-5
LN_EPS = 1e-5
NEG_INF = -1e30
VMEM_LIMIT = 56 * 1024 * 1024


def _cparams(sem):
    return pltpu.CompilerParams(dimension_semantics=sem, vmem_limit_bytes=VMEM_LIMIT)


def _dot(a, b):
    return jnp.dot(a, b, preferred_element_type=F32)


def _dot_nt(a, b):
    return lax.dot_general(a, b, (((1,), (1,)), ((), ())), preferred_element_type=F32)


def _dot_tn(a, b):
    return lax.dot_general(a, b, (((0,), (0,)), ((), ())), preferred_element_type=F32)


def _split2(x):
    hi = x.astype(BF16)
    lo = (x - hi.astype(F32)).astype(BF16)
    return hi, lo


def _split3(x):
    hi = x.astype(BF16)
    r1 = x - hi.astype(F32)
    mid = r1.astype(BF16)
    lo = (r1 - mid.astype(F32)).astype(BF16)
    return hi, mid, lo


def _softplus(x):
    return jnp.maximum(x, 0.0) + jnp.log1p(jnp.exp(-jnp.abs(x)))


def _sigmoid(x):
    return 1.0 / (1.0 + jnp.exp(-x))


_OFF_SHIFT = 0
_OFF_G = _OFF_SHIFT + SHIFT_W
_OFF_QB = _OFF_G + W_MIX
_OFF_KB = _OFF_QB + W_B
_OFF_VB = _OFF_KB + W_B
_OFF_QC = _OFF_VB + W_B
_OFF_KC = _OFF_QC + W_C
_OFF_VC = _OFF_KC + W_C
_OFF_F = _OFF_VC + W_C
_PROJ_COLS = _OFF_F + LANES


def _proj_kernel(x_ref, w_ref, fb_ref, *refs):
    us_ref, g_ref, qb_ref, kb_ref, vb_ref, qc_ref, kc_ref, vc_ref, lf_ref = refs[-9:]
    xb = x_ref[...].astype(BF16)
    for ref, off in ((us_ref, _OFF_SHIFT), (g_ref, _OFF_G), (qb_ref, _OFF_QB),
                     (kb_ref, _OFF_KB), (vb_ref, _OFF_VB), (qc_ref, _OFF_QC),
                     (kc_ref, _OFF_KC), (vc_ref, _OFF_VC)):
        width = ref.shape[-1]
        ref[...] = _dot(xb, w_ref[:, off:off + width])
    f = _dot(xb, w_ref[:, _OFF_F:_OFF_F + LANES]) + fb_ref[...]
    lf_ref[...] = -_softplus(-f)


_PROJ_WIDTHS = (SHIFT_W, W_MIX, W_B, W_B, W_B, W_C, W_C, W_C, LANES)
_PROJ_STACKED = (3, 4, 6, 7)


def _proj(x2d, w_cat, fb_pad, tm, layer, depth, stacks):
    n, d = x2d.shape
    out_specs, out_shape = [], []
    for idx, w in enumerate(_PROJ_WIDTHS):
        if idx in _PROJ_STACKED:
            out_specs.append(pl.BlockSpec((None, tm, w), lambda i: (layer, i, 0)))
            out_shape.append(jax.ShapeDtypeStruct((depth, n, w), F32))
        else:
            out_specs.append(pl.BlockSpec((tm, w), lambda i: (i, 0)))
            out_shape.append(jax.ShapeDtypeStruct((n, w), F32))
    in_specs = [pl.BlockSpec((tm, d), lambda i: (i, 0)),
                pl.BlockSpec((d, _PROJ_COLS), lambda i: (0, 0)),
                pl.BlockSpec((1, LANES), lambda i: (0, 0))]
    args = [x2d, w_cat, fb_pad]
    aliases = {}
    if stacks is not None:
        for j, (idx, st) in enumerate(zip(_PROJ_STACKED, stacks)):
            in_specs.append(pl.BlockSpec(memory_space=pl.ANY))
            args.append(st)
            aliases[3 + j] = idx
    return pl.pallas_call(
        _proj_kernel,
        grid=(n // tm,),
        in_specs=in_specs,
        out_specs=out_specs,
        out_shape=out_shape,
        input_output_aliases=aliases,
        compiler_params=_cparams(("parallel",)),
    )(*args)


def _cumsum_kernel(lf_ref, cum_ref, *, blk):
    tk = lf_ref.shape[1]
    row = lax.broadcasted_iota(jnp.int32, (blk, blk), 0)
    col = lax.broadcasted_iota(jnp.int32, (blk, blk), 1)
    tri = (col <= row).astype(BF16)
    carry = jnp.zeros((1, LANES), F32)
    for s in range(0, tk, blk):
        x = lf_ref[0, s:s + blk, :]
        hi, mid, lo = _split3(x)
        c = _dot(tri, hi) + _dot(tri, mid) + _dot(tri, lo) + carry
        cum_ref[0, s:s + blk, :] = c
        carry = c[blk - 1:blk, :]


def _cumsum(lf, blk=64):
    b, tk, _ = lf.shape
    return pl.pallas_call(
        functools.partial(_cumsum_kernel, blk=blk),
        grid=(b,),
        in_specs=[pl.BlockSpec((1, tk, LANES), lambda i: (i, 0, 0))],
        out_specs=pl.BlockSpec((1, tk, LANES), lambda i: (i, 0, 0)),
        out_shape=jax.ShapeDtypeStruct((b, tk, LANES), F32),
        compiler_params=_cparams(("parallel",)),
    )(lf)


def _seg_sum(x, bd):
    hi, lo = _split2(x)
    return _dot(hi, bd) + _dot(lo, bd)


def _tri_inverse(n1s, eye):
    def sq(xs):
        xb = [x.astype(BF16) for x in xs]
        return [_dot(x, x) for x in xb]

    def pair(xs, ys):
        return [eye + x + y + _dot(x.astype(BF16), y.astype(BF16)) for x, y in zip(xs, ys)]

    def mul(xs, ys):
        return [_dot(x.astype(BF16), y.astype(BF16)) for x, y in zip(xs, ys)]

    n2s = sq(n1s)
    n4s = sq(n2s)
    p1s = pair(n1s, n2s)
    n8s = sq(n4s)
    n16s = sq(n8s)
    p2s = pair(n4s, n8s)
    n32s = sq(n16s)
    p12s = mul(p1s, p2s)
    p3s = pair(n16s, n32s)
    return mul(p12s, p3s)


def _wkv_kernel(us_ref, prev0_ref, s0_ref, mu_ref, w0_ref, wd_ref, a0_ref, wa_ref,
                kkp_ref, ka_ref, rk_ref, lng_ref, lnb_ref,
                oa_ref, sout_ref,
                s_scr, prev_scr, at_s, rt_s, bh_s, kh_s, be_s, ke_s, v_s, ga_s, y_s):
    c_idx = pl.program_id(1)
    nc = pl.num_programs(1)
    C = WKV_CHUNK
    tt = us_ref.shape[1]
    nb = tt // C

    @pl.when(c_idx == 0)
    def _():
        s_scr[...] = s0_ref[0]
        prev_scr[...] = prev0_ref[0]

    u = us_ref[0]
    row1 = lax.broadcasted_iota(jnp.int32, (tt, 1), 0)
    zprev = jnp.where(row1 == 0, prev_scr[...], pltpu.roll(u, 1, axis=0))
    prev_scr[...] = u[tt - 1:tt, :]
    zs = u + (zprev - u) * mu_ref[...]
    r = zs[:, 0:W_A]
    k = zs[:, W_A:2 * W_A]
    v = zs[:, 2 * W_A:3 * W_A]
    lora_in = zs[:, 3 * W_A:3 * W_A + LANES]
    lane1 = lax.broadcasted_iota(jnp.int32, (1, LANES), 1)
    lora_t = jnp.where(lane1 < D_LORA, jnp.tanh(lora_in), lora_in).astype(BF16)
    wl = w0_ref[...] + _dot(lora_t, wd_ref[...])
    w_log = -_softplus(-wl) - 0.5
    lw = -jnp.exp(w_log)
    a = _sigmoid(a0_ref[...] + _dot(lora_t, wa_ref[...]))

    lane_w = lax.broadcasted_iota(jnp.int32, (W_A, W_A), 0) // HEAD_DIM
    lane_c = lax.broadcasted_iota(jnp.int32, (W_A, W_A), 1) // HEAD_DIM
    bd = (lane_w == lane_c).astype(BF16)

    kk = k * kkp_ref[...]
    kk = kk * lax.rsqrt(_seg_sum(kk * kk, bd) + 1e-12)
    k2 = k * (1.0 + (a - 1.0) * ka_ref[...])

    rowt = lax.broadcasted_iota(jnp.int32, (tt, tt), 0)
    colt = lax.broadcasted_iota(jnp.int32, (tt, tt), 1)
    same_chunk = (rowt // C) == (colt // C)
    tri_b = (same_chunk & (colt <= rowt)).astype(BF16)
    ones_b = same_chunk.astype(BF16)
    h1, h2, h3 = _split3(lw)
    cs = _dot(tri_b, h1) + _dot(tri_b, h2) + _dot(tri_b, h3)
    tot = _dot(ones_b, h1) + _dot(ones_b, h2) + _dot(ones_b, h3)
    g_inv = jnp.exp(-cs)
    g_end = jnp.exp(tot - cs)
    kka = kk * a
    at_s[...] = kk * jnp.exp(cs - lw)
    rt_s[...] = r * jnp.exp(cs)
    bh_s[...] = (-kka * g_inv).astype(BF16)
    kh_s[...] = (k2 * g_inv).astype(BF16)
    be_s[...] = (-kka * g_end).astype(BF16)
    ke_s[...] = (k2 * g_end).astype(BF16)
    v_s[...] = v
    ga_s[...] = jnp.exp(tot)

    rowc = lax.broadcasted_iota(jnp.int32, (C, C), 0)
    colc = lax.broadcasted_iota(jnp.int32, (C, C), 1)
    tri_incl = colc <= rowc
    tri_strict = colc < rowc
    eye = (colc == rowc).astype(F32)
    rowp = lax.broadcasted_iota(jnp.int32, (LANES, LANES), 0) // HEAD_DIM
    colp = lax.broadcasted_iota(jnp.int32, (LANES, LANES), 1) // HEAD_DIM
    bd_pair = rowp == colp
    head_of_lane = lane1 // HEAD_DIM
    npair = H_A // 2

    units = [(c, p) for c in range(nb) for p in range(npair)]
    probs = [(ui, hh) for ui in range(len(units)) for hh in range(2)]
    ld = []
    for c, p in units:
        rows = slice(c * C, (c + 1) * C)
        sl = slice(p * LANES, (p + 1) * LANES)
        ld.append(dict(at=at_s[rows, sl], rt=rt_s[rows, sl], v=v_s[rows, sl], bh=bh_s[rows, sl],
                       kh=kh_s[rows, sl], be=be_s[rows, sl], ke=ke_s[rows, sl]))
    masks = [head_of_lane == hh for hh in range(2)]
    l_ms = [jnp.where(masks[hh], jnp.concatenate([ld[ui]["at"], ld[ui]["rt"]], axis=0), 0.0).astype(BF16)
            for ui, hh in probs]
    abs_ = [_dot_nt(l_m, ld[ui]["bh"]) for l_m, (ui, hh) in zip(l_ms, probs)]
    aks = [_dot_nt(l_m, ld[ui]["kh"]) for l_m, (ui, hh) in zip(l_ms, probs)]
    a_abs = [jnp.where(tri_strict, ab[:C], 0.0) for ab in abs_]
    a_rbs = [jnp.where(tri_incl, ab[C:], 0.0).astype(BF16) for ab in abs_]
    a_aks = [jnp.where(tri_strict, ak[:C], 0.0).astype(BF16) for ak in aks]
    a_rks = [jnp.where(tri_incl, ak[C:], 0.0).astype(BF16) for ak in aks]
    v_ms = [jnp.where(masks[hh], ld[ui]["v"], 0.0).astype(BF16) for ui, hh in probs]
    avs = [_dot(a_ak, v_m) for a_ak, v_m in zip(a_aks, v_ms)]
    yvs = [_dot(a_rk, v_m) for a_rk, v_m in zip(a_rks, v_ms)]
    t_invs = _tri_inverse(a_abs, eye)
    zs_ = [jnp.concatenate([jnp.where(masks[hh], ld[ui]["at"], 0.0), av], axis=1).astype(BF16)
           for av, (ui, hh) in zip(avs, probs)]
    xs = [_dot(t.astype(BF16), z) for t, z in zip(t_invs, zs_)]
    rys = [_dot(a_rb, x.astype(BF16)) for a_rb, x in zip(a_rbs, xs)]
    atps, wvs, rps, y0s = [], [], [], []
    for ui in range(len(units)):
        x0, x1 = xs[2 * ui], xs[2 * ui + 1]
        ry0, ry1 = rys[2 * ui], rys[2 * ui + 1]
        atps.append((x0[:, :LANES] + x1[:, :LANES]).astype(BF16))
        wvs.append(jnp.concatenate([x0[:, LANES:] + x1[:, LANES:], ld[ui]["v"]], axis=0).astype(BF16))
        rps.append((ld[ui]["rt"] + ry0[:, :LANES] + ry1[:, :LANES]).astype(BF16))
        y0s.append(ry0[:, LANES:] + ry1[:, LANES:] + yvs[2 * ui] + yvs[2 * ui + 1])
    pps = [jnp.where(bd_pair, _dot_tn(atp, ld[ui]["be"]), 0.0).astype(BF16)
           for ui, atp in enumerate(atps)]
    qs = [jnp.where(bd_pair, _dot_tn(wv, jnp.concatenate([ld[ui]["be"], ld[ui]["ke"]], axis=0)), 0.0)
          for ui, wv in enumerate(wvs)]

    for c in range(nb):
        rows = slice(c * C, (c + 1) * C)
        s_ps = [s_scr[p] for p in range(npair)]
        s_bs = [s_p.astype(BF16) for s_p in s_ps]
        for p in range(npair):
            ui = c * npair + p
            sl = slice(p * LANES, (p + 1) * LANES)
            y_s[rows, sl] = _dot_nt(rps[ui], s_bs[p]) + y0s[ui]
            s_scr[p] = s_ps[p] * ga_s[c * C:c * C + 1, sl] + _dot(s_bs[p], pps[ui]) + qs[ui]

    y = y_s[...]
    inv_n = 1.0 / HEAD_DIM
    mean = _seg_sum(y, bd) * inv_n
    d = y - mean
    var = _seg_sum(d * d, bd) * inv_n
    yn = d * lax.rsqrt(var + GN_EPS) * lng_ref[...] + lnb_ref[...]
    bonus = _seg_sum(r * k2 * rk_ref[...], bd) * v
    oa_ref[0] = yn + bonus

    @pl.when(c_idx == nc - 1)
    def _():
        sout_ref[0] = s_scr[...]


def _wkv(us, prev0, s0_bd, prm, tt):
    b, t, _ = us.shape
    npair = H_A // 2
    vec = lambda w: pl.BlockSpec((1, w), lambda i, j: (0, 0))
    mat = lambda: pl.BlockSpec((LANES, W_A), lambda i, j: (0, 0))
    blk = lambda dt: pltpu.VMEM((tt, W_A), dt)
    return pl.pallas_call(
        _wkv_kernel,
        grid=(b, t // tt),
        in_specs=[pl.BlockSpec((1, tt, SHIFT_W), lambda i, j: (i, j, 0)),
                  pl.BlockSpec((1, 1, SHIFT_W), lambda i, j: (i, 0, 0)),
                  pl.BlockSpec((1, npair, LANES, LANES), lambda i, j: (i, 0, 0, 0)),
                  vec(SHIFT_W), vec(W_A), mat(), vec(W_A), mat(),
                  vec(W_A), vec(W_A), vec(W_A), vec(W_A), vec(W_A)],
        out_specs=[pl.BlockSpec((1, tt, W_A), lambda i, j: (i, j, 0)),
                   pl.BlockSpec((1, npair, LANES, LANES), lambda i, j: (i, 0, 0, 0))],
        out_shape=[jax.ShapeDtypeStruct((b, t, W_A), F32),
                   jax.ShapeDtypeStruct((b, npair, LANES, LANES), F32)],
        scratch_shapes=[pltpu.VMEM((npair, LANES, LANES), F32),
                        pltpu.VMEM((1, SHIFT_W), F32),
                        blk(F32), blk(F32), blk(BF16), blk(BF16), blk(BF16), blk(BF16),
                        blk(F32), blk(F32), blk(F32)],
        compiler_params=_cparams(("parallel", "arbitrary")),
    )(us, prev0, s0_bd, prm["mu"], prm["w0"], prm["wd"], prm["a0"], prm["wa"],
      prm["kkp"], prm["ka"], prm["rk"], prm["lnx_g"], prm["lnx_b"])


_AUG = 8
_PREP_ROWS = 256


def _bias_lanes(cum, pair, head, key_side):
    h = 2 * pair + head
    r = lax.broadcasted_iota(jnp.int32, (LANES, LANES), 0)
    c = lax.broadcasted_iota(jnp.int32, (LANES, LANES), 1)
    lane = lax.broadcasted_iota(jnp.int32, (1, LANES), 1)
    base = _AUG * head
    f_off, one_off = (0, 3) if key_side else (3, 0)
    sign = -1.0 if key_side else 1.0
    out = jnp.where((lane >= base + one_off) & (lane < base + one_off + 3), 1.0, 0.0)
    for t, part in enumerate(_split3(cum)):
        sel = jnp.where((r == h) & (c == base + f_off + t), sign, 0.0).astype(BF16)
        out = out + _dot(part, sel)
    return out


def _stage_rows(total):
    return [(s, min(_PREP_ROWS, total - s)) for s in range(0, total, _PREP_ROWS)]


def _transpose_rows(x):
    n = x.shape[0]
    n_pad = -n % LANES
    if n_pad:
        x = jnp.concatenate([x, jnp.zeros((n_pad, x.shape[1]), x.dtype)], axis=0)
    return x.T[:, :n]


def _fox_kernel(q_ref, k_ref, v_ref, cq_ref, ck_ref, o_ref, ka_scr, vt_scr, acc_scr,
                *, tq, tk, past, single):
    npair = H_B // 2
    tkk = k_ref.shape[1]
    lane1 = lax.broadcasted_iota(jnp.int32, (1, LANES), 1)
    head_of_lane = lane1 // HEAD_DIM

    def stage_kv():
        for s, n in _stage_rows(tkk):
            kb = k_ref[0, s:s + n, :].astype(BF16)
            cum = ck_ref[0, s:s + n, :]
            for p in range(npair):
                aug = _bias_lanes(cum, p, 0, True) + _bias_lanes(cum, p, 1, True)
                ka_scr[p, s:s + n, 0:LANES] = kb[:, p * LANES:(p + 1) * LANES]
                ka_scr[p, s:s + n, LANES:2 * LANES] = aug.astype(BF16)
            vt_scr[:, s:s + n] = _transpose_rows(v_ref[0, s:s + n, :]).astype(BF16)

    if single:
        i = 0
        stage_kv()
    else:
        i = pl.program_id(1)
        pl.when(i == 0)(stage_kv)

    q = q_ref[0] * (HEAD_DIM ** -0.5)
    cq = cq_ref[0]
    rhs = []
    for p in range(npair):
        q_p = q[:, p * LANES:(p + 1) * LANES]
        halves = [jnp.concatenate([jnp.where(head_of_lane == hh, q_p, 0.0),
                                   _bias_lanes(cq, p, hh, False)], axis=1) for hh in range(2)]
        rhs.append(jnp.concatenate(halves, axis=0).astype(BF16))

    n_full = past // tk + i * (tq // tk)
    diag0 = past if single else pl.multiple_of(past + i * tq, tq)
    krow = lax.broadcasted_iota(jnp.int32, (tq, 2 * tq), 0)
    qcol = lax.broadcasted_iota(jnp.int32, (tq, 2 * tq), 1)
    causal_t = krow <= jnp.where(qcol >= tq, qcol - tq, qcol)

    acc_scr[...] = jnp.zeros(acc_scr.shape, F32)

    def update(carry, start, size, mask):
        ms, ls = carry
        st = [_dot_nt(ka_scr[p, pl.ds(start, size), :], rhs[p]) for p in range(npair)]
        if mask is not None:
            st = [jnp.where(mask, s, NEG_INF) for s in st]
        m_new = [jnp.maximum(m, jnp.max(s, axis=0, keepdims=True)) for m, s in zip(ms, st)]
        alpha = [jnp.exp(m - mn) for m, mn in zip(ms, m_new)]
        pt = [jnp.exp(s - mn) for s, mn in zip(st, m_new)]
        l_new = [a * l + jnp.sum(x, axis=0, keepdims=True) for a, l, x in zip(alpha, ls, pt)]
        pv = [_dot(vt_scr[p * LANES:(p + 1) * LANES, pl.ds(start, size)], pt[p].astype(BF16))
              for p in range(npair)]
        for p in range(npair):
            acc_scr[p] = acc_scr[p] * alpha[p] + pv[p]
        return tuple(m_new), tuple(l_new)

    def body(j, carry):
        return update(carry, pl.multiple_of(j * tk, tk), tk, None)

    init = (tuple(jnp.full((1, 2 * tq), NEG_INF, F32) for _ in range(npair)),
            tuple(jnp.zeros((1, 2 * tq), F32) for _ in range(npair)))
    carry = lax.fori_loop(0, n_full, body, init)
    _, ls = update(carry, diag0, tq, causal_t)
    for p in range(npair):
        o_t = (acc_scr[p] / ls[p]).T
        o_ref[0, :, p * LANES:(p + 1) * LANES] = jnp.where(head_of_lane == 0, o_t[:tq], o_t[tq:])


def _fox(q, k_all, v_all, cum_q, cum_k, *, layer, tq, tk, past):
    b, t, _ = q.shape
    tkk = k_all.shape[2]
    npair = H_B // 2
    return pl.pallas_call(
        functools.partial(_fox_kernel, tq=tq, tk=tk, past=past, single=(t == tq)),
        grid=(b, t // tq),
        in_specs=[pl.BlockSpec((1, tq, W_B), lambda i, j: (i, j, 0)),
                  pl.BlockSpec((None, 1, tkk, W_B), lambda i, j: (layer, i, 0, 0)),
                  pl.BlockSpec((None, 1, tkk, W_B), lambda i, j: (layer, i, 0, 0)),
                  pl.BlockSpec((1, tq, LANES), lambda i, j: (i, j, 0)),
                  pl.BlockSpec((1, tkk, LANES), lambda i, j: (i, 0, 0))],
        out_specs=pl.BlockSpec((1, tq, W_B), lambda i, j: (i, j, 0)),
        out_shape=jax.ShapeDtypeStruct((b, t, W_B), F32),
        scratch_shapes=[pltpu.VMEM((npair, tkk, 2 * LANES), BF16), pltpu.VMEM((W_B, tkk), BF16),
                        pltpu.VMEM((npair, LANES, 2 * tq), F32)],
        compiler_params=_cparams(("parallel", "arbitrary")),
    )(q, k_all, v_all, cum_q, cum_k)


def _sb_kernel(q_ref, k_ref, v_ref, o_ref, kb_scr, vt_scr, acc_scr, *, tq, tk, past, single):
    npair = H_C // 2
    tkk = k_ref.shape[1]
    lane1 = lax.broadcasted_iota(jnp.int32, (1, LANES), 1)
    head_of_lane = lane1 // HEAD_DIM

    def stage_kv():
        for s, n in _stage_rows(tkk):
            kb_scr[s:s + n, :] = k_ref[0, s:s + n, :].astype(BF16)
            vt_scr[:, s:s + n] = _transpose_rows(v_ref[0, s:s + n, :]).astype(BF16)

    if single:
        i = 0
        stage_kv()
    else:
        i = pl.program_id(1)
        pl.when(i == 0)(stage_kv)

    q = q_ref[0] * (HEAD_DIM ** -0.5)
    rhs = []
    for p in range(npair):
        q_p = q[:, p * LANES:(p + 1) * LANES]
        rhs.append(jnp.concatenate([jnp.where(head_of_lane == hh, q_p, 0.0) for hh in range(2)],
                                   axis=0).astype(BF16))

    def earlier_matrix(n):
        rr = lax.broadcasted_iota(jnp.int32, (n, n), 0)
        cc = lax.broadcasted_iota(jnp.int32, (n, n), 1)
        return (cc > rr).astype(BF16)

    n_full = past // tk + i * (tq // tk)
    diag0 = past if single else pl.multiple_of(past + i * tq, tq)
    krow = lax.broadcasted_iota(jnp.int32, (tq, 2 * tq), 0)
    qcol = lax.broadcasted_iota(jnp.int32, (tq, 2 * tq), 1)
    strict_t = krow < jnp.where(qcol >= tq, qcol - tq, qcol)

    def update(runs, start, size, mask, first):
        zt = [_dot_nt(kb_scr[pl.ds(start, size), p * LANES:(p + 1) * LANES], rhs[p])
              for p in range(npair)]
        tail = [jnp.log(1.0 + jnp.exp(-jnp.abs(z))) for z in zt]
        l1m = [-jnp.maximum(z, 0.0) - t for z, t in zip(zt, tail)]
        if mask is not None:
            l1m = [jnp.where(mask, x, 0.0) for x in l1m]
        splits = [_split2(x) for x in l1m]
        later = earlier_matrix(size)
        after = [_dot(later, hi) + _dot(later, lo) for hi, lo in splits]
        wt = [jnp.exp(jnp.minimum(z, 0.0) - t + a + r) for z, t, a, r in zip(zt, tail, after, runs)]
        if mask is not None:
            wt = [jnp.where(mask, w, 0.0) for w in wt]
        pv = [_dot(vt_scr[p * LANES:(p + 1) * LANES, pl.ds(start, size)], wt[p].astype(BF16))
              for p in range(npair)]
        for p in range(npair):
            acc_scr[p] = pv[p] if first else acc_scr[p] + pv[p]
        return tuple(r + jnp.sum(x, axis=0, keepdims=True) for r, x in zip(runs, l1m))

    runs = tuple(jnp.zeros((1, 2 * tq), F32) for _ in range(npair))
    runs = update(runs, diag0, tq, strict_t, True)

    def body(jj, runs):
        j = n_full - 1 - jj
        return update(runs, pl.multiple_of(j * tk, tk), tk, None, False)

    lax.fori_loop(0, n_full, body, runs)
    for p in range(npair):
        o_t = acc_scr[p].T
        o_ref[0, :, p * LANES:(p + 1) * LANES] = jnp.where(head_of_lane == 0, o_t[:tq], o_t[tq:])


def _sb(q, k_all, v_all, *, layer, tq, tk, past):
    b, t, _ = q.shape
    tkk = k_all.shape[2]
    npair = H_C // 2
    return pl.pallas_call(
        functools.partial(_sb_kernel, tq=tq, tk=tk, past=past, single=(t == tq)),
        grid=(b, t // tq),
        in_specs=[pl.BlockSpec((1, tq, W_C), lambda i, j: (i, j, 0)),
                  pl.BlockSpec((None, 1, tkk, W_C), lambda i, j: (layer, i, 0, 0)),
                  pl.BlockSpec((None, 1, tkk, W_C), lambda i, j: (layer, i, 0, 0))],
        out_specs=pl.BlockSpec((1, tq, W_C), lambda i, j: (i, j, 0)),
        out_shape=jax.ShapeDtypeStruct((b, t, W_C), F32),
        scratch_shapes=[pltpu.VMEM((tkk, W_C), BF16), pltpu.VMEM((W_C, tkk), BF16),
                        pltpu.VMEM((npair, LANES, 2 * tq), F32)],
        compiler_params=_cparams(("parallel", "arbitrary")),
    )(q, k_all, v_all)


def _out_kernel(oa_ref, ob_ref, oc_ref, g_ref, x_ref, w_ref, lng_ref, lnb_ref, y_ref, *, alpha):
    g = g_ref[...]
    gate = g * _sigmoid(g)
    acc = _dot((oa_ref[...] * gate[:, 0:W_A]).astype(BF16), w_ref[0:W_A, :])
    acc = acc + _dot((ob_ref[...] * gate[:, W_A:W_A + W_B]).astype(BF16), w_ref[W_A:W_A + W_B, :])
    acc = acc + _dot((oc_ref[...] * gate[:, W_A + W_B:]).astype(BF16), w_ref[W_A + W_B:, :])
    z = alpha * x_ref[...] + acc
    mu = jnp.mean(z, axis=-1, keepdims=True)
    d = z - mu
    var = jnp.mean(d * d, axis=-1, keepdims=True)
    y_ref[...] = d * lax.rsqrt(var + LN_EPS) * lng_ref[...] + lnb_ref[...]


def _out(oa, ob, oc, g, x2d, w_out, ln_g, ln_b, alpha, tm):
    n, d = x2d.shape
    row = lambda w: pl.BlockSpec((tm, w), lambda i: (i, 0))
    return pl.pallas_call(
        functools.partial(_out_kernel, alpha=alpha),
        grid=(n // tm,),
        in_specs=[row(W_A), row(W_B), row(W_C), row(W_MIX), row(d),
                  pl.BlockSpec((W_MIX, d), lambda i: (0, 0)),
                  pl.BlockSpec((1, d), lambda i: (0, 0)),
                  pl.BlockSpec((1, d), lambda i: (0, 0))],
        out_specs=row(d),
        out_shape=jax.ShapeDtypeStruct((n, d), F32),
        compiler_params=_cparams(("parallel",)),
    )(oa, ob, oc, g, x2d, w_out, ln_g, ln_b)


def _pad_lanes(a, width=LANES):
    return jnp.pad(a, [(0, 0)] * (a.ndim - 1) + [(0, width - a.shape[-1])])


def _state_to_pairs(s):
    b = s.shape[0]
    s = s.reshape(b, H_A // 2, 2, HEAD_DIM, HEAD_DIM)
    eye2 = jnp.eye(2, dtype=s.dtype)
    bd = jnp.einsum("bphvk,hg->bphvgk", s, eye2)
    return bd.reshape(b, H_A // 2, LANES, LANES)


def _pairs_to_state(sp):
    b = sp.shape[0]
    s = sp.reshape(b, H_A // 2, 2, HEAD_DIM, 2, HEAD_DIM)
    s = jnp.stack([s[:, :, 0, :, 0, :], s[:, :, 1, :, 1, :]], axis=2)
    return s.reshape(b, H_A, HEAD_DIM, HEAD_DIM)


def _layer_params(l, w_in, mu_shift, w0_decay, w_decay, a0, w_aaa, k_k, k_a, r_k,
                  lnx_g, lnx_b, fox_fb, w_out, ln_g, ln_b):
    w = w_in[l]
    sizes = (SHIFT_W, W_A, W_B, W_B, W_B, H_B, W_B, W_C, W_C, W_C, W_C)
    offs = np.concatenate([[0], np.cumsum(sizes)])
    (w_shift, w_ga, w_qb, w_kb, w_vb, w_fb, w_gb, w_qc, w_kc, w_vc, w_gc) = (
        w[:, int(offs[i]):int(offs[i + 1])] for i in range(len(sizes)))
    w_cat = jnp.concatenate([w_shift, w_ga, w_gb, w_gc, w_qb, w_kb, w_vb, w_qc, w_kc, w_vc,
                             _pad_lanes(w_fb)], axis=1).astype(BF16)
    zeros = jnp.zeros((D_LORA, W_A), F32)
    row = lambda a: a.reshape(1, -1).astype(F32)
    return dict(
        w_cat=w_cat, fb=_pad_lanes(row(fox_fb[l])),
        mu=row(mu_shift[l]), w0=row(w0_decay[l]),
        wd=jnp.concatenate([w_decay[l], zeros], axis=0).astype(BF16),
        a0=row(a0[l]), wa=jnp.concatenate([zeros, w_aaa[l]], axis=0).astype(BF16),
        kkp=row(k_k[l]), ka=row(k_a[l]), rk=row(r_k[l]), lnx_g=row(lnx_g[l]), lnx_b=row(lnx_b[l]),
        w_out=w_out[l].astype(BF16), ln_g=row(ln_g[l]), ln_b=row(ln_b[l]))


def _run_layer(x, hist, prm, alpha, layer, depth, stacks):
    b, t, d = x.shape
    n = b * t
    x2d = x.reshape(n, d)
    tm = 256 if n % 256 == 0 else n
    us, g, qb, kb_st, vb_st, qc, kc_st, vc_st, lf = _proj(
        x2d, prm["w_cat"], prm["fb"], tm, layer, depth, stacks)
    r3 = lambda a: a.reshape(b, t, a.shape[-1])
    us, qb, qc, lf = (r3(a) for a in (us, qb, qc, lf))
    r4 = lambda a: a.reshape(depth, b, t, a.shape[-1])

    if hist is None:
        past = 0
        prev0 = jnp.zeros((b, 1, SHIFT_W), F32)
        s0 = jnp.zeros((b, H_A // 2, LANES, LANES), F32)
        kb_all, vb_all, kc_all, vc_all = (r4(a) for a in (kb_st, vb_st, kc_st, vc_st))
        lf_all = lf
        kv_layer = layer
        fox_blk = (min(FOX_BLOCK, t),) * 2
        sb_blk = (min(SB_BLOCK, t),) * 2
    else:
        h_fk, h_fv, h_lf, h_sk, h_sv, h_wkv, h_shift = hist
        past = h_fk.shape[1]
        prev0 = h_shift
        s0 = _state_to_pairs(h_wkv)
        new_rows = lambda st: st[layer].reshape(b, t, st.shape[-1])
        kb_all = jnp.concatenate([h_fk.reshape(b, past, W_B), new_rows(kb_st)], axis=1)[None]
        vb_all = jnp.concatenate([h_fv.reshape(b, past, W_B), new_rows(vb_st)], axis=1)[None]
        lf_all = jnp.concatenate([_pad_lanes(h_lf), lf], axis=1)
        kc_all = jnp.concatenate([h_sk.reshape(b, past, W_C), new_rows(kc_st)], axis=1)[None]
        vc_all = jnp.concatenate([h_sv.reshape(b, past, W_C), new_rows(vc_st)], axis=1)[None]
        kv_layer = 0
        fox_blk = sb_blk = (t, min(SB_BLOCK, past))

    oa, s_fin = _wkv(us, prev0, s0, prm, min(WKV_BLOCK, t))
    cum = _cumsum(lf_all)
    ob = _fox(qb, kb_all, vb_all, cum[:, past:], cum, layer=kv_layer,
              tq=fox_blk[0], tk=fox_blk[1], past=past)
    oc = _sb(qc, kc_all, vc_all, layer=kv_layer, tq=sb_blk[0], tk=sb_blk[1], past=past)
    y = _out(oa.reshape(n, W_A), ob.reshape(n, W_B), oc.reshape(n, W_C), g, x2d,
             prm["w_out"], prm["ln_g"], prm["ln_b"], alpha, tm)
    small = (lf[:, :, :H_B], _pairs_to_state(s_fin), us[:, -1:, :])
    return y.reshape(b, t, d), (kb_st, vb_st, kc_st, vc_st), small


def kernel(x_prompt, x_sample, cache_fox_k, cache_fox_v, cache_fox_logf, cache_sb_k, cache_sb_v, state_wkv, state_shift, w_in, mu_shift, w0_decay, w_decay, a0, w_aaa, k_k, k_a, r_k, lnx_g, lnx_b, fox_fb, w_out, ln_g, ln_b):
    depth = w_in.shape[0]
    alpha = (2 * depth) ** 0.25
    yp, ys = x_prompt, x_sample
    stacks_p, stacks_s = None, None
    small_p, small_s = [], []
    for l in range(depth):
        prm = _layer_params(l, w_in, mu_shift, w0_decay, w_decay, a0, w_aaa, k_k, k_a, r_k,
                            lnx_g, lnx_b, fox_fb, w_out, ln_g, ln_b)
        yp, stacks_p, sm_p = _run_layer(yp, None, prm, alpha, l, depth, stacks_p)
        hist = (cache_fox_k[l], cache_fox_v[l], cache_fox_logf[l], cache_sb_k[l],
                cache_sb_v[l], state_wkv[l], state_shift[l])
        ys, stacks_s, sm_s = _run_layer(ys, hist, prm, alpha, l, depth, stacks_s)
        small_p.append(sm_p)
        small_s.append(sm_s)

    def group_outputs(x, stacks, small):
        b, t, _ = x.shape
        kb, vb, kc, vc = stacks
        logf, wkv, shift = (jnp.stack([sm[i] for sm in small]) for i in range(3))
        return (kb.reshape(depth, b, t, H_B, HEAD_DIM), vb.reshape(depth, b, t, H_B, HEAD_DIM), logf,
                kc.reshape(depth, b, t, H_C, HEAD_DIM), vc.reshape(depth, b, t, H_C, HEAD_DIM),
                wkv, shift)

    return ((yp, ys) + group_outputs(x_prompt, stacks_p, small_p)
            + group_outputs(x_sample, stacks_s, small_s))
```

```python
import functools

import numpy as np
import jax
import jax.numpy as jnp
from jax import lax
from jax.experimental import pallas as pl
from jax.experimental.pallas import tpu as pltpu

F32 = jnp.float32
BF16 = jnp.bfloat16

HEAD_DIM = 64
H_A, H_B, H_C = 6, 6, 4
W_A, W_B, W_C = H_A * HEAD_DIM, H_B * HEAD_DIM, H_C * HEAD_DIM
W_MIX = W_A + W_B + W_C
D_LORA = 64
SHIFT_W = 3 * W_A + 2 * D_LORA
LANES = 128
WKV_CHUNK = 64
WKV_BLOCK = 128
FOX_BLOCK = 512
SB_BLOCK = 256
GN_EPS = 64e-5
LN_EPS = 1e-5
NEG_INF = -1e30
VMEM_LIMIT = 56 * 1024 * 1024


def _cparams(sem):
    return pltpu.CompilerParams(dimension_semantics=sem, vmem_limit_bytes=VMEM_LIMIT)


def _dot(a, b):
    return jnp.dot(a, b, preferred_element_type=F32)


def _dot_nt(a, b):
    return lax.dot_general(a, b, (((1,), (1,)), ((), ())), preferred_element_type=F32)


def _dot_tn(a, b):
    return lax.dot_general(a, b, (((0,), (0,)), ((), ())), preferred_element_type=F32)


def _split2(x):
    hi = x.astype(BF16)
    lo = (x - hi.astype(F32)).astype(BF16)
    return hi, lo


def _split3(x):
    hi = x.astype(BF16)
    r1 = x - hi.astype(F32)
    mid = r1.astype(BF16)
    lo = (r1 - mid.astype(F32)).astype(BF16)
    return hi, mid, lo


def _softplus(x):
    return jnp.maximum(x, 0.0) + jnp.log1p(jnp.exp(-jnp.abs(x)))


def _sigmoid(x):
    return 1.0 / (1.0 + jnp.exp(-x))


_OFF_SHIFT = 0
_OFF_G = _OFF_SHIFT + SHIFT_W
_OFF_QB = _OFF_G + W_MIX
_OFF_KB = _OFF_QB + W_B
_OFF_VB = _OFF_KB + W_B
_OFF_QC = _OFF_VB + W_B
_OFF_KC = _OFF_QC + W_C
_OFF_VC = _OFF_KC + W_C
_OFF_F = _OFF_VC + W_C
_PROJ_COLS = _OFF_F + LANES


def _proj_kernel(x_ref, w_ref, fb_ref, *refs):
    us_ref, g_ref, qb_ref, kb_ref, vb_ref, qc_ref, kc_ref, vc_ref, lf_ref = refs[-9:]
    xb = x_ref[...].astype(BF16)
    for ref, off in ((us_ref, _OFF_SHIFT), (g_ref, _OFF_G), (qb_ref, _OFF_QB),
                     (kb_ref, _OFF_KB), (vb_ref, _OFF_VB), (qc_ref, _OFF_QC),
                     (kc_ref, _OFF_KC), (vc_ref, _OFF_VC)):
        width = ref.shape[-1]
        ref[...] = _dot(xb, w_ref[:, off:off + width])
    f = _dot(xb, w_ref[:, _OFF_F:_OFF_F + LANES]) + fb_ref[...]
    lf_ref[...] = -_softplus(-f)


_PROJ_WIDTHS = (SHIFT_W, W_MIX, W_B, W_B, W_B, W_C, W_C, W_C, LANES)
_PROJ_STACKED = (3, 4, 6, 7)


def _proj(x2d, w_cat, fb_pad, tm, layer, depth, stacks):
    n, d = x2d.shape
    out_specs, out_shape = [], []
    for idx, w in enumerate(_PROJ_WIDTHS):
        if idx in _PROJ_STACKED:
            out_specs.append(pl.BlockSpec((None, tm, w), lambda i: (layer, i, 0)))
            out_shape.append(jax.ShapeDtypeStruct((depth, n, w), F32))
        else:
            out_specs.append(pl.BlockSpec((tm, w), lambda i: (i, 0)))
            out_shape.append(jax.ShapeDtypeStruct((n, w), F32))
    in_specs = [pl.BlockSpec((tm, d), lambda i: (i, 0)),
                pl.BlockSpec((d, _PROJ_COLS), lambda i: (0, 0)),
                pl.BlockSpec((1, LANES), lambda i: (0, 0))]
    args = [x2d, w_cat, fb_pad]
    aliases = {}
    if stacks is not None:
        for j, (idx, st) in enumerate(zip(_PROJ_STACKED, stacks)):
            in_specs.append(pl.BlockSpec(memory_space=pl.ANY))
            args.append(st)
            aliases[3 + j] = idx
    return pl.pallas_call(
        _proj_kernel,
        grid=(n // tm,),
        in_specs=in_specs,
        out_specs=out_specs,
        out_shape=out_shape,
        input_output_aliases=aliases,
        compiler_params=_cparams(("parallel",)),
    )(*args)


def _cumsum_kernel(lf_ref, cum_ref, *, blk):
    tk = lf_ref.shape[1]
    row = lax.broadcasted_iota(jnp.int32, (blk, blk), 0)
    col = lax.broadcasted_iota(jnp.int32, (blk, blk), 1)
    tri = (col <= row).astype(BF16)
    carry = jnp.zeros((1, LANES), F32)
    for s in range(0, tk, blk):
        x = lf_ref[0, s:s + blk, :]
        hi, mid, lo = _split3(x)
        c = _dot(tri, hi) + _dot(tri, mid) + _dot(tri, lo) + carry
        cum_ref[0, s:s + blk, :] = c
        carry = c[blk - 1:blk, :]


def _cumsum(lf, blk=64):
    b, tk, _ = lf.shape
    return pl.pallas_call(
        functools.partial(_cumsum_kernel, blk=blk),
        grid=(b,),
        in_specs=[pl.BlockSpec((1, tk, LANES), lambda i: (i, 0, 0))],
        out_specs=pl.BlockSpec((1, tk, LANES), lambda i: (i, 0, 0)),
        out_shape=jax.ShapeDtypeStruct((b, tk, LANES), F32),
        compiler_params=_cparams(("parallel",)),
    )(lf)


def _seg_sum(x, bd):
    hi, lo = _split2(x)
    return _dot(hi, bd) + _dot(lo, bd)


def _tri_inverse(n1s, eye):
    def sq(xs):
        xb = [x.astype(BF16) for x in xs]
        return [_dot(x, x) for x in xb]

    def pair(xs, ys):
        return [eye + x + y + _dot(x.astype(BF16), y.astype(BF16)) for x, y in zip(xs, ys)]

    def mul(xs, ys):
        return [_dot(x.astype(BF16), y.astype(BF16)) for x, y in zip(xs, ys)]

    n2s = sq(n1s)
    n4s = sq(n2s)
    p1s = pair(n1s, n2s)
    n8s = sq(n4s)
    n16s = sq(n8s)
    p2s = pair(n4s, n8s)
    n32s = sq(n16s)
    p12s = mul(p1s, p2s)
    p3s = pair(n16s, n32s)
    return mul(p12s, p3s)


def _wkv_kernel(us_ref, prev0_ref, s0_ref, mu_ref, w0_ref, wd_ref, a0_ref, wa_ref,
                kkp_ref, ka_ref, rk_ref, lng_ref, lnb_ref,
                oa_ref, sout_ref,
                s_scr, prev_scr, at_s, rt_s, bh_s, kh_s, be_s, ke_s, v_s, ga_s, y_s):
    c_idx = pl.program_id(1)
    nc = pl.num_programs(1)
    C = WKV_CHUNK
    tt = us_ref.shape[1]
    nb = tt // C

    @pl.when(c_idx == 0)
    def _():
        s_scr[...] = s0_ref[0]
        prev_scr[...] = prev0_ref[0]

    u = us_ref[0]
    row1 = lax.broadcasted_iota(jnp.int32, (tt, 1), 0)
    zprev = jnp.where(row1 == 0, prev_scr[...], pltpu.roll(u, 1, axis=0))
    prev_scr[...] = u[tt - 1:tt, :]
    zs = u + (zprev - u) * mu_ref[...]
    r = zs[:, 0:W_A]
    k = zs[:, W_A:2 * W_A]
    v = zs[:, 2 * W_A:3 * W_A]
    lora_in = zs[:, 3 * W_A:3 * W_A + LANES]
    lane1 = lax.broadcasted_iota(jnp.int32, (1, LANES), 1)
    lora_t = jnp.where(lane1 < D_LORA, jnp.tanh(lora_in), lora_in).astype(BF16)
    wl = w0_ref[...] + _dot(lora_t, wd_ref[...])
    w_log = -_softplus(-wl) - 0.5
    lw = -jnp.exp(w_log)
    a = _sigmoid(a0_ref[...] + _dot(lora_t, wa_ref[...]))

    lane_w = lax.broadcasted_iota(jnp.int32, (W_A, W_A), 0) // HEAD_DIM
    lane_c = lax.broadcasted_iota(jnp.int32, (W_A, W_A), 1) // HEAD_DIM
    bd = (lane_w == lane_c).astype(BF16)

    kk = k * kkp_ref[...]
    kk = kk * lax.rsqrt(_seg_sum(kk * kk, bd) + 1e-12)
    k2 = k * (1.0 + (a - 1.0) * ka_ref[...])

    rowt = lax.broadcasted_iota(jnp.int32, (tt, tt), 0)
    colt = lax.broadcasted_iota(jnp.int32, (tt, tt), 1)
    same_chunk = (rowt // C) == (colt // C)
    tri_b = (same_chunk & (colt <= rowt)).astype(BF16)
    ones_b = same_chunk.astype(BF16)
    h1, h2, h3 = _split3(lw)
    cs = _dot(tri_b, h1) + _dot(tri_b, h2) + _dot(tri_b, h3)
    tot = _dot(ones_b, h1) + _dot(ones_b, h2) + _dot(ones_b, h3)
    g_inv = jnp.exp(-cs)
    g_end = jnp.exp(tot - cs)
    kka = kk * a
    at_s[...] = kk * jnp.exp(cs - lw)
    rt_s[...] = r * jnp.exp(cs)
    bh_s[...] = (-kka * g_inv).astype(BF16)
    kh_s[...] = (k2 * g_inv).astype(BF16)
    be_s[...] = (-kka * g_end).astype(BF16)
    ke_s[...] = (k2 * g_end).astype(BF16)
    v_s[...] = v
    ga_s[...] = jnp.exp(tot)

    rowc = lax.broadcasted_iota(jnp.int32, (C, C), 0)
    colc = lax.broadcasted_iota(jnp.int32, (C, C), 1)
    tri_incl = colc <= rowc
    tri_strict = colc < rowc
    eye = (colc == rowc).astype(F32)
    rowp = lax.broadcasted_iota(jnp.int32, (LANES, LANES), 0) // HEAD_DIM
    colp = lax.broadcasted_iota(jnp.int32, (LANES, LANES), 1) // HEAD_DIM
    bd_pair = rowp == colp
    head_of_lane = lane1 // HEAD_DIM
    npair = H_A // 2

    units = [(c, p) for c in range(nb) for p in range(npair)]
    probs = [(ui, hh) for ui in range(len(units)) for hh in range(2)]
    ld = []
    for c, p in units:
        rows = slice(c * C, (c + 1) * C)
        sl = slice(p * LANES, (p + 1) * LANES)
        ld.append(dict(at=at_s[rows, sl], rt=rt_s[rows, sl], v=v_s[rows, sl], bh=bh_s[rows, sl],
                       kh=kh_s[rows, sl], be=be_s[rows, sl], ke=ke_s[rows, sl]))
    masks = [head_of_lane == hh for hh in range(2)]
    l_ms = [jnp.where(masks[hh], jnp.concatenate([ld[ui]["at"], ld[ui]["rt"]], axis=0), 0.0).astype(BF16)
            for ui, hh in probs]
    abs_ = [_dot_nt(l_m, ld[ui]["bh"]) for l_m, (ui, hh) in zip(l_ms, probs)]
    aks = [_dot_nt(l_m, ld[ui]["kh"]) for l_m, (ui, hh) in zip(l_ms, probs)]
    a_abs = [jnp.where(tri_strict, ab[:C], 0.0) for ab in abs_]
    a_rbs = [jnp.where(tri_incl, ab[C:], 0.0).astype(BF16) for ab in abs_]
    a_aks = [jnp.where(tri_strict, ak[:C], 0.0).astype(BF16) for ak in aks]
    a_rks = [jnp.where(tri_incl, ak[C:], 0.0).astype(BF16) for ak in aks]
    v_ms = [jnp.where(masks[hh], ld[ui]["v"], 0.0).astype(BF16) for ui, hh in probs]
    avs = [_dot(a_ak, v_m) for a_ak, v_m in zip(a_aks, v_ms)]
    yvs = [_dot(a_rk, v_m) for a_rk, v_m in zip(a_rks, v_ms)]
    t_invs = _tri_inverse(a_abs, eye)
    zs_ = [jnp.concatenate([jnp.where(masks[hh], ld[ui]["at"], 0.0), av], axis=1).astype(BF16)
           for av, (ui, hh) in zip(avs, probs)]
    xs = [_dot(t.astype(BF16), z) for t, z in zip(t_invs, zs_)]
    rys = [_dot(a_rb, x.astype(BF16)) for a_rb, x in zip(a_rbs, xs)]
    atps, wvs, rps, y0s = [], [], [], []
    for ui in range(len(units)):
        x0, x1 = xs[2 * ui], xs[2 * ui + 1]
        ry0, ry1 = rys[2 * ui], rys[2 * ui + 1]
        atps.append((x0[:, :LANES] + x1[:, :LANES]).astype(BF16))
        wvs.append(jnp.concatenate([x0[:, LANES:] + x1[:, LANES:], ld[ui]["v"]], axis=0).astype(BF16))
        rps.append((ld[ui]["rt"] + ry0[:, :LANES] + ry1[:, :LANES]).astype(BF16))
        y0s.append(ry0[:, LANES:] + ry1[:, LANES:] + yvs[2 * ui] + yvs[2 * ui + 1])
    pps = [jnp.where(bd_pair, _dot_tn(atp, ld[ui]["be"]), 0.0).astype(BF16)
           for ui, atp in enumerate(atps)]
    qs = [jnp.where(bd_pair, _dot_tn(wv, jnp.concatenate([ld[ui]["be"], ld[ui]["ke"]], axis=0)), 0.0)
          for ui, wv in enumerate(wvs)]

    for c in range(nb):
        rows = slice(c * C, (c + 1) * C)
        s_ps = [s_scr[p] for p in range(npair)]
        s_bs = [s_p.astype(BF16) for s_p in s_ps]
        for p in range(npair):
            ui = c * npair + p
            sl = slice(p * LANES, (p + 1) * LANES)
            y_s[rows, sl] = _dot_nt(rps[ui], s_bs[p]) + y0s[ui]
            s_scr[p] = s_ps[p] * ga_s[c * C:c * C + 1, sl] + _dot(s_bs[p], pps[ui]) + qs[ui]

    y = y_s[...]
    inv_n = 1.0 / HEAD_DIM
    mean = _seg_sum(y, bd) * inv_n
    d = y - mean
    var = _seg_sum(d * d, bd) * inv_n
    yn = d * lax.rsqrt(var + GN_EPS) * lng_ref[...] + lnb_ref[...]
    bonus = _seg_sum(r * k2 * rk_ref[...], bd) * v
    oa_ref[0] = yn + bonus

    @pl.when(c_idx == nc - 1)
    def _():
        sout_ref[0] = s_scr[...]


def _wkv(us, prev0, s0_bd, prm, tt):
    b, t, _ = us.shape
    npair = H_A // 2
    vec = lambda w: pl.BlockSpec((1, w), lambda i, j: (0, 0))
    mat = lambda: pl.BlockSpec((LANES, W_A), lambda i, j: (0, 0))
    blk = lambda dt: pltpu.VMEM((tt, W_A), dt)
    return pl.pallas_call(
        _wkv_kernel,
        grid=(b, t // tt),
        in_specs=[pl.BlockSpec((1, tt, SHIFT_W), lambda i, j: (i, j, 0)),
                  pl.BlockSpec((1, 1, SHIFT_W), lambda i, j: (i, 0, 0)),
                  pl.BlockSpec((1, npair, LANES, LANES), lambda i, j: (i, 0, 0, 0)),
                  vec(SHIFT_W), vec(W_A), mat(), vec(W_A), mat(),
                  vec(W_A), vec(W_A), vec(W_A), vec(W_A), vec(W_A)],
        out_specs=[pl.BlockSpec((1, tt, W_A), lambda i, j: (i, j, 0)),
                   pl.BlockSpec((1, npair, LANES, LANES), lambda i, j: (i, 0, 0, 0))],
        out_shape=[jax.ShapeDtypeStruct((b, t, W_A), F32),
                   jax.ShapeDtypeStruct((b, npair, LANES, LANES), F32)],
        scratch_shapes=[pltpu.VMEM((npair, LANES, LANES), F32),
                        pltpu.VMEM((1, SHIFT_W), F32),
                        blk(F32), blk(F32), blk(BF16), blk(BF16), blk(BF16), blk(BF16),
                        blk(F32), blk(F32), blk(F32)],
        compiler_params=_cparams(("parallel", "arbitrary")),
    )(us, prev0, s0_bd, prm["mu"], prm["w0"], prm["wd"], prm["a0"], prm["wa"],
      prm["kkp"], prm["ka"], prm["rk"], prm["lnx_g"], prm["lnx_b"])


_AUG = 8
_PREP_ROWS = 256


def _bias_lanes(cum, pair, head, key_side):
    h = 2 * pair + head
    r = lax.broadcasted_iota(jnp.int32, (LANES, LANES), 0)
    c = lax.broadcasted_iota(jnp.int32, (LANES, LANES), 1)
    lane = lax.broadcasted_iota(jnp.int32, (1, LANES), 1)
    base = _AUG * head
    f_off, one_off = (0, 3) if key_side else (3, 0)
    sign = -1.0 if key_side else 1.0
    out = jnp.where((lane >= base + one_off) & (lane < base + one_off + 3), 1.0, 0.0)
    for t, part in enumerate(_split3(cum)):
        sel = jnp.where((r == h) & (c == base + f_off + t), sign, 0.0).astype(BF16)
        out = out + _dot(part, sel)
    return out


def _stage_rows(total):
    return [(s, min(_PREP_ROWS, total - s)) for s in range(0, total, _PREP_ROWS)]


def _transpose_rows(x):
    n = x.shape[0]
    n_pad = -n % LANES
    if n_pad:
        x = jnp.concatenate([x, jnp.zeros((n_pad, x.shape[1]), x.dtype)], axis=0)
    return x.T[:, :n]


def _fox_kernel(*refs, tq, tk, past, single):
    if past:
        q_ref, kc_ref, vc_ref, k_ref, v_ref, cq_ref, ck_ref, o_ref, ka_scr, vt_scr, acc_scr = refs
    else:
        q_ref, k_ref, v_ref, cq_ref, ck_ref, o_ref, ka_scr, vt_scr, acc_scr = refs
    npair = H_B // 2
    lane1 = lax.broadcasted_iota(jnp.int32, (1, LANES), 1)
    head_of_lane = lane1 // HEAD_DIM

    def stage_bias(s, n):
        cum = ck_ref[0, s:s + n, :]
        for p in range(npair):
            aug = _bias_lanes(cum, p, 0, True) + _bias_lanes(cum, p, 1, True)
            ka_scr[p, s:s + n, LANES:2 * LANES] = aug.astype(BF16)

    def stage_kv():
        for s, n in _stage_rows(past):
            for p in range(npair):
                ka_scr[p, s:s + n, 0:LANES] = kc_ref[0, p * LANES:(p + 1) * LANES, s:s + n].T.astype(BF16)
            vt_scr[:, s:s + n] = vc_ref[0, :, s:s + n].astype(BF16)
            stage_bias(s, n)
        for s, n in _stage_rows(k_ref.shape[1]):
            kb = k_ref[0, s:s + n, :].astype(BF16)
            for p in range(npair):
                ka_scr[p, past + s:past + s + n, 0:LANES] = kb[:, p * LANES:(p + 1) * LANES]
            vt_scr[:, past + s:past + s + n] = _transpose_rows(v_ref[0, s:s + n, :]).astype(BF16)
            stage_bias(past + s, n)

    if single:
        i = 0
        stage_kv()
    else:
        i = pl.program_id(1)
        pl.when(i == 0)(stage_kv)

    q = q_ref[0] * (HEAD_DIM ** -0.5)
    cq = cq_ref[0]
    rhs = []
    for p in range(npair):
        q_p = q[:, p * LANES:(p + 1) * LANES]
        halves = [jnp.concatenate([jnp.where(head_of_lane == hh, q_p, 0.0),
                                   _bias_lanes(cq, p, hh, False)], axis=1) for hh in range(2)]
        rhs.append(jnp.concatenate(halves, axis=0).astype(BF16))

    n_full = past // tk + i * (tq // tk)
    diag0 = past if single else pl.multiple_of(past + i * tq, tq)
    krow = lax.broadcasted_iota(jnp.int32, (tq, 2 * tq), 0)
    qcol = lax.broadcasted_iota(jnp.int32, (tq, 2 * tq), 1)
    causal_t = krow <= jnp.where(qcol >= tq, qcol - tq, qcol)

    acc_scr[...] = jnp.zeros(acc_scr.shape, F32)

    def update(carry, start, size, mask):
        ms, ls = carry
        st = [_dot_nt(ka_scr[p, pl.ds(start, size), :], rhs[p]) for p in range(npair)]
        if mask is not None:
            st = [jnp.where(mask, s, NEG_INF) for s in st]
        m_new = [jnp.maximum(m, jnp.max(s, axis=0, keepdims=True)) for m, s in zip(ms, st)]
        alpha = [jnp.exp(m - mn) for m, mn in zip(ms, m_new)]
        pt = [jnp.exp(s - mn) for s, mn in zip(st, m_new)]
        l_new = [a * l + jnp.sum(x, axis=0, keepdims=True) for a, l, x in zip(alpha, ls, pt)]
        pv = [_dot(vt_scr[p * LANES:(p + 1) * LANES, pl.ds(start, size)], pt[p].astype(BF16))
              for p in range(npair)]
        for p in range(npair):
            acc_scr[p] = acc_scr[p] * alpha[p] + pv[p]
        return tuple(m_new), tuple(l_new)

    def body(j, carry):
        return update(carry, pl.multiple_of(j * tk, tk), tk, None)

    init = (tuple(jnp.full((1, 2 * tq), NEG_INF, F32) for _ in range(npair)),
            tuple(jnp.zeros((1, 2 * tq), F32) for _ in range(npair)))
    carry = lax.fori_loop(0, n_full, body, init)
    _, ls = update(carry, diag0, tq, causal_t)
    for p in range(npair):
        o_t = (acc_scr[p] / ls[p]).T
        o_ref[0, :, p * LANES:(p + 1) * LANES] = jnp.where(head_of_lane == 0, o_t[:tq], o_t[tq:])


def _kv_specs(k_new, cache, layer, width):
    specs, args = [], []
    t_new = k_new[0].shape[2]
    if cache is not None:
        past = cache[0].shape[3]
        for c in cache:
            specs.append(pl.BlockSpec((None, 1, width, past), lambda i, j: (layer, i, 0, 0)))
            args.append(c)
    for a in k_new:
        specs.append(pl.BlockSpec((None, 1, t_new, width), lambda i, j: (layer, i, 0, 0)))
        args.append(a)
    return specs, args


def _fox(q, kv_new, cache, cum_q, cum_k, *, layer, tq, tk, past):
    b, t, _ = q.shape
    tkk = past + kv_new[0].shape[2]
    npair = H_B // 2
    kv_specs, kv_args = _kv_specs(kv_new, cache, layer, W_B)
    return pl.pallas_call(
        functools.partial(_fox_kernel, tq=tq, tk=tk, past=past, single=(t == tq)),
        grid=(b, t // tq),
        in_specs=[pl.BlockSpec((1, tq, W_B), lambda i, j: (i, j, 0))] + kv_specs + [
                  pl.BlockSpec((1, tq, LANES), lambda i, j: (i, j, 0)),
                  pl.BlockSpec((1, tkk, LANES), lambda i, j: (i, 0, 0))],
        out_specs=pl.BlockSpec((1, tq, W_B), lambda i, j: (i, j, 0)),
        out_shape=jax.ShapeDtypeStruct((b, t, W_B), F32),
        scratch_shapes=[pltpu.VMEM((npair, tkk, 2 * LANES), BF16), pltpu.VMEM((W_B, tkk), BF16),
                        pltpu.VMEM((npair, LANES, 2 * tq), F32)],
        compiler_params=_cparams(("parallel", "arbitrary")),
    )(q, *kv_args, cum_q, cum_k)


def _sb_kernel(*refs, tq, tk, past, single):
    if past:
        q_ref, kc_ref, vc_ref, k_ref, v_ref, o_ref, kb_scr, vt_scr, acc_scr = refs
    else:
        q_ref, k_ref, v_ref, o_ref, kb_scr, vt_scr, acc_scr = refs
    npair = H_C // 2
    lane1 = lax.broadcasted_iota(jnp.int32, (1, LANES), 1)
    head_of_lane = lane1 // HEAD_DIM

    def stage_kv():
        for s, n in _stage_rows(past):
            kb_scr[s:s + n, :] = kc_ref[0, :, s:s + n].T.astype(BF16)
            vt_scr[:, s:s + n] = vc_ref[0, :, s:s + n].astype(BF16)
        for s, n in _stage_rows(k_ref.shape[1]):
            kb_scr[past + s:past + s + n, :] = k_ref[0, s:s + n, :].astype(BF16)
            vt_scr[:, past + s:past + s + n] = _transpose_rows(v_ref[0, s:s + n, :]).astype(BF16)

    if single:
        i = 0
        stage_kv()
    else:
        i = pl.program_id(1)
        pl.when(i == 0)(stage_kv)

    q = q_ref[0] * (HEAD_DIM ** -0.5)
    rhs = []
    for p in range(npair):
        q_p = q[:, p * LANES:(p + 1) * LANES]
        rhs.append(jnp.concatenate([jnp.where(head_of_lane == hh, q_p, 0.0) for hh in range(2)],
                                   axis=0).astype(BF16))

    def earlier_matrix(n):
        rr = lax.broadcasted_iota(jnp.int32, (n, n), 0)
        cc = lax.broadcasted_iota(jnp.int32, (n, n), 1)
        return (cc > rr).astype(BF16)

    n_full = past // tk + i * (tq // tk)
    diag0 = past if single else pl.multiple_of(past + i * tq, tq)
    krow = lax.broadcasted_iota(jnp.int32, (tq, 2 * tq), 0)
    qcol = lax.broadcasted_iota(jnp.int32, (tq, 2 * tq), 1)
    strict_t = krow < jnp.where(qcol >= tq, qcol - tq, qcol)

    def update(runs, start, size, mask, first):
        zt = [_dot_nt(kb_scr[pl.ds(start, size), p * LANES:(p + 1) * LANES], rhs[p])
              for p in range(npair)]
        tail = [jnp.log(1.0 + jnp.exp(-jnp.abs(z))) for z in zt]
        l1m = [-jnp.maximum(z, 0.0) - t for z, t in zip(zt, tail)]
        if mask is not None:
            l1m = [jnp.where(mask, x, 0.0) for x in l1m]
        splits = [_split2(x) for x in l1m]
        later = earlier_matrix(size)
        after = [_dot(later, hi) + _dot(later, lo) for hi, lo in splits]
        wt = [jnp.exp(jnp.minimum(z, 0.0) - t + a + r) for z, t, a, r in zip(zt, tail, after, runs)]
        if mask is not None:
            wt = [jnp.where(mask, w, 0.0) for w in wt]
        pv = [_dot(vt_scr[p * LANES:(p + 1) * LANES, pl.ds(start, size)], wt[p].astype(BF16))
              for p in range(npair)]
        for p in range(npair):
            acc_scr[p] = pv[p] if first else acc_scr[p] + pv[p]
        return tuple(r + jnp.sum(x, axis=0, keepdims=True) for r, x in zip(runs, l1m))

    runs = tuple(jnp.zeros((1, 2 * tq), F32) for _ in range(npair))
    runs = update(runs, diag0, tq, strict_t, True)

    def body(jj, runs):
        j = n_full - 1 - jj
        return update(runs, pl.multiple_of(j * tk, tk), tk, None, False)

    lax.fori_loop(0, n_full, body, runs)
    for p in range(npair):
        o_t = acc_scr[p].T
        o_ref[0, :, p * LANES:(p + 1) * LANES] = jnp.where(head_of_lane == 0, o_t[:tq], o_t[tq:])


def _sb(q, kv_new, cache, *, layer, tq, tk, past):
    b, t, _ = q.shape
    tkk = past + kv_new[0].shape[2]
    npair = H_C // 2
    kv_specs, kv_args = _kv_specs(kv_new, cache, layer, W_C)
    return pl.pallas_call(
        functools.partial(_sb_kernel, tq=tq, tk=tk, past=past, single=(t == tq)),
        grid=(b, t // tq),
        in_specs=[pl.BlockSpec((1, tq, W_C), lambda i, j: (i, j, 0))] + kv_specs,
        out_specs=pl.BlockSpec((1, tq, W_C), lambda i, j: (i, j, 0)),
        out_shape=jax.ShapeDtypeStruct((b, t, W_C), F32),
        scratch_shapes=[pltpu.VMEM((tkk, W_C), BF16), pltpu.VMEM((W_C, tkk), BF16),
                        pltpu.VMEM((npair, LANES, 2 * tq), F32)],
        compiler_params=_cparams(("parallel", "arbitrary")),
    )(q, *kv_args)


def _out_kernel(oa_ref, ob_ref, oc_ref, g_ref, x_ref, w_ref, lng_ref, lnb_ref, y_ref, *, alpha):
    g = g_ref[...]
    gate = g * _sigmoid(g)
    acc = _dot((oa_ref[...] * gate[:, 0:W_A]).astype(BF16), w_ref[0:W_A, :])
    acc = acc + _dot((ob_ref[...] * gate[:, W_A:W_A + W_B]).astype(BF16), w_ref[W_A:W_A + W_B, :])
    acc = acc + _dot((oc_ref[...] * gate[:, W_A + W_B:]).astype(BF16), w_ref[W_A + W_B:, :])
    z = alpha * x_ref[...] + acc
    mu = jnp.mean(z, axis=-1, keepdims=True)
    d = z - mu
    var = jnp.mean(d * d, axis=-1, keepdims=True)
    y_ref[...] = d * lax.rsqrt(var + LN_EPS) * lng_ref[...] + lnb_ref[...]


def _out(oa, ob, oc, g, x2d, w_out, ln_g, ln_b, alpha, tm):
    n, d = x2d.shape
    row = lambda w: pl.BlockSpec((tm, w), lambda i: (i, 0))
    return pl.pallas_call(
        functools.partial(_out_kernel, alpha=alpha),
        grid=(n // tm,),
        in_specs=[row(W_A), row(W_B), row(W_C), row(W_MIX), row(d),
                  pl.BlockSpec((W_MIX, d), lambda i: (0, 0)),
                  pl.BlockSpec((1, d), lambda i: (0, 0)),
                  pl.BlockSpec((1, d), lambda i: (0, 0))],
        out_specs=row(d),
        out_shape=jax.ShapeDtypeStruct((n, d), F32),
        compiler_params=_cparams(("parallel",)),
    )(oa, ob, oc, g, x2d, w_out, ln_g, ln_b)


def _pad_lanes(a, width=LANES):
    return jnp.pad(a, [(0, 0)] * (a.ndim - 1) + [(0, width - a.shape[-1])])


def _state_to_pairs(s):
    b = s.shape[0]
    s = s.reshape(b, H_A // 2, 2, HEAD_DIM, HEAD_DIM)
    eye2 = jnp.eye(2, dtype=s.dtype)
    bd = jnp.einsum("bphvk,hg->bphvgk", s, eye2)
    return bd.reshape(b, H_A // 2, LANES, LANES)


def _pairs_to_state(sp):
    b = sp.shape[0]
    s = sp.reshape(b, H_A // 2, 2, HEAD_DIM, 2, HEAD_DIM)
    s = jnp.stack([s[:, :, 0, :, 0, :], s[:, :, 1, :, 1, :]], axis=2)
    return s.reshape(b, H_A, HEAD_DIM, HEAD_DIM)


_IN_SIZES = (SHIFT_W, W_A, W_B, W_B, W_B, H_B, W_B, W_C, W_C, W_C, W_C)
_IN_OFFS = tuple(int(v) for v in np.concatenate([[0], np.cumsum(_IN_SIZES)]))
_SRC_SHIFT, _SRC_GA, _SRC_QB, _SRC_KB, _SRC_VB, _SRC_F, _SRC_GB, _SRC_QC, _SRC_KC, _SRC_VC, _SRC_GC = (
    _IN_OFFS[:-1])
_W_MOVES = ((_SRC_SHIFT, _OFF_SHIFT, SHIFT_W), (_SRC_GA, _OFF_G, W_A), (_SRC_GB, _OFF_G + W_A, W_B),
            (_SRC_GC, _OFF_G + W_A + W_B, W_C), (_SRC_QB, _OFF_QB, W_B), (_SRC_KB, _OFF_KB, W_B),
            (_SRC_VB, _OFF_VB, W_B), (_SRC_QC, _OFF_QC, W_C), (_SRC_KC, _OFF_KC, W_C),
            (_SRC_VC, _OFF_VC, W_C))
_W_STAGE_ROWS = 128


def _stage_w_kernel(w_ref, o_ref):
    for src, dst, width in _W_MOVES:
        o_ref[:, dst:dst + width] = w_ref[:, src:src + width].astype(BF16)
    lane = lax.broadcasted_iota(jnp.int32, (1, LANES), 1)
    o_ref[:, _OFF_F:_OFF_F + LANES] = jnp.where(
        lane < H_B, w_ref[:, _SRC_F:_SRC_F + LANES], 0.0).astype(BF16)


def _stage_w(w_in, layer):
    _, d, cols = w_in.shape
    return pl.pallas_call(
        _stage_w_kernel,
        grid=(d // _W_STAGE_ROWS,),
        in_specs=[pl.BlockSpec((None, _W_STAGE_ROWS, cols), lambda i: (layer, i, 0))],
        out_specs=pl.BlockSpec((_W_STAGE_ROWS, _PROJ_COLS), lambda i: (i, 0)),
        out_shape=jax.ShapeDtypeStruct((d, _PROJ_COLS), BF16),
        compiler_params=_cparams(("parallel",)),
    )(w_in)


def _layer_params(l, w_in, mu_shift, w0_decay, w_decay, a0, w_aaa, k_k, k_a, r_k,
                  lnx_g, lnx_b, fox_fb, w_out, ln_g, ln_b):
    w_cat = _stage_w(w_in, l)
    zeros = jnp.zeros((D_LORA, W_A), F32)
    row = lambda a: a.reshape(1, -1).astype(F32)
    return dict(
        w_cat=w_cat, fb=_pad_lanes(row(fox_fb[l])),
        mu=row(mu_shift[l]), w0=row(w0_decay[l]),
        wd=jnp.concatenate([w_decay[l], zeros], axis=0).astype(BF16),
        a0=row(a0[l]), wa=jnp.concatenate([zeros, w_aaa[l]], axis=0).astype(BF16),
        kkp=row(k_k[l]), ka=row(k_a[l]), rk=row(r_k[l]), lnx_g=row(lnx_g[l]), lnx_b=row(lnx_b[l]),
        w_out=w_out[l].astype(BF16), ln_g=row(ln_g[l]), ln_b=row(ln_b[l]))


def _run_layer(x, hist, prm, alpha, layer, depth, stacks):
    b, t, d = x.shape
    n = b * t
    x2d = x.reshape(n, d)
    tm = 256 if n % 256 == 0 else n
    us, g, qb, kb_st, vb_st, qc, kc_st, vc_st, lf = _proj(
        x2d, prm["w_cat"], prm["fb"], tm, layer, depth, stacks)
    r3 = lambda a: a.reshape(b, t, a.shape[-1])
    us, qb, qc, lf = (r3(a) for a in (us, qb, qc, lf))
    r4 = lambda a: a.reshape(depth, b, t, a.shape[-1])

    if hist is None:
        past = 0
        prev0 = jnp.zeros((b, 1, SHIFT_W), F32)
        s0 = jnp.zeros((b, H_A // 2, LANES, LANES), F32)
        cache_b = cache_c = None
        lf_all = lf
        fox_blk = (min(FOX_BLOCK, t),) * 2
        sb_blk = (min(SB_BLOCK, t),) * 2
    else:
        fk_t, fv_t, h_lf, sk_t, sv_t, h_wkv, h_shift = hist
        past = fk_t.shape[3]
        prev0 = h_shift
        s0 = _state_to_pairs(h_wkv)
        cache_b, cache_c = (fk_t, fv_t), (sk_t, sv_t)
        lf_all = jnp.concatenate([_pad_lanes(h_lf), lf], axis=1)
        fox_blk = sb_blk = (t, min(SB_BLOCK, past))

    oa, s_fin = _wkv(us, prev0, s0, prm, min(WKV_BLOCK, t))
    cum = _cumsum(lf_all)
    ob = _fox(qb, (r4(kb_st), r4(vb_st)), cache_b, cum[:, past:], cum, layer=layer,
              tq=fox_blk[0], tk=fox_blk[1], past=past)
    oc = _sb(qc, (r4(kc_st), r4(vc_st)), cache_c, layer=layer, tq=sb_blk[0], tk=sb_blk[1], past=past)
    y = _out(oa.reshape(n, W_A), ob.reshape(n, W_B), oc.reshape(n, W_C), g, x2d,
             prm["w_out"], prm["ln_g"], prm["ln_b"], alpha, tm)
    small = (lf[:, :, :H_B], _pairs_to_state(s_fin), us[:, -1:, :])
    return y.reshape(b, t, d), (kb_st, vb_st, kc_st, vc_st), small


def kernel(x_prompt, x_sample, cache_fox_k, cache_fox_v, cache_fox_logf, cache_sb_k, cache_sb_v, state_wkv, state_shift, w_in, mu_shift, w0_decay, w_decay, a0, w_aaa, k_k, k_a, r_k, lnx_g, lnx_b, fox_fb, w_out, ln_g, ln_b):
    depth = w_in.shape[0]
    alpha = (2 * depth) ** 0.25
    yp, ys = x_prompt, x_sample

    def time_on_lanes(cache):
        nl, nb, past, nh, hd = cache.shape
        return jnp.transpose(cache, (0, 1, 3, 4, 2)).reshape(nl, nb, nh * hd, past)

    fk_t, fv_t, sk_t, sv_t = (time_on_lanes(c) for c in (cache_fox_k, cache_fox_v, cache_sb_k, cache_sb_v))
    stacks_p, stacks_s = None, None
    small_p, small_s = [], []
    for l in range(depth):
        prm = _layer_params(l, w_in, mu_shift, w0_decay, w_decay, a0, w_aaa, k_k, k_a, r_k,
                            lnx_g, lnx_b, fox_fb, w_out, ln_g, ln_b)
        yp, stacks_p, sm_p = _run_layer(yp, None, prm, alpha, l, depth, stacks_p)
        hist = (fk_t, fv_t, cache_fox_logf[l], sk_t, sv_t, state_wkv[l], state_shift[l])
        ys, stacks_s, sm_s = _run_layer(ys, hist, prm, alpha, l, depth, stacks_s)
        small_p.append(sm_p)
        small_s.append(sm_s)

    def group_outputs(x, stacks, small):
        b, t, _ = x.shape
        kb, vb, kc, vc = stacks
        logf, wkv, shift = (jnp.stack([sm[i] for sm in small]) for i in range(3))
        return (kb.reshape(depth, b, t, H_B, HEAD_DIM), vb.reshape(depth, b, t, H_B, HEAD_DIM), logf,
                kc.reshape(depth, b, t, H_C, HEAD_DIM), vc.reshape(depth, b, t, H_C, HEAD_DIM),
                wkv, shift)

    return ((yp, ys) + group_outputs(x_prompt, stacks_p, small_p)
            + group_outputs(x_sample, stacks_s, small_s))
```

```python
import functools

import numpy as np
import jax
import jax.numpy as jnp
from jax import lax
from jax.experimental import pallas as pl
from jax.experimental.pallas import tpu as pltpu

F32 = jnp.float32
BF16 = jnp.bfloat16

HEAD_DIM = 64
H_A, H_B, H_C = 6, 6, 4
W_A, W_B, W_C = H_A * HEAD_DIM, H_B * HEAD_DIM, H_C * HEAD_DIM
W_MIX = W_A + W_B + W_C
D_LORA = 64
SHIFT_W = 3 * W_A + 2 * D_LORA
LANES = 128
WKV_CHUNK = 64
WKV_BLOCK = 256
FOX_BLOCK = 512
SB_BLOCK = 256
GN_EPS = 64e-5
LN_EPS = 1e-5
NEG_INF = -1e30
VMEM_LIMIT = 56 * 1024 * 1024


def _cparams(sem):
    return pltpu.CompilerParams(dimension_semantics=sem, vmem_limit_bytes=VMEM_LIMIT)


def _dot(a, b):
    return jnp.dot(a, b, preferred_element_type=F32)


def _dot_nt(a, b):
    return lax.dot_general(a, b, (((1,), (1,)), ((), ())), preferred_element_type=F32)


def _dot_tn(a, b):
    return lax.dot_general(a, b, (((0,), (0,)), ((), ())), preferred_element_type=F32)


def _split2(x):
    hi = x.astype(BF16)
    lo = (x - hi.astype(F32)).astype(BF16)
    return hi, lo


def _split3(x):
    hi = x.astype(BF16)
    r1 = x - hi.astype(F32)
    mid = r1.astype(BF16)
    lo = (r1 - mid.astype(F32)).astype(BF16)
    return hi, mid, lo


def _softplus(x):
    return jnp.maximum(x, 0.0) + jnp.log1p(jnp.exp(-jnp.abs(x)))


def _sigmoid(x):
    return 1.0 / (1.0 + jnp.exp(-x))


_OFF_SHIFT = 0
_OFF_G = _OFF_SHIFT + SHIFT_W
_OFF_QB = _OFF_G + W_MIX
_OFF_KB = _OFF_QB + W_B
_OFF_VB = _OFF_KB + W_B
_OFF_QC = _OFF_VB + W_B
_OFF_KC = _OFF_QC + W_C
_OFF_VC = _OFF_KC + W_C
_OFF_F = _OFF_VC + W_C
_PROJ_COLS = _OFF_F + LANES


def _proj_kernel(x_ref, w_ref, fb_ref, *refs):
    us_ref, g_ref, qb_ref, kb_ref, vb_ref, qc_ref, kc_ref, vc_ref, lf_ref = refs[-9:]
    xb = x_ref[...].astype(BF16)
    for ref, off in ((us_ref, _OFF_SHIFT), (g_ref, _OFF_G), (qb_ref, _OFF_QB),
                     (kb_ref, _OFF_KB), (vb_ref, _OFF_VB), (qc_ref, _OFF_QC),
                     (kc_ref, _OFF_KC), (vc_ref, _OFF_VC)):
        width = ref.shape[-1]
        ref[...] = _dot(xb, w_ref[:, off:off + width])
    f = _dot(xb, w_ref[:, _OFF_F:_OFF_F + LANES]) + fb_ref[...]
    lf_ref[...] = -_softplus(-f)


_PROJ_WIDTHS = (SHIFT_W, W_MIX, W_B, W_B, W_B, W_C, W_C, W_C, LANES)
_PROJ_STACKED = (3, 4, 6, 7)


def _proj(x2d, w_cat, fb_pad, tm, layer, depth, stacks):
    n, d = x2d.shape
    out_specs, out_shape = [], []
    for idx, w in enumerate(_PROJ_WIDTHS):
        if idx in _PROJ_STACKED:
            out_specs.append(pl.BlockSpec((None, tm, w), lambda i: (layer, i, 0)))
            out_shape.append(jax.ShapeDtypeStruct((depth, n, w), F32))
        else:
            out_specs.append(pl.BlockSpec((tm, w), lambda i: (i, 0)))
            out_shape.append(jax.ShapeDtypeStruct((n, w), F32))
    in_specs = [pl.BlockSpec((tm, d), lambda i: (i, 0)),
                pl.BlockSpec((d, _PROJ_COLS), lambda i: (0, 0)),
                pl.BlockSpec((1, LANES), lambda i: (0, 0))]
    args = [x2d, w_cat, fb_pad]
    aliases = {}
    if stacks is not None:
        for j, (idx, st) in enumerate(zip(_PROJ_STACKED, stacks)):
            in_specs.append(pl.BlockSpec(memory_space=pl.ANY))
            args.append(st)
            aliases[3 + j] = idx
    return pl.pallas_call(
        _proj_kernel,
        grid=(n // tm,),
        in_specs=in_specs,
        out_specs=out_specs,
        out_shape=out_shape,
        input_output_aliases=aliases,
        compiler_params=_cparams(("parallel",)),
    )(*args)


def _cumsum_kernel(lf_ref, cum_ref, *, blk):
    tk = lf_ref.shape[1]
    row = lax.broadcasted_iota(jnp.int32, (blk, blk), 0)
    col = lax.broadcasted_iota(jnp.int32, (blk, blk), 1)
    tri = (col <= row).astype(BF16)
    carry = jnp.zeros((1, LANES), F32)
    for s in range(0, tk, blk):
        x = lf_ref[0, s:s + blk, :]
        hi, mid, lo = _split3(x)
        c = _dot(tri, hi) + _dot(tri, mid) + _dot(tri, lo) + carry
        cum_ref[0, s:s + blk, :] = c
        carry = c[blk - 1:blk, :]


def _cumsum(lf, blk=64):
    b, tk, _ = lf.shape
    return pl.pallas_call(
        functools.partial(_cumsum_kernel, blk=blk),
        grid=(b,),
        in_specs=[pl.BlockSpec((1, tk, LANES), lambda i: (i, 0, 0))],
        out_specs=pl.BlockSpec((1, tk, LANES), lambda i: (i, 0, 0)),
        out_shape=jax.ShapeDtypeStruct((b, tk, LANES), F32),
        compiler_params=_cparams(("parallel",)),
    )(lf)


def _seg_sum(x, bd):
    xb = x.astype(BF16)
    return jnp.concatenate([_dot(xb[:, s:s + LANES], bd) for s in range(0, x.shape[1], LANES)], axis=1)


def _wkv_kernel(us_ref, prev0_ref, s0_ref, mu_ref, w0_ref, wd_ref, a0_ref, wa_ref,
                kkp_ref, ka_ref, rk_ref, lng_ref, lnb_ref,
                oa_ref, sout_ref,
                s_scr, prev_scr, at_s, rt_s, bh_s, kh_s, be_s, ke_s, v_s, ga_s, y_s):
    c_idx = pl.program_id(1)
    nc = pl.num_programs(1)
    C = WKV_CHUNK
    tt = us_ref.shape[1]
    nb = tt // C

    @pl.when(c_idx == 0)
    def _():
        s_scr[...] = s0_ref[0]
        prev_scr[...] = prev0_ref[0]

    u = us_ref[0]
    row1 = lax.broadcasted_iota(jnp.int32, (tt, 1), 0)
    zprev = jnp.where(row1 == 0, prev_scr[...], pltpu.roll(u, 1, axis=0))
    prev_scr[...] = u[tt - 1:tt, :]
    zs = u + (zprev - u) * mu_ref[...]
    r = zs[:, 0:W_A]
    k = zs[:, W_A:2 * W_A]
    v = zs[:, 2 * W_A:3 * W_A]
    lora_in = zs[:, 3 * W_A:3 * W_A + LANES]
    lane1 = lax.broadcasted_iota(jnp.int32, (1, LANES), 1)
    lora_t = jnp.where(lane1 < D_LORA, jnp.tanh(lora_in), lora_in).astype(BF16)
    wl = w0_ref[...] + _dot(lora_t, wd_ref[...])
    w_log = -_softplus(-wl) - 0.5
    lw = -jnp.exp(w_log)
    a = _sigmoid(a0_ref[...] + _dot(lora_t, wa_ref[...]))

    lane_w = lax.broadcasted_iota(jnp.int32, (LANES, LANES), 0) // HEAD_DIM
    lane_c = lax.broadcasted_iota(jnp.int32, (LANES, LANES), 1) // HEAD_DIM
    bd = (lane_w == lane_c).astype(BF16)

    kk = k * kkp_ref[...]
    kk = kk * lax.rsqrt(_seg_sum(kk * kk, bd) + 1e-12)
    k2 = k * (1.0 + (a - 1.0) * ka_ref[...])

    rowt = lax.broadcasted_iota(jnp.int32, (tt, tt), 0)
    colt = lax.broadcasted_iota(jnp.int32, (tt, tt), 1)
    tri_b = (((rowt // C) == (colt // C)) & (colt <= rowt)).astype(BF16)
    h1, h2, h3 = _split3(lw)
    cs = _dot(tri_b, h1) + _dot(tri_b, h2) + _dot(tri_b, h3)
    tot = jnp.concatenate([jnp.broadcast_to(cs[c * C + C - 1:c * C + C, :], (C, W_A))
                           for c in range(nb)], axis=0)
    g_inv = jnp.exp(-cs)
    g_end = jnp.exp(tot - cs)
    kka = kk * a
    at_s[...] = kk * jnp.exp(cs - lw)
    rt_s[...] = r * jnp.exp(cs)
    bh_s[...] = (-kka * g_inv).astype(BF16)
    kh_s[...] = (k2 * g_inv).astype(BF16)
    be_s[...] = (-kka * g_end).astype(BF16)
    ke_s[...] = (k2 * g_end).astype(BF16)
    v_s[...] = v
    ga_s[...] = jnp.exp(tot)

    rowp = lax.broadcasted_iota(jnp.int32, (LANES, LANES), 0) // HEAD_DIM
    colp = lax.broadcasted_iota(jnp.int32, (LANES, LANES), 1) // HEAD_DIM
    bd_pair = rowp == colp
    npair = H_A // 2
    G = 4 * HEAD_DIM
    row_g = lax.broadcasted_iota(jnp.int32, (C, G), 0)
    lane_g = lax.broadcasted_iota(jnp.int32, (C, G), 1) % C
    tri_incl = lane_g <= row_g
    tri_strict = lane_g < row_g
    eye = (lane_g == row_g).astype(F32)
    diag4 = (lax.broadcasted_iota(jnp.int32, (G, G), 0) // HEAD_DIM
             == lax.broadcasted_iota(jnp.int32, (G, G), 1) // HEAD_DIM)

    def stack4(x):
        xb = x.astype(BF16)
        return jnp.where(diag4, jnp.concatenate([xb, xb, xb, xb], axis=0), jnp.zeros((), BF16))

    def mul4(xs, ys):
        return [_dot(x.astype(BF16), stack4(y)) for x, y in zip(xs, ys)]

    groups = []
    for c in range(nb):
        groups.append([(c, 0), (c, LANES)])
    for c in range(0, nb, 2):
        groups.append([(c, 2 * LANES)] + ([(c + 1, 2 * LANES)] if c + 1 < nb else [None]))

    def gather(ref, grp):
        parts = [jnp.zeros((C, LANES), ref.dtype) if u is None
                 else ref[u[0] * C:(u[0] + 1) * C, u[1]:u[1] + LANES] for u in grp]
        return jnp.concatenate(parts, axis=1)

    at_g = [gather(at_s, g) for g in groups]
    rt_g = [gather(rt_s, g) for g in groups]
    v_g = [gather(v_s, g) for g in groups]
    lhs = [jnp.concatenate([a, r], axis=0).astype(BF16) for a, r in zip(at_g, rt_g)]
    rhs = [jnp.concatenate([stack4(gather(bh_s, g)), stack4(gather(kh_s, g))], axis=0) for g in groups]
    a4 = [_dot_nt(l, r) for l, r in zip(lhs, rhs)]
    n1 = [jnp.where(tri_strict, a[:C, :G], 0.0) for a in a4]
    a_ak = [jnp.where(tri_strict, a[:C, G:], 0.0) for a in a4]
    a_rb = [jnp.where(tri_incl, a[C:, :G], 0.0) for a in a4]
    a_rk = [jnp.where(tri_incl, a[C:, G:], 0.0) for a in a4]
    vst = [stack4(v) for v in v_g]
    avy = [_dot(jnp.concatenate([ak, rk], axis=0).astype(BF16), vs)
           for ak, rk, vs in zip(a_ak, a_rk, vst)]
    pair4 = lambda xs, ys: [eye + x + y + m for x, y, m in zip(xs, ys, mul4(xs, ys))]
    n2 = mul4(n1, n1)
    n4 = mul4(n2, n2)
    p1 = pair4(n1, n2)
    n8 = mul4(n4, n4)
    n16 = mul4(n8, n8)
    p2 = pair4(n4, n8)
    n32 = mul4(n16, n16)
    p12 = mul4(p1, p2)
    p3 = pair4(n16, n32)
    t_inv = mul4(p12, p3)
    xs = [_dot(t.astype(BF16), jnp.concatenate([stack4(a), stack4(y[:C])], axis=1))
          for t, a, y in zip(t_inv, at_g, avy)]
    rys = [_dot(rb.astype(BF16), jnp.concatenate([stack4(x[:, :G]), stack4(x[:, G:])], axis=1))
           for rb, x in zip(a_rb, xs)]
    rp_g = [r + ry[:, :G] for r, ry in zip(rt_g, rys)]
    y0_g = [ry[:, G:] + y[C:] for ry, y in zip(rys, avy)]

    pps, qs, rps, y0s = {}, {}, {}, {}
    for gi, grp in enumerate(groups):
        for k, u in enumerate(grp):
            if u is None:
                continue
            c, p = u[0], u[1] // LANES
            rows = slice(c * C, (c + 1) * C)
            sl = slice(p * LANES, (p + 1) * LANES)
            gl = slice(k * LANES, (k + 1) * LANES)
            be_u, ke_u = be_s[rows, sl], ke_s[rows, sl]
            atp = xs[gi][:, gl].astype(BF16)
            wv = jnp.concatenate([xs[gi][:, G + k * LANES:G + (k + 1) * LANES], v_s[rows, sl]],
                                 axis=0).astype(BF16)
            pps[c, p] = jnp.where(bd_pair, _dot_tn(atp, be_u), 0.0).astype(BF16)
            qs[c, p] = jnp.where(bd_pair, _dot_tn(wv, jnp.concatenate([be_u, ke_u], axis=0)), 0.0)
            rps[c, p] = rp_g[gi][:, gl].astype(BF16)
            y0s[c, p] = y0_g[gi][:, gl]

    for c in range(nb):
        rows = slice(c * C, (c + 1) * C)
        s_ps = [s_scr[p] for p in range(npair)]
        s_bs = [s_p.astype(BF16) for s_p in s_ps]
        for p in range(npair):
            sl = slice(p * LANES, (p + 1) * LANES)
            y_s[rows, sl] = _dot_nt(rps[c, p], s_bs[p]) + y0s[c, p]
            s_scr[p] = s_ps[p] * ga_s[c * C:c * C + 1, sl] + _dot(s_bs[p], pps[c, p]) + qs[c, p]

    y = y_s[...]
    inv_n = 1.0 / HEAD_DIM
    mean = _seg_sum(y, bd) * inv_n
    d = y - mean
    var = _seg_sum(d * d, bd) * inv_n
    yn = d * lax.rsqrt(var + GN_EPS) * lng_ref[...] + lnb_ref[...]
    bonus = _seg_sum(r * k2 * rk_ref[...], bd) * v
    oa_ref[0] = yn + bonus

    @pl.when(c_idx == nc - 1)
    def _():
        sout_ref[0] = s_scr[...]


def _wkv(us, prev0, s0_bd, prm, tt):
    b, t, _ = us.shape
    npair = H_A // 2
    vec = lambda w: pl.BlockSpec((1, w), lambda i, j: (0, 0))
    mat = lambda: pl.BlockSpec((LANES, W_A), lambda i, j: (0, 0))
    blk = lambda dt: pltpu.VMEM((tt, W_A), dt)
    return pl.pallas_call(
        _wkv_kernel,
        grid=(b, t // tt),
        in_specs=[pl.BlockSpec((1, tt, SHIFT_W), lambda i, j: (i, j, 0)),
                  pl.BlockSpec((1, 1, SHIFT_W), lambda i, j: (i, 0, 0)),
                  pl.BlockSpec((1, npair, LANES, LANES), lambda i, j: (i, 0, 0, 0)),
                  vec(SHIFT_W), vec(W_A), mat(), vec(W_A), mat(),
                  vec(W_A), vec(W_A), vec(W_A), vec(W_A), vec(W_A)],
        out_specs=[pl.BlockSpec((1, tt, W_A), lambda i, j: (i, j, 0)),
                   pl.BlockSpec((1, npair, LANES, LANES), lambda i, j: (i, 0, 0, 0))],
        out_shape=[jax.ShapeDtypeStruct((b, t, W_A), F32),
                   jax.ShapeDtypeStruct((b, npair, LANES, LANES), F32)],
        scratch_shapes=[pltpu.VMEM((npair, LANES, LANES), F32),
                        pltpu.VMEM((1, SHIFT_W), F32),
                        blk(F32), blk(F32), blk(BF16), blk(BF16), blk(BF16), blk(BF16),
                        blk(F32), blk(F32), blk(F32)],
        compiler_params=_cparams(("parallel", "arbitrary")),
    )(us, prev0, s0_bd, prm["mu"], prm["w0"], prm["wd"], prm["a0"], prm["wa"],
      prm["kkp"], prm["ka"], prm["rk"], prm["lnx_g"], prm["lnx_b"])


_AUG = 8
_PREP_ROWS = 256


def _bias_lanes(cum, pair, head, key_side):
    h = 2 * pair + head
    r = lax.broadcasted_iota(jnp.int32, (LANES, LANES), 0)
    c = lax.broadcasted_iota(jnp.int32, (LANES, LANES), 1)
    lane = lax.broadcasted_iota(jnp.int32, (1, LANES), 1)
    base = _AUG * head
    f_off, one_off = (0, 3) if key_side else (3, 0)
    sign = -1.0 if key_side else 1.0
    out = jnp.where((lane >= base + one_off) & (lane < base + one_off + 3), 1.0, 0.0)
    for t, part in enumerate(_split3(cum)):
        sel = jnp.where((r == h) & (c == base + f_off + t), sign, 0.0).astype(BF16)
        out = out + _dot(part, sel)
    return out


def _stage_rows(total):
    return [(s, min(_PREP_ROWS, total - s)) for s in range(0, total, _PREP_ROWS)]


def _transpose_rows(x):
    n = x.shape[0]
    n_pad = -n % LANES
    if n_pad:
        x = jnp.concatenate([x, jnp.zeros((n_pad, x.shape[1]), x.dtype)], axis=0)
    return x.T[:, :n]


def _fox_kernel(*refs, tq, tk, past, single):
    if past:
        q_ref, kc_ref, vc_ref, k_ref, v_ref, cq_ref, ck_ref, o_ref, ka_scr, vt_scr, acc_scr = refs
    else:
        q_ref, k_ref, v_ref, cq_ref, ck_ref, o_ref, ka_scr, vt_scr, acc_scr = refs
    npair = H_B // 2
    lane1 = lax.broadcasted_iota(jnp.int32, (1, LANES), 1)
    head_of_lane = lane1 // HEAD_DIM

    def stage_bias(s, n):
        cum = ck_ref[0, s:s + n, :]
        for p in range(npair):
            aug = _bias_lanes(cum, p, 0, True) + _bias_lanes(cum, p, 1, True)
            ka_scr[p, s:s + n, LANES:2 * LANES] = aug.astype(BF16)

    def stage_kv():
        for s, n in _stage_rows(past):
            for p in range(npair):
                ka_scr[p, s:s + n, 0:LANES] = kc_ref[0, p * LANES:(p + 1) * LANES, s:s + n].T.astype(BF16)
            vt_scr[:, s:s + n] = vc_ref[0, :, s:s + n].astype(BF16)
            stage_bias(s, n)
        for s, n in _stage_rows(k_ref.shape[1]):
            kb = k_ref[0, s:s + n, :].astype(BF16)
            for p in range(npair):
                ka_scr[p, past + s:past + s + n, 0:LANES] = kb[:, p * LANES:(p + 1) * LANES]
            vt_scr[:, past + s:past + s + n] = _transpose_rows(v_ref[0, s:s + n, :]).astype(BF16)
            stage_bias(past + s, n)

    if single:
        i = 0
        stage_kv()
    else:
        i = pl.program_id(1)
        pl.when(i == 0)(stage_kv)

    q = q_ref[0] * (HEAD_DIM ** -0.5)
    cq = cq_ref[0]
    rhs = []
    for p in range(npair):
        q_p = q[:, p * LANES:(p + 1) * LANES]
        halves = [jnp.concatenate([jnp.where(head_of_lane == hh, q_p, 0.0),
                                   _bias_lanes(cq, p, hh, False)], axis=1) for hh in range(2)]
        rhs.append(jnp.concatenate(halves, axis=0).astype(BF16))

    n_full = past // tk + i * (tq // tk)
    diag0 = past if single else pl.multiple_of(past + i * tq, tq)
    krow = lax.broadcasted_iota(jnp.int32, (tq, 2 * tq), 0)
    qcol = lax.broadcasted_iota(jnp.int32, (tq, 2 * tq), 1)
    causal_t = krow <= jnp.where(qcol >= tq, qcol - tq, qcol)

    acc_scr[...] = jnp.zeros(acc_scr.shape, F32)

    def update(carry, start, size, mask):
        ms, ls = carry
        st = [_dot_nt(ka_scr[p, pl.ds(start, size), :], rhs[p]) for p in range(npair)]
        if mask is not None:
            st = [jnp.where(mask, s, NEG_INF) for s in st]
        m_new = [jnp.maximum(m, jnp.max(s, axis=0, keepdims=True)) for m, s in zip(ms, st)]
        alpha = [jnp.exp(m - mn) for m, mn in zip(ms, m_new)]
        pt = [jnp.exp(s - mn) for s, mn in zip(st, m_new)]
        l_new = [a * l + jnp.sum(x, axis=0, keepdims=True) for a, l, x in zip(alpha, ls, pt)]
        pv = [_dot(vt_scr[p * LANES:(p + 1) * LANES, pl.ds(start, size)], pt[p].astype(BF16))
              for p in range(npair)]
        for p in range(npair):
            acc_scr[p] = acc_scr[p] * alpha[p] + pv[p]
        return tuple(m_new), tuple(l_new)

    def body(j, carry):
        return update(carry, pl.multiple_of(j * tk, tk), tk, None)

    init = (tuple(jnp.full((1, 2 * tq), NEG_INF, F32) for _ in range(npair)),
            tuple(jnp.zeros((1, 2 * tq), F32) for _ in range(npair)))
    carry = lax.fori_loop(0, n_full, body, init)
    _, ls = update(carry, diag0, tq, causal_t)
    for p in range(npair):
        o_t = (acc_scr[p] / ls[p]).T
        o_ref[0, :, p * LANES:(p + 1) * LANES] = jnp.where(head_of_lane == 0, o_t[:tq], o_t[tq:])


def _kv_specs(k_new, cache, layer, width):
    specs, args = [], []
    t_new = k_new[0].shape[2]
    if cache is not None:
        past = cache[0].shape[3]
        for c in cache:
            specs.append(pl.BlockSpec((None, 1, width, past), lambda i, j: (layer, i, 0, 0)))
            args.append(c)
    for a in k_new:
        specs.append(pl.BlockSpec((None, 1, t_new, width), lambda i, j: (layer, i, 0, 0)))
        args.append(a)
    return specs, args


def _fox(q, kv_new, cache, cum_q, cum_k, *, layer, tq, tk, past):
    b, t, _ = q.shape
    tkk = past + kv_new[0].shape[2]
    npair = H_B // 2
    kv_specs, kv_args = _kv_specs(kv_new, cache, layer, W_B)
    return pl.pallas_call(
        functools.partial(_fox_kernel, tq=tq, tk=tk, past=past, single=(t == tq)),
        grid=(b, t // tq),
        in_specs=[pl.BlockSpec((1, tq, W_B), lambda i, j: (i, j, 0))] + kv_specs + [
                  pl.BlockSpec((1, tq, LANES), lambda i, j: (i, j, 0)),
                  pl.BlockSpec((1, tkk, LANES), lambda i, j: (i, 0, 0))],
        out_specs=pl.BlockSpec((1, tq, W_B), lambda i, j: (i, j, 0)),
        out_shape=jax.ShapeDtypeStruct((b, t, W_B), F32),
        scratch_shapes=[pltpu.VMEM((npair, tkk, 2 * LANES), BF16), pltpu.VMEM((W_B, tkk), BF16),
                        pltpu.VMEM((npair, LANES, 2 * tq), F32)],
        compiler_params=_cparams(("parallel", "arbitrary")),
    )(q, *kv_args, cum_q, cum_k)


def _sb_kernel(*refs, tq, tk, past, single):
    if past:
        q_ref, kc_ref, vc_ref, k_ref, v_ref, o_ref, kb_scr, vt_scr, acc_scr = refs
    else:
        q_ref, k_ref, v_ref, o_ref, kb_scr, vt_scr, acc_scr = refs
    npair = H_C // 2
    lane1 = lax.broadcasted_iota(jnp.int32, (1, LANES), 1)
    head_of_lane = lane1 // HEAD_DIM

    def stage_kv():
        for s, n in _stage_rows(past):
            kb_scr[s:s + n, :] = kc_ref[0, :, s:s + n].T.astype(BF16)
            vt_scr[:, s:s + n] = vc_ref[0, :, s:s + n].astype(BF16)
        for s, n in _stage_rows(k_ref.shape[1]):
            kb_scr[past + s:past + s + n, :] = k_ref[0, s:s + n, :].astype(BF16)
            vt_scr[:, past + s:past + s + n] = _transpose_rows(v_ref[0, s:s + n, :]).astype(BF16)

    if single:
        i = 0
        stage_kv()
    else:
        i = pl.program_id(1)
        pl.when(i == 0)(stage_kv)

    q = q_ref[0] * (HEAD_DIM ** -0.5)
    rhs = []
    for p in range(npair):
        q_p = q[:, p * LANES:(p + 1) * LANES]
        rhs.append(jnp.concatenate([jnp.where(head_of_lane == hh, q_p, 0.0) for hh in range(2)],
                                   axis=0).astype(BF16))

    def earlier_matrix(n):
        rr = lax.broadcasted_iota(jnp.int32, (n, n), 0)
        cc = lax.broadcasted_iota(jnp.int32, (n, n), 1)
        return (cc > rr).astype(BF16)

    n_full = past // tk + i * (tq // tk)
    diag0 = past if single else pl.multiple_of(past + i * tq, tq)
    krow = lax.broadcasted_iota(jnp.int32, (tq, 2 * tq), 0)
    qcol = lax.broadcasted_iota(jnp.int32, (tq, 2 * tq), 1)
    strict_t = krow < jnp.where(qcol >= tq, qcol - tq, qcol)

    def update(runs, start, size, mask, first):
        zt = [_dot_nt(kb_scr[pl.ds(start, size), p * LANES:(p + 1) * LANES], rhs[p])
              for p in range(npair)]
        tail = [jnp.log(1.0 + jnp.exp(-jnp.abs(z))) for z in zt]
        l1m = [-jnp.maximum(z, 0.0) - t for z, t in zip(zt, tail)]
        if mask is not None:
            l1m = [jnp.where(mask, x, 0.0) for x in l1m]
        splits = [_split2(x) for x in l1m]
        later = earlier_matrix(size)
        after = [_dot(later, hi) + _dot(later, lo) for hi, lo in splits]
        wt = [jnp.exp(jnp.minimum(z, 0.0) - t + a + r) for z, t, a, r in zip(zt, tail, after, runs)]
        if mask is not None:
            wt = [jnp.where(mask, w, 0.0) for w in wt]
        pv = [_dot(vt_scr[p * LANES:(p + 1) * LANES, pl.ds(start, size)], wt[p].astype(BF16))
              for p in range(npair)]
        for p in range(npair):
            acc_scr[p] = pv[p] if first else acc_scr[p] + pv[p]
        return tuple(r + jnp.sum(x, axis=0, keepdims=True) for r, x in zip(runs, l1m))

    runs = tuple(jnp.zeros((1, 2 * tq), F32) for _ in range(npair))
    runs = update(runs, diag0, tq, strict_t, True)

    def body(jj, runs):
        j = n_full - 1 - jj
        return update(runs, pl.multiple_of(j * tk, tk), tk, None, False)

    lax.fori_loop(0, n_full, body, runs)
    for p in range(npair):
        o_t = acc_scr[p].T
        o_ref[0, :, p * LANES:(p + 1) * LANES] = jnp.where(head_of_lane == 0, o_t[:tq], o_t[tq:])


def _sb(q, kv_new, cache, *, layer, tq, tk, past):
    b, t, _ = q.shape
    tkk = past + kv_new[0].shape[2]
    npair = H_C // 2
    kv_specs, kv_args = _kv_specs(kv_new, cache, layer, W_C)
    return pl.pallas_call(
        functools.partial(_sb_kernel, tq=tq, tk=tk, past=past, single=(t == tq)),
        grid=(b, t // tq),
        in_specs=[pl.BlockSpec((1, tq, W_C), lambda i, j: (i, j, 0))] + kv_specs,
        out_specs=pl.BlockSpec((1, tq, W_C), lambda i, j: (i, j, 0)),
        out_shape=jax.ShapeDtypeStruct((b, t, W_C), F32),
        scratch_shapes=[pltpu.VMEM((tkk, W_C), BF16), pltpu.VMEM((W_C, tkk), BF16),
                        pltpu.VMEM((npair, LANES, 2 * tq), F32)],
        compiler_params=_cparams(("parallel", "arbitrary")),
    )(q, *kv_args)


def _out_kernel(oa_ref, ob_ref, oc_ref, g_ref, x_ref, w_ref, lng_ref, lnb_ref, y_ref, *, alpha):
    g = g_ref[...]
    gate = g * _sigmoid(g)
    acc = _dot((oa_ref[...] * gate[:, 0:W_A]).astype(BF16), w_ref[0:W_A, :])
    acc = acc + _dot((ob_ref[...] * gate[:, W_A:W_A + W_B]).astype(BF16), w_ref[W_A:W_A + W_B, :])
    acc = acc + _dot((oc_ref[...] * gate[:, W_A + W_B:]).astype(BF16), w_ref[W_A + W_B:, :])
    z = alpha * x_ref[...] + acc
    mu = jnp.mean(z, axis=-1, keepdims=True)
    d = z - mu
    var = jnp.mean(d * d, axis=-1, keepdims=True)
    y_ref[...] = d * lax.rsqrt(var + LN_EPS) * lng_ref[...] + lnb_ref[...]


def _out(oa, ob, oc, g, x2d, w_out, ln_g, ln_b, alpha, tm):
    n, d = x2d.shape
    row = lambda w: pl.BlockSpec((tm, w), lambda i: (i, 0))
    return pl.pallas_call(
        functools.partial(_out_kernel, alpha=alpha),
        grid=(n // tm,),
        in_specs=[row(W_A), row(W_B), row(W_C), row(W_MIX), row(d),
                  pl.BlockSpec((W_MIX, d), lambda i: (0, 0)),
                  pl.BlockSpec((1, d), lambda i: (0, 0)),
                  pl.BlockSpec((1, d), lambda i: (0, 0))],
        out_specs=row(d),
        out_shape=jax.ShapeDtypeStruct((n, d), F32),
        compiler_params=_cparams(("parallel",)),
    )(oa, ob, oc, g, x2d, w_out, ln_g, ln_b)


def _pad_lanes(a, width=LANES):
    return jnp.pad(a, [(0, 0)] * (a.ndim - 1) + [(0, width - a.shape[-1])])


def _state_to_pairs(s):
    b = s.shape[0]
    s = s.reshape(b, H_A // 2, 2, HEAD_DIM, HEAD_DIM)
    eye2 = jnp.eye(2, dtype=s.dtype)
    bd = jnp.einsum("bphvk,hg->bphvgk", s, eye2)
    return bd.reshape(b, H_A // 2, LANES, LANES)


def _pairs_to_state(sp):
    b = sp.shape[0]
    s = sp.reshape(b, H_A // 2, 2, HEAD_DIM, 2, HEAD_DIM)
    s = jnp.stack([s[:, :, 0, :, 0, :], s[:, :, 1, :, 1, :]], axis=2)
    return s.reshape(b, H_A, HEAD_DIM, HEAD_DIM)


_IN_SIZES = (SHIFT_W, W_A, W_B, W_B, W_B, H_B, W_B, W_C, W_C, W_C, W_C)
_IN_OFFS = tuple(int(v) for v in np.concatenate([[0], np.cumsum(_IN_SIZES)]))
_SRC_SHIFT, _SRC_GA, _SRC_QB, _SRC_KB, _SRC_VB, _SRC_F, _SRC_GB, _SRC_QC, _SRC_KC, _SRC_VC, _SRC_GC = (
    _IN_OFFS[:-1])
_W_MOVES = ((_SRC_SHIFT, _OFF_SHIFT, SHIFT_W), (_SRC_GA, _OFF_G, W_A), (_SRC_GB, _OFF_G + W_A, W_B),
            (_SRC_GC, _OFF_G + W_A + W_B, W_C), (_SRC_QB, _OFF_QB, W_B), (_SRC_KB, _OFF_KB, W_B),
            (_SRC_VB, _OFF_VB, W_B), (_SRC_QC, _OFF_QC, W_C), (_SRC_KC, _OFF_KC, W_C),
            (_SRC_VC, _OFF_VC, W_C))
_W_STAGE_ROWS = 128


def _stage_w_kernel(w_ref, o_ref):
    for src, dst, width in _W_MOVES:
        o_ref[:, dst:dst + width] = w_ref[:, src:src + width].astype(BF16)
    lane = lax.broadcasted_iota(jnp.int32, (1, LANES), 1)
    o_ref[:, _OFF_F:_OFF_F + LANES] = jnp.where(
        lane < H_B, w_ref[:, _SRC_F:_SRC_F + LANES], 0.0).astype(BF16)


def _stage_w(w_in, layer):
    _, d, cols = w_in.shape
    return pl.pallas_call(
        _stage_w_kernel,
        grid=(d // _W_STAGE_ROWS,),
        in_specs=[pl.BlockSpec((None, _W_STAGE_ROWS, cols), lambda i: (layer, i, 0))],
        out_specs=pl.BlockSpec((_W_STAGE_ROWS, _PROJ_COLS), lambda i: (i, 0)),
        out_shape=jax.ShapeDtypeStruct((d, _PROJ_COLS), BF16),
        compiler_params=_cparams(("parallel",)),
    )(w_in)


def _layer_params(l, w_in, mu_shift, w0_decay, w_decay, a0, w_aaa, k_k, k_a, r_k,
                  lnx_g, lnx_b, fox_fb, w_out, ln_g, ln_b):
    w_cat = _stage_w(w_in, l)
    zeros = jnp.zeros((D_LORA, W_A), F32)
    row = lambda a: a.reshape(1, -1).astype(F32)
    return dict(
        w_cat=w_cat, fb=_pad_lanes(row(fox_fb[l])),
        mu=row(mu_shift[l]), w0=row(w0_decay[l]),
        wd=jnp.concatenate([w_decay[l], zeros], axis=0).astype(BF16),
        a0=row(a0[l]), wa=jnp.concatenate([zeros, w_aaa[l]], axis=0).astype(BF16),
        kkp=row(k_k[l]), ka=row(k_a[l]), rk=row(r_k[l]), lnx_g=row(lnx_g[l]), lnx_b=row(lnx_b[l]),
        w_out=w_out[l].astype(BF16), ln_g=row(ln_g[l]), ln_b=row(ln_b[l]))


def _run_layer(x, hist, prm, alpha, layer, depth, stacks):
    b, t, d = x.shape
    n = b * t
    x2d = x.reshape(n, d)
    tm = 256 if n % 256 == 0 else n
    us, g, qb, kb_st, vb_st, qc, kc_st, vc_st, lf = _proj(
        x2d, prm["w_cat"], prm["fb"], tm, layer, depth, stacks)
    r3 = lambda a: a.reshape(b, t, a.shape[-1])
    us, qb, qc, lf = (r3(a) for a in (us, qb, qc, lf))
    r4 = lambda a: a.reshape(depth, b, t, a.shape[-1])

    if hist is None:
        past = 0
        prev0 = jnp.zeros((b, 1, SHIFT_W), F32)
        s0 = jnp.zeros((b, H_A // 2, LANES, LANES), F32)
        cache_b = cache_c = None
        lf_all = lf
        fox_blk = (min(FOX_BLOCK, t),) * 2
        sb_blk = (min(SB_BLOCK, t),) * 2
    else:
        fk_t, fv_t, h_lf, sk_t, sv_t, h_wkv, h_shift = hist
        past = fk_t.shape[3]
        prev0 = h_shift
        s0 = _state_to_pairs(h_wkv)
        cache_b, cache_c = (fk_t, fv_t), (sk_t, sv_t)
        lf_all = jnp.concatenate([_pad_lanes(h_lf), lf], axis=1)
        fox_blk = sb_blk = (t, min(SB_BLOCK, past))

    oa, s_fin = _wkv(us, prev0, s0, prm, min(WKV_BLOCK, t))
    cum = _cumsum(lf_all)
    ob = _fox(qb, (r4(kb_st), r4(vb_st)), cache_b, cum[:, past:], cum, layer=layer,
              tq=fox_blk[0], tk=fox_blk[1], past=past)
    oc = _sb(qc, (r4(kc_st), r4(vc_st)), cache_c, layer=layer, tq=sb_blk[0], tk=sb_blk[1], past=past)
    y = _out(oa.reshape(n, W_A), ob.reshape(n, W_B), oc.reshape(n, W_C), g, x2d,
             prm["w_out"], prm["ln_g"], prm["ln_b"], alpha, tm)
    small = (lf[:, :, :H_B], _pairs_to_state(s_fin), us[:, -1:, :])
    return y.reshape(b, t, d), (kb_st, vb_st, kc_st, vc_st), small


def kernel(x_prompt, x_sample, cache_fox_k, cache_fox_v, cache_fox_logf, cache_sb_k, cache_sb_v, state_wkv, state_shift, w_in, mu_shift, w0_decay, w_decay, a0, w_aaa, k_k, k_a, r_k, lnx_g, lnx_b, fox_fb, w_out, ln_g, ln_b):
    depth = w_in.shape[0]
    alpha = (2 * depth) ** 0.25
    yp, ys = x_prompt, x_sample

    def time_on_lanes(cache):
        nl, nb, past, nh, hd = cache.shape
        return jnp.transpose(cache, (0, 1, 3, 4, 2)).reshape(nl, nb, nh * hd, past)

    fk_t, fv_t, sk_t, sv_t = (time_on_lanes(c) for c in (cache_fox_k, cache_fox_v, cache_sb_k, cache_sb_v))
    stacks_p, stacks_s = None, None
    small_p, small_s = [], []
    for l in range(depth):
        prm = _layer_params(l, w_in, mu_shift, w0_decay, w_decay, a0, w_aaa, k_k, k_a, r_k,
                            lnx_g, lnx_b, fox_fb, w_out, ln_g, ln_b)
        yp, stacks_p, sm_p = _run_layer(yp, None, prm, alpha, l, depth, stacks_p)
        hist = (fk_t, fv_t, cache_fox_logf[l], sk_t, sv_t, state_wkv[l], state_shift[l])
        ys, stacks_s, sm_s = _run_layer(ys, hist, prm, alpha, l, depth, stacks_s)
        small_p.append(sm_p)
        small_s.append(sm_s)

    def group_outputs(x, stacks, small):
        b, t, _ = x.shape
        kb, vb, kc, vc = stacks
        logf, wkv, shift = (jnp.stack([sm[i] for sm in small]) for i in range(3))
        return (kb.reshape(depth, b, t, H_B, HEAD_DIM), vb.reshape(depth, b, t, H_B, HEAD_DIM), logf,
                kc.reshape(depth, b, t, H_C, HEAD_DIM), vc.reshape(depth, b, t, H_C, HEAD_DIM),
                wkv, shift)

    return ((yp, ys) + group_outputs(x_prompt, stacks_p, small_p)
            + group_outputs(x_sample, stacks_s, small_s))
```

```python
import functools

import numpy as np
import jax
import jax.numpy as jnp
from jax import lax
from jax.experimental import pallas as pl
from jax.experimental.pallas import tpu as pltpu

F32 = jnp.float32
BF16 = jnp.bfloat16

HEAD_DIM = 64
H_A, H_B, H_C = 6, 6, 4
W_A, W_B, W_C = H_A * HEAD_DIM, H_B * HEAD_DIM, H_C * HEAD_DIM
W_MIX = W_A + W_B + W_C
D_LORA = 64
SHIFT_W = 3 * W_A + 2 * D_LORA
LANES = 128
WKV_CHUNK = 64
WKV_BLOCK = 256
FOX_BLOCK = 512
SB_BLOCK = 512
GN_EPS = 64e-5
LN_EPS = 1e-5
NEG_INF = -1e30
LOG2E = 1.4426950408889634
VMEM_LIMIT = 56 * 1024 * 1024


def _cparams(sem):
    return pltpu.CompilerParams(dimension_semantics=sem, vmem_limit_bytes=VMEM_LIMIT)


def _dot(a, b):
    return jnp.dot(a, b, preferred_element_type=F32)


def _dot_nt(a, b):
    return lax.dot_general(a, b, (((1,), (1,)), ((), ())), preferred_element_type=F32)


def _dot_tn(a, b):
    return lax.dot_general(a, b, (((0,), (0,)), ((), ())), preferred_element_type=F32)


def _split2(x):
    hi = x.astype(BF16)
    lo = (x - hi.astype(F32)).astype(BF16)
    return hi, lo


def _split3(x):
    hi = x.astype(BF16)
    r1 = x - hi.astype(F32)
    mid = r1.astype(BF16)
    lo = (r1 - mid.astype(F32)).astype(BF16)
    return hi, mid, lo


def _softplus(x):
    return jnp.maximum(x, 0.0) + jnp.log1p(jnp.exp(-jnp.abs(x)))


def _sigmoid(x):
    return 1.0 / (1.0 + jnp.exp(-x))


_OFF_SHIFT = 0
_OFF_G = _OFF_SHIFT + SHIFT_W
_OFF_QB = _OFF_G + W_MIX
_OFF_KB = _OFF_QB + W_B
_OFF_VB = _OFF_KB + W_B
_OFF_QC = _OFF_VB + W_B
_OFF_KC = _OFF_QC + W_C
_OFF_VC = _OFF_KC + W_C
_OFF_F = _OFF_VC + W_C
_PROJ_COLS = _OFF_F + LANES


def _proj_kernel(x_ref, w_ref, fb_ref, *refs):
    us_ref, g_ref, qb_ref, kb_ref, vb_ref, qc_ref, kc_ref, vc_ref, lf_ref = refs[-9:]
    xb = x_ref[...].astype(BF16)
    for ref, off in ((us_ref, _OFF_SHIFT), (g_ref, _OFF_G), (qb_ref, _OFF_QB),
                     (kb_ref, _OFF_KB), (vb_ref, _OFF_VB), (qc_ref, _OFF_QC),
                     (kc_ref, _OFF_KC), (vc_ref, _OFF_VC)):
        width = ref.shape[-1]
        ref[...] = _dot(xb, w_ref[:, off:off + width])
    f = _dot(xb, w_ref[:, _OFF_F:_OFF_F + LANES]) + fb_ref[...]
    lf_ref[...] = -_softplus(-f)


_PROJ_WIDTHS = (SHIFT_W, W_MIX, W_B, W_B, W_B, W_C, W_C, W_C, LANES)
_PROJ_STACKED = (3, 4, 6, 7)


def _proj(x2d, w_cat, fb_pad, tm, layer, depth, stacks):
    n, d = x2d.shape
    out_specs, out_shape = [], []
    for idx, w in enumerate(_PROJ_WIDTHS):
        if idx in _PROJ_STACKED:
            out_specs.append(pl.BlockSpec((None, tm, w), lambda i: (layer, i, 0)))
            out_shape.append(jax.ShapeDtypeStruct((depth, n, w), F32))
        else:
            out_specs.append(pl.BlockSpec((tm, w), lambda i: (i, 0)))
            out_shape.append(jax.ShapeDtypeStruct((n, w), F32))
    in_specs = [pl.BlockSpec((tm, d), lambda i: (i, 0)),
                pl.BlockSpec((d, _PROJ_COLS), lambda i: (0, 0)),
                pl.BlockSpec((1, LANES), lambda i: (0, 0))]
    args = [x2d, w_cat, fb_pad]
    aliases = {}
    if stacks is not None:
        for j, (idx, st) in enumerate(zip(_PROJ_STACKED, stacks)):
            in_specs.append(pl.BlockSpec(memory_space=pl.ANY))
            args.append(st)
            aliases[3 + j] = idx
    return pl.pallas_call(
        _proj_kernel,
        grid=(n // tm,),
        in_specs=in_specs,
        out_specs=out_specs,
        out_shape=out_shape,
        input_output_aliases=aliases,
        compiler_params=_cparams(("parallel",)),
    )(*args)


def _cumsum_kernel(lf_ref, cum_ref, *, blk):
    tk = lf_ref.shape[1]
    row = lax.broadcasted_iota(jnp.int32, (blk, blk), 0)
    col = lax.broadcasted_iota(jnp.int32, (blk, blk), 1)
    tri = (col <= row).astype(BF16)
    carry = jnp.zeros((1, LANES), F32)
    for s in range(0, tk, blk):
        x = lf_ref[0, s:s + blk, :]
        hi, mid, lo = _split3(x)
        c = _dot(tri, hi) + _dot(tri, mid) + _dot(tri, lo) + carry
        cum_ref[0, s:s + blk, :] = c
        carry = c[blk - 1:blk, :]


def _cumsum(lf, blk=64):
    b, tk, _ = lf.shape
    return pl.pallas_call(
        functools.partial(_cumsum_kernel, blk=blk),
        grid=(b,),
        in_specs=[pl.BlockSpec((1, tk, LANES), lambda i: (i, 0, 0))],
        out_specs=pl.BlockSpec((1, tk, LANES), lambda i: (i, 0, 0)),
        out_shape=jax.ShapeDtypeStruct((b, tk, LANES), F32),
        compiler_params=_cparams(("parallel",)),
    )(lf)


def _seg_sum(x, bd):
    xb = x.astype(BF16)
    return jnp.concatenate([_dot(xb[:, s:s + LANES], bd) for s in range(0, x.shape[1], LANES)], axis=1)


def _wkv_kernel(us_ref, prev0_ref, s0_ref, mu_ref, w0_ref, wd_ref, a0_ref, wa_ref,
                kkp_ref, ka_ref, rk_ref, lng_ref, lnb_ref,
                oa_ref, sout_ref,
                s_scr, prev_scr, at_s, rt_s, bh_s, kh_s, be_s, ke_s, v_s, ga_s, y_s):
    c_idx = pl.program_id(1)
    nc = pl.num_programs(1)
    C = WKV_CHUNK
    tt = us_ref.shape[1]
    nb = tt // C

    @pl.when(c_idx == 0)
    def _():
        s_scr[...] = s0_ref[0]
        prev_scr[...] = prev0_ref[0]

    u = us_ref[0]
    row1 = lax.broadcasted_iota(jnp.int32, (tt, 1), 0)
    zprev = jnp.where(row1 == 0, prev_scr[...], pltpu.roll(u, 1, axis=0))
    prev_scr[...] = u[tt - 1:tt, :]
    zs = u + (zprev - u) * mu_ref[...]
    r = zs[:, 0:W_A]
    k = zs[:, W_A:2 * W_A]
    v = zs[:, 2 * W_A:3 * W_A]
    lora_in = zs[:, 3 * W_A:3 * W_A + LANES]
    lane1 = lax.broadcasted_iota(jnp.int32, (1, LANES), 1)
    lora_t = jnp.where(lane1 < D_LORA, jnp.tanh(lora_in), lora_in).astype(BF16)
    wl = w0_ref[...] + _dot(lora_t, wd_ref[...])
    w_log = -_softplus(-wl) - 0.5
    lw = -jnp.exp(w_log)
    a = _sigmoid(a0_ref[...] + _dot(lora_t, wa_ref[...]))

    lane_w = lax.broadcasted_iota(jnp.int32, (LANES, LANES), 0) // HEAD_DIM
    lane_c = lax.broadcasted_iota(jnp.int32, (LANES, LANES), 1) // HEAD_DIM
    bd = (lane_w == lane_c).astype(BF16)

    kk = k * kkp_ref[...]
    kk = kk * lax.rsqrt(_seg_sum(kk * kk, bd) + 1e-12)
    k2 = k * (1.0 + (a - 1.0) * ka_ref[...])

    rowt = lax.broadcasted_iota(jnp.int32, (tt, tt), 0)
    colt = lax.broadcasted_iota(jnp.int32, (tt, tt), 1)
    tri_b = (((rowt // C) == (colt // C)) & (colt <= rowt)).astype(BF16)
    h1, h2, h3 = _split3(lw)
    cs = _dot(tri_b, h1) + _dot(tri_b, h2) + _dot(tri_b, h3)
    tot = jnp.concatenate([jnp.broadcast_to(cs[c * C + C - 1:c * C + C, :], (C, W_A))
                           for c in range(nb)], axis=0)
    g_inv = jnp.exp(-cs)
    g_end = jnp.exp(tot - cs)
    kka = kk * a
    at_s[...] = kk * jnp.exp(cs - lw)
    rt_s[...] = r * jnp.exp(cs)
    bh_s[...] = (-kka * g_inv).astype(BF16)
    kh_s[...] = (k2 * g_inv).astype(BF16)
    be_s[...] = (-kka * g_end).astype(BF16)
    ke_s[...] = (k2 * g_end).astype(BF16)
    v_s[...] = v
    ga_s[...] = jnp.exp(tot)

    rowp = lax.broadcasted_iota(jnp.int32, (LANES, LANES), 0) // HEAD_DIM
    colp = lax.broadcasted_iota(jnp.int32, (LANES, LANES), 1) // HEAD_DIM
    bd_pair = rowp == colp
    npair = H_A // 2
    G = 4 * HEAD_DIM
    row_g = lax.broadcasted_iota(jnp.int32, (C, G), 0)
    lane_g = lax.broadcasted_iota(jnp.int32, (C, G), 1) % C
    tri_incl = lane_g <= row_g
    tri_strict = lane_g < row_g
    eye = (lane_g == row_g).astype(F32)
    diag4 = (lax.broadcasted_iota(jnp.int32, (G, G), 0) // HEAD_DIM
             == lax.broadcasted_iota(jnp.int32, (G, G), 1) // HEAD_DIM)

    def stack4(x):
        xb = x.astype(BF16)
        return jnp.where(diag4, jnp.concatenate([xb, xb, xb, xb], axis=0), jnp.zeros((), BF16))

    def mul4(xs, ys):
        return [_dot(x.astype(BF16), stack4(y)) for x, y in zip(xs, ys)]

    groups = []
    for c in range(nb):
        groups.append([(c, 0), (c, LANES)])
    for c in range(0, nb, 2):
        groups.append([(c, 2 * LANES)] + ([(c + 1, 2 * LANES)] if c + 1 < nb else [None]))

    def gather(ref, grp):
        parts = [jnp.zeros((C, LANES), ref.dtype) if u is None
                 else ref[u[0] * C:(u[0] + 1) * C, u[1]:u[1] + LANES] for u in grp]
        return jnp.concatenate(parts, axis=1)

    at_g = [gather(at_s, g) for g in groups]
    rt_g = [gather(rt_s, g) for g in groups]
    v_g = [gather(v_s, g) for g in groups]
    lhs = [jnp.concatenate([a, r], axis=0).astype(BF16) for a, r in zip(at_g, rt_g)]
    rhs = [jnp.concatenate([stack4(gather(bh_s, g)), stack4(gather(kh_s, g))], axis=0) for g in groups]
    a4 = [_dot_nt(l, r) for l, r in zip(lhs, rhs)]
    n1 = [jnp.where(tri_strict, a[:C, :G], 0.0) for a in a4]
    a_ak = [jnp.where(tri_strict, a[:C, G:], 0.0) for a in a4]
    a_rb = [jnp.where(tri_incl, a[C:, :G], 0.0) for a in a4]
    a_rk = [jnp.where(tri_incl, a[C:, G:], 0.0) for a in a4]
    vst = [stack4(v) for v in v_g]
    avy = [_dot(jnp.concatenate([ak, rk], axis=0).astype(BF16), vs)
           for ak, rk, vs in zip(a_ak, a_rk, vst)]
    pair4 = lambda xs, ys: [eye + x + y + m for x, y, m in zip(xs, ys, mul4(xs, ys))]
    n2 = mul4(n1, n1)
    n4 = mul4(n2, n2)
    p1 = pair4(n1, n2)
    n8 = mul4(n4, n4)
    n16 = mul4(n8, n8)
    p2 = pair4(n4, n8)
    n32 = mul4(n16, n16)
    p12 = mul4(p1, p2)
    p3 = pair4(n16, n32)
    t_inv = mul4(p12, p3)
    xs = [_dot(t.astype(BF16), jnp.concatenate([stack4(a), stack4(y[:C])], axis=1))
          for t, a, y in zip(t_inv, at_g, avy)]
    rys = [_dot(rb.astype(BF16), jnp.concatenate([stack4(x[:, :G]), stack4(x[:, G:])], axis=1))
           for rb, x in zip(a_rb, xs)]
    rp_g = [r + ry[:, :G] for r, ry in zip(rt_g, rys)]
    y0_g = [ry[:, G:] + y[C:] for ry, y in zip(rys, avy)]

    pps, qs, rps, y0s = {}, {}, {}, {}
    for gi, grp in enumerate(groups):
        for k, u in enumerate(grp):
            if u is None:
                continue
            c, p = u[0], u[1] // LANES
            rows = slice(c * C, (c + 1) * C)
            sl = slice(p * LANES, (p + 1) * LANES)
            gl = slice(k * LANES, (k + 1) * LANES)
            be_u, ke_u = be_s[rows, sl], ke_s[rows, sl]
            atp = xs[gi][:, gl].astype(BF16)
            wv = jnp.concatenate([xs[gi][:, G + k * LANES:G + (k + 1) * LANES], v_s[rows, sl]],
                                 axis=0).astype(BF16)
            pps[c, p] = jnp.where(bd_pair, _dot_tn(atp, be_u), 0.0).astype(BF16)
            qs[c, p] = jnp.where(bd_pair, _dot_tn(wv, jnp.concatenate([be_u, ke_u], axis=0)), 0.0)
            rps[c, p] = rp_g[gi][:, gl].astype(BF16)
            y0s[c, p] = y0_g[gi][:, gl]

    for c in range(nb):
        rows = slice(c * C, (c + 1) * C)
        s_ps = [s_scr[p] for p in range(npair)]
        s_bs = [s_p.astype(BF16) for s_p in s_ps]
        for p in range(npair):
            sl = slice(p * LANES, (p + 1) * LANES)
            y_s[rows, sl] = _dot_nt(rps[c, p], s_bs[p]) + y0s[c, p]
            s_scr[p] = s_ps[p] * ga_s[c * C:c * C + 1, sl] + _dot(s_bs[p], pps[c, p]) + qs[c, p]

    y = y_s[...]
    inv_n = 1.0 / HEAD_DIM
    mean = _seg_sum(y, bd) * inv_n
    d = y - mean
    var = _seg_sum(d * d, bd) * inv_n
    yn = d * lax.rsqrt(var + GN_EPS) * lng_ref[...] + lnb_ref[...]
    bonus = _seg_sum(r * k2 * rk_ref[...], bd) * v
    oa_ref[0] = yn + bonus

    @pl.when(c_idx == nc - 1)
    def _():
        sout_ref[0] = s_scr[...]


def _wkv(us, prev0, s0_bd, prm, tt):
    b, t, _ = us.shape
    npair = H_A // 2
    vec = lambda w: pl.BlockSpec((1, w), lambda i, j: (0, 0))
    mat = lambda: pl.BlockSpec((LANES, W_A), lambda i, j: (0, 0))
    blk = lambda dt: pltpu.VMEM((tt, W_A), dt)
    return pl.pallas_call(
        _wkv_kernel,
        grid=(b, t // tt),
        in_specs=[pl.BlockSpec((1, tt, SHIFT_W), lambda i, j: (i, j, 0)),
                  pl.BlockSpec((1, 1, SHIFT_W), lambda i, j: (i, 0, 0)),
                  pl.BlockSpec((1, npair, LANES, LANES), lambda i, j: (i, 0, 0, 0)),
                  vec(SHIFT_W), vec(W_A), mat(), vec(W_A), mat(),
                  vec(W_A), vec(W_A), vec(W_A), vec(W_A), vec(W_A)],
        out_specs=[pl.BlockSpec((1, tt, W_A), lambda i, j: (i, j, 0)),
                   pl.BlockSpec((1, npair, LANES, LANES), lambda i, j: (i, 0, 0, 0))],
        out_shape=[jax.ShapeDtypeStruct((b, t, W_A), F32),
                   jax.ShapeDtypeStruct((b, npair, LANES, LANES), F32)],
        scratch_shapes=[pltpu.VMEM((npair, LANES, LANES), F32),
                        pltpu.VMEM((1, SHIFT_W), F32),
                        blk(F32), blk(F32), blk(BF16), blk(BF16), blk(BF16), blk(BF16),
                        blk(F32), blk(F32), blk(F32)],
        compiler_params=_cparams(("parallel", "arbitrary")),
    )(us, prev0, s0_bd, prm["mu"], prm["w0"], prm["wd"], prm["a0"], prm["wa"],
      prm["kkp"], prm["ka"], prm["rk"], prm["lnx_g"], prm["lnx_b"])


_AUG = 8
_PREP_ROWS = 256


def _bias_lanes(cum, pair, head, key_side):
    h = 2 * pair + head
    r = lax.broadcasted_iota(jnp.int32, (LANES, LANES), 0)
    c = lax.broadcasted_iota(jnp.int32, (LANES, LANES), 1)
    lane = lax.broadcasted_iota(jnp.int32, (1, LANES), 1)
    base = _AUG * head
    f_off, one_off = (0, 3) if key_side else (3, 0)
    sign = -1.0 if key_side else 1.0
    out = jnp.where((lane >= base + one_off) & (lane < base + one_off + 3), 1.0, 0.0)
    for t, part in enumerate(_split3(cum)):
        sel = jnp.where((r == h) & (c == base + f_off + t), sign, 0.0).astype(BF16)
        out = out + _dot(part, sel)
    return out


def _stage_rows(total):
    return [(s, min(_PREP_ROWS, total - s)) for s in range(0, total, _PREP_ROWS)]


def _transpose_rows(x):
    n = x.shape[0]
    n_pad = -n % LANES
    if n_pad:
        x = jnp.concatenate([x, jnp.zeros((n_pad, x.shape[1]), x.dtype)], axis=0)
    return x.T[:, :n]


def _fox_kernel(*refs, tq, tk, past, single):
    if past:
        q_ref, kc_ref, vc_ref, k_ref, v_ref, cq_ref, ck_ref, o_ref, ka_scr, vt_scr, acc_scr = refs
    else:
        q_ref, k_ref, v_ref, cq_ref, ck_ref, o_ref, ka_scr, vt_scr, acc_scr = refs
    npair = H_B // 2
    lane1 = lax.broadcasted_iota(jnp.int32, (1, LANES), 1)
    head_of_lane = lane1 // HEAD_DIM

    def stage_bias(s, n):
        cum = ck_ref[0, s:s + n, :] * LOG2E
        for p in range(npair):
            aug = _bias_lanes(cum, p, 0, True) + _bias_lanes(cum, p, 1, True)
            ka_scr[p, s:s + n, LANES:2 * LANES] = aug.astype(BF16)

    def stage_kv():
        for s, n in _stage_rows(past):
            for p in range(npair):
                ka_scr[p, s:s + n, 0:LANES] = kc_ref[0, p * LANES:(p + 1) * LANES, s:s + n].T.astype(BF16)
            vt_scr[:, s:s + n] = vc_ref[0, :, s:s + n].astype(BF16)
            stage_bias(s, n)
        for s, n in _stage_rows(k_ref.shape[1]):
            kb = k_ref[0, s:s + n, :].astype(BF16)
            for p in range(npair):
                ka_scr[p, past + s:past + s + n, 0:LANES] = kb[:, p * LANES:(p + 1) * LANES]
            vt_scr[:, past + s:past + s + n] = _transpose_rows(v_ref[0, s:s + n, :]).astype(BF16)
            stage_bias(past + s, n)

    if single:
        i = 0
        stage_kv()
    else:
        i = pl.program_id(1)
        pl.when(i == 0)(stage_kv)

    q = q_ref[0] * (HEAD_DIM ** -0.5 * LOG2E)
    cq = cq_ref[0] * LOG2E
    rhs = []
    for p in range(npair):
        q_p = q[:, p * LANES:(p + 1) * LANES]
        halves = [jnp.concatenate([jnp.where(head_of_lane == hh, q_p, 0.0),
                                   _bias_lanes(cq, p, hh, False)], axis=1) for hh in range(2)]
        rhs.append(jnp.concatenate(halves, axis=0).astype(BF16))

    n_full = past // tk + i * (tq // tk)
    diag0 = past if single else pl.multiple_of(past + i * tq, tq)
    krow = lax.broadcasted_iota(jnp.int32, (tq, 2 * tq), 0)
    qcol = lax.broadcasted_iota(jnp.int32, (tq, 2 * tq), 1)
    causal_t = krow <= jnp.where(qcol >= tq, qcol - tq, qcol)

    acc_scr[...] = jnp.zeros(acc_scr.shape, F32)

    def update(carry, start, size, mask):
        ms, ls = carry
        st = [_dot_nt(ka_scr[p, pl.ds(start, size), :], rhs[p]) for p in range(npair)]
        if mask is not None:
            st = [jnp.where(mask, s, NEG_INF) for s in st]
        m_new = [jnp.maximum(m, jnp.max(s, axis=0, keepdims=True)) for m, s in zip(ms, st)]
        alpha = [jnp.exp2(m - mn) for m, mn in zip(ms, m_new)]
        pt = [jnp.exp2(s - mn) for s, mn in zip(st, m_new)]
        l_new = [a * l + jnp.sum(x, axis=0, keepdims=True) for a, l, x in zip(alpha, ls, pt)]
        pv = [_dot(vt_scr[p * LANES:(p + 1) * LANES, pl.ds(start, size)], pt[p].astype(BF16))
              for p in range(npair)]
        for p in range(npair):
            acc_scr[p] = acc_scr[p] * alpha[p] + pv[p]
        return tuple(m_new), tuple(l_new)

    def body(j, carry):
        return update(carry, pl.multiple_of(j * tk, tk), tk, None)

    init = (tuple(jnp.full((1, 2 * tq), NEG_INF, F32) for _ in range(npair)),
            tuple(jnp.zeros((1, 2 * tq), F32) for _ in range(npair)))
    carry = lax.fori_loop(0, n_full, body, init)
    _, ls = update(carry, diag0, tq, causal_t)
    for p in range(npair):
        o_t = (acc_scr[p] / ls[p]).T
        o_ref[0, :, p * LANES:(p + 1) * LANES] = jnp.where(head_of_lane == 0, o_t[:tq], o_t[tq:])


def _kv_specs(k_new, cache, layer, width):
    specs, args = [], []
    t_new = k_new[0].shape[2]
    if cache is not None:
        past = cache[0].shape[3]
        for c in cache:
            specs.append(pl.BlockSpec((None, 1, width, past), lambda i, j: (layer, i, 0, 0)))
            args.append(c)
    for a in k_new:
        specs.append(pl.BlockSpec((None, 1, t_new, width), lambda i, j: (layer, i, 0, 0)))
        args.append(a)
    return specs, args


def _fox(q, kv_new, cache, cum_q, cum_k, *, layer, tq, tk, past):
    b, t, _ = q.shape
    tkk = past + kv_new[0].shape[2]
    npair = H_B // 2
    kv_specs, kv_args = _kv_specs(kv_new, cache, layer, W_B)
    return pl.pallas_call(
        functools.partial(_fox_kernel, tq=tq, tk=tk, past=past, single=(t == tq)),
        grid=(b, t // tq),
        in_specs=[pl.BlockSpec((1, tq, W_B), lambda i, j: (i, j, 0))] + kv_specs + [
                  pl.BlockSpec((1, tq, LANES), lambda i, j: (i, j, 0)),
                  pl.BlockSpec((1, tkk, LANES), lambda i, j: (i, 0, 0))],
        out_specs=pl.BlockSpec((1, tq, W_B), lambda i, j: (i, j, 0)),
        out_shape=jax.ShapeDtypeStruct((b, t, W_B), F32),
        scratch_shapes=[pltpu.VMEM((npair, tkk, 2 * LANES), BF16), pltpu.VMEM((W_B, tkk), BF16),
                        pltpu.VMEM((npair, LANES, 2 * tq), F32)],
        compiler_params=_cparams(("parallel", "arbitrary")),
    )(q, *kv_args, cum_q, cum_k)


def _sb_kernel(*refs, tq, tk, past, single):
    if past:
        q_ref, kc_ref, vc_ref, k_ref, v_ref, o_ref, kb_scr, vt_scr, acc_scr = refs
    else:
        q_ref, k_ref, v_ref, o_ref, kb_scr, vt_scr, acc_scr = refs
    npair = H_C // 2
    lane1 = lax.broadcasted_iota(jnp.int32, (1, LANES), 1)
    head_of_lane = lane1 // HEAD_DIM

    def stage_kv():
        for s, n in _stage_rows(past):
            kb_scr[s:s + n, :] = kc_ref[0, :, s:s + n].T.astype(BF16)
            vt_scr[:, s:s + n] = vc_ref[0, :, s:s + n].astype(BF16)
        for s, n in _stage_rows(k_ref.shape[1]):
            kb_scr[past + s:past + s + n, :] = k_ref[0, s:s + n, :].astype(BF16)
            vt_scr[:, past + s:past + s + n] = _transpose_rows(v_ref[0, s:s + n, :]).astype(BF16)

    if single:
        i = 0
        stage_kv()
    else:
        i = pl.program_id(1)
        pl.when(i == 0)(stage_kv)

    q = q_ref[0] * (HEAD_DIM ** -0.5 * LOG2E)
    rhs = []
    for p in range(npair):
        q_p = q[:, p * LANES:(p + 1) * LANES]
        rhs.append(jnp.concatenate([jnp.where(head_of_lane == hh, q_p, 0.0) for hh in range(2)],
                                   axis=0).astype(BF16))

    def earlier_matrix(n):
        rr = lax.broadcasted_iota(jnp.int32, (n, n), 0)
        cc = lax.broadcasted_iota(jnp.int32, (n, n), 1)
        return (cc > rr).astype(BF16)

    n_full = past // tk + i * (tq // tk)
    diag0 = past if single else pl.multiple_of(past + i * tq, tq)
    krow = lax.broadcasted_iota(jnp.int32, (tq, 2 * tq), 0)
    qcol = lax.broadcasted_iota(jnp.int32, (tq, 2 * tq), 1)
    strict_t = krow < jnp.where(qcol >= tq, qcol - tq, qcol)

    def update(runs, start, size, mask, first):
        zt = [_dot_nt(kb_scr[pl.ds(start, size), p * LANES:(p + 1) * LANES], rhs[p])
              for p in range(npair)]
        mz = [jnp.minimum(z, 0.0) for z in zt]
        tail = [jnp.log(1.0 + jnp.exp2(m + m - z)) * LOG2E for m, z in zip(mz, zt)]
        lsig = [m - t for m, t in zip(mz, tail)]
        l1m = [s - z for s, z in zip(lsig, zt)]
        if mask is not None:
            l1m = [jnp.where(mask, x, 0.0) for x in l1m]
        later = earlier_matrix(size)
        after = [_dot(later, x.astype(BF16)) for x in l1m]
        wt = [jnp.exp2(s + a + r) for s, a, r in zip(lsig, after, runs)]
        if mask is not None:
            wt = [jnp.where(mask, w, 0.0) for w in wt]
        pv = [_dot(vt_scr[p * LANES:(p + 1) * LANES, pl.ds(start, size)], wt[p].astype(BF16))
              for p in range(npair)]
        for p in range(npair):
            acc_scr[p] = pv[p] if first else acc_scr[p] + pv[p]
        return tuple(r + jnp.sum(x, axis=0, keepdims=True) for r, x in zip(runs, l1m))

    runs = tuple(jnp.zeros((1, 2 * tq), F32) for _ in range(npair))
    runs = update(runs, diag0, tq, strict_t, True)

    def body(jj, runs):
        j = n_full - 1 - jj
        return update(runs, pl.multiple_of(j * tk, tk), tk, None, False)

    lax.fori_loop(0, n_full, body, runs)
    for p in range(npair):
        o_t = acc_scr[p].T
        o_ref[0, :, p * LANES:(p + 1) * LANES] = jnp.where(head_of_lane == 0, o_t[:tq], o_t[tq:])


def _sb(q, kv_new, cache, *, layer, tq, tk, past):
    b, t, _ = q.shape
    tkk = past + kv_new[0].shape[2]
    npair = H_C // 2
    kv_specs, kv_args = _kv_specs(kv_new, cache, layer, W_C)
    return pl.pallas_call(
        functools.partial(_sb_kernel, tq=tq, tk=tk, past=past, single=(t == tq)),
        grid=(b, t // tq),
        in_specs=[pl.BlockSpec((1, tq, W_C), lambda i, j: (i, j, 0))] + kv_specs,
        out_specs=pl.BlockSpec((1, tq, W_C), lambda i, j: (i, j, 0)),
        out_shape=jax.ShapeDtypeStruct((b, t, W_C), F32),
        scratch_shapes=[pltpu.VMEM((tkk, W_C), BF16), pltpu.VMEM((W_C, tkk), BF16),
                        pltpu.VMEM((npair, LANES, 2 * tq), F32)],
        compiler_params=_cparams(("parallel", "arbitrary")),
    )(q, *kv_args)


def _out_kernel(oa_ref, ob_ref, oc_ref, g_ref, x_ref, w_ref, lng_ref, lnb_ref, y_ref, *, alpha):
    g = g_ref[...]
    gate = g * _sigmoid(g)
    acc = _dot((oa_ref[...] * gate[:, 0:W_A]).astype(BF16), w_ref[0:W_A, :])
    acc = acc + _dot((ob_ref[...] * gate[:, W_A:W_A + W_B]).astype(BF16), w_ref[W_A:W_A + W_B, :])
    acc = acc + _dot((oc_ref[...] * gate[:, W_A + W_B:]).astype(BF16), w_ref[W_A + W_B:, :])
    z = alpha * x_ref[...] + acc
    mu = jnp.mean(z, axis=-1, keepdims=True)
    d = z - mu
    var = jnp.mean(d * d, axis=-1, keepdims=True)
    y_ref[...] = d * lax.rsqrt(var + LN_EPS) * lng_ref[...] + lnb_ref[...]


def _out(oa, ob, oc, g, x2d, w_out, ln_g, ln_b, alpha, tm):
    n, d = x2d.shape
    row = lambda w: pl.BlockSpec((tm, w), lambda i: (i, 0))
    return pl.pallas_call(
        functools.partial(_out_kernel, alpha=alpha),
        grid=(n // tm,),
        in_specs=[row(W_A), row(W_B), row(W_C), row(W_MIX), row(d),
                  pl.BlockSpec((W_MIX, d), lambda i: (0, 0)),
                  pl.BlockSpec((1, d), lambda i: (0, 0)),
                  pl.BlockSpec((1, d), lambda i: (0, 0))],
        out_specs=row(d),
        out_shape=jax.ShapeDtypeStruct((n, d), F32),
        compiler_params=_cparams(("parallel",)),
    )(oa, ob, oc, g, x2d, w_out, ln_g, ln_b)


def _pad_lanes(a, width=LANES):
    return jnp.pad(a, [(0, 0)] * (a.ndim - 1) + [(0, width - a.shape[-1])])


def _state_to_pairs(s):
    b = s.shape[0]
    s = s.reshape(b, H_A // 2, 2, HEAD_DIM, HEAD_DIM)
    eye2 = jnp.eye(2, dtype=s.dtype)
    bd = jnp.einsum("bphvk,hg->bphvgk", s, eye2)
    return bd.reshape(b, H_A // 2, LANES, LANES)


def _pairs_to_state(sp):
    b = sp.shape[0]
    s = sp.reshape(b, H_A // 2, 2, HEAD_DIM, 2, HEAD_DIM)
    s = jnp.stack([s[:, :, 0, :, 0, :], s[:, :, 1, :, 1, :]], axis=2)
    return s.reshape(b, H_A, HEAD_DIM, HEAD_DIM)


_IN_SIZES = (SHIFT_W, W_A, W_B, W_B, W_B, H_B, W_B, W_C, W_C, W_C, W_C)
_IN_OFFS = tuple(int(v) for v in np.concatenate([[0], np.cumsum(_IN_SIZES)]))
_SRC_SHIFT, _SRC_GA, _SRC_QB, _SRC_KB, _SRC_VB, _SRC_F, _SRC_GB, _SRC_QC, _SRC_KC, _SRC_VC, _SRC_GC = (
    _IN_OFFS[:-1])
_W_MOVES = ((_SRC_SHIFT, _OFF_SHIFT, SHIFT_W), (_SRC_GA, _OFF_G, W_A), (_SRC_GB, _OFF_G + W_A, W_B),
            (_SRC_GC, _OFF_G + W_A + W_B, W_C), (_SRC_QB, _OFF_QB, W_B), (_SRC_KB, _OFF_KB, W_B),
            (_SRC_VB, _OFF_VB, W_B), (_SRC_QC, _OFF_QC, W_C), (_SRC_KC, _OFF_KC, W_C),
            (_SRC_VC, _OFF_VC, W_C))
_W_STAGE_ROWS = 128


def _stage_w_kernel(w_ref, o_ref):
    for src, dst, width in _W_MOVES:
        o_ref[:, dst:dst + width] = w_ref[:, src:src + width].astype(BF16)
    lane = lax.broadcasted_iota(jnp.int32, (1, LANES), 1)
    o_ref[:, _OFF_F:_OFF_F + LANES] = jnp.where(
        lane < H_B, w_ref[:, _SRC_F:_SRC_F + LANES], 0.0).astype(BF16)


def _stage_w(w_in, layer):
    _, d, cols = w_in.shape
    return pl.pallas_call(
        _stage_w_kernel,
        grid=(d // _W_STAGE_ROWS,),
        in_specs=[pl.BlockSpec((None, _W_STAGE_ROWS, cols), lambda i: (layer, i, 0))],
        out_specs=pl.BlockSpec((_W_STAGE_ROWS, _PROJ_COLS), lambda i: (i, 0)),
        out_shape=jax.ShapeDtypeStruct((d, _PROJ_COLS), BF16),
        compiler_params=_cparams(("parallel",)),
    )(w_in)


def _layer_params(l, w_in, mu_shift, w0_decay, w_decay, a0, w_aaa, k_k, k_a, r_k,
                  lnx_g, lnx_b, fox_fb, w_out, ln_g, ln_b):
    w_cat = _stage_w(w_in, l)
    zeros = jnp.zeros((D_LORA, W_A), F32)
    row = lambda a: a.reshape(1, -1).astype(F32)
    return dict(
        w_cat=w_cat, fb=_pad_lanes(row(fox_fb[l])),
        mu=row(mu_shift[l]), w0=row(w0_decay[l]),
        wd=jnp.concatenate([w_decay[l], zeros], axis=0).astype(BF16),
        a0=row(a0[l]), wa=jnp.concatenate([zeros, w_aaa[l]], axis=0).astype(BF16),
        kkp=row(k_k[l]), ka=row(k_a[l]), rk=row(r_k[l]), lnx_g=row(lnx_g[l]), lnx_b=row(lnx_b[l]),
        w_out=w_out[l].astype(BF16), ln_g=row(ln_g[l]), ln_b=row(ln_b[l]))


def _run_layer(x, hist, prm, alpha, layer, depth, stacks):
    b, t, d = x.shape
    n = b * t
    x2d = x.reshape(n, d)
    tm = 256 if n % 256 == 0 else n
    us, g, qb, kb_st, vb_st, qc, kc_st, vc_st, lf = _proj(
        x2d, prm["w_cat"], prm["fb"], tm, layer, depth, stacks)
    r3 = lambda a: a.reshape(b, t, a.shape[-1])
    us, qb, qc, lf = (r3(a) for a in (us, qb, qc, lf))
    r4 = lambda a: a.reshape(depth, b, t, a.shape[-1])

    if hist is None:
        past = 0
        prev0 = jnp.zeros((b, 1, SHIFT_W), F32)
        s0 = jnp.zeros((b, H_A // 2, LANES, LANES), F32)
        cache_b = cache_c = None
        lf_all = lf
        fox_blk = (min(FOX_BLOCK, t),) * 2
        sb_blk = (min(SB_BLOCK, t),) * 2
    else:
        fk_t, fv_t, h_lf, sk_t, sv_t, h_wkv, h_shift = hist
        past = fk_t.shape[3]
        prev0 = h_shift
        s0 = _state_to_pairs(h_wkv)
        cache_b, cache_c = (fk_t, fv_t), (sk_t, sv_t)
        lf_all = jnp.concatenate([_pad_lanes(h_lf), lf], axis=1)
        fox_blk = sb_blk = (t, min(SB_BLOCK, past))

    oa, s_fin = _wkv(us, prev0, s0, prm, min(WKV_BLOCK, t))
    cum = _cumsum(lf_all)
    ob = _fox(qb, (r4(kb_st), r4(vb_st)), cache_b, cum[:, past:], cum, layer=layer,
              tq=fox_blk[0], tk=fox_blk[1], past=past)
    oc = _sb(qc, (r4(kc_st), r4(vc_st)), cache_c, layer=layer, tq=sb_blk[0], tk=sb_blk[1], past=past)
    y = _out(oa.reshape(n, W_A), ob.reshape(n, W_B), oc.reshape(n, W_C), g, x2d,
             prm["w_out"], prm["ln_g"], prm["ln_b"], alpha, tm)
    small = (lf[:, :, :H_B], _pairs_to_state(s_fin), us[:, -1:, :])
    return y.reshape(b, t, d), (kb_st, vb_st, kc_st, vc_st), small


def kernel(x_prompt, x_sample, cache_fox_k, cache_fox_v, cache_fox_logf, cache_sb_k, cache_sb_v, state_wkv, state_shift, w_in, mu_shift, w0_decay, w_decay, a0, w_aaa, k_k, k_a, r_k, lnx_g, lnx_b, fox_fb, w_out, ln_g, ln_b):
    depth = w_in.shape[0]
    alpha = (2 * depth) ** 0.25
    yp, ys = x_prompt, x_sample

    def time_on_lanes(cache):
        nl, nb, past, nh, hd = cache.shape
        return jnp.transpose(cache, (0, 1, 3, 4, 2)).reshape(nl, nb, nh * hd, past)

    fk_t, fv_t, sk_t, sv_t = (time_on_lanes(c) for c in (cache_fox_k, cache_fox_v, cache_sb_k, cache_sb_v))
    stacks_p, stacks_s = None, None
    small_p, small_s = [], []
    for l in range(depth):
        prm = _layer_params(l, w_in, mu_shift, w0_decay, w_decay, a0, w_aaa, k_k, k_a, r_k,
                            lnx_g, lnx_b, fox_fb, w_out, ln_g, ln_b)
        yp, stacks_p, sm_p = _run_layer(yp, None, prm, alpha, l, depth, stacks_p)
        hist = (fk_t, fv_t, cache_fox_logf[l], sk_t, sv_t, state_wkv[l], state_shift[l])
        ys, stacks_s, sm_s = _run_layer(ys, hist, prm, alpha, l, depth, stacks_s)
        small_p.append(sm_p)
        small_s.append(sm_s)

    def group_outputs(x, stacks, small):
        b, t, _ = x.shape
        kb, vb, kc, vc = stacks
        logf, wkv, shift = (jnp.stack([sm[i] for sm in small]) for i in range(3))
        return (kb.reshape(depth, b, t, H_B, HEAD_DIM), vb.reshape(depth, b, t, H_B, HEAD_DIM), logf,
                kc.reshape(depth, b, t, H_C, HEAD_DIM), vc.reshape(depth, b, t, H_C, HEAD_DIM),
                wkv, shift)

    return ((yp, ys) + group_outputs(x_prompt, stacks_p, small_p)
            + group_outputs(x_sample, stacks_s, small_s))
```

```python
import functools

import numpy as np
import jax
import jax.numpy as jnp
from jax import lax
from jax.experimental import pallas as pl
from jax.experimental.pallas import tpu as pltpu

F32 = jnp.float32
BF16 = jnp.bfloat16

HEAD_DIM = 64
H_A, H_B, H_C = 6, 6, 4
W_A, W_B, W_C = H_A * HEAD_DIM, H_B * HEAD_DIM, H_C * HEAD_DIM
W_MIX = W_A + W_B + W_C
D_LORA = 64
SHIFT_W = 3 * W_A + 2 * D_LORA
LANES = 128
WKV_CHUNK = 64
WKV_BLOCK = 256
ROW_BLOCK = 512
FOX_BLOCK = 512
SB_BLOCK = 512
GN_EPS = 64e-5
LN_EPS = 1e-5
NEG_INF = -1e30
LOG2E = 1.4426950408889634
VMEM_LIMIT = 56 * 1024 * 1024


def _cparams(sem):
    return pltpu.CompilerParams(dimension_semantics=sem, vmem_limit_bytes=VMEM_LIMIT)


def _dot(a, b):
    return jnp.dot(a, b, preferred_element_type=F32)


def _dot_nt(a, b):
    return lax.dot_general(a, b, (((1,), (1,)), ((), ())), preferred_element_type=F32)


def _dot_tn(a, b):
    return lax.dot_general(a, b, (((0,), (0,)), ((), ())), preferred_element_type=F32)


def _split2(x):
    hi = x.astype(BF16)
    lo = (x - hi.astype(F32)).astype(BF16)
    return hi, lo


def _split3(x):
    hi = x.astype(BF16)
    r1 = x - hi.astype(F32)
    mid = r1.astype(BF16)
    lo = (r1 - mid.astype(F32)).astype(BF16)
    return hi, mid, lo


def _softplus(x):
    return jnp.maximum(x, 0.0) + jnp.log1p(jnp.exp(-jnp.abs(x)))


def _sigmoid(x):
    return 1.0 / (1.0 + jnp.exp(-x))


_OFF_SHIFT = 0
_OFF_G = _OFF_SHIFT + SHIFT_W
_OFF_QB = _OFF_G + W_MIX
_OFF_KB = _OFF_QB + W_B
_OFF_VB = _OFF_KB + W_B
_OFF_QC = _OFF_VB + W_B
_OFF_KC = _OFF_QC + W_C
_OFF_VC = _OFF_KC + W_C
_OFF_F = _OFF_VC + W_C
_PROJ_COLS = _OFF_F + LANES


def _proj_kernel(x_ref, w_ref, fb_ref, *refs):
    us_ref, g_ref, qb_ref, kb_ref, vb_ref, qc_ref, kc_ref, vc_ref, lf_ref = refs[-9:]
    xb = x_ref[...].astype(BF16)
    for ref, off in ((us_ref, _OFF_SHIFT), (g_ref, _OFF_G), (qb_ref, _OFF_QB),
                     (kb_ref, _OFF_KB), (vb_ref, _OFF_VB), (qc_ref, _OFF_QC),
                     (kc_ref, _OFF_KC), (vc_ref, _OFF_VC)):
        width = ref.shape[-1]
        ref[...] = _dot(xb, w_ref[:, off:off + width]).astype(ref.dtype)
    f = _dot(xb, w_ref[:, _OFF_F:_OFF_F + LANES]) + fb_ref[...]
    lf_ref[...] = -_softplus(-f)


_PROJ_WIDTHS = (SHIFT_W, W_MIX, W_B, W_B, W_B, W_C, W_C, W_C, LANES)
_PROJ_GATES = 1
_PROJ_STACKED = (3, 4, 6, 7)


def _proj(x2d, w_cat, fb_pad, tm, layer, depth, stacks):
    n, d = x2d.shape
    out_specs, out_shape = [], []
    for idx, w in enumerate(_PROJ_WIDTHS):
        if idx in _PROJ_STACKED:
            out_specs.append(pl.BlockSpec((None, tm, w), lambda i: (layer, i, 0)))
            out_shape.append(jax.ShapeDtypeStruct((depth, n, w), F32))
        else:
            out_specs.append(pl.BlockSpec((tm, w), lambda i: (i, 0)))
            out_shape.append(jax.ShapeDtypeStruct((n, w), BF16 if idx == _PROJ_GATES else F32))
    in_specs = [pl.BlockSpec((tm, d), lambda i: (i, 0)),
                pl.BlockSpec((d, _PROJ_COLS), lambda i: (0, 0)),
                pl.BlockSpec((1, LANES), lambda i: (0, 0))]
    args = [x2d, w_cat, fb_pad]
    aliases = {}
    if stacks is not None:
        for j, (idx, st) in enumerate(zip(_PROJ_STACKED, stacks)):
            in_specs.append(pl.BlockSpec(memory_space=pl.ANY))
            args.append(st)
            aliases[3 + j] = idx
    return pl.pallas_call(
        _proj_kernel,
        grid=(n // tm,),
        in_specs=in_specs,
        out_specs=out_specs,
        out_shape=out_shape,
        input_output_aliases=aliases,
        compiler_params=_cparams(("parallel",)),
    )(*args)


def _cumsum_kernel(lf_ref, cum_ref, *, blk):
    tk = lf_ref.shape[1]
    row = lax.broadcasted_iota(jnp.int32, (blk, blk), 0)
    col = lax.broadcasted_iota(jnp.int32, (blk, blk), 1)
    tri = (col <= row).astype(BF16)
    carry = jnp.zeros((1, LANES), F32)
    for s in range(0, tk, blk):
        x = lf_ref[0, s:s + blk, :]
        hi, mid, lo = _split3(x)
        c = _dot(tri, hi) + _dot(tri, mid) + _dot(tri, lo) + carry
        cum_ref[0, s:s + blk, :] = c
        carry = c[blk - 1:blk, :]


def _cumsum(lf, blk=64):
    b, tk, _ = lf.shape
    return pl.pallas_call(
        functools.partial(_cumsum_kernel, blk=blk),
        grid=(b,),
        in_specs=[pl.BlockSpec((1, tk, LANES), lambda i: (i, 0, 0))],
        out_specs=pl.BlockSpec((1, tk, LANES), lambda i: (i, 0, 0)),
        out_shape=jax.ShapeDtypeStruct((b, tk, LANES), F32),
        compiler_params=_cparams(("parallel",)),
    )(lf)


def _seg_sum(x, bd):
    xb = x.astype(BF16)
    return jnp.concatenate([_dot(xb[:, s:s + LANES], bd) for s in range(0, x.shape[1], LANES)], axis=1)


def _wkv_kernel(us_ref, prev0_ref, s0_ref, mu_ref, w0_ref, wd_ref, a0_ref, wa_ref,
                kkp_ref, ka_ref, rk_ref, lng_ref, lnb_ref,
                oa_ref, sout_ref,
                s_scr, prev_scr, at_s, rt_s, bh_s, kh_s, be_s, ke_s, v_s, ga_s, y_s):
    c_idx = pl.program_id(1)
    nc = pl.num_programs(1)
    C = WKV_CHUNK
    tt = us_ref.shape[1]
    nb = tt // C

    @pl.when(c_idx == 0)
    def _():
        s_scr[...] = s0_ref[0]
        prev_scr[...] = prev0_ref[0]

    u = us_ref[0]
    row1 = lax.broadcasted_iota(jnp.int32, (tt, 1), 0)
    zprev = jnp.where(row1 == 0, prev_scr[...], pltpu.roll(u, 1, axis=0))
    prev_scr[...] = u[tt - 1:tt, :]
    zs = u + (zprev - u) * mu_ref[...]
    r = zs[:, 0:W_A]
    k = zs[:, W_A:2 * W_A]
    v = zs[:, 2 * W_A:3 * W_A]
    lora_in = zs[:, 3 * W_A:3 * W_A + LANES]
    lane1 = lax.broadcasted_iota(jnp.int32, (1, LANES), 1)
    lora_t = jnp.where(lane1 < D_LORA, jnp.tanh(lora_in), lora_in).astype(BF16)
    wl = w0_ref[...] + _dot(lora_t, wd_ref[...])
    w_log = -_softplus(-wl) - 0.5
    lw = -jnp.exp(w_log)
    a = _sigmoid(a0_ref[...] + _dot(lora_t, wa_ref[...]))

    lane_w = lax.broadcasted_iota(jnp.int32, (LANES, LANES), 0) // HEAD_DIM
    lane_c = lax.broadcasted_iota(jnp.int32, (LANES, LANES), 1) // HEAD_DIM
    bd = (lane_w == lane_c).astype(BF16)

    kk = k * kkp_ref[...]
    kk = kk * lax.rsqrt(_seg_sum(kk * kk, bd) + 1e-12)
    k2 = k * (1.0 + (a - 1.0) * ka_ref[...])

    rowt = lax.broadcasted_iota(jnp.int32, (tt, tt), 0)
    colt = lax.broadcasted_iota(jnp.int32, (tt, tt), 1)
    tri_b = (((rowt // C) == (colt // C)) & (colt <= rowt)).astype(BF16)
    h1, h2, h3 = _split3(lw)
    cs = _dot(tri_b, h1) + _dot(tri_b, h2) + _dot(tri_b, h3)
    tot = jnp.concatenate([jnp.broadcast_to(cs[c * C + C - 1:c * C + C, :], (C, W_A))
                           for c in range(nb)], axis=0)
    g_inv = jnp.exp(-cs)
    g_end = jnp.exp(tot - cs)
    kka = kk * a
    at_s[...] = kk * jnp.exp(cs - lw)
    rt_s[...] = r * jnp.exp(cs)
    bh_s[...] = (-kka * g_inv).astype(BF16)
    kh_s[...] = (k2 * g_inv).astype(BF16)
    be_s[...] = (-kka * g_end).astype(BF16)
    ke_s[...] = (k2 * g_end).astype(BF16)
    v_s[...] = v
    ga_s[...] = jnp.exp(tot)

    rowp = lax.broadcasted_iota(jnp.int32, (LANES, LANES), 0) // HEAD_DIM
    colp = lax.broadcasted_iota(jnp.int32, (LANES, LANES), 1) // HEAD_DIM
    bd_pair = rowp == colp
    npair = H_A // 2
    G = 4 * HEAD_DIM
    row_g = lax.broadcasted_iota(jnp.int32, (C, G), 0)
    lane_g = lax.broadcasted_iota(jnp.int32, (C, G), 1) % C
    tri_incl = lane_g <= row_g
    tri_strict = lane_g < row_g
    eye = (lane_g == row_g).astype(F32)
    diag4 = (lax.broadcasted_iota(jnp.int32, (G, G), 0) // HEAD_DIM
             == lax.broadcasted_iota(jnp.int32, (G, G), 1) // HEAD_DIM).astype(BF16)

    def stack4(x):
        xb = x.astype(BF16)
        return jnp.concatenate([xb, xb, xb, xb], axis=0) * diag4

    def mul4(xs, ys):
        return [_dot(x.astype(BF16), stack4(y)) for x, y in zip(xs, ys)]

    groups = []
    for c in range(nb):
        groups.append([(c, 0), (c, LANES)])
    for c in range(0, nb, 2):
        groups.append([(c, 2 * LANES)] + ([(c + 1, 2 * LANES)] if c + 1 < nb else [None]))

    def gather(ref, grp):
        parts = [jnp.zeros((C, LANES), ref.dtype) if u is None
                 else ref[u[0] * C:(u[0] + 1) * C, u[1]:u[1] + LANES] for u in grp]
        return jnp.concatenate(parts, axis=1)

    at_g = [gather(at_s, g) for g in groups]
    rt_g = [gather(rt_s, g) for g in groups]
    v_g = [gather(v_s, g) for g in groups]
    lhs = [jnp.concatenate([a, r], axis=0).astype(BF16) for a, r in zip(at_g, rt_g)]
    rhs = [jnp.concatenate([stack4(gather(bh_s, g)), stack4(gather(kh_s, g))], axis=0) for g in groups]
    a4 = [_dot_nt(l, r) for l, r in zip(lhs, rhs)]
    n1 = [jnp.where(tri_strict, a[:C, :G], 0.0) for a in a4]
    a_ak = [jnp.where(tri_strict, a[:C, G:], 0.0) for a in a4]
    a_rb = [jnp.where(tri_incl, a[C:, :G], 0.0) for a in a4]
    a_rk = [jnp.where(tri_incl, a[C:, G:], 0.0) for a in a4]
    vst = [stack4(v) for v in v_g]
    avy = [_dot(jnp.concatenate([ak, rk], axis=0).astype(BF16), vs)
           for ak, rk, vs in zip(a_ak, a_rk, vst)]
    pair4 = lambda xs, ys: [eye + x + y + m for x, y, m in zip(xs, ys, mul4(xs, ys))]
    n2 = mul4(n1, n1)
    n4 = mul4(n2, n2)
    p1 = pair4(n1, n2)
    n8 = mul4(n4, n4)
    n16 = mul4(n8, n8)
    p2 = pair4(n4, n8)
    n32 = mul4(n16, n16)
    p12 = mul4(p1, p2)
    p3 = pair4(n16, n32)
    t_inv = mul4(p12, p3)
    xs = [_dot(t.astype(BF16), jnp.concatenate([stack4(a), stack4(y[:C])], axis=1))
          for t, a, y in zip(t_inv, at_g, avy)]
    rys = [_dot(rb.astype(BF16), jnp.concatenate([stack4(x[:, :G]), stack4(x[:, G:])], axis=1))
           for rb, x in zip(a_rb, xs)]
    rp_g = [r + ry[:, :G] for r, ry in zip(rt_g, rys)]
    y0_g = [ry[:, G:] + y[C:] for ry, y in zip(rys, avy)]

    pps, qs, rps, y0s = {}, {}, {}, {}
    for gi, grp in enumerate(groups):
        for k, u in enumerate(grp):
            if u is None:
                continue
            c, p = u[0], u[1] // LANES
            rows = slice(c * C, (c + 1) * C)
            sl = slice(p * LANES, (p + 1) * LANES)
            gl = slice(k * LANES, (k + 1) * LANES)
            be_u, ke_u = be_s[rows, sl], ke_s[rows, sl]
            atp = xs[gi][:, gl].astype(BF16)
            wv = jnp.concatenate([xs[gi][:, G + k * LANES:G + (k + 1) * LANES], v_s[rows, sl]],
                                 axis=0).astype(BF16)
            pps[c, p] = jnp.where(bd_pair, _dot_tn(atp, be_u), 0.0).astype(BF16)
            qs[c, p] = jnp.where(bd_pair, _dot_tn(wv, jnp.concatenate([be_u, ke_u], axis=0)), 0.0)
            rps[c, p] = rp_g[gi][:, gl].astype(BF16)
            y0s[c, p] = y0_g[gi][:, gl]

    for c in range(nb):
        rows = slice(c * C, (c + 1) * C)
        s_ps = [s_scr[p] for p in range(npair)]
        s_bs = [s_p.astype(BF16) for s_p in s_ps]
        for p in range(npair):
            sl = slice(p * LANES, (p + 1) * LANES)
            y_s[rows, sl] = _dot_nt(rps[c, p], s_bs[p]) + y0s[c, p]
            s_scr[p] = s_ps[p] * ga_s[c * C:c * C + 1, sl] + _dot(s_bs[p], pps[c, p]) + qs[c, p]

    y = y_s[...]
    inv_n = 1.0 / HEAD_DIM
    mean = _seg_sum(y, bd) * inv_n
    d = y - mean
    var = _seg_sum(d * d, bd) * inv_n
    yn = d * lax.rsqrt(var + GN_EPS) * lng_ref[...] + lnb_ref[...]
    bonus = _seg_sum(r * k2 * rk_ref[...], bd) * v
    oa_ref[0] = (yn + bonus).astype(oa_ref.dtype)

    @pl.when(c_idx == nc - 1)
    def _():
        sout_ref[0] = s_scr[...]


def _wkv(us, prev0, s0_bd, prm, tt):
    b, t, _ = us.shape
    npair = H_A // 2
    vec = lambda w: pl.BlockSpec((1, w), lambda i, j: (0, 0))
    mat = lambda: pl.BlockSpec((LANES, W_A), lambda i, j: (0, 0))
    blk = lambda dt: pltpu.VMEM((tt, W_A), dt)
    return pl.pallas_call(
        _wkv_kernel,
        grid=(b, t // tt),
        in_specs=[pl.BlockSpec((1, tt, SHIFT_W), lambda i, j: (i, j, 0)),
                  pl.BlockSpec((1, 1, SHIFT_W), lambda i, j: (i, 0, 0)),
                  pl.BlockSpec((1, npair, LANES, LANES), lambda i, j: (i, 0, 0, 0)),
                  vec(SHIFT_W), vec(W_A), mat(), vec(W_A), mat(),
                  vec(W_A), vec(W_A), vec(W_A), vec(W_A), vec(W_A)],
        out_specs=[pl.BlockSpec((1, tt, W_A), lambda i, j: (i, j, 0)),
                   pl.BlockSpec((1, npair, LANES, LANES), lambda i, j: (i, 0, 0, 0))],
        out_shape=[jax.ShapeDtypeStruct((b, t, W_A), BF16),
                   jax.ShapeDtypeStruct((b, npair, LANES, LANES), F32)],
        scratch_shapes=[pltpu.VMEM((npair, LANES, LANES), F32),
                        pltpu.VMEM((1, SHIFT_W), F32),
                        blk(F32), blk(F32), blk(BF16), blk(BF16), blk(BF16), blk(BF16),
                        blk(F32), blk(F32), blk(F32)],
        compiler_params=_cparams(("parallel", "arbitrary")),
    )(us, prev0, s0_bd, prm["mu"], prm["w0"], prm["wd"], prm["a0"], prm["wa"],
      prm["kkp"], prm["ka"], prm["rk"], prm["lnx_g"], prm["lnx_b"])


_AUG = 8
_PREP_ROWS = 256


def _bias_lanes(cum, key_side):
    ntile = H_B // 2 if key_side else H_B
    width = ntile * LANES
    r = lax.broadcasted_iota(jnp.int32, (LANES, width), 0)
    c = lax.broadcasted_iota(jnp.int32, (LANES, width), 1)
    lane = lax.broadcasted_iota(jnp.int32, (1, width), 1)
    f_off, one_off = (0, 3) if key_side else (3, 0)
    sign = -1.0 if key_side else 1.0
    head_of_tile = (c // LANES) * 2 + (c % LANES) // _AUG if key_side else c // LANES
    slot = c % _AUG
    in_head_lanes = (c % LANES) // _AUG == r % 2
    one_head = (lane % LANES) // _AUG < 2 if key_side else (lane % LANES) // _AUG == (lane // LANES) % 2
    out = jnp.where(one_head & (lane % _AUG >= one_off) & (lane % _AUG < one_off + 3), 1.0, 0.0)
    for t, part in enumerate(_split3(cum)):
        sel = jnp.where((head_of_tile == r) & in_head_lanes & (slot == f_off + t), sign, 0.0).astype(BF16)
        out = out + _dot(part, sel)
    return [out[:, i * LANES:(i + 1) * LANES] for i in range(ntile)]


def _stage_rows(total):
    return [(s, min(_PREP_ROWS, total - s)) for s in range(0, total, _PREP_ROWS)]


def _transpose_rows(x):
    n = x.shape[0]
    n_pad = -n % LANES
    if n_pad:
        x = jnp.concatenate([x, jnp.zeros((n_pad, x.shape[1]), x.dtype)], axis=0)
    return x.T[:, :n]


def _fox_kernel(*refs, tq, tk, past, single):
    if past:
        q_ref, kc_ref, vc_ref, k_ref, v_ref, cq_ref, ck_ref, o_ref, ka_scr, vt_scr, acc_scr = refs
    else:
        q_ref, k_ref, v_ref, cq_ref, ck_ref, o_ref, ka_scr, vt_scr, acc_scr = refs
    npair = H_B // 2
    lane1 = lax.broadcasted_iota(jnp.int32, (1, LANES), 1)
    head_of_lane = lane1 // HEAD_DIM

    def stage_bias(s, n):
        cum = ck_ref[0, s:s + n, :] * LOG2E
        for p, aug in enumerate(_bias_lanes(cum, True)):
            ka_scr[p, s:s + n, LANES:2 * LANES] = aug.astype(BF16)

    def stage_kv():
        for s, n in _stage_rows(past):
            for p in range(npair):
                ka_scr[p, s:s + n, 0:LANES] = kc_ref[0, p * LANES:(p + 1) * LANES, s:s + n].T.astype(BF16)
            vt_scr[:, s:s + n] = vc_ref[0, :, s:s + n].astype(BF16)
            stage_bias(s, n)
        for s, n in _stage_rows(k_ref.shape[1]):
            kb = k_ref[0, s:s + n, :].astype(BF16)
            for p in range(npair):
                ka_scr[p, past + s:past + s + n, 0:LANES] = kb[:, p * LANES:(p + 1) * LANES]
            vt_scr[:, past + s:past + s + n] = _transpose_rows(v_ref[0, s:s + n, :]).astype(BF16)
            stage_bias(past + s, n)

    if single:
        i = 0
        stage_kv()
    else:
        i = pl.program_id(1)
        pl.when(i == 0)(stage_kv)

    q = q_ref[0] * (HEAD_DIM ** -0.5 * LOG2E)
    cq = cq_ref[0] * LOG2E
    rhs = []
    q_bias = _bias_lanes(cq, False)
    for p in range(npair):
        q_p = q[:, p * LANES:(p + 1) * LANES]
        halves = [jnp.concatenate([jnp.where(head_of_lane == hh, q_p, 0.0), q_bias[2 * p + hh]], axis=1)
                  for hh in range(2)]
        rhs.append(jnp.concatenate(halves, axis=0).astype(BF16))

    n_full = past // tk + i * (tq // tk)
    diag0 = past if single else pl.multiple_of(past + i * tq, tq)
    krow = lax.broadcasted_iota(jnp.int32, (tq, 2 * tq), 0)
    qcol = lax.broadcasted_iota(jnp.int32, (tq, 2 * tq), 1)
    causal_t = krow <= jnp.where(qcol >= tq, qcol - tq, qcol)

    acc_scr[...] = jnp.zeros(acc_scr.shape, F32)

    def update(carry, start, size, mask):
        ms, ls = carry
        st = [_dot_nt(ka_scr[p, pl.ds(start, size), :], rhs[p]) for p in range(npair)]
        if mask is not None:
            st = [jnp.where(mask, s, NEG_INF) for s in st]
        m_new = [jnp.maximum(m, jnp.max(s, axis=0, keepdims=True)) for m, s in zip(ms, st)]
        alpha = [jnp.exp2(m - mn) for m, mn in zip(ms, m_new)]
        pt = [jnp.exp2(s - mn) for s, mn in zip(st, m_new)]
        l_new = [a * l + jnp.sum(x, axis=0, keepdims=True) for a, l, x in zip(alpha, ls, pt)]
        pv = [_dot(vt_scr[p * LANES:(p + 1) * LANES, pl.ds(start, size)], pt[p].astype(BF16))
              for p in range(npair)]
        for p in range(npair):
            acc_scr[p] = acc_scr[p] * alpha[p] + pv[p]
        return tuple(m_new), tuple(l_new)

    def body(j, carry):
        return update(carry, pl.multiple_of(j * tk, tk), tk, None)

    init = (tuple(jnp.full((1, 2 * tq), NEG_INF, F32) for _ in range(npair)),
            tuple(jnp.zeros((1, 2 * tq), F32) for _ in range(npair)))
    carry = lax.fori_loop(0, n_full, body, init)
    _, ls = update(carry, diag0, tq, causal_t)
    for p in range(npair):
        o_t = (acc_scr[p] / ls[p]).T
        o_ref[0, :, p * LANES:(p + 1) * LANES] = jnp.where(
            head_of_lane == 0, o_t[:tq], o_t[tq:]).astype(o_ref.dtype)


def _kv_specs(k_new, cache, layer, width):
    specs, args = [], []
    t_new = k_new[0].shape[2]
    if cache is not None:
        past = cache[0].shape[3]
        for c in cache:
            specs.append(pl.BlockSpec((None, 1, width, past), lambda i, j: (layer, i, 0, 0)))
            args.append(c)
    for a in k_new:
        specs.append(pl.BlockSpec((None, 1, t_new, width), lambda i, j: (layer, i, 0, 0)))
        args.append(a)
    return specs, args


def _fox(q, kv_new, cache, cum_q, cum_k, *, layer, tq, tk, past):
    b, t, _ = q.shape
    tkk = past + kv_new[0].shape[2]
    npair = H_B // 2
    kv_specs, kv_args = _kv_specs(kv_new, cache, layer, W_B)
    return pl.pallas_call(
        functools.partial(_fox_kernel, tq=tq, tk=tk, past=past, single=(t == tq)),
        grid=(b, t // tq),
        in_specs=[pl.BlockSpec((1, tq, W_B), lambda i, j: (i, j, 0))] + kv_specs + [
                  pl.BlockSpec((1, tq, LANES), lambda i, j: (i, j, 0)),
                  pl.BlockSpec((1, tkk, LANES), lambda i, j: (i, 0, 0))],
        out_specs=pl.BlockSpec((1, tq, W_B), lambda i, j: (i, j, 0)),
        out_shape=jax.ShapeDtypeStruct((b, t, W_B), BF16),
        scratch_shapes=[pltpu.VMEM((npair, tkk, 2 * LANES), BF16), pltpu.VMEM((W_B, tkk), BF16),
                        pltpu.VMEM((npair, LANES, 2 * tq), F32)],
        compiler_params=_cparams(("parallel", "arbitrary")),
    )(q, *kv_args, cum_q, cum_k)


def _sb_kernel(*refs, tq, tk, past, single):
    if past:
        q_ref, kc_ref, vc_ref, k_ref, v_ref, o_ref, kb_scr, vt_scr, acc_scr = refs
    else:
        q_ref, k_ref, v_ref, o_ref, kb_scr, vt_scr, acc_scr = refs
    npair = H_C // 2
    lane1 = lax.broadcasted_iota(jnp.int32, (1, LANES), 1)
    head_of_lane = lane1 // HEAD_DIM

    def stage_kv():
        for s, n in _stage_rows(past):
            kb_scr[s:s + n, :] = kc_ref[0, :, s:s + n].T.astype(BF16)
            vt_scr[:, s:s + n] = vc_ref[0, :, s:s + n].astype(BF16)
        for s, n in _stage_rows(k_ref.shape[1]):
            kb_scr[past + s:past + s + n, :] = k_ref[0, s:s + n, :].astype(BF16)
            vt_scr[:, past + s:past + s + n] = _transpose_rows(v_ref[0, s:s + n, :]).astype(BF16)

    if single:
        i = 0
        stage_kv()
    else:
        i = pl.program_id(1)
        pl.when(i == 0)(stage_kv)

    q = q_ref[0] * (HEAD_DIM ** -0.5 * LOG2E)
    rhs = []
    for p in range(npair):
        q_p = q[:, p * LANES:(p + 1) * LANES]
        rhs.append(jnp.concatenate([jnp.where(head_of_lane == hh, q_p, 0.0) for hh in range(2)],
                                   axis=0).astype(BF16))

    def earlier_matrix(n):
        rr = lax.broadcasted_iota(jnp.int32, (n, n), 0)
        cc = lax.broadcasted_iota(jnp.int32, (n, n), 1)
        return (cc > rr).astype(BF16)

    n_full = past // tk + i * (tq // tk)
    diag0 = past if single else pl.multiple_of(past + i * tq, tq)
    krow = lax.broadcasted_iota(jnp.int32, (tq, 2 * tq), 0)
    qcol = lax.broadcasted_iota(jnp.int32, (tq, 2 * tq), 1)
    strict_t = krow < jnp.where(qcol >= tq, qcol - tq, qcol)

    def update(runs, start, size, mask, first):
        zt = [_dot_nt(kb_scr[pl.ds(start, size), p * LANES:(p + 1) * LANES], rhs[p])
              for p in range(npair)]
        mz = [jnp.minimum(z, 0.0) for z in zt]
        tail = [jnp.log(1.0 + jnp.exp2(m + m - z)) * LOG2E for m, z in zip(mz, zt)]
        lsig = [m - t for m, t in zip(mz, tail)]
        l1m = [s - z for s, z in zip(lsig, zt)]
        if mask is not None:
            l1m = [jnp.where(mask, x, 0.0) for x in l1m]
        later = earlier_matrix(size)
        after = [_dot(later, x.astype(BF16)) for x in l1m]
        wt = [jnp.exp2(s + a + r) for s, a, r in zip(lsig, after, runs)]
        if mask is not None:
            wt = [jnp.where(mask, w, 0.0) for w in wt]
        pv = [_dot(vt_scr[p * LANES:(p + 1) * LANES, pl.ds(start, size)], wt[p].astype(BF16))
              for p in range(npair)]
        for p in range(npair):
            acc_scr[p] = pv[p] if first else acc_scr[p] + pv[p]
        return tuple(r + jnp.sum(x, axis=0, keepdims=True) for r, x in zip(runs, l1m))

    runs = tuple(jnp.zeros((1, 2 * tq), F32) for _ in range(npair))
    runs = update(runs, diag0, tq, strict_t, True)

    def body(jj, runs):
        j = n_full - 1 - jj
        return update(runs, pl.multiple_of(j * tk, tk), tk, None, False)

    lax.fori_loop(0, n_full, body, runs)
    for p in range(npair):
        o_t = acc_scr[p].T
        o_ref[0, :, p * LANES:(p + 1) * LANES] = jnp.where(
            head_of_lane == 0, o_t[:tq], o_t[tq:]).astype(o_ref.dtype)


def _sb(q, kv_new, cache, *, layer, tq, tk, past):
    b, t, _ = q.shape
    tkk = past + kv_new[0].shape[2]
    npair = H_C // 2
    kv_specs, kv_args = _kv_specs(kv_new, cache, layer, W_C)
    return pl.pallas_call(
        functools.partial(_sb_kernel, tq=tq, tk=tk, past=past, single=(t == tq)),
        grid=(b, t // tq),
        in_specs=[pl.BlockSpec((1, tq, W_C), lambda i, j: (i, j, 0))] + kv_specs,
        out_specs=pl.BlockSpec((1, tq, W_C), lambda i, j: (i, j, 0)),
        out_shape=jax.ShapeDtypeStruct((b, t, W_C), BF16),
        scratch_shapes=[pltpu.VMEM((tkk, W_C), BF16), pltpu.VMEM((W_C, tkk), BF16),
                        pltpu.VMEM((npair, LANES, 2 * tq), F32)],
        compiler_params=_cparams(("parallel", "arbitrary")),
    )(q, *kv_args)


def _out_kernel(oa_ref, ob_ref, oc_ref, g_ref, x_ref, w_ref, lng_ref, lnb_ref, y_ref, *, alpha):
    g = g_ref[...].astype(F32)
    gate = g * _sigmoid(g)
    gated = lambda o_ref, lo, hi: (o_ref[...].astype(F32) * gate[:, lo:hi]).astype(BF16)
    acc = _dot(gated(oa_ref, 0, W_A), w_ref[0:W_A, :])
    acc = acc + _dot(gated(ob_ref, W_A, W_A + W_B), w_ref[W_A:W_A + W_B, :])
    acc = acc + _dot(gated(oc_ref, W_A + W_B, W_MIX), w_ref[W_A + W_B:, :])
    z = alpha * x_ref[...] + acc
    mu = jnp.mean(z, axis=-1, keepdims=True)
    d = z - mu
    var = jnp.mean(d * d, axis=-1, keepdims=True)
    y_ref[...] = d * lax.rsqrt(var + LN_EPS) * lng_ref[...] + lnb_ref[...]


def _out(oa, ob, oc, g, x2d, w_out, ln_g, ln_b, alpha, tm):
    n, d = x2d.shape
    row = lambda w: pl.BlockSpec((tm, w), lambda i: (i, 0))
    return pl.pallas_call(
        functools.partial(_out_kernel, alpha=alpha),
        grid=(n // tm,),
        in_specs=[row(W_A), row(W_B), row(W_C), row(W_MIX), row(d),
                  pl.BlockSpec((W_MIX, d), lambda i: (0, 0)),
                  pl.BlockSpec((1, d), lambda i: (0, 0)),
                  pl.BlockSpec((1, d), lambda i: (0, 0))],
        out_specs=row(d),
        out_shape=jax.ShapeDtypeStruct((n, d), F32),
        compiler_params=_cparams(("parallel",)),
    )(oa, ob, oc, g, x2d, w_out, ln_g, ln_b)


def _pad_lanes(a, width=LANES):
    return jnp.pad(a, [(0, 0)] * (a.ndim - 1) + [(0, width - a.shape[-1])])


def _state_to_pairs(s):
    b = s.shape[0]
    s = s.reshape(b, H_A // 2, 2, HEAD_DIM, HEAD_DIM)
    eye2 = jnp.eye(2, dtype=s.dtype)
    bd = jnp.einsum("bphvk,hg->bphvgk", s, eye2)
    return bd.reshape(b, H_A // 2, LANES, LANES)


def _pairs_to_state(sp):
    b = sp.shape[0]
    s = sp.reshape(b, H_A // 2, 2, HEAD_DIM, 2, HEAD_DIM)
    s = jnp.stack([s[:, :, 0, :, 0, :], s[:, :, 1, :, 1, :]], axis=2)
    return s.reshape(b, H_A, HEAD_DIM, HEAD_DIM)


_IN_SIZES = (SHIFT_W, W_A, W_B, W_B, W_B, H_B, W_B, W_C, W_C, W_C, W_C)
_IN_OFFS = tuple(int(v) for v in np.concatenate([[0], np.cumsum(_IN_SIZES)]))
_SRC_SHIFT, _SRC_GA, _SRC_QB, _SRC_KB, _SRC_VB, _SRC_F, _SRC_GB, _SRC_QC, _SRC_KC, _SRC_VC, _SRC_GC = (
    _IN_OFFS[:-1])
_W_MOVES = ((_SRC_SHIFT, _OFF_SHIFT, SHIFT_W), (_SRC_GA, _OFF_G, W_A), (_SRC_GB, _OFF_G + W_A, W_B),
            (_SRC_GC, _OFF_G + W_A + W_B, W_C), (_SRC_QB, _OFF_QB, W_B), (_SRC_KB, _OFF_KB, W_B),
            (_SRC_VB, _OFF_VB, W_B), (_SRC_QC, _OFF_QC, W_C), (_SRC_KC, _OFF_KC, W_C),
            (_SRC_VC, _OFF_VC, W_C))
_W_STAGE_ROWS = 128


def _stage_w_kernel(w_ref, o_ref):
    for src, dst, width in _W_MOVES:
        o_ref[:, dst:dst + width] = w_ref[:, src:src + width].astype(BF16)
    lane = lax.broadcasted_iota(jnp.int32, (1, LANES), 1)
    o_ref[:, _OFF_F:_OFF_F + LANES] = jnp.where(
        lane < H_B, w_ref[:, _SRC_F:_SRC_F + LANES], 0.0).astype(BF16)


def _stage_w(w_in, layer):
    _, d, cols = w_in.shape
    return pl.pallas_call(
        _stage_w_kernel,
        grid=(d // _W_STAGE_ROWS,),
        in_specs=[pl.BlockSpec((None, _W_STAGE_ROWS, cols), lambda i: (layer, i, 0))],
        out_specs=pl.BlockSpec((_W_STAGE_ROWS, _PROJ_COLS), lambda i: (i, 0)),
        out_shape=jax.ShapeDtypeStruct((d, _PROJ_COLS), BF16),
        compiler_params=_cparams(("parallel",)),
    )(w_in)


def _layer_params(l, w_in, mu_shift, w0_decay, w_decay, a0, w_aaa, k_k, k_a, r_k,
                  lnx_g, lnx_b, fox_fb, w_out, ln_g, ln_b):
    w_cat = _stage_w(w_in, l)
    zeros = jnp.zeros((D_LORA, W_A), F32)
    row = lambda a: a.reshape(1, -1).astype(F32)
    return dict(
        w_cat=w_cat, fb=_pad_lanes(row(fox_fb[l])),
        mu=row(mu_shift[l]), w0=row(w0_decay[l]),
        wd=jnp.concatenate([w_decay[l], zeros], axis=0).astype(BF16),
        a0=row(a0[l]), wa=jnp.concatenate([zeros, w_aaa[l]], axis=0).astype(BF16),
        kkp=row(k_k[l]), ka=row(k_a[l]), rk=row(r_k[l]), lnx_g=row(lnx_g[l]), lnx_b=row(lnx_b[l]),
        w_out=w_out[l].astype(BF16), ln_g=row(ln_g[l]), ln_b=row(ln_b[l]))


def _run_layer(x, hist, prm, alpha, layer, depth, stacks):
    b, t, d = x.shape
    n = b * t
    x2d = x.reshape(n, d)
    tm = min(ROW_BLOCK, n)
    us, g, qb, kb_st, vb_st, qc, kc_st, vc_st, lf = _proj(
        x2d, prm["w_cat"], prm["fb"], tm, layer, depth, stacks)
    r3 = lambda a: a.reshape(b, t, a.shape[-1])
    us, qb, qc, lf = (r3(a) for a in (us, qb, qc, lf))
    r4 = lambda a: a.reshape(depth, b, t, a.shape[-1])

    if hist is None:
        past = 0
        prev0 = jnp.zeros((b, 1, SHIFT_W), F32)
        s0 = jnp.zeros((b, H_A // 2, LANES, LANES), F32)
        cache_b = cache_c = None
        lf_all = lf
        fox_blk = (min(FOX_BLOCK, t),) * 2
        sb_blk = (min(SB_BLOCK, t),) * 2
    else:
        fk_t, fv_t, h_lf, sk_t, sv_t, h_wkv, h_shift = hist
        past = fk_t.shape[3]
        prev0 = h_shift
        s0 = _state_to_pairs(h_wkv)
        cache_b, cache_c = (fk_t, fv_t), (sk_t, sv_t)
        lf_all = jnp.concatenate([_pad_lanes(h_lf), lf], axis=1)
        fox_blk = sb_blk = (t, min(SB_BLOCK, past))

    oa, s_fin = _wkv(us, prev0, s0, prm, min(WKV_BLOCK, t))
    cum = _cumsum(lf_all)
    ob = _fox(qb, (r4(kb_st), r4(vb_st)), cache_b, cum[:, past:], cum, layer=layer,
              tq=fox_blk[0], tk=fox_blk[1], past=past)
    oc = _sb(qc, (r4(kc_st), r4(vc_st)), cache_c, layer=layer, tq=sb_blk[0], tk=sb_blk[1], past=past)
    y = _out(oa.reshape(n, W_A), ob.reshape(n, W_B), oc.reshape(n, W_C), g, x2d,
             prm["w_out"], prm["ln_g"], prm["ln_b"], alpha, tm)
    small = (lf[:, :, :H_B], _pairs_to_state(s_fin), us[:, -1:, :])
    return y.reshape(b, t, d), (kb_st, vb_st, kc_st, vc_st), small


def kernel(x_prompt, x_sample, cache_fox_k, cache_fox_v, cache_fox_logf, cache_sb_k, cache_sb_v, state_wkv, state_shift, w_in, mu_shift, w0_decay, w_decay, a0, w_aaa, k_k, k_a, r_k, lnx_g, lnx_b, fox_fb, w_out, ln_g, ln_b):
    depth = w_in.shape[0]
    alpha = (2 * depth) ** 0.25
    yp, ys = x_prompt, x_sample

    def time_on_lanes(cache):
        nl, nb, past, nh, hd = cache.shape
        return jnp.transpose(cache, (0, 1, 3, 4, 2)).reshape(nl, nb, nh * hd, past)

    fk_t, fv_t, sk_t, sv_t = (time_on_lanes(c) for c in (cache_fox_k, cache_fox_v, cache_sb_k, cache_sb_v))
    stacks_p, stacks_s = None, None
    small_p, small_s = [], []
    for l in range(depth):
        prm = _layer_params(l, w_in, mu_shift, w0_decay, w_decay, a0, w_aaa, k_k, k_a, r_k,
                            lnx_g, lnx_b, fox_fb, w_out, ln_g, ln_b)
        yp, stacks_p, sm_p = _run_layer(yp, None, prm, alpha, l, depth, stacks_p)
        hist = (fk_t, fv_t, cache_fox_logf[l], sk_t, sv_t, state_wkv[l], state_shift[l])
        ys, stacks_s, sm_s = _run_layer(ys, hist, prm, alpha, l, depth, stacks_s)
        small_p.append(sm_p)
        small_s.append(sm_s)

    def group_outputs(x, stacks, small):
        b, t, _ = x.shape
        kb, vb, kc, vc = stacks
        logf, wkv, shift = (jnp.stack([sm[i] for sm in small]) for i in range(3))
        return (kb.reshape(depth, b, t, H_B, HEAD_DIM), vb.reshape(depth, b, t, H_B, HEAD_DIM), logf,
                kc.reshape(depth, b, t, H_C, HEAD_DIM), vc.reshape(depth, b, t, H_C, HEAD_DIM),
                wkv, shift)

    return ((yp, ys) + group_outputs(x_prompt, stacks_p, small_p)
            + group_outputs(x_sample, stacks_s, small_s))
```

```python
import functools

import numpy as np
import jax
import jax.numpy as jnp
from jax import lax
from jax.experimental import pallas as pl
from jax.experimental.pallas import tpu as pltpu

F32 = jnp.float32
BF16 = jnp.bfloat16

HEAD_DIM = 64
H_A, H_B, H_C = 6, 6, 4
W_A, W_B, W_C = H_A * HEAD_DIM, H_B * HEAD_DIM, H_C * HEAD_DIM
W_MIX = W_A + W_B + W_C
D_LORA = 64
SHIFT_W = 3 * W_A + 2 * D_LORA
LANES = 128
WKV_CHUNK = 64
WKV_BLOCK = 512
WKV_SUB = 256
ROW_BLOCK = 512
FOX_BLOCK = 512
SB_BLOCK = 512
GN_EPS = 64e-5
LN_EPS = 1e-5
NEG_INF = -1e30
LOG2E = 1.4426950408889634
VMEM_LIMIT = 56 * 1024 * 1024


def _cparams(sem):
    return pltpu.CompilerParams(dimension_semantics=sem, vmem_limit_bytes=VMEM_LIMIT)


def _dot(a, b):
    return jnp.dot(a, b, preferred_element_type=F32)


def _dot_nt(a, b):
    return lax.dot_general(a, b, (((1,), (1,)), ((), ())), preferred_element_type=F32)


def _dot_tn(a, b):
    return lax.dot_general(a, b, (((0,), (0,)), ((), ())), preferred_element_type=F32)


def _split3(x):
    hi = x.astype(BF16)
    r1 = x - hi.astype(F32)
    mid = r1.astype(BF16)
    lo = (r1 - mid.astype(F32)).astype(BF16)
    return hi, mid, lo


def _softplus(x):
    return jnp.maximum(x, 0.0) + jnp.log1p(jnp.exp(-jnp.abs(x)))


def _sigmoid(x):
    return 1.0 / (1.0 + jnp.exp(-x))


_OFF_SHIFT = 0
_OFF_G = _OFF_SHIFT + SHIFT_W
_OFF_QB = _OFF_G + W_MIX
_OFF_KB = _OFF_QB + W_B
_OFF_VB = _OFF_KB + W_B
_OFF_QC = _OFF_VB + W_B
_OFF_KC = _OFF_QC + W_C
_OFF_VC = _OFF_KC + W_C
_OFF_F = _OFF_VC + W_C
_PROJ_COLS = _OFF_F + LANES


def _proj_kernel(x_ref, w_ref, fb_ref, *refs):
    us_ref, g_ref, qb_ref, kb_ref, vb_ref, qc_ref, kc_ref, vc_ref, lf_ref = refs[-9:]
    xb = x_ref[...].astype(BF16)
    for ref, off in ((us_ref, _OFF_SHIFT), (g_ref, _OFF_G), (qb_ref, _OFF_QB),
                     (kb_ref, _OFF_KB), (vb_ref, _OFF_VB), (qc_ref, _OFF_QC),
                     (kc_ref, _OFF_KC), (vc_ref, _OFF_VC)):
        width = ref.shape[-1]
        ref[...] = _dot(xb, w_ref[:, off:off + width]).astype(ref.dtype)
    f = _dot(xb, w_ref[:, _OFF_F:_OFF_F + LANES]) + fb_ref[...]
    lf_ref[...] = -_softplus(-f)


_PROJ_WIDTHS = (SHIFT_W, W_MIX, W_B, W_B, W_B, W_C, W_C, W_C, LANES)
_PROJ_BF16 = (1, 2, 5)
_PROJ_STACKED = (3, 4, 6, 7)


def _proj(x2d, w_cat, fb_pad, tm, layer, depth, stacks):
    n, d = x2d.shape
    out_specs, out_shape = [], []
    for idx, w in enumerate(_PROJ_WIDTHS):
        if idx in _PROJ_STACKED:
            out_specs.append(pl.BlockSpec((None, tm, w), lambda i: (layer, i, 0)))
            out_shape.append(jax.ShapeDtypeStruct((depth, n, w), F32))
        else:
            out_specs.append(pl.BlockSpec((tm, w), lambda i: (i, 0)))
            out_shape.append(jax.ShapeDtypeStruct((n, w), BF16 if idx in _PROJ_BF16 else F32))
    in_specs = [pl.BlockSpec((tm, d), lambda i: (i, 0)),
                pl.BlockSpec((d, _PROJ_COLS), lambda i: (0, 0)),
                pl.BlockSpec((1, LANES), lambda i: (0, 0))]
    args = [x2d, w_cat, fb_pad]
    aliases = {}
    if stacks is not None:
        for j, (idx, st) in enumerate(zip(_PROJ_STACKED, stacks)):
            in_specs.append(pl.BlockSpec(memory_space=pl.ANY))
            args.append(st)
            aliases[3 + j] = idx
    return pl.pallas_call(
        _proj_kernel,
        grid=(n // tm,),
        in_specs=in_specs,
        out_specs=out_specs,
        out_shape=out_shape,
        input_output_aliases=aliases,
        compiler_params=_cparams(("parallel",)),
    )(*args)


def _cumsum_kernel(lf_ref, cum_ref, *, blk):
    tk = lf_ref.shape[1]
    row = lax.broadcasted_iota(jnp.int32, (blk, blk), 0)
    col = lax.broadcasted_iota(jnp.int32, (blk, blk), 1)
    tri = (col <= row).astype(BF16)
    carry = jnp.zeros((1, LANES), F32)
    for s in range(0, tk, blk):
        x = lf_ref[0, s:s + blk, :]
        hi, mid, lo = _split3(x)
        c = _dot(tri, hi) + _dot(tri, mid) + _dot(tri, lo) + carry
        cum_ref[0, s:s + blk, :] = c
        carry = c[blk - 1:blk, :]


def _cumsum(lf, blk=64):
    b, tk, _ = lf.shape
    return pl.pallas_call(
        functools.partial(_cumsum_kernel, blk=blk),
        grid=(b,),
        in_specs=[pl.BlockSpec((1, tk, LANES), lambda i: (i, 0, 0))],
        out_specs=pl.BlockSpec((1, tk, LANES), lambda i: (i, 0, 0)),
        out_shape=jax.ShapeDtypeStruct((b, tk, LANES), F32),
        compiler_params=_cparams(("parallel",)),
    )(lf)


def _seg_sum(x, bd):
    xb = x.astype(BF16)
    return jnp.concatenate([_dot(xb[:, s:s + LANES], bd) for s in range(0, x.shape[1], LANES)], axis=1)


def _alternate(*stages):
    gens = [g for g in stages if g is not None]
    while gens:
        for g in list(gens):
            try:
                next(g)
            except StopIteration:
                gens.remove(g)


def _wkv_kernel(us_ref, prev0_ref, s0_ref, mu_ref, w0_ref, wd_ref, a0_ref, wa_ref,
                kkp_ref, ka_ref, rk_ref, lng_ref, lnb_ref,
                oa_ref, sout_ref,
                s_scr, prev_scr, at_s, rt_s, bh_s, kh_s, be_s, ke_s, v_s, ga_s, bn_s, y_s, *, nsub):
    c_idx = pl.program_id(1)
    nc = pl.num_programs(1)
    C = WKV_CHUNK
    tt = us_ref.shape[1]
    ts = tt // nsub
    nb = ts // C
    npair = H_A // 2
    G = 4 * HEAD_DIM

    @pl.when(c_idx == 0)
    def _():
        s_scr[...] = s0_ref[0]
        prev_scr[...] = prev0_ref[0]

    lane1 = lax.broadcasted_iota(jnp.int32, (1, LANES), 1)
    lane_w = lax.broadcasted_iota(jnp.int32, (LANES, LANES), 0) // HEAD_DIM
    lane_c = lax.broadcasted_iota(jnp.int32, (LANES, LANES), 1) // HEAD_DIM
    bd_pair = lane_w == lane_c
    bd = bd_pair.astype(BF16)
    row_g = lax.broadcasted_iota(jnp.int32, (C, G), 0)
    lane_g = lax.broadcasted_iota(jnp.int32, (C, G), 1) % C
    tri_incl = lane_g <= row_g
    tri_strict = lane_g < row_g
    eye = (lane_g == row_g).astype(F32)
    diag4 = (lax.broadcasted_iota(jnp.int32, (G, G), 0) // HEAD_DIM
             == lax.broadcasted_iota(jnp.int32, (G, G), 1) // HEAD_DIM).astype(BF16)
    rowt = lax.broadcasted_iota(jnp.int32, (ts, ts), 0)
    colt = lax.broadcasted_iota(jnp.int32, (ts, ts), 1)
    tri_b = (((rowt // C) == (colt // C)) & (colt <= rowt)).astype(BF16)
    row1 = lax.broadcasted_iota(jnp.int32, (ts, 1), 0)

    def stack4(x):
        xb = x.astype(BF16)
        return jnp.concatenate([xb, xb, xb, xb], axis=0) * diag4

    def mul4(xs, ys):
        return [_dot(x.astype(BF16), stack4(y)) for x, y in zip(xs, ys)]

    def prepare(sb):
        base = sb * ts
        u = us_ref[0, base:base + ts, :]
        before = prev_scr[...] if sb == 0 else us_ref[0, base - 1:base, :]
        zprev = jnp.where(row1 == 0, before, pltpu.roll(u, 1, axis=0))
        zs = u + (zprev - u) * mu_ref[...]
        r = zs[:, 0:W_A]
        k = zs[:, W_A:2 * W_A]
        v = zs[:, 2 * W_A:3 * W_A]
        lora_in = zs[:, 3 * W_A:3 * W_A + LANES]
        lora_t = jnp.where(lane1 < D_LORA, jnp.tanh(lora_in), lora_in).astype(BF16)
        yield
        wl = w0_ref[...] + _dot(lora_t, wd_ref[...])
        a = _sigmoid(a0_ref[...] + _dot(lora_t, wa_ref[...]))
        w_log = -_softplus(-wl) - 0.5
        lw = -jnp.exp(w_log)
        yield
        kk = k * kkp_ref[...]
        kk = kk * lax.rsqrt(_seg_sum(kk * kk, bd) + 1e-12)
        k2 = k * (1.0 + (a - 1.0) * ka_ref[...])
        bn_s[sb] = _seg_sum(r * k2 * rk_ref[...], bd) * v
        v_s[sb] = v
        yield
        h1, h2, h3 = _split3(lw)
        cs = _dot(tri_b, h1) + _dot(tri_b, h2) + _dot(tri_b, h3)
        tot = jnp.concatenate([jnp.broadcast_to(cs[c * C + C - 1:c * C + C, :], (C, W_A))
                               for c in range(nb)], axis=0)
        yield
        g_inv = jnp.exp(-cs)
        g_end = jnp.exp(tot - cs)
        kka = kk * a
        at_s[sb] = kk * jnp.exp(cs - lw)
        rt_s[sb] = r * jnp.exp(cs)
        yield
        bh_s[sb] = (-kka * g_inv).astype(BF16)
        kh_s[sb] = (k2 * g_inv).astype(BF16)
        be_s[sb] = (-kka * g_end).astype(BF16)
        ke_s[sb] = (k2 * g_end).astype(BF16)
        ga_s[sb] = jnp.exp(tot)
        yield

    local = {}

    def chunk_terms(sb):
        groups = []
        for c in range(nb):
            groups.append([(c, 0), (c, LANES)])
        for c in range(0, nb, 2):
            groups.append([(c, 2 * LANES)] + ([(c + 1, 2 * LANES)] if c + 1 < nb else [None]))

        def gather(ref, grp):
            parts = [jnp.zeros((C, LANES), ref.dtype) if u is None
                     else ref[sb, u[0] * C:(u[0] + 1) * C, u[1]:u[1] + LANES] for u in grp]
            return jnp.concatenate(parts, axis=1)

        at_g = [gather(at_s, g) for g in groups]
        rt_g = [gather(rt_s, g) for g in groups]
        v_g = [gather(v_s, g) for g in groups]
        lhs = [jnp.concatenate([a, r], axis=0).astype(BF16) for a, r in zip(at_g, rt_g)]
        rhs = [jnp.concatenate([stack4(gather(bh_s, g)), stack4(gather(kh_s, g))], axis=0) for g in groups]
        a4 = [_dot_nt(l, r) for l, r in zip(lhs, rhs)]
        yield
        n1 = [jnp.where(tri_strict, a[:C, :G], 0.0) for a in a4]
        a_ak = [jnp.where(tri_strict, a[:C, G:], 0.0) for a in a4]
        a_rb = [jnp.where(tri_incl, a[C:, :G], 0.0) for a in a4]
        a_rk = [jnp.where(tri_incl, a[C:, G:], 0.0) for a in a4]
        vst = [stack4(v) for v in v_g]
        avy = [_dot(jnp.concatenate([ak, rk], axis=0).astype(BF16), vs)
               for ak, rk, vs in zip(a_ak, a_rk, vst)]
        yield
        pair4 = lambda xs, ys: [eye + x + y + m for x, y, m in zip(xs, ys, mul4(xs, ys))]
        n2 = mul4(n1, n1)
        yield
        n4 = mul4(n2, n2)
        p1 = pair4(n1, n2)
        yield
        n8 = mul4(n4, n4)
        yield
        n16 = mul4(n8, n8)
        p2 = pair4(n4, n8)
        yield
        n32 = mul4(n16, n16)
        p12 = mul4(p1, p2)
        yield
        p3 = pair4(n16, n32)
        yield
        t_inv = mul4(p12, p3)
        yield
        xs = [_dot(t.astype(BF16), jnp.concatenate([stack4(a), stack4(y[:C])], axis=1))
              for t, a, y in zip(t_inv, at_g, avy)]
        yield
        rys = [_dot(rb.astype(BF16), jnp.concatenate([stack4(x[:, :G]), stack4(x[:, G:])], axis=1))
               for rb, x in zip(a_rb, xs)]
        rp_g = [r + ry[:, :G] for r, ry in zip(rt_g, rys)]
        y0_g = [ry[:, G:] + y[C:] for ry, y in zip(rys, avy)]
        yield
        for gi, grp in enumerate(groups):
            for kpos, u in enumerate(grp):
                if u is None:
                    continue
                c, p = u[0], u[1] // LANES
                rows = slice(c * C, (c + 1) * C)
                sl = slice(p * LANES, (p + 1) * LANES)
                gl = slice(kpos * LANES, (kpos + 1) * LANES)
                be_u, ke_u = be_s[sb, rows, sl], ke_s[sb, rows, sl]
                atp = xs[gi][:, gl].astype(BF16)
                wv = jnp.concatenate([xs[gi][:, G + kpos * LANES:G + (kpos + 1) * LANES],
                                      v_s[sb, rows, sl]], axis=0).astype(BF16)
                pp = jnp.where(bd_pair, _dot_tn(atp, be_u), 0.0).astype(BF16)
                q = jnp.where(bd_pair, _dot_tn(wv, jnp.concatenate([be_u, ke_u], axis=0)), 0.0)
                local[sb, c, p] = (rp_g[gi][:, gl].astype(BF16), y0_g[gi][:, gl], pp, q)
        yield

    def sweep(sb):
        for c in range(nb):
            rows = slice(c * C, (c + 1) * C)
            s_ps = [s_scr[p] for p in range(npair)]
            s_bs = [s_p.astype(BF16) for s_p in s_ps]
            for p in range(npair):
                sl = slice(p * LANES, (p + 1) * LANES)
                rp, y0, pp, q = local[sb, c, p]
                y_s[sb, rows, sl] = _dot_nt(rp, s_bs[p]) + y0
                s_scr[p] = s_ps[p] * ga_s[sb, c * C:c * C + 1, sl] + _dot(s_bs[p], pp) + q
            yield
        y = y_s[sb]
        inv_n = 1.0 / HEAD_DIM
        mean = _seg_sum(y, bd) * inv_n
        d = y - mean
        yield
        var = _seg_sum(d * d, bd) * inv_n
        yn = d * lax.rsqrt(var + GN_EPS) * lng_ref[...] + lnb_ref[...]
        oa_ref[0, sb * ts:(sb + 1) * ts, :] = (yn + bn_s[sb]).astype(oa_ref.dtype)
        yield

    _alternate(prepare(0))
    for sb in range(nsub):
        _alternate(chunk_terms(sb),
                   prepare(sb + 1) if sb + 1 < nsub else None,
                   sweep(sb - 1) if sb > 0 else None)
    _alternate(sweep(nsub - 1))
    prev_scr[...] = us_ref[0, tt - 1:tt, :]

    @pl.when(c_idx == nc - 1)
    def _():
        sout_ref[0] = s_scr[...]


def _wkv(us, prev0, s0_bd, prm, tt):
    b, t, _ = us.shape
    npair = H_A // 2
    nsub = 2 if tt >= 2 * WKV_SUB else 1
    ts = tt // nsub
    vec = lambda w: pl.BlockSpec((1, w), lambda i, j: (0, 0))
    mat = lambda: pl.BlockSpec((LANES, W_A), lambda i, j: (0, 0))
    blk = lambda dt: pltpu.VMEM((nsub, ts, W_A), dt)
    return pl.pallas_call(
        functools.partial(_wkv_kernel, nsub=nsub),
        grid=(b, t // tt),
        in_specs=[pl.BlockSpec((1, tt, SHIFT_W), lambda i, j: (i, j, 0)),
                  pl.BlockSpec((1, 1, SHIFT_W), lambda i, j: (i, 0, 0)),
                  pl.BlockSpec((1, npair, LANES, LANES), lambda i, j: (i, 0, 0, 0)),
                  vec(SHIFT_W), vec(W_A), mat(), vec(W_A), mat(),
                  vec(W_A), vec(W_A), vec(W_A), vec(W_A), vec(W_A)],
        out_specs=[pl.BlockSpec((1, tt, W_A), lambda i, j: (i, j, 0)),
                   pl.BlockSpec((1, npair, LANES, LANES), lambda i, j: (i, 0, 0, 0))],
        out_shape=[jax.ShapeDtypeStruct((b, t, W_A), BF16),
                   jax.ShapeDtypeStruct((b, npair, LANES, LANES), F32)],
        scratch_shapes=[pltpu.VMEM((npair, LANES, LANES), F32),
                        pltpu.VMEM((1, SHIFT_W), F32),
                        blk(F32), blk(F32), blk(BF16), blk(BF16), blk(BF16), blk(BF16),
                        blk(F32), blk(F32), blk(F32), blk(F32)],
        compiler_params=_cparams(("parallel", "arbitrary")),
    )(us, prev0, s0_bd, prm["mu"], prm["w0"], prm["wd"], prm["a0"], prm["wa"],
      prm["kkp"], prm["ka"], prm["rk"], prm["lnx_g"], prm["lnx_b"])


_AUG = 8
_PREP_ROWS = 256


def _bias_lanes(cum, key_side):
    ntile = H_B // 2 if key_side else H_B
    width = ntile * LANES
    r = lax.broadcasted_iota(jnp.int32, (LANES, width), 0)
    c = lax.broadcasted_iota(jnp.int32, (LANES, width), 1)
    lane = lax.broadcasted_iota(jnp.int32, (1, width), 1)
    f_off, one_off = (0, 3) if key_side else (3, 0)
    sign = -1.0 if key_side else 1.0
    head_of_tile = (c // LANES) * 2 + (c % LANES) // _AUG if key_side else c // LANES
    slot = c % _AUG
    in_head_lanes = (c % LANES) // _AUG == r % 2
    one_head = (lane % LANES) // _AUG < 2 if key_side else (lane % LANES) // _AUG == (lane // LANES) % 2
    out = jnp.where(one_head & (lane % _AUG >= one_off) & (lane % _AUG < one_off + 3), 1.0, 0.0)
    for t, part in enumerate(_split3(cum)):
        sel = jnp.where((head_of_tile == r) & in_head_lanes & (slot == f_off + t), sign, 0.0).astype(BF16)
        out = out + _dot(part, sel)
    return [out[:, i * LANES:(i + 1) * LANES] for i in range(ntile)]


def _stage_rows(total):
    return [(s, min(_PREP_ROWS, total - s)) for s in range(0, total, _PREP_ROWS)]


def _transpose_rows(x):
    n = x.shape[0]
    n_pad = -n % LANES
    if n_pad:
        x = jnp.concatenate([x, jnp.zeros((n_pad, x.shape[1]), x.dtype)], axis=0)
    return x.T[:, :n]


def _fox_kernel(*refs, tq, tk, past, single):
    if past:
        q_ref, kc_ref, vc_ref, k_ref, v_ref, cq_ref, ck_ref, o_ref, ka_scr, vt_scr, acc_scr = refs
    else:
        q_ref, k_ref, v_ref, cq_ref, ck_ref, o_ref, ka_scr, vt_scr, acc_scr = refs
    npair = H_B // 2
    lane1 = lax.broadcasted_iota(jnp.int32, (1, LANES), 1)
    head_of_lane = lane1 // HEAD_DIM

    def stage_bias(s, n):
        cum = ck_ref[0, s:s + n, :] * LOG2E
        for p, aug in enumerate(_bias_lanes(cum, True)):
            ka_scr[p, s:s + n, LANES:2 * LANES] = aug.astype(BF16)

    def stage_kv():
        for s, n in _stage_rows(past):
            for p in range(npair):
                ka_scr[p, s:s + n, 0:LANES] = kc_ref[0, p * LANES:(p + 1) * LANES, s:s + n].T.astype(BF16)
            vt_scr[:, s:s + n] = vc_ref[0, :, s:s + n].astype(BF16)
            stage_bias(s, n)
        for s, n in _stage_rows(k_ref.shape[1]):
            kb = k_ref[0, s:s + n, :].astype(BF16)
            for p in range(npair):
                ka_scr[p, past + s:past + s + n, 0:LANES] = kb[:, p * LANES:(p + 1) * LANES]
            vt_scr[:, past + s:past + s + n] = _transpose_rows(v_ref[0, s:s + n, :]).astype(BF16)
            stage_bias(past + s, n)

    if single:
        i = 0
        stage_kv()
    else:
        i = pl.program_id(1)
        pl.when(i == 0)(stage_kv)

    q = q_ref[0].astype(F32) * (HEAD_DIM ** -0.5 * LOG2E)
    cq = cq_ref[0] * LOG2E
    rhs = []
    q_bias = _bias_lanes(cq, False)
    for p in range(npair):
        q_p = q[:, p * LANES:(p + 1) * LANES]
        halves = [jnp.concatenate([jnp.where(head_of_lane == hh, q_p, 0.0), q_bias[2 * p + hh]], axis=1)
                  for hh in range(2)]
        rhs.append(jnp.concatenate(halves, axis=0).astype(BF16))

    n_full = past // tk + i * (tq // tk)
    diag0 = past if single else pl.multiple_of(past + i * tq, tq)
    krow = lax.broadcasted_iota(jnp.int32, (tq, 2 * tq), 0)
    qcol = lax.broadcasted_iota(jnp.int32, (tq, 2 * tq), 1)
    causal_t = krow <= jnp.where(qcol >= tq, qcol - tq, qcol)

    acc_scr[...] = jnp.zeros(acc_scr.shape, F32)

    def update(carry, start, size, mask):
        ms, ls = carry
        st = [_dot_nt(ka_scr[p, pl.ds(start, size), :], rhs[p]) for p in range(npair)]
        if mask is not None:
            st = [jnp.where(mask, s, NEG_INF) for s in st]
        m_new = [jnp.maximum(m, jnp.max(s, axis=0, keepdims=True)) for m, s in zip(ms, st)]
        alpha = [jnp.exp2(m - mn) for m, mn in zip(ms, m_new)]
        pt = [jnp.exp2(s - mn) for s, mn in zip(st, m_new)]
        l_new = [a * l + jnp.sum(x, axis=0, keepdims=True) for a, l, x in zip(alpha, ls, pt)]
        pv = [_dot(vt_scr[p * LANES:(p + 1) * LANES, pl.ds(start, size)], pt[p].astype(BF16))
              for p in range(npair)]
        for p in range(npair):
            acc_scr[p] = acc_scr[p] * alpha[p] + pv[p]
        return tuple(m_new), tuple(l_new)

    def body(j, carry):
        return update(carry, pl.multiple_of(j * tk, tk), tk, None)

    init = (tuple(jnp.full((1, 2 * tq), NEG_INF, F32) for _ in range(npair)),
            tuple(jnp.zeros((1, 2 * tq), F32) for _ in range(npair)))
    carry = lax.fori_loop(0, n_full, body, init)
    _, ls = update(carry, diag0, tq, causal_t)
    for p in range(npair):
        o_t = (acc_scr[p] / ls[p]).T
        o_ref[0, :, p * LANES:(p + 1) * LANES] = jnp.where(
            head_of_lane == 0, o_t[:tq], o_t[tq:]).astype(o_ref.dtype)


def _kv_specs(k_new, cache, layer, width):
    specs, args = [], []
    t_new = k_new[0].shape[2]
    if cache is not None:
        past = cache[0].shape[3]
        for c in cache:
            specs.append(pl.BlockSpec((None, 1, width, past), lambda i, j: (layer, i, 0, 0)))
            args.append(c)
    for a in k_new:
        specs.append(pl.BlockSpec((None, 1, t_new, width), lambda i, j: (layer, i, 0, 0)))
        args.append(a)
    return specs, args


def _fox(q, kv_new, cache, cum_q, cum_k, *, layer, tq, tk, past):
    b, t, _ = q.shape
    tkk = past + kv_new[0].shape[2]
    npair = H_B // 2
    kv_specs, kv_args = _kv_specs(kv_new, cache, layer, W_B)
    return pl.pallas_call(
        functools.partial(_fox_kernel, tq=tq, tk=tk, past=past, single=(t == tq)),
        grid=(b, t // tq),
        in_specs=[pl.BlockSpec((1, tq, W_B), lambda i, j: (i, j, 0))] + kv_specs + [
                  pl.BlockSpec((1, tq, LANES), lambda i, j: (i, j, 0)),
                  pl.BlockSpec((1, tkk, LANES), lambda i, j: (i, 0, 0))],
        out_specs=pl.BlockSpec((1, tq, W_B), lambda i, j: (i, j, 0)),
        out_shape=jax.ShapeDtypeStruct((b, t, W_B), BF16),
        scratch_shapes=[pltpu.VMEM((npair, tkk, 2 * LANES), BF16), pltpu.VMEM((W_B, tkk), BF16),
                        pltpu.VMEM((npair, LANES, 2 * tq), F32)],
        compiler_params=_cparams(("parallel", "arbitrary")),
    )(q, *kv_args, cum_q, cum_k)


def _sb_kernel(*refs, tq, tk, past, single):
    if past:
        q_ref, kc_ref, vc_ref, k_ref, v_ref, o_ref, kb_scr, vt_scr, acc_scr = refs
    else:
        q_ref, k_ref, v_ref, o_ref, kb_scr, vt_scr, acc_scr = refs
    npair = H_C // 2
    lane1 = lax.broadcasted_iota(jnp.int32, (1, LANES), 1)
    head_of_lane = lane1 // HEAD_DIM

    def stage_kv():
        for s, n in _stage_rows(past):
            kb_scr[s:s + n, :] = kc_ref[0, :, s:s + n].T.astype(BF16)
            vt_scr[:, s:s + n] = vc_ref[0, :, s:s + n].astype(BF16)
        for s, n in _stage_rows(k_ref.shape[1]):
            kb_scr[past + s:past + s + n, :] = k_ref[0, s:s + n, :].astype(BF16)
            vt_scr[:, past + s:past + s + n] = _transpose_rows(v_ref[0, s:s + n, :]).astype(BF16)

    if single:
        i = 0
        stage_kv()
    else:
        i = pl.program_id(1)
        pl.when(i == 0)(stage_kv)

    q = q_ref[0].astype(F32) * (HEAD_DIM ** -0.5 * LOG2E)
    rhs = []
    for p in range(npair):
        q_p = q[:, p * LANES:(p + 1) * LANES]
        rhs.append(jnp.concatenate([jnp.where(head_of_lane == hh, q_p, 0.0) for hh in range(2)],
                                   axis=0).astype(BF16))

    def earlier_matrix(n):
        rr = lax.broadcasted_iota(jnp.int32, (n, n), 0)
        cc = lax.broadcasted_iota(jnp.int32, (n, n), 1)
        return (cc > rr).astype(BF16)

    n_full = past // tk + i * (tq // tk)
    diag0 = past if single else pl.multiple_of(past + i * tq, tq)
    krow = lax.broadcasted_iota(jnp.int32, (tq, 2 * tq), 0)
    qcol = lax.broadcasted_iota(jnp.int32, (tq, 2 * tq), 1)
    strict_t = krow < jnp.where(qcol >= tq, qcol - tq, qcol)

    def update(runs, start, size, mask, first):
        zt = [_dot_nt(kb_scr[pl.ds(start, size), p * LANES:(p + 1) * LANES], rhs[p])
              for p in range(npair)]
        mz = [jnp.minimum(z, 0.0) for z in zt]
        tail = [jnp.log(1.0 + jnp.exp2(m + m - z)) * LOG2E for m, z in zip(mz, zt)]
        lsig = [m - t for m, t in zip(mz, tail)]
        l1m = [s - z for s, z in zip(lsig, zt)]
        if mask is not None:
            l1m = [jnp.where(mask, x, 0.0) for x in l1m]
        later = earlier_matrix(size)
        after = [_dot(later, x.astype(BF16)) for x in l1m]
        wt = [jnp.exp2(s + a + r) for s, a, r in zip(lsig, after, runs)]
        if mask is not None:
            wt = [jnp.where(mask, w, 0.0) for w in wt]
        pv = [_dot(vt_scr[p * LANES:(p + 1) * LANES, pl.ds(start, size)], wt[p].astype(BF16))
              for p in range(npair)]
        for p in range(npair):
            acc_scr[p] = pv[p] if first else acc_scr[p] + pv[p]
        return tuple(r + jnp.sum(x, axis=0, keepdims=True) for r, x in zip(runs, l1m))

    runs = tuple(jnp.zeros((1, 2 * tq), F32) for _ in range(npair))
    runs = update(runs, diag0, tq, strict_t, True)

    def body(jj, runs):
        j = n_full - 1 - jj
        return update(runs, pl.multiple_of(j * tk, tk), tk, None, False)

    lax.fori_loop(0, n_full, body, runs)
    for p in range(npair):
        o_t = acc_scr[p].T
        o_ref[0, :, p * LANES:(p + 1) * LANES] = jnp.where(
            head_of_lane == 0, o_t[:tq], o_t[tq:]).astype(o_ref.dtype)


def _sb(q, kv_new, cache, *, layer, tq, tk, past):
    b, t, _ = q.shape
    tkk = past + kv_new[0].shape[2]
    npair = H_C // 2
    kv_specs, kv_args = _kv_specs(kv_new, cache, layer, W_C)
    return pl.pallas_call(
        functools.partial(_sb_kernel, tq=tq, tk=tk, past=past, single=(t == tq)),
        grid=(b, t // tq),
        in_specs=[pl.BlockSpec((1, tq, W_C), lambda i, j: (i, j, 0))] + kv_specs,
        out_specs=pl.BlockSpec((1, tq, W_C), lambda i, j: (i, j, 0)),
        out_shape=jax.ShapeDtypeStruct((b, t, W_C), BF16),
        scratch_shapes=[pltpu.VMEM((tkk, W_C), BF16), pltpu.VMEM((W_C, tkk), BF16),
                        pltpu.VMEM((npair, LANES, 2 * tq), F32)],
        compiler_params=_cparams(("parallel", "arbitrary")),
    )(q, *kv_args)


def _out_kernel(oa_ref, ob_ref, oc_ref, g_ref, x_ref, w_ref, lng_ref, lnb_ref, y_ref, *, alpha):
    g = g_ref[...].astype(F32)
    gate = g * _sigmoid(g)
    gated = lambda o_ref, lo, hi: (o_ref[...].astype(F32) * gate[:, lo:hi]).astype(BF16)
    acc = _dot(gated(oa_ref, 0, W_A), w_ref[0:W_A, :])
    acc = acc + _dot(gated(ob_ref, W_A, W_A + W_B), w_ref[W_A:W_A + W_B, :])
    acc = acc + _dot(gated(oc_ref, W_A + W_B, W_MIX), w_ref[W_A + W_B:, :])
    z = alpha * x_ref[...] + acc
    mu = jnp.mean(z, axis=-1, keepdims=True)
    d = z - mu
    var = jnp.mean(d * d, axis=-1, keepdims=True)
    y_ref[...] = d * lax.rsqrt(var + LN_EPS) * lng_ref[...] + lnb_ref[...]


def _out(oa, ob, oc, g, x2d, w_out, ln_g, ln_b, alpha, tm):
    n, d = x2d.shape
    row = lambda w: pl.BlockSpec((tm, w), lambda i: (i, 0))
    return pl.pallas_call(
        functools.partial(_out_kernel, alpha=alpha),
        grid=(n // tm,),
        in_specs=[row(W_A), row(W_B), row(W_C), row(W_MIX), row(d),
                  pl.BlockSpec((W_MIX, d), lambda i: (0, 0)),
                  pl.BlockSpec((1, d), lambda i: (0, 0)),
                  pl.BlockSpec((1, d), lambda i: (0, 0))],
        out_specs=row(d),
        out_shape=jax.ShapeDtypeStruct((n, d), F32),
        compiler_params=_cparams(("parallel",)),
    )(oa, ob, oc, g, x2d, w_out, ln_g, ln_b)


def _pad_lanes(a, width=LANES):
    return jnp.pad(a, [(0, 0)] * (a.ndim - 1) + [(0, width - a.shape[-1])])


def _state_to_pairs(s):
    b = s.shape[0]
    s = s.reshape(b, H_A // 2, 2, HEAD_DIM, HEAD_DIM)
    eye2 = jnp.eye(2, dtype=s.dtype)
    bd = jnp.einsum("bphvk,hg->bphvgk", s, eye2)
    return bd.reshape(b, H_A // 2, LANES, LANES)


def _pairs_to_state(sp):
    b = sp.shape[0]
    s = sp.reshape(b, H_A // 2, 2, HEAD_DIM, 2, HEAD_DIM)
    s = jnp.stack([s[:, :, 0, :, 0, :], s[:, :, 1, :, 1, :]], axis=2)
    return s.reshape(b, H_A, HEAD_DIM, HEAD_DIM)


_IN_SIZES = (SHIFT_W, W_A, W_B, W_B, W_B, H_B, W_B, W_C, W_C, W_C, W_C)
_IN_OFFS = tuple(int(v) for v in np.concatenate([[0], np.cumsum(_IN_SIZES)]))
_SRC_SHIFT, _SRC_GA, _SRC_QB, _SRC_KB, _SRC_VB, _SRC_F, _SRC_GB, _SRC_QC, _SRC_KC, _SRC_VC, _SRC_GC = (
    _IN_OFFS[:-1])
_W_MOVES = ((_SRC_SHIFT, _OFF_SHIFT, SHIFT_W), (_SRC_GA, _OFF_G, W_A), (_SRC_GB, _OFF_G + W_A, W_B),
            (_SRC_GC, _OFF_G + W_A + W_B, W_C), (_SRC_QB, _OFF_QB, W_B), (_SRC_KB, _OFF_KB, W_B),
            (_SRC_VB, _OFF_VB, W_B), (_SRC_QC, _OFF_QC, W_C), (_SRC_KC, _OFF_KC, W_C),
            (_SRC_VC, _OFF_VC, W_C))
_W_STAGE_ROWS = 128


def _stage_w_kernel(w_ref, o_ref):
    for src, dst, width in _W_MOVES:
        o_ref[:, dst:dst + width] = w_ref[:, src:src + width].astype(BF16)
    lane = lax.broadcasted_iota(jnp.int32, (1, LANES), 1)
    o_ref[:, _OFF_F:_OFF_F + LANES] = jnp.where(
        lane < H_B, w_ref[:, _SRC_F:_SRC_F + LANES], 0.0).astype(BF16)


def _stage_w(w_in, layer):
    _, d, cols = w_in.shape
    return pl.pallas_call(
        _stage_w_kernel,
        grid=(d // _W_STAGE_ROWS,),
        in_specs=[pl.BlockSpec((None, _W_STAGE_ROWS, cols), lambda i: (layer, i, 0))],
        out_specs=pl.BlockSpec((_W_STAGE_ROWS, _PROJ_COLS), lambda i: (i, 0)),
        out_shape=jax.ShapeDtypeStruct((d, _PROJ_COLS), BF16),
        compiler_params=_cparams(("parallel",)),
    )(w_in)


def _layer_params(l, w_in, mu_shift, w0_decay, w_decay, a0, w_aaa, k_k, k_a, r_k,
                  lnx_g, lnx_b, fox_fb, w_out, ln_g, ln_b):
    w_cat = _stage_w(w_in, l)
    zeros = jnp.zeros((D_LORA, W_A), F32)
    row = lambda a: a.reshape(1, -1).astype(F32)
    return dict(
        w_cat=w_cat, fb=_pad_lanes(row(fox_fb[l])),
        mu=row(mu_shift[l]), w0=row(w0_decay[l]),
        wd=jnp.concatenate([w_decay[l], zeros], axis=0).astype(BF16),
        a0=row(a0[l]), wa=jnp.concatenate([zeros, w_aaa[l]], axis=0).astype(BF16),
        kkp=row(k_k[l]), ka=row(k_a[l]), rk=row(r_k[l]), lnx_g=row(lnx_g[l]), lnx_b=row(lnx_b[l]),
        w_out=w_out[l].astype(BF16), ln_g=row(ln_g[l]), ln_b=row(ln_b[l]))


def _run_layer(x, hist, prm, alpha, layer, depth, stacks):
    b, t, d = x.shape
    n = b * t
    x2d = x.reshape(n, d)
    tm = min(ROW_BLOCK, n)
    us, g, qb, kb_st, vb_st, qc, kc_st, vc_st, lf = _proj(
        x2d, prm["w_cat"], prm["fb"], tm, layer, depth, stacks)
    r3 = lambda a: a.reshape(b, t, a.shape[-1])
    us, qb, qc, lf = (r3(a) for a in (us, qb, qc, lf))
    r4 = lambda a: a.reshape(depth, b, t, a.shape[-1])

    if hist is None:
        past = 0
        prev0 = jnp.zeros((b, 1, SHIFT_W), F32)
        s0 = jnp.zeros((b, H_A // 2, LANES, LANES), F32)
        cache_b = cache_c = None
        lf_all = lf
        fox_blk = (min(FOX_BLOCK, t),) * 2
        sb_blk = (min(SB_BLOCK, t),) * 2
    else:
        fk_t, fv_t, h_lf, sk_t, sv_t, h_wkv, h_shift = hist
        past = fk_t.shape[3]
        prev0 = h_shift
        s0 = _state_to_pairs(h_wkv)
        cache_b, cache_c = (fk_t, fv_t), (sk_t, sv_t)
        lf_all = jnp.concatenate([_pad_lanes(h_lf), lf], axis=1)
        fox_blk = sb_blk = (t, min(SB_BLOCK, past))

    oa, s_fin = _wkv(us, prev0, s0, prm, min(WKV_BLOCK, t))
    cum = _cumsum(lf_all)
    ob = _fox(qb, (r4(kb_st), r4(vb_st)), cache_b, cum[:, past:], cum, layer=layer,
              tq=fox_blk[0], tk=fox_blk[1], past=past)
    oc = _sb(qc, (r4(kc_st), r4(vc_st)), cache_c, layer=layer, tq=sb_blk[0], tk=sb_blk[1], past=past)
    y = _out(oa.reshape(n, W_A), ob.reshape(n, W_B), oc.reshape(n, W_C), g, x2d,
             prm["w_out"], prm["ln_g"], prm["ln_b"], alpha, tm)
    small = (lf[:, :, :H_B], _pairs_to_state(s_fin), us[:, -1:, :])
    return y.reshape(b, t, d), (kb_st, vb_st, kc_st, vc_st), small


def kernel(x_prompt, x_sample, cache_fox_k, cache_fox_v, cache_fox_logf, cache_sb_k, cache_sb_v, state_wkv, state_shift, w_in, mu_shift, w0_decay, w_decay, a0, w_aaa, k_k, k_a, r_k, lnx_g, lnx_b, fox_fb, w_out, ln_g, ln_b):
    depth = w_in.shape[0]
    alpha = (2 * depth) ** 0.25
    yp, ys = x_prompt, x_sample

    def time_on_lanes(cache):
        nl, nb, past, nh, hd = cache.shape
        return jnp.transpose(cache, (0, 1, 3, 4, 2)).reshape(nl, nb, nh * hd, past)

    fk_t, fv_t, sk_t, sv_t = (time_on_lanes(c) for c in (cache_fox_k, cache_fox_v, cache_sb_k, cache_sb_v))
    stacks_p, stacks_s = None, None
    small_p, small_s = [], []
    for l in range(depth):
        prm = _layer_params(l, w_in, mu_shift, w0_decay, w_decay, a0, w_aaa, k_k, k_a, r_k,
                            lnx_g, lnx_b, fox_fb, w_out, ln_g, ln_b)
        yp, stacks_p, sm_p = _run_layer(yp, None, prm, alpha, l, depth, stacks_p)
        hist = (fk_t, fv_t, cache_fox_logf[l], sk_t, sv_t, state_wkv[l], state_shift[l])
        ys, stacks_s, sm_s = _run_layer(ys, hist, prm, alpha, l, depth, stacks_s)
        small_p.append(sm_p)
        small_s.append(sm_s)

    def group_outputs(x, stacks, small):
        b, t, _ = x.shape
        kb, vb, kc, vc = stacks
        logf, wkv, shift = (jnp.stack([sm[i] for sm in small]) for i in range(3))
        return (kb.reshape(depth, b, t, H_B, HEAD_DIM), vb.reshape(depth, b, t, H_B, HEAD_DIM), logf,
                kc.reshape(depth, b, t, H_C, HEAD_DIM), vc.reshape(depth, b, t, H_C, HEAD_DIM),
                wkv, shift)

    return ((yp, ys) + group_outputs(x_prompt, stacks_p, small_p)
            + group_outputs(x_sample, stacks_s, small_s))
```

```python
import functools

import numpy as np
import jax
import jax.numpy as jnp
from jax import lax
from jax.experimental import pallas as pl
from jax.experimental.pallas import tpu as pltpu

F32 = jnp.float32
BF16 = jnp.bfloat16

HEAD_DIM = 64
H_A, H_B, H_C = 6, 6, 4
W_A, W_B, W_C = H_A * HEAD_DIM, H_B * HEAD_DIM, H_C * HEAD_DIM
W_MIX = W_A + W_B + W_C
D_LORA = 64
SHIFT_W = 3 * W_A + 2 * D_LORA
LANES = 128
WKV_CHUNK = 64
WKV_BLOCK = 512
WKV_SUB = 256
ROW_BLOCK = 512
FOX_BLOCK = 512
SB_BLOCK = 512
GN_EPS = 64e-5
LN_EPS = 1e-5
NEG_INF = -1e30
LOG2E = 1.4426950408889634
VMEM_LIMIT = 56 * 1024 * 1024


def _cparams(sem):
    return pltpu.CompilerParams(dimension_semantics=sem, vmem_limit_bytes=VMEM_LIMIT)


def _dot(a, b):
    return jnp.dot(a, b, preferred_element_type=F32)


def _dot_nt(a, b):
    return lax.dot_general(a, b, (((1,), (1,)), ((), ())), preferred_element_type=F32)


def _dot_tn(a, b):
    return lax.dot_general(a, b, (((0,), (0,)), ((), ())), preferred_element_type=F32)


def _split3(x):
    hi = x.astype(BF16)
    r1 = x - hi.astype(F32)
    mid = r1.astype(BF16)
    lo = (r1 - mid.astype(F32)).astype(BF16)
    return hi, mid, lo


def _softplus(x):
    return jnp.maximum(x, 0.0) + jnp.log1p(jnp.exp(-jnp.abs(x)))


def _sigmoid(x):
    return 1.0 / (1.0 + jnp.exp(-x))


_OFF_SHIFT = 0
_OFF_G = _OFF_SHIFT + SHIFT_W
_OFF_QB = _OFF_G + W_MIX
_OFF_KB = _OFF_QB + W_B
_OFF_VB = _OFF_KB + W_B
_OFF_QC = _OFF_VB + W_B
_OFF_KC = _OFF_QC + W_C
_OFF_VC = _OFF_KC + W_C
_OFF_F = _OFF_VC + W_C
_PROJ_COLS = _OFF_F + LANES


def _proj_kernel(x_ref, w_ref, fb_ref, *refs):
    us_ref, g_ref, qb_ref, kb_ref, vb_ref, qc_ref, kc_ref, vc_ref, lf_ref = refs[-9:]
    xb = x_ref[...].astype(BF16)
    for ref, off in ((us_ref, _OFF_SHIFT), (g_ref, _OFF_G), (qb_ref, _OFF_QB),
                     (kb_ref, _OFF_KB), (vb_ref, _OFF_VB), (qc_ref, _OFF_QC),
                     (kc_ref, _OFF_KC), (vc_ref, _OFF_VC)):
        width = ref.shape[-1]
        ref[...] = _dot(xb, w_ref[:, off:off + width]).astype(ref.dtype)
    f = _dot(xb, w_ref[:, _OFF_F:_OFF_F + LANES]) + fb_ref[...]
    lf_ref[...] = -_softplus(-f)


_PROJ_WIDTHS = (SHIFT_W, W_MIX, W_B, W_B, W_B, W_C, W_C, W_C, LANES)
_PROJ_BF16 = (1, 2, 5)
_PROJ_STACKED = (3, 4, 6, 7)


def _proj(x2d, w_cat, fb_pad, tm, layer, depth, stacks):
    n, d = x2d.shape
    out_specs, out_shape = [], []
    for idx, w in enumerate(_PROJ_WIDTHS):
        if idx in _PROJ_STACKED:
            out_specs.append(pl.BlockSpec((None, tm, w), lambda i: (layer, i, 0)))
            out_shape.append(jax.ShapeDtypeStruct((depth, n, w), F32))
        else:
            out_specs.append(pl.BlockSpec((tm, w), lambda i: (i, 0)))
            out_shape.append(jax.ShapeDtypeStruct((n, w), BF16 if idx in _PROJ_BF16 else F32))
    in_specs = [pl.BlockSpec((tm, d), lambda i: (i, 0)),
                pl.BlockSpec((d, _PROJ_COLS), lambda i: (0, 0)),
                pl.BlockSpec((1, LANES), lambda i: (0, 0))]
    args = [x2d, w_cat, fb_pad]
    aliases = {}
    if stacks is not None:
        for j, (idx, st) in enumerate(zip(_PROJ_STACKED, stacks)):
            in_specs.append(pl.BlockSpec(memory_space=pl.ANY))
            args.append(st)
            aliases[3 + j] = idx
    return pl.pallas_call(
        _proj_kernel,
        grid=(n // tm,),
        in_specs=in_specs,
        out_specs=out_specs,
        out_shape=out_shape,
        input_output_aliases=aliases,
        compiler_params=_cparams(("parallel",)),
    )(*args)


def _cumsum_kernel(lf_ref, cum_ref, *, blk):
    tk = lf_ref.shape[1]
    row = lax.broadcasted_iota(jnp.int32, (blk, blk), 0)
    col = lax.broadcasted_iota(jnp.int32, (blk, blk), 1)
    tri = (col <= row).astype(BF16)
    carry = jnp.zeros((1, LANES), F32)
    for s in range(0, tk, blk):
        x = lf_ref[0, s:s + blk, :]
        hi, mid, lo = _split3(x)
        c = _dot(tri, hi) + _dot(tri, mid) + _dot(tri, lo) + carry
        cum_ref[0, s:s + blk, :] = c
        carry = c[blk - 1:blk, :]


def _cumsum(lf, blk=64):
    b, tk, _ = lf.shape
    return pl.pallas_call(
        functools.partial(_cumsum_kernel, blk=blk),
        grid=(b,),
        in_specs=[pl.BlockSpec((1, tk, LANES), lambda i: (i, 0, 0))],
        out_specs=pl.BlockSpec((1, tk, LANES), lambda i: (i, 0, 0)),
        out_shape=jax.ShapeDtypeStruct((b, tk, LANES), F32),
        compiler_params=_cparams(("parallel",)),
    )(lf)


def _seg_sum(x, bd):
    xb = x.astype(BF16)
    return jnp.concatenate([_dot(xb[:, s:s + LANES], bd) for s in range(0, x.shape[1], LANES)], axis=1)


def _alternate(*stages):
    gens = [g for g in stages if g is not None]
    while gens:
        for g in list(gens):
            try:
                next(g)
            except StopIteration:
                gens.remove(g)


def _wkv_kernel(us_ref, prev0_ref, s0_ref, mu_ref, w0_ref, wd_ref, a0_ref, wa_ref,
                kkp_ref, ka_ref, rk_ref, lng_ref, lnb_ref,
                oa_ref, sout_ref,
                s_scr, prev_scr, at_s, rt_s, bh_s, kh_s, be_s, ke_s, v_s, ga_s, bn_s, y_s, *, nsub):
    c_idx = pl.program_id(1)
    nc = pl.num_programs(1)
    C = WKV_CHUNK
    tt = us_ref.shape[1]
    ts = tt // nsub
    nb = ts // C
    npair = H_A // 2
    G = 4 * HEAD_DIM

    @pl.when(c_idx == 0)
    def _():
        s_scr[...] = s0_ref[0]
        prev_scr[...] = prev0_ref[0]

    lane1 = lax.broadcasted_iota(jnp.int32, (1, LANES), 1)
    lane_w = lax.broadcasted_iota(jnp.int32, (LANES, LANES), 0) // HEAD_DIM
    lane_c = lax.broadcasted_iota(jnp.int32, (LANES, LANES), 1) // HEAD_DIM
    bd_pair = lane_w == lane_c
    bd = bd_pair.astype(BF16)
    row_g = lax.broadcasted_iota(jnp.int32, (C, G), 0)
    lane_g = lax.broadcasted_iota(jnp.int32, (C, G), 1) % C
    tri_incl = lane_g <= row_g
    tri_strict = lane_g < row_g
    eye = (lane_g == row_g).astype(F32)
    diag4 = (lax.broadcasted_iota(jnp.int32, (G, G), 0) // HEAD_DIM
             == lax.broadcasted_iota(jnp.int32, (G, G), 1) // HEAD_DIM).astype(BF16)
    rowt = lax.broadcasted_iota(jnp.int32, (ts, ts), 0)
    colt = lax.broadcasted_iota(jnp.int32, (ts, ts), 1)
    tri_b = (((rowt // C) == (colt // C)) & (colt <= rowt)).astype(BF16)
    row1 = lax.broadcasted_iota(jnp.int32, (ts, 1), 0)

    def stack4(x):
        xb = x.astype(BF16)
        return jnp.concatenate([xb, xb, xb, xb], axis=0) * diag4

    def mul4(xs, ys):
        return [_dot(x.astype(BF16), stack4(y)) for x, y in zip(xs, ys)]

    def prepare(sb):
        base = sb * ts
        u = us_ref[0, base:base + ts, :]
        before = prev_scr[...] if sb == 0 else us_ref[0, base - 1:base, :]
        zprev = jnp.where(row1 == 0, before, pltpu.roll(u, 1, axis=0))
        zs = u + (zprev - u) * mu_ref[...]
        r = zs[:, 0:W_A]
        k = zs[:, W_A:2 * W_A]
        v = zs[:, 2 * W_A:3 * W_A]
        lora_in = zs[:, 3 * W_A:3 * W_A + LANES]
        lora_t = jnp.where(lane1 < D_LORA, jnp.tanh(lora_in), lora_in).astype(BF16)
        yield
        wl = w0_ref[...] + _dot(lora_t, wd_ref[...])
        a = _sigmoid(a0_ref[...] + _dot(lora_t, wa_ref[...]))
        w_log = -_softplus(-wl) - 0.5
        lw = -jnp.exp(w_log)
        yield
        kk = k * kkp_ref[...]
        kk = kk * lax.rsqrt(_seg_sum(kk * kk, bd) + 1e-12)
        k2 = k * (1.0 + (a - 1.0) * ka_ref[...])
        bn_s[sb] = _seg_sum(r * k2 * rk_ref[...], bd) * v
        v_s[sb] = v
        yield
        h1, h2, h3 = _split3(lw)
        cs = _dot(tri_b, h1) + _dot(tri_b, h2) + _dot(tri_b, h3)
        tot = jnp.concatenate([jnp.broadcast_to(cs[c * C + C - 1:c * C + C, :], (C, W_A))
                               for c in range(nb)], axis=0)
        yield
        g_inv = jnp.exp(-cs)
        g_end = jnp.exp(tot - cs)
        kka = kk * a
        at_s[sb] = kk * jnp.exp(cs - lw)
        rt_s[sb] = r * jnp.exp(cs)
        yield
        bh_s[sb] = (-kka * g_inv).astype(BF16)
        kh_s[sb] = (k2 * g_inv).astype(BF16)
        be_s[sb] = (-kka * g_end).astype(BF16)
        ke_s[sb] = (k2 * g_end).astype(BF16)
        ga_s[sb] = jnp.exp(tot)
        yield

    local = {}

    def chunk_terms(sb):
        groups = []
        for c in range(nb):
            groups.append([(c, 0), (c, LANES)])
        for c in range(0, nb, 2):
            groups.append([(c, 2 * LANES)] + ([(c + 1, 2 * LANES)] if c + 1 < nb else [None]))

        def gather(ref, grp):
            parts = [jnp.zeros((C, LANES), ref.dtype) if u is None
                     else ref[sb, u[0] * C:(u[0] + 1) * C, u[1]:u[1] + LANES] for u in grp]
            return jnp.concatenate(parts, axis=1)

        at_g = [gather(at_s, g) for g in groups]
        rt_g = [gather(rt_s, g) for g in groups]
        v_g = [gather(v_s, g) for g in groups]
        lhs = [jnp.concatenate([a, r], axis=0).astype(BF16) for a, r in zip(at_g, rt_g)]
        rhs = [jnp.concatenate([stack4(gather(bh_s, g)), stack4(gather(kh_s, g))], axis=0) for g in groups]
        a4 = [_dot_nt(l, r) for l, r in zip(lhs, rhs)]
        yield
        n1 = [jnp.where(tri_strict, a[:C, :G], 0.0) for a in a4]
        a_ak = [jnp.where(tri_strict, a[:C, G:], 0.0) for a in a4]
        a_rb = [jnp.where(tri_incl, a[C:, :G], 0.0) for a in a4]
        a_rk = [jnp.where(tri_incl, a[C:, G:], 0.0) for a in a4]
        vst = [stack4(v) for v in v_g]
        avy = [_dot(jnp.concatenate([ak, rk], axis=0).astype(BF16), vs)
               for ak, rk, vs in zip(a_ak, a_rk, vst)]
        yield
        pair4 = lambda xs, ys: [eye + x + y + m for x, y, m in zip(xs, ys, mul4(xs, ys))]
        n2 = mul4(n1, n1)
        yield
        n4 = mul4(n2, n2)
        p1 = pair4(n1, n2)
        yield
        n8 = mul4(n4, n4)
        yield
        n16 = mul4(n8, n8)
        p2 = pair4(n4, n8)
        yield
        n32 = mul4(n16, n16)
        p12 = mul4(p1, p2)
        yield
        p3 = pair4(n16, n32)
        yield
        t_inv = mul4(p12, p3)
        yield
        xs = [_dot(t.astype(BF16), jnp.concatenate([stack4(a), stack4(y[:C])], axis=1))
              for t, a, y in zip(t_inv, at_g, avy)]
        yield
        rys = [_dot(rb.astype(BF16), jnp.concatenate([stack4(x[:, :G]), stack4(x[:, G:])], axis=1))
               for rb, x in zip(a_rb, xs)]
        rp_g = [r + ry[:, :G] for r, ry in zip(rt_g, rys)]
        y0_g = [ry[:, G:] + y[C:] for ry, y in zip(rys, avy)]
        yield
        for gi, grp in enumerate(groups):
            for kpos, u in enumerate(grp):
                if u is None:
                    continue
                c, p = u[0], u[1] // LANES
                rows = slice(c * C, (c + 1) * C)
                sl = slice(p * LANES, (p + 1) * LANES)
                gl = slice(kpos * LANES, (kpos + 1) * LANES)
                be_u, ke_u = be_s[sb, rows, sl], ke_s[sb, rows, sl]
                atp = xs[gi][:, gl].astype(BF16)
                wv = jnp.concatenate([xs[gi][:, G + kpos * LANES:G + (kpos + 1) * LANES],
                                      v_s[sb, rows, sl]], axis=0).astype(BF16)
                pp = jnp.where(bd_pair, _dot_tn(atp, be_u), 0.0).astype(BF16)
                q = jnp.where(bd_pair, _dot_tn(wv, jnp.concatenate([be_u, ke_u], axis=0)), 0.0)
                local[sb, c, p] = (rp_g[gi][:, gl].astype(BF16), y0_g[gi][:, gl], pp, q)
        yield

    def sweep(sb):
        for c in range(nb):
            rows = slice(c * C, (c + 1) * C)
            s_ps = [s_scr[p] for p in range(npair)]
            s_bs = [s_p.astype(BF16) for s_p in s_ps]
            for p in range(npair):
                sl = slice(p * LANES, (p + 1) * LANES)
                rp, y0, pp, q = local[sb, c, p]
                y_s[sb, rows, sl] = _dot_nt(rp, s_bs[p]) + y0
                s_scr[p] = s_ps[p] * ga_s[sb, c * C:c * C + 1, sl] + _dot(s_bs[p], pp) + q
            yield
        y = y_s[sb]
        inv_n = 1.0 / HEAD_DIM
        mean = _seg_sum(y, bd) * inv_n
        d = y - mean
        yield
        var = _seg_sum(d * d, bd) * inv_n
        yn = d * lax.rsqrt(var + GN_EPS) * lng_ref[...] + lnb_ref[...]
        oa_ref[0, sb * ts:(sb + 1) * ts, :] = (yn + bn_s[sb]).astype(oa_ref.dtype)
        yield

    _alternate(prepare(0))
    for sb in range(nsub):
        _alternate(chunk_terms(sb),
                   prepare(sb + 1) if sb + 1 < nsub else None,
                   sweep(sb - 1) if sb > 0 else None)
    _alternate(sweep(nsub - 1))
    prev_scr[...] = us_ref[0, tt - 1:tt, :]

    @pl.when(c_idx == nc - 1)
    def _():
        sout_ref[0] = s_scr[...]


def _wkv(us, prev0, s0_bd, prm, tt):
    b, t, _ = us.shape
    npair = H_A // 2
    nsub = 2 if tt >= 2 * WKV_SUB else 1
    ts = tt // nsub
    vec = lambda w: pl.BlockSpec((1, w), lambda i, j: (0, 0))
    mat = lambda: pl.BlockSpec((LANES, W_A), lambda i, j: (0, 0))
    blk = lambda dt: pltpu.VMEM((nsub, ts, W_A), dt)
    return pl.pallas_call(
        functools.partial(_wkv_kernel, nsub=nsub),
        grid=(b, t // tt),
        in_specs=[pl.BlockSpec((1, tt, SHIFT_W), lambda i, j: (i, j, 0)),
                  pl.BlockSpec((1, 1, SHIFT_W), lambda i, j: (i, 0, 0)),
                  pl.BlockSpec((1, npair, LANES, LANES), lambda i, j: (i, 0, 0, 0)),
                  vec(SHIFT_W), vec(W_A), mat(), vec(W_A), mat(),
                  vec(W_A), vec(W_A), vec(W_A), vec(W_A), vec(W_A)],
        out_specs=[pl.BlockSpec((1, tt, W_A), lambda i, j: (i, j, 0)),
                   pl.BlockSpec((1, npair, LANES, LANES), lambda i, j: (i, 0, 0, 0))],
        out_shape=[jax.ShapeDtypeStruct((b, t, W_A), BF16),
                   jax.ShapeDtypeStruct((b, npair, LANES, LANES), F32)],
        scratch_shapes=[pltpu.VMEM((npair, LANES, LANES), F32),
                        pltpu.VMEM((1, SHIFT_W), F32),
                        blk(F32), blk(F32), blk(BF16), blk(BF16), blk(BF16), blk(BF16),
                        blk(F32), blk(F32), blk(F32), blk(F32)],
        compiler_params=_cparams(("parallel", "arbitrary")),
    )(us, prev0, s0_bd, prm["mu"], prm["w0"], prm["wd"], prm["a0"], prm["wa"],
      prm["kkp"], prm["ka"], prm["rk"], prm["lnx_g"], prm["lnx_b"])


_AUG = 8
_PREP_ROWS = 256


def _bias_lanes(cum, key_side):
    ntile = H_B // 2 if key_side else H_B
    width = ntile * LANES
    r = lax.broadcasted_iota(jnp.int32, (LANES, width), 0)
    c = lax.broadcasted_iota(jnp.int32, (LANES, width), 1)
    lane = lax.broadcasted_iota(jnp.int32, (1, width), 1)
    f_off, one_off = (0, 3) if key_side else (3, 0)
    sign = -1.0 if key_side else 1.0
    head_of_tile = (c // LANES) * 2 + (c % LANES) // _AUG if key_side else c // LANES
    slot = c % _AUG
    in_head_lanes = (c % LANES) // _AUG == r % 2
    one_head = (lane % LANES) // _AUG < 2 if key_side else (lane % LANES) // _AUG == (lane // LANES) % 2
    out = jnp.where(one_head & (lane % _AUG >= one_off) & (lane % _AUG < one_off + 3), 1.0, 0.0)
    for t, part in enumerate(_split3(cum)):
        sel = jnp.where((head_of_tile == r) & in_head_lanes & (slot == f_off + t), sign, 0.0).astype(BF16)
        out = out + _dot(part, sel)
    return [out[:, i * LANES:(i + 1) * LANES] for i in range(ntile)]


def _stage_rows(total):
    return [(s, min(_PREP_ROWS, total - s)) for s in range(0, total, _PREP_ROWS)]


def _transpose_rows(x):
    n = x.shape[0]
    n_pad = -n % LANES
    if n_pad:
        x = jnp.concatenate([x, jnp.zeros((n_pad, x.shape[1]), x.dtype)], axis=0)
    return x.T[:, :n]


def _fox_kernel(*refs, tq, tk, past, single):
    if past:
        q_ref, kc_ref, vc_ref, k_ref, v_ref, cq_ref, ck_ref, o_ref, ka_scr, vt_scr, acc_scr = refs
    else:
        q_ref, k_ref, v_ref, cq_ref, ck_ref, o_ref, ka_scr, vt_scr, acc_scr = refs
    npair = H_B // 2
    lane1 = lax.broadcasted_iota(jnp.int32, (1, LANES), 1)
    head_of_lane = lane1 // HEAD_DIM

    def stage_bias(s, n):
        cum = ck_ref[0, s:s + n, :] * LOG2E
        for p, aug in enumerate(_bias_lanes(cum, True)):
            ka_scr[p, s:s + n, LANES:2 * LANES] = aug.astype(BF16)

    def stage_kv():
        for s, n in _stage_rows(past):
            for p in range(npair):
                ka_scr[p, s:s + n, 0:LANES] = kc_ref[0, p * LANES:(p + 1) * LANES, s:s + n].T.astype(BF16)
            vt_scr[:, s:s + n] = vc_ref[0, :, s:s + n].astype(BF16)
            stage_bias(s, n)
        for s, n in _stage_rows(k_ref.shape[1]):
            kb = k_ref[0, s:s + n, :].astype(BF16)
            for p in range(npair):
                ka_scr[p, past + s:past + s + n, 0:LANES] = kb[:, p * LANES:(p + 1) * LANES]
            vt_scr[:, past + s:past + s + n] = _transpose_rows(v_ref[0, s:s + n, :]).astype(BF16)
            stage_bias(past + s, n)

    if single:
        i = 0
        stage_kv()
    else:
        i = pl.program_id(1)
        pl.when(i == 0)(stage_kv)

    q = q_ref[0].astype(F32) * (HEAD_DIM ** -0.5 * LOG2E)
    cq = cq_ref[0] * LOG2E
    rhs = []
    q_bias = _bias_lanes(cq, False)
    for p in range(npair):
        q_p = q[:, p * LANES:(p + 1) * LANES]
        halves = [jnp.concatenate([jnp.where(head_of_lane == hh, q_p, 0.0), q_bias[2 * p + hh]], axis=1)
                  for hh in range(2)]
        rhs.append(jnp.concatenate(halves, axis=0).astype(BF16))

    n_full = past // tk + i * (tq // tk)
    diag0 = past if single else pl.multiple_of(past + i * tq, tq)
    krow = lax.broadcasted_iota(jnp.int32, (tq, 2 * tq), 0)
    qcol = lax.broadcasted_iota(jnp.int32, (tq, 2 * tq), 1)
    causal_t = krow <= jnp.where(qcol >= tq, qcol - tq, qcol)

    acc_scr[...] = jnp.zeros(acc_scr.shape, F32)

    def update(carry, start, size, mask):
        ms, ls = carry
        st = [_dot_nt(ka_scr[p, pl.ds(start, size), :], rhs[p]) for p in range(npair)]
        if mask is not None:
            st = [jnp.where(mask, s, NEG_INF) for s in st]
        m_new = [jnp.maximum(m, jnp.max(s, axis=0, keepdims=True)) for m, s in zip(ms, st)]
        alpha = [jnp.exp2(m - mn) for m, mn in zip(ms, m_new)]
        pt = [jnp.exp2(s - mn) for s, mn in zip(st, m_new)]
        l_new = [a * l + jnp.sum(x, axis=0, keepdims=True) for a, l, x in zip(alpha, ls, pt)]
        pv = [_dot(vt_scr[p * LANES:(p + 1) * LANES, pl.ds(start, size)], pt[p].astype(BF16))
              for p in range(npair)]
        for p in range(npair):
            acc_scr[p] = acc_scr[p] * alpha[p] + pv[p]
        return tuple(m_new), tuple(l_new)

    def body(j, carry):
        return update(carry, pl.multiple_of(j * tk, tk), tk, None)

    init = (tuple(jnp.full((1, 2 * tq), NEG_INF, F32) for _ in range(npair)),
            tuple(jnp.zeros((1, 2 * tq), F32) for _ in range(npair)))
    carry = lax.fori_loop(0, n_full, body, init)
    _, ls = update(carry, diag0, tq, causal_t)
    for p in range(npair):
        o_t = (acc_scr[p] / ls[p]).T
        o_ref[0, :, p * LANES:(p + 1) * LANES] = jnp.where(
            head_of_lane == 0, o_t[:tq], o_t[tq:]).astype(o_ref.dtype)


def _kv_specs(k_new, cache, layer, width):
    specs, args = [], []
    t_new = k_new[0].shape[2]
    if cache is not None:
        past = cache[0].shape[3]
        for c in cache:
            specs.append(pl.BlockSpec((None, 1, width, past), lambda i, j: (layer, i, 0, 0)))
            args.append(c)
    for a in k_new:
        specs.append(pl.BlockSpec((None, 1, t_new, width), lambda i, j: (layer, i, 0, 0)))
        args.append(a)
    return specs, args


def _fox(q, kv_new, cache, cum_q, cum_k, *, layer, tq, tk, past):
    b, t, _ = q.shape
    tkk = past + kv_new[0].shape[2]
    npair = H_B // 2
    kv_specs, kv_args = _kv_specs(kv_new, cache, layer, W_B)
    return pl.pallas_call(
        functools.partial(_fox_kernel, tq=tq, tk=tk, past=past, single=(t == tq)),
        grid=(b, t // tq),
        in_specs=[pl.BlockSpec((1, tq, W_B), lambda i, j: (i, j, 0))] + kv_specs + [
                  pl.BlockSpec((1, tq, LANES), lambda i, j: (i, j, 0)),
                  pl.BlockSpec((1, tkk, LANES), lambda i, j: (i, 0, 0))],
        out_specs=pl.BlockSpec((1, tq, W_B), lambda i, j: (i, j, 0)),
        out_shape=jax.ShapeDtypeStruct((b, t, W_B), BF16),
        scratch_shapes=[pltpu.VMEM((npair, tkk, 2 * LANES), BF16), pltpu.VMEM((W_B, tkk), BF16),
                        pltpu.VMEM((npair, LANES, 2 * tq), F32)],
        compiler_params=_cparams(("parallel", "arbitrary")),
    )(q, *kv_args, cum_q, cum_k)


def _sb_kernel(*refs, tq, tk, past, single):
    if past:
        q_ref, kc_ref, vc_ref, k_ref, v_ref, o_ref, kb_scr, vt_scr, acc_scr = refs
    else:
        q_ref, k_ref, v_ref, o_ref, kb_scr, vt_scr, acc_scr = refs
    npair = H_C // 2
    lane1 = lax.broadcasted_iota(jnp.int32, (1, LANES), 1)
    head_of_lane = lane1 // HEAD_DIM

    def stage_kv():
        for s, n in _stage_rows(past):
            kb_scr[s:s + n, :] = kc_ref[0, :, s:s + n].T.astype(BF16)
            vt_scr[:, s:s + n] = vc_ref[0, :, s:s + n].astype(BF16)
        for s, n in _stage_rows(k_ref.shape[1]):
            kb_scr[past + s:past + s + n, :] = k_ref[0, s:s + n, :].astype(BF16)
            vt_scr[:, past + s:past + s + n] = _transpose_rows(v_ref[0, s:s + n, :]).astype(BF16)

    if single:
        i = 0
        stage_kv()
    else:
        i = pl.program_id(1)
        pl.when(i == 0)(stage_kv)

    q = q_ref[0].astype(F32) * (HEAD_DIM ** -0.5 * LOG2E)
    rhs = []
    for p in range(npair):
        q_p = q[:, p * LANES:(p + 1) * LANES]
        rhs.append(jnp.concatenate([jnp.where(head_of_lane == hh, q_p, 0.0) for hh in range(2)],
                                   axis=0).astype(BF16))

    def earlier_matrix(n):
        rr = lax.broadcasted_iota(jnp.int32, (n, n), 0)
        cc = lax.broadcasted_iota(jnp.int32, (n, n), 1)
        return (cc > rr).astype(BF16)

    n_full = past // tk + i * (tq // tk)
    diag0 = past if single else pl.multiple_of(past + i * tq, tq)
    krow = lax.broadcasted_iota(jnp.int32, (tq, 2 * tq), 0)
    qcol = lax.broadcasted_iota(jnp.int32, (tq, 2 * tq), 1)
    strict_t = krow < jnp.where(qcol >= tq, qcol - tq, qcol)

    def update(runs, start, size, mask, first):
        zt = [_dot_nt(kb_scr[pl.ds(start, size), p * LANES:(p + 1) * LANES], rhs[p])
              for p in range(npair)]
        mz = [jnp.minimum(z, 0.0) for z in zt]
        tail = [jnp.log(1.0 + jnp.exp2(m + m - z)) * LOG2E for m, z in zip(mz, zt)]
        lsig = [m - t for m, t in zip(mz, tail)]
        l1m = [s - z for s, z in zip(lsig, zt)]
        if mask is not None:
            l1m = [jnp.where(mask, x, 0.0) for x in l1m]
        later = earlier_matrix(size)
        after = [_dot(later, x.astype(BF16)) for x in l1m]
        wt = [jnp.exp2(s + a + r) for s, a, r in zip(lsig, after, runs)]
        if mask is not None:
            wt = [jnp.where(mask, w, 0.0) for w in wt]
        pv = [_dot(vt_scr[p * LANES:(p + 1) * LANES, pl.ds(start, size)], wt[p].astype(BF16))
              for p in range(npair)]
        for p in range(npair):
            acc_scr[p] = pv[p] if first else acc_scr[p] + pv[p]
        return tuple(r + jnp.sum(x, axis=0, keepdims=True) for r, x in zip(runs, l1m))

    runs = tuple(jnp.zeros((1, 2 * tq), F32) for _ in range(npair))
    runs = update(runs, diag0, tq, strict_t, True)

    def body(jj, runs):
        j = n_full - 1 - jj
        return update(runs, pl.multiple_of(j * tk, tk), tk, None, False)

    lax.fori_loop(0, n_full, body, runs)
    for p in range(npair):
        o_t = acc_scr[p].T
        o_ref[0, :, p * LANES:(p + 1) * LANES] = jnp.where(
            head_of_lane == 0, o_t[:tq], o_t[tq:]).astype(o_ref.dtype)


def _sb(q, kv_new, cache, *, layer, tq, tk, past):
    b, t, _ = q.shape
    tkk = past + kv_new[0].shape[2]
    npair = H_C // 2
    kv_specs, kv_args = _kv_specs(kv_new, cache, layer, W_C)
    return pl.pallas_call(
        functools.partial(_sb_kernel, tq=tq, tk=tk, past=past, single=(t == tq)),
        grid=(b, t // tq),
        in_specs=[pl.BlockSpec((1, tq, W_C), lambda i, j: (i, j, 0))] + kv_specs,
        out_specs=pl.BlockSpec((1, tq, W_C), lambda i, j: (i, j, 0)),
        out_shape=jax.ShapeDtypeStruct((b, t, W_C), BF16),
        scratch_shapes=[pltpu.VMEM((tkk, W_C), BF16), pltpu.VMEM((W_C, tkk), BF16),
                        pltpu.VMEM((npair, LANES, 2 * tq), F32)],
        compiler_params=_cparams(("parallel", "arbitrary")),
    )(q, *kv_args)


def _out_kernel(oa_ref, ob_ref, oc_ref, g_ref, x_ref, w_ref, lng_ref, lnb_ref, y_ref, *, alpha):
    g = g_ref[...].astype(F32)
    gate = g * _sigmoid(g)
    gated = lambda o_ref, lo, hi: (o_ref[...].astype(F32) * gate[:, lo:hi]).astype(BF16)
    acc = _dot(gated(oa_ref, 0, W_A), w_ref[0:W_A, :])
    acc = acc + _dot(gated(ob_ref, W_A, W_A + W_B), w_ref[W_A:W_A + W_B, :])
    acc = acc + _dot(gated(oc_ref, W_A + W_B, W_MIX), w_ref[W_A + W_B:, :])
    z = alpha * x_ref[...] + acc
    mu = jnp.mean(z, axis=-1, keepdims=True)
    d = z - mu
    var = jnp.mean(d * d, axis=-1, keepdims=True)
    y_ref[...] = d * lax.rsqrt(var + LN_EPS) * lng_ref[...] + lnb_ref[...]


def _out(oa, ob, oc, g, x2d, w_out, ln_g, ln_b, alpha, tm):
    n, d = x2d.shape
    row = lambda w: pl.BlockSpec((tm, w), lambda i: (i, 0))
    return pl.pallas_call(
        functools.partial(_out_kernel, alpha=alpha),
        grid=(n // tm,),
        in_specs=[row(W_A), row(W_B), row(W_C), row(W_MIX), row(d),
                  pl.BlockSpec((W_MIX, d), lambda i: (0, 0)),
                  pl.BlockSpec((1, d), lambda i: (0, 0)),
                  pl.BlockSpec((1, d), lambda i: (0, 0))],
        out_specs=row(d),
        out_shape=jax.ShapeDtypeStruct((n, d), F32),
        compiler_params=_cparams(("parallel",)),
    )(oa, ob, oc, g, x2d, w_out, ln_g, ln_b)


def _pad_lanes(a, width=LANES):
    return jnp.pad(a, [(0, 0)] * (a.ndim - 1) + [(0, width - a.shape[-1])])


def _state_to_pairs(s):
    b = s.shape[0]
    odd_head = (jnp.arange(H_A) % 2 == 1).reshape(1, H_A, 1, 1)
    upper_half = (jnp.arange(LANES) >= HEAD_DIM).reshape(1, 1, 1, LANES)
    wide = jnp.where(odd_head == upper_half, jnp.concatenate([s, s], axis=-1), 0.0)
    return wide.reshape(b, H_A // 2, LANES, LANES)


def _pairs_to_state(sp):
    b = sp.shape[0]
    wide = sp.reshape(b, H_A, HEAD_DIM, LANES)
    odd_head = (jnp.arange(H_A) % 2 == 1).reshape(1, H_A, 1, 1)
    return jnp.where(odd_head, wide[..., HEAD_DIM:], wide[..., :HEAD_DIM])


_IN_SIZES = (SHIFT_W, W_A, W_B, W_B, W_B, H_B, W_B, W_C, W_C, W_C, W_C)
_IN_OFFS = tuple(int(v) for v in np.concatenate([[0], np.cumsum(_IN_SIZES)]))
_SRC_SHIFT, _SRC_GA, _SRC_QB, _SRC_KB, _SRC_VB, _SRC_F, _SRC_GB, _SRC_QC, _SRC_KC, _SRC_VC, _SRC_GC = (
    _IN_OFFS[:-1])
_W_MOVES = ((_SRC_SHIFT, _OFF_SHIFT, SHIFT_W), (_SRC_GA, _OFF_G, W_A), (_SRC_GB, _OFF_G + W_A, W_B),
            (_SRC_GC, _OFF_G + W_A + W_B, W_C), (_SRC_QB, _OFF_QB, W_B), (_SRC_KB, _OFF_KB, W_B),
            (_SRC_VB, _OFF_VB, W_B), (_SRC_QC, _OFF_QC, W_C), (_SRC_KC, _OFF_KC, W_C),
            (_SRC_VC, _OFF_VC, W_C))
_W_STAGE_ROWS = 128


def _stage_w_kernel(w_ref, o_ref):
    for src, dst, width in _W_MOVES:
        o_ref[:, dst:dst + width] = w_ref[:, src:src + width].astype(BF16)
    lane = lax.broadcasted_iota(jnp.int32, (1, LANES), 1)
    o_ref[:, _OFF_F:_OFF_F + LANES] = jnp.where(
        lane < H_B, w_ref[:, _SRC_F:_SRC_F + LANES], 0.0).astype(BF16)


def _stage_w(w_in, layer):
    _, d, cols = w_in.shape
    return pl.pallas_call(
        _stage_w_kernel,
        grid=(d // _W_STAGE_ROWS,),
        in_specs=[pl.BlockSpec((None, _W_STAGE_ROWS, cols), lambda i: (layer, i, 0))],
        out_specs=pl.BlockSpec((_W_STAGE_ROWS, _PROJ_COLS), lambda i: (i, 0)),
        out_shape=jax.ShapeDtypeStruct((d, _PROJ_COLS), BF16),
        compiler_params=_cparams(("parallel",)),
    )(w_in)


def _layer_params(l, w_in, mu_shift, w0_decay, w_decay, a0, w_aaa, k_k, k_a, r_k,
                  lnx_g, lnx_b, fox_fb, w_out, ln_g, ln_b):
    w_cat = _stage_w(w_in, l)
    zeros = jnp.zeros((D_LORA, W_A), F32)
    row = lambda a: a.reshape(1, -1).astype(F32)
    return dict(
        w_cat=w_cat, fb=_pad_lanes(row(fox_fb[l])),
        mu=row(mu_shift[l]), w0=row(w0_decay[l]),
        wd=jnp.concatenate([w_decay[l], zeros], axis=0).astype(BF16),
        a0=row(a0[l]), wa=jnp.concatenate([zeros, w_aaa[l]], axis=0).astype(BF16),
        kkp=row(k_k[l]), ka=row(k_a[l]), rk=row(r_k[l]), lnx_g=row(lnx_g[l]), lnx_b=row(lnx_b[l]),
        w_out=w_out[l].astype(BF16), ln_g=row(ln_g[l]), ln_b=row(ln_b[l]))


def _run_layer(x, hist, prm, alpha, layer, depth, stacks):
    b, t, d = x.shape
    n = b * t
    x2d = x.reshape(n, d)
    tm = min(ROW_BLOCK, n)
    us, g, qb, kb_st, vb_st, qc, kc_st, vc_st, lf = _proj(
        x2d, prm["w_cat"], prm["fb"], tm, layer, depth, stacks)
    r3 = lambda a: a.reshape(b, t, a.shape[-1])
    us, qb, qc, lf = (r3(a) for a in (us, qb, qc, lf))
    r4 = lambda a: a.reshape(depth, b, t, a.shape[-1])

    if hist is None:
        past = 0
        prev0 = jnp.zeros((b, 1, SHIFT_W), F32)
        s0 = jnp.zeros((b, H_A // 2, LANES, LANES), F32)
        cache_b = cache_c = None
        lf_all = lf
        fox_blk = (min(FOX_BLOCK, t),) * 2
        sb_blk = (min(SB_BLOCK, t),) * 2
    else:
        fk_t, fv_t, h_lf, sk_t, sv_t, h_wkv, h_shift = hist
        past = fk_t.shape[3]
        prev0 = h_shift
        s0 = _state_to_pairs(h_wkv)
        cache_b, cache_c = (fk_t, fv_t), (sk_t, sv_t)
        lf_all = jnp.concatenate([_pad_lanes(h_lf), lf], axis=1)
        fox_blk = sb_blk = (t, min(SB_BLOCK, past))

    oa, s_fin = _wkv(us, prev0, s0, prm, min(WKV_BLOCK, t))
    cum = _cumsum(lf_all)
    ob = _fox(qb, (r4(kb_st), r4(vb_st)), cache_b, cum[:, past:], cum, layer=layer,
              tq=fox_blk[0], tk=fox_blk[1], past=past)
    oc = _sb(qc, (r4(kc_st), r4(vc_st)), cache_c, layer=layer, tq=sb_blk[0], tk=sb_blk[1], past=past)
    y = _out(oa.reshape(n, W_A), ob.reshape(n, W_B), oc.reshape(n, W_C), g, x2d,
             prm["w_out"], prm["ln_g"], prm["ln_b"], alpha, tm)
    small = (lf[:, :, :H_B], _pairs_to_state(s_fin), us[:, -1:, :])
    return y.reshape(b, t, d), (kb_st, vb_st, kc_st, vc_st), small


def kernel(x_prompt, x_sample, cache_fox_k, cache_fox_v, cache_fox_logf, cache_sb_k, cache_sb_v, state_wkv, state_shift, w_in, mu_shift, w0_decay, w_decay, a0, w_aaa, k_k, k_a, r_k, lnx_g, lnx_b, fox_fb, w_out, ln_g, ln_b):
    depth = w_in.shape[0]
    alpha = (2 * depth) ** 0.25
    yp, ys = x_prompt, x_sample

    def time_on_lanes(cache):
        nl, nb, past, nh, hd = cache.shape
        return jnp.transpose(cache, (0, 1, 3, 4, 2)).reshape(nl, nb, nh * hd, past)

    fk_t, fv_t, sk_t, sv_t = (time_on_lanes(c) for c in (cache_fox_k, cache_fox_v, cache_sb_k, cache_sb_v))
    stacks_p, stacks_s = None, None
    small_p, small_s = [], []
    for l in range(depth):
        prm = _layer_params(l, w_in, mu_shift, w0_decay, w_decay, a0, w_aaa, k_k, k_a, r_k,
                            lnx_g, lnx_b, fox_fb, w_out, ln_g, ln_b)
        yp, stacks_p, sm_p = _run_layer(yp, None, prm, alpha, l, depth, stacks_p)
        hist = (fk_t, fv_t, cache_fox_logf[l], sk_t, sv_t, state_wkv[l], state_shift[l])
        ys, stacks_s, sm_s = _run_layer(ys, hist, prm, alpha, l, depth, stacks_s)
        small_p.append(sm_p)
        small_s.append(sm_s)

    def group_outputs(x, stacks, small):
        b, t, _ = x.shape
        kb, vb, kc, vc = stacks
        logf, wkv, shift = (jnp.stack([sm[i] for sm in small]) for i in range(3))
        return (kb.reshape(depth, b, t, H_B, HEAD_DIM), vb.reshape(depth, b, t, H_B, HEAD_DIM), logf,
                kc.reshape(depth, b, t, H_C, HEAD_DIM), vc.reshape(depth, b, t, H_C, HEAD_DIM),
                wkv, shift)

    return ((yp, ys) + group_outputs(x_prompt, stacks_p, small_p)
            + group_outputs(x_sample, stacks_s, small_s))
```

```python
import functools

import numpy as np
import jax
import jax.numpy as jnp
from jax import lax
from jax.experimental import pallas as pl
from jax.experimental.pallas import tpu as pltpu

F32 = jnp.float32
BF16 = jnp.bfloat16

HEAD_DIM = 64
H_A, H_B, H_C = 6, 6, 4
W_A, W_B, W_C = H_A * HEAD_DIM, H_B * HEAD_DIM, H_C * HEAD_DIM
W_MIX = W_A + W_B + W_C
D_LORA = 64
SHIFT_W = 3 * W_A + 2 * D_LORA
LANES = 128
WKV_CHUNK = 64
WKV_BLOCK = 1024
WKV_SUB = 256
ROW_BLOCK = 512
FOX_BLOCK = 512
SB_BLOCK = 512
GN_EPS = 64e-5
LN_EPS = 1e-5
NEG_INF = -1e30
LOG2E = 1.4426950408889634
VMEM_LIMIT = 56 * 1024 * 1024


def _cparams(sem):
    return pltpu.CompilerParams(dimension_semantics=sem, vmem_limit_bytes=VMEM_LIMIT)


def _dot(a, b):
    return jnp.dot(a, b, preferred_element_type=F32)


def _dot_nt(a, b):
    return lax.dot_general(a, b, (((1,), (1,)), ((), ())), preferred_element_type=F32)


def _dot_tn(a, b):
    return lax.dot_general(a, b, (((0,), (0,)), ((), ())), preferred_element_type=F32)


def _split3(x):
    hi = x.astype(BF16)
    r1 = x - hi.astype(F32)
    mid = r1.astype(BF16)
    lo = (r1 - mid.astype(F32)).astype(BF16)
    return hi, mid, lo


def _softplus(x):
    return jnp.maximum(x, 0.0) + jnp.log1p(jnp.exp(-jnp.abs(x)))


def _sigmoid(x):
    return 1.0 / (1.0 + jnp.exp(-x))


_OFF_SHIFT = 0
_OFF_G = _OFF_SHIFT + SHIFT_W
_OFF_QB = _OFF_G + W_MIX
_OFF_KB = _OFF_QB + W_B
_OFF_VB = _OFF_KB + W_B
_OFF_QC = _OFF_VB + W_B
_OFF_KC = _OFF_QC + W_C
_OFF_VC = _OFF_KC + W_C
_OFF_F = _OFF_VC + W_C
_PROJ_COLS = _OFF_F + LANES


def _proj_kernel(x_ref, w_ref, fb_ref, *refs):
    us_ref, g_ref, qb_ref, kb_ref, vb_ref, qc_ref, kc_ref, vc_ref, lf_ref = refs[-9:]
    xb = x_ref[...].astype(BF16)
    for ref, off in ((us_ref, _OFF_SHIFT), (g_ref, _OFF_G), (qb_ref, _OFF_QB),
                     (kb_ref, _OFF_KB), (vb_ref, _OFF_VB), (qc_ref, _OFF_QC),
                     (kc_ref, _OFF_KC), (vc_ref, _OFF_VC)):
        time_on_lanes = ref.shape[-1] == xb.shape[0] and ref.shape[0] != xb.shape[0]
        width = ref.shape[0] if time_on_lanes else ref.shape[-1]
        res = _dot(xb, w_ref[:, off:off + width])
        ref[...] = (res.T if time_on_lanes else res).astype(ref.dtype)
    f = _dot(xb, w_ref[:, _OFF_F:_OFF_F + LANES]) + fb_ref[...]
    lf_ref[...] = -_softplus(-f)


_PROJ_WIDTHS = (SHIFT_W, W_MIX, W_B, W_B, W_B, W_C, W_C, W_C, LANES)
_PROJ_BF16 = (1, 2, 5)
_PROJ_STACKED = (3, 4, 6, 7)
_PROJ_FOX_KV = (3, 4)


def _proj(x2d, w_cat, fb_pad, tm, layer, depth, stacks, fox_time_on_lanes):
    n, d = x2d.shape
    assert tm != W_B
    out_specs, out_shape = [], []
    for idx, w in enumerate(_PROJ_WIDTHS):
        if idx in _PROJ_FOX_KV and fox_time_on_lanes:
            out_specs.append(pl.BlockSpec((None, w, tm), lambda i: (layer, 0, i)))
            out_shape.append(jax.ShapeDtypeStruct((depth, w, n), F32))
        elif idx in _PROJ_STACKED:
            out_specs.append(pl.BlockSpec((None, tm, w), lambda i: (layer, i, 0)))
            out_shape.append(jax.ShapeDtypeStruct((depth, n, w), F32))
        else:
            out_specs.append(pl.BlockSpec((tm, w), lambda i: (i, 0)))
            out_shape.append(jax.ShapeDtypeStruct((n, w), BF16 if idx in _PROJ_BF16 else F32))
    in_specs = [pl.BlockSpec((tm, d), lambda i: (i, 0)),
                pl.BlockSpec((d, _PROJ_COLS), lambda i: (0, 0)),
                pl.BlockSpec((1, LANES), lambda i: (0, 0))]
    args = [x2d, w_cat, fb_pad]
    aliases = {}
    if stacks is not None:
        for j, (idx, st) in enumerate(zip(_PROJ_STACKED, stacks)):
            in_specs.append(pl.BlockSpec(memory_space=pl.ANY))
            args.append(st)
            aliases[3 + j] = idx
    return pl.pallas_call(
        _proj_kernel,
        grid=(n // tm,),
        in_specs=in_specs,
        out_specs=out_specs,
        out_shape=out_shape,
        input_output_aliases=aliases,
        compiler_params=_cparams(("parallel",)),
    )(*args)


def _cumsum_kernel(lf_ref, cum_ref, *, blk):
    tk = lf_ref.shape[1]
    row = lax.broadcasted_iota(jnp.int32, (blk, blk), 0)
    col = lax.broadcasted_iota(jnp.int32, (blk, blk), 1)
    tri = (col <= row).astype(BF16)
    carry = jnp.zeros((1, LANES), F32)
    for s in range(0, tk, blk):
        x = lf_ref[0, s:s + blk, :]
        hi, mid, lo = _split3(x)
        c = _dot(tri, hi) + _dot(tri, mid) + _dot(tri, lo) + carry
        cum_ref[0, s:s + blk, :] = c
        carry = c[blk - 1:blk, :]


def _cumsum(lf, blk=64):
    b, tk, _ = lf.shape
    return pl.pallas_call(
        functools.partial(_cumsum_kernel, blk=blk),
        grid=(b,),
        in_specs=[pl.BlockSpec((1, tk, LANES), lambda i: (i, 0, 0))],
        out_specs=pl.BlockSpec((1, tk, LANES), lambda i: (i, 0, 0)),
        out_shape=jax.ShapeDtypeStruct((b, tk, LANES), F32),
        compiler_params=_cparams(("parallel",)),
    )(lf)


def _seg_sum(x, bd):
    xb = x.astype(BF16)
    return jnp.concatenate([_dot(xb[:, s:s + LANES], bd) for s in range(0, x.shape[1], LANES)], axis=1)


def _alternate(*stages):
    gens = [g for g in stages if g is not None]
    while gens:
        for g in list(gens):
            try:
                next(g)
            except StopIteration:
                gens.remove(g)


def _wkv_kernel(us_ref, prev0_ref, s0_ref, mu_ref, w0_ref, wd_ref, a0_ref, wa_ref,
                kkp_ref, ka_ref, rk_ref, lng_ref, lnb_ref,
                oa_ref, sout_ref,
                s_scr, prev_scr, at_s, rt_s, bh_s, kh_s, be_s, ke_s, v_s, ga_s, bn_s, y_s, *, nsub):
    c_idx = pl.program_id(1)
    nc = pl.num_programs(1)
    C = WKV_CHUNK
    tt = us_ref.shape[1]
    ts = tt // nsub
    nb = ts // C
    npair = H_A // 2
    G = 4 * HEAD_DIM

    @pl.when(c_idx == 0)
    def _():
        s_scr[...] = s0_ref[0]
        prev_scr[...] = prev0_ref[0]

    lane1 = lax.broadcasted_iota(jnp.int32, (1, LANES), 1)
    lane_w = lax.broadcasted_iota(jnp.int32, (LANES, LANES), 0) // HEAD_DIM
    lane_c = lax.broadcasted_iota(jnp.int32, (LANES, LANES), 1) // HEAD_DIM
    bd_pair = lane_w == lane_c
    bd = bd_pair.astype(BF16)
    row_g = lax.broadcasted_iota(jnp.int32, (C, G), 0)
    lane_g = lax.broadcasted_iota(jnp.int32, (C, G), 1) % C
    tri_incl = lane_g <= row_g
    tri_strict = lane_g < row_g
    eye = (lane_g == row_g).astype(F32)
    diag4 = (lax.broadcasted_iota(jnp.int32, (G, G), 0) // HEAD_DIM
             == lax.broadcasted_iota(jnp.int32, (G, G), 1) // HEAD_DIM).astype(BF16)
    rowt = lax.broadcasted_iota(jnp.int32, (ts, ts), 0)
    colt = lax.broadcasted_iota(jnp.int32, (ts, ts), 1)
    tri_b = (((rowt // C) == (colt // C)) & (colt <= rowt)).astype(BF16)
    row1 = lax.broadcasted_iota(jnp.int32, (ts, 1), 0)

    def stack4(x):
        xb = x.astype(BF16)
        return jnp.concatenate([xb, xb, xb, xb], axis=0) * diag4

    def mul4(xs, ys):
        return [_dot(x.astype(BF16), stack4(y)) for x, y in zip(xs, ys)]

    def prepare(sb):
        base = sb * ts
        u = us_ref[0, base:base + ts, :]
        before = prev_scr[...] if sb == 0 else us_ref[0, base - 1:base, :]
        zprev = jnp.where(row1 == 0, before, pltpu.roll(u, 1, axis=0))
        zs = u + (zprev - u) * mu_ref[...]
        r = zs[:, 0:W_A]
        k = zs[:, W_A:2 * W_A]
        v = zs[:, 2 * W_A:3 * W_A]
        lora_in = zs[:, 3 * W_A:3 * W_A + LANES]
        lora_t = jnp.where(lane1 < D_LORA, jnp.tanh(lora_in), lora_in).astype(BF16)
        yield
        wl = w0_ref[...] + _dot(lora_t, wd_ref[...])
        a = _sigmoid(a0_ref[...] + _dot(lora_t, wa_ref[...]))
        w_log = -_softplus(-wl) - 0.5
        lw = -jnp.exp(w_log)
        yield
        kk = k * kkp_ref[...]
        kk = kk * lax.rsqrt(_seg_sum(kk * kk, bd) + 1e-12)
        k2 = k * (1.0 + (a - 1.0) * ka_ref[...])
        bn_s[sb] = _seg_sum(r * k2 * rk_ref[...], bd) * v
        v_s[sb] = v
        yield
        h1, h2, h3 = _split3(lw)
        cs = _dot(tri_b, h1) + _dot(tri_b, h2) + _dot(tri_b, h3)
        tot = jnp.concatenate([jnp.broadcast_to(cs[c * C + C - 1:c * C + C, :], (C, W_A))
                               for c in range(nb)], axis=0)
        yield
        g_inv = jnp.exp(-cs)
        g_end = jnp.exp(tot - cs)
        kka = kk * a
        at_s[sb] = kk * jnp.exp(cs - lw)
        rt_s[sb] = r * jnp.exp(cs)
        yield
        bh_s[sb] = (-kka * g_inv).astype(BF16)
        kh_s[sb] = (k2 * g_inv).astype(BF16)
        be_s[sb] = (-kka * g_end).astype(BF16)
        ke_s[sb] = (k2 * g_end).astype(BF16)
        ga_s[sb] = jnp.exp(tot)
        yield

    local = {}

    def chunk_terms(sb):
        groups = []
        for c in range(nb):
            groups.append([(c, 0), (c, LANES)])
        for c in range(0, nb, 2):
            groups.append([(c, 2 * LANES)] + ([(c + 1, 2 * LANES)] if c + 1 < nb else [None]))

        def gather(ref, grp):
            parts = [jnp.zeros((C, LANES), ref.dtype) if u is None
                     else ref[sb, u[0] * C:(u[0] + 1) * C, u[1]:u[1] + LANES] for u in grp]
            return jnp.concatenate(parts, axis=1)

        at_g = [gather(at_s, g) for g in groups]
        rt_g = [gather(rt_s, g) for g in groups]
        v_g = [gather(v_s, g) for g in groups]
        lhs = [jnp.concatenate([a, r], axis=0).astype(BF16) for a, r in zip(at_g, rt_g)]
        rhs = [jnp.concatenate([stack4(gather(bh_s, g)), stack4(gather(kh_s, g))], axis=0) for g in groups]
        a4 = [_dot_nt(l, r) for l, r in zip(lhs, rhs)]
        yield
        n1 = [jnp.where(tri_strict, a[:C, :G], 0.0) for a in a4]
        a_ak = [jnp.where(tri_strict, a[:C, G:], 0.0) for a in a4]
        a_rb = [jnp.where(tri_incl, a[C:, :G], 0.0) for a in a4]
        a_rk = [jnp.where(tri_incl, a[C:, G:], 0.0) for a in a4]
        vst = [stack4(v) for v in v_g]
        avy = [_dot(jnp.concatenate([ak, rk], axis=0).astype(BF16), vs)
               for ak, rk, vs in zip(a_ak, a_rk, vst)]
        yield
        pair4 = lambda xs, ys: [eye + x + y + m for x, y, m in zip(xs, ys, mul4(xs, ys))]
        n2 = mul4(n1, n1)
        yield
        n4 = mul4(n2, n2)
        p1 = pair4(n1, n2)
        yield
        n8 = mul4(n4, n4)
        yield
        n16 = mul4(n8, n8)
        p2 = pair4(n4, n8)
        yield
        n32 = mul4(n16, n16)
        p12 = mul4(p1, p2)
        yield
        p3 = pair4(n16, n32)
        yield
        t_inv = mul4(p12, p3)
        yield
        xs = [_dot(t.astype(BF16), jnp.concatenate([stack4(a), stack4(y[:C])], axis=1))
              for t, a, y in zip(t_inv, at_g, avy)]
        yield
        rys = [_dot(rb.astype(BF16), jnp.concatenate([stack4(x[:, :G]), stack4(x[:, G:])], axis=1))
               for rb, x in zip(a_rb, xs)]
        rp_g = [r + ry[:, :G] for r, ry in zip(rt_g, rys)]
        y0_g = [ry[:, G:] + y[C:] for ry, y in zip(rys, avy)]
        yield
        for gi, grp in enumerate(groups):
            for kpos, u in enumerate(grp):
                if u is None:
                    continue
                c, p = u[0], u[1] // LANES
                rows = slice(c * C, (c + 1) * C)
                sl = slice(p * LANES, (p + 1) * LANES)
                gl = slice(kpos * LANES, (kpos + 1) * LANES)
                be_u, ke_u = be_s[sb, rows, sl], ke_s[sb, rows, sl]
                atp = xs[gi][:, gl].astype(BF16)
                wv = jnp.concatenate([xs[gi][:, G + kpos * LANES:G + (kpos + 1) * LANES],
                                      v_s[sb, rows, sl]], axis=0).astype(BF16)
                pp = jnp.where(bd_pair, _dot_tn(atp, be_u), 0.0).astype(BF16)
                q = jnp.where(bd_pair, _dot_tn(wv, jnp.concatenate([be_u, ke_u], axis=0)), 0.0)
                local[sb, c, p] = (rp_g[gi][:, gl].astype(BF16), y0_g[gi][:, gl], pp, q)
        yield

    def sweep(sb):
        for c in range(nb):
            rows = slice(c * C, (c + 1) * C)
            s_ps = [s_scr[p] for p in range(npair)]
            s_bs = [s_p.astype(BF16) for s_p in s_ps]
            for p in range(npair):
                sl = slice(p * LANES, (p + 1) * LANES)
                rp, y0, pp, q = local[sb, c, p]
                y_s[sb, rows, sl] = _dot_nt(rp, s_bs[p]) + y0
                s_scr[p] = s_ps[p] * ga_s[sb, c * C:c * C + 1, sl] + _dot(s_bs[p], pp) + q
            yield
        y = y_s[sb]
        inv_n = 1.0 / HEAD_DIM
        mean = _seg_sum(y, bd) * inv_n
        d = y - mean
        yield
        var = _seg_sum(d * d, bd) * inv_n
        yn = d * lax.rsqrt(var + GN_EPS) * lng_ref[...] + lnb_ref[...]
        oa_ref[0, sb * ts:(sb + 1) * ts, :] = (yn + bn_s[sb]).astype(oa_ref.dtype)
        yield

    _alternate(prepare(0))
    for sb in range(nsub):
        _alternate(chunk_terms(sb),
                   prepare(sb + 1) if sb + 1 < nsub else None,
                   sweep(sb - 1) if sb > 0 else None)
    _alternate(sweep(nsub - 1))
    prev_scr[...] = us_ref[0, tt - 1:tt, :]

    @pl.when(c_idx == nc - 1)
    def _():
        sout_ref[0] = s_scr[...]


def _wkv(us, prev0, s0_bd, prm, tt):
    b, t, _ = us.shape
    npair = H_A // 2
    nsub = max(1, tt // WKV_SUB)
    ts = tt // nsub
    vec = lambda w: pl.BlockSpec((1, w), lambda i, j: (0, 0))
    mat = lambda: pl.BlockSpec((LANES, W_A), lambda i, j: (0, 0))
    blk = lambda dt: pltpu.VMEM((nsub, ts, W_A), dt)
    return pl.pallas_call(
        functools.partial(_wkv_kernel, nsub=nsub),
        grid=(b, t // tt),
        in_specs=[pl.BlockSpec((1, tt, SHIFT_W), lambda i, j: (i, j, 0)),
                  pl.BlockSpec((1, 1, SHIFT_W), lambda i, j: (i, 0, 0)),
                  pl.BlockSpec((1, npair, LANES, LANES), lambda i, j: (i, 0, 0, 0)),
                  vec(SHIFT_W), vec(W_A), mat(), vec(W_A), mat(),
                  vec(W_A), vec(W_A), vec(W_A), vec(W_A), vec(W_A)],
        out_specs=[pl.BlockSpec((1, tt, W_A), lambda i, j: (i, j, 0)),
                   pl.BlockSpec((1, npair, LANES, LANES), lambda i, j: (i, 0, 0, 0))],
        out_shape=[jax.ShapeDtypeStruct((b, t, W_A), BF16),
                   jax.ShapeDtypeStruct((b, npair, LANES, LANES), F32)],
        scratch_shapes=[pltpu.VMEM((npair, LANES, LANES), F32),
                        pltpu.VMEM((1, SHIFT_W), F32),
                        blk(F32), blk(F32), blk(BF16), blk(BF16), blk(BF16), blk(BF16),
                        blk(F32), blk(F32), blk(F32), blk(F32)],
        compiler_params=_cparams(("parallel", "arbitrary")),
    )(us, prev0, s0_bd, prm["mu"], prm["w0"], prm["wd"], prm["a0"], prm["wa"],
      prm["kkp"], prm["ka"], prm["rk"], prm["lnx_g"], prm["lnx_b"])


_AUG = 8
_PREP_ROWS = 256


def _bias_lanes(cum, key_side):
    ntile = H_B // 2 if key_side else H_B
    width = ntile * LANES
    r = lax.broadcasted_iota(jnp.int32, (LANES, width), 0)
    c = lax.broadcasted_iota(jnp.int32, (LANES, width), 1)
    lane = lax.broadcasted_iota(jnp.int32, (1, width), 1)
    f_off, one_off = (0, 3) if key_side else (3, 0)
    sign = -1.0 if key_side else 1.0
    head_of_tile = (c // LANES) * 2 + (c % LANES) // _AUG if key_side else c // LANES
    slot = c % _AUG
    in_head_lanes = (c % LANES) // _AUG == r % 2
    one_head = (lane % LANES) // _AUG < 2 if key_side else (lane % LANES) // _AUG == (lane // LANES) % 2
    out = jnp.where(one_head & (lane % _AUG >= one_off) & (lane % _AUG < one_off + 3), 1.0, 0.0)
    for t, part in enumerate(_split3(cum)):
        sel = jnp.where((head_of_tile == r) & in_head_lanes & (slot == f_off + t), sign, 0.0).astype(BF16)
        out = out + _dot(part, sel)
    return [out[:, i * LANES:(i + 1) * LANES] for i in range(ntile)]


def _stage_rows(total):
    return [(s, min(_PREP_ROWS, total - s)) for s in range(0, total, _PREP_ROWS)]


def _transpose_rows(x):
    n = x.shape[0]
    n_pad = -n % LANES
    if n_pad:
        x = jnp.concatenate([x, jnp.zeros((n_pad, x.shape[1]), x.dtype)], axis=0)
    return x.T[:, :n]


def _fox_kernel(*refs, tq, tk, past, single, new_time_on_lanes):
    if past:
        q_ref, kc_ref, vc_ref, k_ref, v_ref, cq_ref, ck_ref, o_ref, ka_scr, vt_scr, acc_scr = refs
    else:
        q_ref, k_ref, v_ref, cq_ref, ck_ref, o_ref, ka_scr, vt_scr, acc_scr = refs
    npair = H_B // 2
    lane1 = lax.broadcasted_iota(jnp.int32, (1, LANES), 1)
    head_of_lane = lane1 // HEAD_DIM

    def stage_bias(s, n):
        cum = ck_ref[0, s:s + n, :] * LOG2E
        for p, aug in enumerate(_bias_lanes(cum, True)):
            ka_scr[p, s:s + n, LANES:2 * LANES] = aug.astype(BF16)

    def stage_kv():
        for s, n in _stage_rows(past):
            for p in range(npair):
                ka_scr[p, s:s + n, 0:LANES] = kc_ref[0, p * LANES:(p + 1) * LANES, s:s + n].T.astype(BF16)
            vt_scr[:, s:s + n] = vc_ref[0, :, s:s + n].astype(BF16)
            stage_bias(s, n)
        if new_time_on_lanes:
            for s, n in _stage_rows(k_ref.shape[1]):
                for p in range(npair):
                    ka_scr[p, past + s:past + s + n, 0:LANES] = (
                        k_ref[p * LANES:(p + 1) * LANES, s:s + n].T.astype(BF16))
                vt_scr[:, past + s:past + s + n] = v_ref[:, s:s + n].astype(BF16)
                stage_bias(past + s, n)
            return
        for s, n in _stage_rows(k_ref.shape[1]):
            kb = k_ref[0, s:s + n, :].astype(BF16)
            for p in range(npair):
                ka_scr[p, past + s:past + s + n, 0:LANES] = kb[:, p * LANES:(p + 1) * LANES]
            vt_scr[:, past + s:past + s + n] = _transpose_rows(v_ref[0, s:s + n, :]).astype(BF16)
            stage_bias(past + s, n)

    if single:
        i = 0
        stage_kv()
    else:
        i = pl.program_id(1)
        pl.when(i == 0)(stage_kv)

    q = q_ref[0].astype(F32) * (HEAD_DIM ** -0.5 * LOG2E)
    cq = cq_ref[0] * LOG2E
    rhs = []
    q_bias = _bias_lanes(cq, False)
    for p in range(npair):
        q_p = q[:, p * LANES:(p + 1) * LANES]
        halves = [jnp.concatenate([jnp.where(head_of_lane == hh, q_p, 0.0), q_bias[2 * p + hh]], axis=1)
                  for hh in range(2)]
        rhs.append(jnp.concatenate(halves, axis=0).astype(BF16))

    n_full = past // tk + i * (tq // tk)
    diag0 = past if single else pl.multiple_of(past + i * tq, tq)
    krow = lax.broadcasted_iota(jnp.int32, (tq, 2 * tq), 0)
    qcol = lax.broadcasted_iota(jnp.int32, (tq, 2 * tq), 1)
    causal_t = krow <= jnp.where(qcol >= tq, qcol - tq, qcol)

    acc_scr[...] = jnp.zeros(acc_scr.shape, F32)

    def update(carry, start, size, mask):
        ms, ls = carry
        st = [_dot_nt(ka_scr[p, pl.ds(start, size), :], rhs[p]) for p in range(npair)]
        if mask is not None:
            st = [jnp.where(mask, s, NEG_INF) for s in st]
        m_new = [jnp.maximum(m, jnp.max(s, axis=0, keepdims=True)) for m, s in zip(ms, st)]
        alpha = [jnp.exp2(m - mn) for m, mn in zip(ms, m_new)]
        pt = [jnp.exp2(s - mn) for s, mn in zip(st, m_new)]
        l_new = [a * l + jnp.sum(x, axis=0, keepdims=True) for a, l, x in zip(alpha, ls, pt)]
        pv = [_dot(vt_scr[p * LANES:(p + 1) * LANES, pl.ds(start, size)], pt[p].astype(BF16))
              for p in range(npair)]
        for p in range(npair):
            acc_scr[p] = acc_scr[p] * alpha[p] + pv[p]
        return tuple(m_new), tuple(l_new)

    def body(j, carry):
        return update(carry, pl.multiple_of(j * tk, tk), tk, None)

    init = (tuple(jnp.full((1, 2 * tq), NEG_INF, F32) for _ in range(npair)),
            tuple(jnp.zeros((1, 2 * tq), F32) for _ in range(npair)))
    carry = lax.fori_loop(0, n_full, body, init)
    _, ls = update(carry, diag0, tq, causal_t)
    for p in range(npair):
        o_t = (acc_scr[p] / ls[p]).T
        o_ref[0, :, p * LANES:(p + 1) * LANES] = jnp.where(
            head_of_lane == 0, o_t[:tq], o_t[tq:]).astype(o_ref.dtype)


def _kv_specs(k_new, cache, layer, width, t_new, new_time_on_lanes=False):
    specs, args = [], []
    if cache is not None:
        past = cache[0].shape[3]
        for c in cache:
            specs.append(pl.BlockSpec((None, 1, width, past), lambda i, j: (layer, i, 0, 0)))
            args.append(c)
    for a in k_new:
        if new_time_on_lanes:
            specs.append(pl.BlockSpec((None, width, t_new), lambda i, j: (layer, 0, i)))
        else:
            specs.append(pl.BlockSpec((None, 1, t_new, width), lambda i, j: (layer, i, 0, 0)))
        args.append(a)
    return specs, args


def _fox(q, kv_new, cache, cum_q, cum_k, *, layer, tq, tk, past, new_time_on_lanes):
    b, t, _ = q.shape
    tkk = past + t
    npair = H_B // 2
    kv_specs, kv_args = _kv_specs(kv_new, cache, layer, W_B, t, new_time_on_lanes)
    return pl.pallas_call(
        functools.partial(_fox_kernel, tq=tq, tk=tk, past=past, single=(t == tq),
                          new_time_on_lanes=new_time_on_lanes),
        grid=(b, t // tq),
        in_specs=[pl.BlockSpec((1, tq, W_B), lambda i, j: (i, j, 0))] + kv_specs + [
                  pl.BlockSpec((1, tq, LANES), lambda i, j: (i, j, 0)),
                  pl.BlockSpec((1, tkk, LANES), lambda i, j: (i, 0, 0))],
        out_specs=pl.BlockSpec((1, tq, W_B), lambda i, j: (i, j, 0)),
        out_shape=jax.ShapeDtypeStruct((b, t, W_B), BF16),
        scratch_shapes=[pltpu.VMEM((npair, tkk, 2 * LANES), BF16), pltpu.VMEM((W_B, tkk), BF16),
                        pltpu.VMEM((npair, LANES, 2 * tq), F32)],
        compiler_params=_cparams(("parallel", "arbitrary")),
    )(q, *kv_args, cum_q, cum_k)


def _sb_kernel(*refs, tq, tk, past, single):
    if past:
        q_ref, kc_ref, vc_ref, k_ref, v_ref, o_ref, kb_scr, vt_scr, acc_scr = refs
    else:
        q_ref, k_ref, v_ref, o_ref, kb_scr, vt_scr, acc_scr = refs
    npair = H_C // 2
    lane1 = lax.broadcasted_iota(jnp.int32, (1, LANES), 1)
    head_of_lane = lane1 // HEAD_DIM

    def stage_kv():
        for s, n in _stage_rows(past):
            kb_scr[s:s + n, :] = kc_ref[0, :, s:s + n].T.astype(BF16)
            vt_scr[:, s:s + n] = vc_ref[0, :, s:s + n].astype(BF16)
        for s, n in _stage_rows(k_ref.shape[1]):
            kb_scr[past + s:past + s + n, :] = k_ref[0, s:s + n, :].astype(BF16)
            vt_scr[:, past + s:past + s + n] = _transpose_rows(v_ref[0, s:s + n, :]).astype(BF16)

    if single:
        i = 0
        stage_kv()
    else:
        i = pl.program_id(1)
        pl.when(i == 0)(stage_kv)

    q = q_ref[0].astype(F32) * (HEAD_DIM ** -0.5 * LOG2E)
    rhs = []
    for p in range(npair):
        q_p = q[:, p * LANES:(p + 1) * LANES]
        rhs.append(jnp.concatenate([jnp.where(head_of_lane == hh, q_p, 0.0) for hh in range(2)],
                                   axis=0).astype(BF16))

    def earlier_matrix(n):
        rr = lax.broadcasted_iota(jnp.int32, (n, n), 0)
        cc = lax.broadcasted_iota(jnp.int32, (n, n), 1)
        return (cc > rr).astype(BF16)

    n_full = past // tk + i * (tq // tk)
    diag0 = past if single else pl.multiple_of(past + i * tq, tq)
    krow = lax.broadcasted_iota(jnp.int32, (tq, 2 * tq), 0)
    qcol = lax.broadcasted_iota(jnp.int32, (tq, 2 * tq), 1)
    strict_t = krow < jnp.where(qcol >= tq, qcol - tq, qcol)

    def update(runs, start, size, mask, first):
        zt = [_dot_nt(kb_scr[pl.ds(start, size), p * LANES:(p + 1) * LANES], rhs[p])
              for p in range(npair)]
        mz = [jnp.minimum(z, 0.0) for z in zt]
        tail = [jnp.log(1.0 + jnp.exp2(m + m - z)) * LOG2E for m, z in zip(mz, zt)]
        lsig = [m - t for m, t in zip(mz, tail)]
        l1m = [s - z for s, z in zip(lsig, zt)]
        if mask is not None:
            l1m = [jnp.where(mask, x, 0.0) for x in l1m]
        later = earlier_matrix(size)
        after = [_dot(later, x.astype(BF16)) for x in l1m]
        wt = [jnp.exp2(s + a + r) for s, a, r in zip(lsig, after, runs)]
        if mask is not None:
            wt = [jnp.where(mask, w, 0.0) for w in wt]
        pv = [_dot(vt_scr[p * LANES:(p + 1) * LANES, pl.ds(start, size)], wt[p].astype(BF16))
              for p in range(npair)]
        for p in range(npair):
            acc_scr[p] = pv[p] if first else acc_scr[p] + pv[p]
        return tuple(r + jnp.sum(x, axis=0, keepdims=True) for r, x in zip(runs, l1m))

    runs = tuple(jnp.zeros((1, 2 * tq), F32) for _ in range(npair))
    runs = update(runs, diag0, tq, strict_t, True)

    def body(jj, runs):
        j = n_full - 1 - jj
        return update(runs, pl.multiple_of(j * tk, tk), tk, None, False)

    lax.fori_loop(0, n_full, body, runs)
    for p in range(npair):
        o_t = acc_scr[p].T
        o_ref[0, :, p * LANES:(p + 1) * LANES] = jnp.where(
            head_of_lane == 0, o_t[:tq], o_t[tq:]).astype(o_ref.dtype)


def _sb(q, kv_new, cache, *, layer, tq, tk, past):
    b, t, _ = q.shape
    tkk = past + kv_new[0].shape[2]
    npair = H_C // 2
    kv_specs, kv_args = _kv_specs(kv_new, cache, layer, W_C, t)
    return pl.pallas_call(
        functools.partial(_sb_kernel, tq=tq, tk=tk, past=past, single=(t == tq)),
        grid=(b, t // tq),
        in_specs=[pl.BlockSpec((1, tq, W_C), lambda i, j: (i, j, 0))] + kv_specs,
        out_specs=pl.BlockSpec((1, tq, W_C), lambda i, j: (i, j, 0)),
        out_shape=jax.ShapeDtypeStruct((b, t, W_C), BF16),
        scratch_shapes=[pltpu.VMEM((tkk, W_C), BF16), pltpu.VMEM((W_C, tkk), BF16),
                        pltpu.VMEM((npair, LANES, 2 * tq), F32)],
        compiler_params=_cparams(("parallel", "arbitrary")),
    )(q, *kv_args)


def _out_kernel(oa_ref, ob_ref, oc_ref, g_ref, x_ref, w_ref, lng_ref, lnb_ref, y_ref, *, alpha):
    g = g_ref[...].astype(F32)
    gate = g * _sigmoid(g)
    gated = lambda o_ref, lo, hi: (o_ref[...].astype(F32) * gate[:, lo:hi]).astype(BF16)
    acc = _dot(gated(oa_ref, 0, W_A), w_ref[0:W_A, :])
    acc = acc + _dot(gated(ob_ref, W_A, W_A + W_B), w_ref[W_A:W_A + W_B, :])
    acc = acc + _dot(gated(oc_ref, W_A + W_B, W_MIX), w_ref[W_A + W_B:, :])
    z = alpha * x_ref[...] + acc
    mu = jnp.mean(z, axis=-1, keepdims=True)
    d = z - mu
    var = jnp.mean(d * d, axis=-1, keepdims=True)
    y_ref[...] = d * lax.rsqrt(var + LN_EPS) * lng_ref[...] + lnb_ref[...]


def _out(oa, ob, oc, g, x2d, w_out, ln_g, ln_b, alpha, tm):
    n, d = x2d.shape
    row = lambda w: pl.BlockSpec((tm, w), lambda i: (i, 0))
    return pl.pallas_call(
        functools.partial(_out_kernel, alpha=alpha),
        grid=(n // tm,),
        in_specs=[row(W_A), row(W_B), row(W_C), row(W_MIX), row(d),
                  pl.BlockSpec((W_MIX, d), lambda i: (0, 0)),
                  pl.BlockSpec((1, d), lambda i: (0, 0)),
                  pl.BlockSpec((1, d), lambda i: (0, 0))],
        out_specs=row(d),
        out_shape=jax.ShapeDtypeStruct((n, d), F32),
        compiler_params=_cparams(("parallel",)),
    )(oa, ob, oc, g, x2d, w_out, ln_g, ln_b)


def _pad_lanes(a, width=LANES):
    return jnp.pad(a, [(0, 0)] * (a.ndim - 1) + [(0, width - a.shape[-1])])


def _state_to_pairs(s):
    b = s.shape[0]
    odd_head = (jnp.arange(H_A) % 2 == 1).reshape(1, H_A, 1, 1)
    upper_half = (jnp.arange(LANES) >= HEAD_DIM).reshape(1, 1, 1, LANES)
    wide = jnp.where(odd_head == upper_half, jnp.concatenate([s, s], axis=-1), 0.0)
    return wide.reshape(b, H_A // 2, LANES, LANES)


def _pairs_to_state(sp):
    b = sp.shape[0]
    wide = sp.reshape(b, H_A, HEAD_DIM, LANES)
    odd_head = (jnp.arange(H_A) % 2 == 1).reshape(1, H_A, 1, 1)
    return jnp.where(odd_head, wide[..., HEAD_DIM:], wide[..., :HEAD_DIM])


_IN_SIZES = (SHIFT_W, W_A, W_B, W_B, W_B, H_B, W_B, W_C, W_C, W_C, W_C)
_IN_OFFS = tuple(int(v) for v in np.concatenate([[0], np.cumsum(_IN_SIZES)]))
_SRC_SHIFT, _SRC_GA, _SRC_QB, _SRC_KB, _SRC_VB, _SRC_F, _SRC_GB, _SRC_QC, _SRC_KC, _SRC_VC, _SRC_GC = (
    _IN_OFFS[:-1])
_W_MOVES = ((_SRC_SHIFT, _OFF_SHIFT, SHIFT_W), (_SRC_GA, _OFF_G, W_A), (_SRC_GB, _OFF_G + W_A, W_B),
            (_SRC_GC, _OFF_G + W_A + W_B, W_C), (_SRC_QB, _OFF_QB, W_B), (_SRC_KB, _OFF_KB, W_B),
            (_SRC_VB, _OFF_VB, W_B), (_SRC_QC, _OFF_QC, W_C), (_SRC_KC, _OFF_KC, W_C),
            (_SRC_VC, _OFF_VC, W_C))
_W_STAGE_ROWS = 128


def _stage_w_kernel(w_ref, o_ref):
    for src, dst, width in _W_MOVES:
        o_ref[:, dst:dst + width] = w_ref[:, src:src + width].astype(BF16)
    lane = lax.broadcasted_iota(jnp.int32, (1, LANES), 1)
    o_ref[:, _OFF_F:_OFF_F + LANES] = jnp.where(
        lane < H_B, w_ref[:, _SRC_F:_SRC_F + LANES], 0.0).astype(BF16)


def _stage_w(w_in, layer):
    _, d, cols = w_in.shape
    return pl.pallas_call(
        _stage_w_kernel,
        grid=(d // _W_STAGE_ROWS,),
        in_specs=[pl.BlockSpec((None, _W_STAGE_ROWS, cols), lambda i: (layer, i, 0))],
        out_specs=pl.BlockSpec((_W_STAGE_ROWS, _PROJ_COLS), lambda i: (i, 0)),
        out_shape=jax.ShapeDtypeStruct((d, _PROJ_COLS), BF16),
        compiler_params=_cparams(("parallel",)),
    )(w_in)


def _layer_params(l, w_in, mu_shift, w0_decay, w_decay, a0, w_aaa, k_k, k_a, r_k,
                  lnx_g, lnx_b, fox_fb, w_out, ln_g, ln_b):
    w_cat = _stage_w(w_in, l)
    zeros = jnp.zeros((D_LORA, W_A), F32)
    row = lambda a: a.reshape(1, -1).astype(F32)
    return dict(
        w_cat=w_cat, fb=_pad_lanes(row(fox_fb[l])),
        mu=row(mu_shift[l]), w0=row(w0_decay[l]),
        wd=jnp.concatenate([w_decay[l], zeros], axis=0).astype(BF16),
        a0=row(a0[l]), wa=jnp.concatenate([zeros, w_aaa[l]], axis=0).astype(BF16),
        kkp=row(k_k[l]), ka=row(k_a[l]), rk=row(r_k[l]), lnx_g=row(lnx_g[l]), lnx_b=row(lnx_b[l]),
        w_out=w_out[l].astype(BF16), ln_g=row(ln_g[l]), ln_b=row(ln_b[l]))


def _run_layer(x, hist, prm, alpha, layer, depth, stacks):
    b, t, d = x.shape
    n = b * t
    x2d = x.reshape(n, d)
    tm = min(ROW_BLOCK, n)
    fox_t = hist is None
    us, g, qb, kb_st, vb_st, qc, kc_st, vc_st, lf = _proj(
        x2d, prm["w_cat"], prm["fb"], tm, layer, depth, stacks, fox_t)
    r3 = lambda a: a.reshape(b, t, a.shape[-1])
    us, qb, qc, lf = (r3(a) for a in (us, qb, qc, lf))
    r4 = lambda a: a.reshape(depth, b, t, a.shape[-1])

    if hist is None:
        past = 0
        prev0 = jnp.zeros((b, 1, SHIFT_W), F32)
        s0 = jnp.zeros((b, H_A // 2, LANES, LANES), F32)
        cache_b = cache_c = None
        lf_all = lf
        fox_blk = (min(FOX_BLOCK, t),) * 2
        sb_blk = (min(SB_BLOCK, t),) * 2
    else:
        fk_t, fv_t, h_lf, sk_t, sv_t, h_wkv, h_shift = hist
        past = fk_t.shape[3]
        prev0 = h_shift
        s0 = _state_to_pairs(h_wkv)
        cache_b, cache_c = (fk_t, fv_t), (sk_t, sv_t)
        lf_all = jnp.concatenate([_pad_lanes(h_lf), lf], axis=1)
        fox_blk = sb_blk = (t, min(SB_BLOCK, past))

    oa, s_fin = _wkv(us, prev0, s0, prm, min(WKV_BLOCK, t))
    cum = _cumsum(lf_all)
    fox_kv = (kb_st, vb_st) if fox_t else (r4(kb_st), r4(vb_st))
    ob = _fox(qb, fox_kv, cache_b, cum[:, past:], cum, layer=layer,
              tq=fox_blk[0], tk=fox_blk[1], past=past, new_time_on_lanes=fox_t)
    oc = _sb(qc, (r4(kc_st), r4(vc_st)), cache_c, layer=layer, tq=sb_blk[0], tk=sb_blk[1], past=past)
    y = _out(oa.reshape(n, W_A), ob.reshape(n, W_B), oc.reshape(n, W_C), g, x2d,
             prm["w_out"], prm["ln_g"], prm["ln_b"], alpha, tm)
    small = (lf[:, :, :H_B], _pairs_to_state(s_fin), us[:, -1:, :])
    return y.reshape(b, t, d), (kb_st, vb_st, kc_st, vc_st), small


def kernel(x_prompt, x_sample, cache_fox_k, cache_fox_v, cache_fox_logf, cache_sb_k, cache_sb_v, state_wkv, state_shift, w_in, mu_shift, w0_decay, w_decay, a0, w_aaa, k_k, k_a, r_k, lnx_g, lnx_b, fox_fb, w_out, ln_g, ln_b):
    depth = w_in.shape[0]
    alpha = (2 * depth) ** 0.25
    yp, ys = x_prompt, x_sample

    def time_on_lanes(cache):
        nl, nb, past, nh, hd = cache.shape
        return jnp.transpose(cache, (0, 1, 3, 4, 2)).reshape(nl, nb, nh * hd, past)

    fk_t, fv_t, sk_t, sv_t = (time_on_lanes(c) for c in (cache_fox_k, cache_fox_v, cache_sb_k, cache_sb_v))
    stacks_p, stacks_s = None, None
    small_p, small_s = [], []
    for l in range(depth):
        prm = _layer_params(l, w_in, mu_shift, w0_decay, w_decay, a0, w_aaa, k_k, k_a, r_k,
                            lnx_g, lnx_b, fox_fb, w_out, ln_g, ln_b)
        yp, stacks_p, sm_p = _run_layer(yp, None, prm, alpha, l, depth, stacks_p)
        hist = (fk_t, fv_t, cache_fox_logf[l], sk_t, sv_t, state_wkv[l], state_shift[l])
        ys, stacks_s, sm_s = _run_layer(ys, hist, prm, alpha, l, depth, stacks_s)
        small_p.append(sm_p)
        small_s.append(sm_s)

    def group_outputs(x, stacks, small, fox_t):
        b, t, _ = x.shape
        kb, vb, kc, vc = stacks
        logf, wkv, shift = (jnp.stack([sm[i] for sm in small]) for i in range(3))
        if fox_t:
            kb, vb = (jnp.transpose(a.reshape(depth, H_B, HEAD_DIM, b, t), (0, 3, 4, 1, 2)) for a in (kb, vb))
        return (kb.reshape(depth, b, t, H_B, HEAD_DIM), vb.reshape(depth, b, t, H_B, HEAD_DIM), logf,
                kc.reshape(depth, b, t, H_C, HEAD_DIM), vc.reshape(depth, b, t, H_C, HEAD_DIM),
                wkv, shift)

    return ((yp, ys) + group_outputs(x_prompt, stacks_p, small_p, True)
            + group_outputs(x_sample, stacks_s, small_s, False))
```

```python
import functools

import numpy as np
import jax
import jax.numpy as jnp
from jax import lax
from jax.experimental import pallas as pl
from jax.experimental.pallas import tpu as pltpu

F32 = jnp.float32
BF16 = jnp.bfloat16

HEAD_DIM = 64
H_A, H_B, H_C = 6, 6, 4
W_A, W_B, W_C = H_A * HEAD_DIM, H_B * HEAD_DIM, H_C * HEAD_DIM
W_MIX = W_A + W_B + W_C
D_LORA = 64
SHIFT_W = 3 * W_A + 2 * D_LORA
LANES = 128
WKV_CHUNK = 64
WKV_BLOCK = 1024
WKV_SUB = 256
ROW_BLOCK = 512
FOX_BLOCK = 512
SB_BLOCK = 512
GN_EPS = 64e-5
LN_EPS = 1e-5
NEG_INF = -1e30
LOG2E = 1.4426950408889634
VMEM_LIMIT = 56 * 1024 * 1024


def _cparams(sem):
    return pltpu.CompilerParams(dimension_semantics=sem, vmem_limit_bytes=VMEM_LIMIT)


def _dot(a, b):
    return jnp.dot(a, b, preferred_element_type=F32)


def _dot_nt(a, b):
    return lax.dot_general(a, b, (((1,), (1,)), ((), ())), preferred_element_type=F32)


def _dot_tn(a, b):
    return lax.dot_general(a, b, (((0,), (0,)), ((), ())), preferred_element_type=F32)


def _split3(x):
    hi = x.astype(BF16)
    r1 = x - hi.astype(F32)
    mid = r1.astype(BF16)
    lo = (r1 - mid.astype(F32)).astype(BF16)
    return hi, mid, lo


def _softplus(x):
    return jnp.maximum(x, 0.0) + jnp.log1p(jnp.exp(-jnp.abs(x)))


def _sigmoid(x):
    return 1.0 / (1.0 + jnp.exp(-x))


_OFF_SHIFT = 0
_OFF_G = _OFF_SHIFT + SHIFT_W
_OFF_QB = _OFF_G + W_MIX
_OFF_KB = _OFF_QB + W_B
_OFF_VB = _OFF_KB + W_B
_OFF_QC = _OFF_VB + W_B
_OFF_KC = _OFF_QC + W_C
_OFF_VC = _OFF_KC + W_C
_OFF_F = _OFF_VC + W_C
_PROJ_COLS = _OFF_F + LANES


def _proj_kernel(x_ref, w_ref, fb_ref, *refs):
    us_ref, g_ref, qb_ref, kb_ref, vb_ref, qc_ref, kc_ref, vc_ref, lf_ref = refs[-9:]
    xb = x_ref[...].astype(BF16)
    for ref, off in ((us_ref, _OFF_SHIFT), (g_ref, _OFF_G), (qb_ref, _OFF_QB),
                     (kb_ref, _OFF_KB), (vb_ref, _OFF_VB), (qc_ref, _OFF_QC),
                     (kc_ref, _OFF_KC), (vc_ref, _OFF_VC)):
        time_on_lanes = ref.shape[-1] == xb.shape[0] and ref.shape[0] != xb.shape[0]
        width = ref.shape[0] if time_on_lanes else ref.shape[-1]
        res = _dot(xb, w_ref[:, off:off + width])
        ref[...] = (res.T if time_on_lanes else res).astype(ref.dtype)
    f = _dot(xb, w_ref[:, _OFF_F:_OFF_F + LANES]) + fb_ref[...]
    lf_ref[...] = -_softplus(-f)


_PROJ_WIDTHS = (SHIFT_W, W_MIX, W_B, W_B, W_B, W_C, W_C, W_C, LANES)
_PROJ_BF16 = (1, 2, 5)
_PROJ_STACKED = (3, 4, 6, 7)
_PROJ_FOX_KV = (3, 4)


def _proj(x2d, w_cat, fb_pad, tm, layer, depth, stacks, seq_len, fox_time_on_lanes):
    n, d = x2d.shape
    assert tm != W_B
    per_row = seq_len // tm
    out_specs, out_shape = [], []
    for idx, w in enumerate(_PROJ_WIDTHS):
        if idx in _PROJ_FOX_KV and fox_time_on_lanes:
            out_specs.append(pl.BlockSpec((None, None, w, tm),
                                          lambda i: (layer, i // per_row, 0, i % per_row)))
            out_shape.append(jax.ShapeDtypeStruct((depth, n // seq_len, w, seq_len), F32))
        elif idx in _PROJ_STACKED:
            out_specs.append(pl.BlockSpec((None, tm, w), lambda i: (layer, i, 0)))
            out_shape.append(jax.ShapeDtypeStruct((depth, n, w), F32))
        else:
            out_specs.append(pl.BlockSpec((tm, w), lambda i: (i, 0)))
            out_shape.append(jax.ShapeDtypeStruct((n, w), BF16 if idx in _PROJ_BF16 else F32))
    in_specs = [pl.BlockSpec((tm, d), lambda i: (i, 0)),
                pl.BlockSpec((d, _PROJ_COLS), lambda i: (0, 0)),
                pl.BlockSpec((1, LANES), lambda i: (0, 0))]
    args = [x2d, w_cat, fb_pad]
    aliases = {}
    if stacks is not None:
        for j, (idx, st) in enumerate(zip(_PROJ_STACKED, stacks)):
            in_specs.append(pl.BlockSpec(memory_space=pl.ANY))
            args.append(st)
            aliases[3 + j] = idx
    return pl.pallas_call(
        _proj_kernel,
        grid=(n // tm,),
        in_specs=in_specs,
        out_specs=out_specs,
        out_shape=out_shape,
        input_output_aliases=aliases,
        compiler_params=_cparams(("parallel",)),
    )(*args)


def _cumsum_kernel(lf_ref, cum_ref, *, blk):
    tk = lf_ref.shape[1]
    row = lax.broadcasted_iota(jnp.int32, (blk, blk), 0)
    col = lax.broadcasted_iota(jnp.int32, (blk, blk), 1)
    tri = (col <= row).astype(BF16)
    carry = jnp.zeros((1, LANES), F32)
    for s in range(0, tk, blk):
        x = lf_ref[0, s:s + blk, :]
        hi, mid, lo = _split3(x)
        c = _dot(tri, hi) + _dot(tri, mid) + _dot(tri, lo) + carry
        cum_ref[0, s:s + blk, :] = c
        carry = c[blk - 1:blk, :]


def _cumsum(lf, blk=64):
    b, tk, _ = lf.shape
    return pl.pallas_call(
        functools.partial(_cumsum_kernel, blk=blk),
        grid=(b,),
        in_specs=[pl.BlockSpec((1, tk, LANES), lambda i: (i, 0, 0))],
        out_specs=pl.BlockSpec((1, tk, LANES), lambda i: (i, 0, 0)),
        out_shape=jax.ShapeDtypeStruct((b, tk, LANES), F32),
        compiler_params=_cparams(("parallel",)),
    )(lf)


def _seg_sum(x, bd):
    xb = x.astype(BF16)
    return jnp.concatenate([_dot(xb[:, s:s + LANES], bd) for s in range(0, x.shape[1], LANES)], axis=1)


def _alternate(*stages):
    gens = [g for g in stages if g is not None]
    while gens:
        for g in list(gens):
            try:
                next(g)
            except StopIteration:
                gens.remove(g)


def _wkv_kernel(us_ref, prev0_ref, s0_ref, mu_ref, w0_ref, wd_ref, a0_ref, wa_ref,
                kkp_ref, ka_ref, rk_ref, lng_ref, lnb_ref,
                oa_ref, sout_ref,
                s_scr, prev_scr, at_s, rt_s, bh_s, kh_s, be_s, ke_s, v_s, ga_s, bn_s, y_s, *, nsub):
    c_idx = pl.program_id(1)
    nc = pl.num_programs(1)
    C = WKV_CHUNK
    tt = us_ref.shape[1]
    ts = tt // nsub
    nb = ts // C
    npair = H_A // 2
    G = 4 * HEAD_DIM

    @pl.when(c_idx == 0)
    def _():
        s_scr[...] = s0_ref[0]
        prev_scr[...] = prev0_ref[0]

    lane1 = lax.broadcasted_iota(jnp.int32, (1, LANES), 1)
    lane_w = lax.broadcasted_iota(jnp.int32, (LANES, LANES), 0) // HEAD_DIM
    lane_c = lax.broadcasted_iota(jnp.int32, (LANES, LANES), 1) // HEAD_DIM
    bd_pair = lane_w == lane_c
    bd = bd_pair.astype(BF16)
    row_g = lax.broadcasted_iota(jnp.int32, (C, G), 0)
    lane_g = lax.broadcasted_iota(jnp.int32, (C, G), 1) % C
    tri_incl = lane_g <= row_g
    tri_strict = lane_g < row_g
    eye = (lane_g == row_g).astype(F32)
    diag4 = (lax.broadcasted_iota(jnp.int32, (G, G), 0) // HEAD_DIM
             == lax.broadcasted_iota(jnp.int32, (G, G), 1) // HEAD_DIM).astype(BF16)
    rowt = lax.broadcasted_iota(jnp.int32, (ts, ts), 0)
    colt = lax.broadcasted_iota(jnp.int32, (ts, ts), 1)
    tri_b = (((rowt // C) == (colt // C)) & (colt <= rowt)).astype(BF16)
    row1 = lax.broadcasted_iota(jnp.int32, (ts, 1), 0)

    def stack4(x):
        xb = x.astype(BF16)
        return jnp.concatenate([xb, xb, xb, xb], axis=0) * diag4

    def mul4(xs, ys):
        return [_dot(x.astype(BF16), stack4(y)) for x, y in zip(xs, ys)]

    def prepare(sb):
        base = sb * ts
        u = us_ref[0, base:base + ts, :]
        before = prev_scr[...] if sb == 0 else us_ref[0, base - 1:base, :]
        zprev = jnp.where(row1 == 0, before, pltpu.roll(u, 1, axis=0))
        zs = u + (zprev - u) * mu_ref[...]
        r = zs[:, 0:W_A]
        k = zs[:, W_A:2 * W_A]
        v = zs[:, 2 * W_A:3 * W_A]
        lora_in = zs[:, 3 * W_A:3 * W_A + LANES]
        lora_t = jnp.where(lane1 < D_LORA, jnp.tanh(lora_in), lora_in).astype(BF16)
        yield
        wl = w0_ref[...] + _dot(lora_t, wd_ref[...])
        a = _sigmoid(a0_ref[...] + _dot(lora_t, wa_ref[...]))
        w_log = -_softplus(-wl) - 0.5
        lw = -jnp.exp(w_log)
        yield
        kk = k * kkp_ref[...]
        kk = kk * lax.rsqrt(_seg_sum(kk * kk, bd) + 1e-12)
        k2 = k * (1.0 + (a - 1.0) * ka_ref[...])
        bn_s[sb] = _seg_sum(r * k2 * rk_ref[...], bd) * v
        v_s[sb] = v
        yield
        h1, h2, h3 = _split3(lw)
        cs = _dot(tri_b, h1) + _dot(tri_b, h2) + _dot(tri_b, h3)
        tot = jnp.concatenate([jnp.broadcast_to(cs[c * C + C - 1:c * C + C, :], (C, W_A))
                               for c in range(nb)], axis=0)
        yield
        g_inv = jnp.exp(-cs)
        g_end = jnp.exp(tot - cs)
        kka = kk * a
        at_s[sb] = kk * jnp.exp(cs - lw)
        rt_s[sb] = r * jnp.exp(cs)
        yield
        bh_s[sb] = (-kka * g_inv).astype(BF16)
        kh_s[sb] = (k2 * g_inv).astype(BF16)
        be_s[sb] = (-kka * g_end).astype(BF16)
        ke_s[sb] = (k2 * g_end).astype(BF16)
        ga_s[sb] = jnp.exp(tot)
        yield

    local = {}

    def chunk_terms(sb):
        groups = []
        for c in range(nb):
            groups.append([(c, 0), (c, LANES)])
        for c in range(0, nb, 2):
            groups.append([(c, 2 * LANES)] + ([(c + 1, 2 * LANES)] if c + 1 < nb else [None]))

        def gather(ref, grp):
            parts = [jnp.zeros((C, LANES), ref.dtype) if u is None
                     else ref[sb, u[0] * C:(u[0] + 1) * C, u[1]:u[1] + LANES] for u in grp]
            return jnp.concatenate(parts, axis=1)

        at_g = [gather(at_s, g) for g in groups]
        rt_g = [gather(rt_s, g) for g in groups]
        v_g = [gather(v_s, g) for g in groups]
        lhs = [jnp.concatenate([a, r], axis=0).astype(BF16) for a, r in zip(at_g, rt_g)]
        rhs = [jnp.concatenate([stack4(gather(bh_s, g)), stack4(gather(kh_s, g))], axis=0) for g in groups]
        a4 = [_dot_nt(l, r) for l, r in zip(lhs, rhs)]
        yield
        n1 = [jnp.where(tri_strict, a[:C, :G], 0.0) for a in a4]
        a_ak = [jnp.where(tri_strict, a[:C, G:], 0.0) for a in a4]
        a_rb = [jnp.where(tri_incl, a[C:, :G], 0.0) for a in a4]
        a_rk = [jnp.where(tri_incl, a[C:, G:], 0.0) for a in a4]
        vst = [stack4(v) for v in v_g]
        avy = [_dot(jnp.concatenate([ak, rk], axis=0).astype(BF16), vs)
               for ak, rk, vs in zip(a_ak, a_rk, vst)]
        yield
        pair4 = lambda xs, ys: [eye + x + y + m for x, y, m in zip(xs, ys, mul4(xs, ys))]
        n2 = mul4(n1, n1)
        yield
        n4 = mul4(n2, n2)
        p1 = pair4(n1, n2)
        yield
        n8 = mul4(n4, n4)
        yield
        n16 = mul4(n8, n8)
        p2 = pair4(n4, n8)
        yield
        n32 = mul4(n16, n16)
        p12 = mul4(p1, p2)
        yield
        p3 = pair4(n16, n32)
        yield
        t_inv = mul4(p12, p3)
        yield
        xs = [_dot(t.astype(BF16), jnp.concatenate([stack4(a), stack4(y[:C])], axis=1))
              for t, a, y in zip(t_inv, at_g, avy)]
        yield
        rys = [_dot(rb.astype(BF16), jnp.concatenate([stack4(x[:, :G]), stack4(x[:, G:])], axis=1))
               for rb, x in zip(a_rb, xs)]
        rp_g = [r + ry[:, :G] for r, ry in zip(rt_g, rys)]
        y0_g = [ry[:, G:] + y[C:] for ry, y in zip(rys, avy)]
        yield
        for gi, grp in enumerate(groups):
            for kpos, u in enumerate(grp):
                if u is None:
                    continue
                c, p = u[0], u[1] // LANES
                rows = slice(c * C, (c + 1) * C)
                sl = slice(p * LANES, (p + 1) * LANES)
                gl = slice(kpos * LANES, (kpos + 1) * LANES)
                be_u, ke_u = be_s[sb, rows, sl], ke_s[sb, rows, sl]
                atp = xs[gi][:, gl].astype(BF16)
                wv = jnp.concatenate([xs[gi][:, G + kpos * LANES:G + (kpos + 1) * LANES],
                                      v_s[sb, rows, sl]], axis=0).astype(BF16)
                pp = jnp.where(bd_pair, _dot_tn(atp, be_u), 0.0).astype(BF16)
                q = jnp.where(bd_pair, _dot_tn(wv, jnp.concatenate([be_u, ke_u], axis=0)), 0.0)
                local[sb, c, p] = (rp_g[gi][:, gl].astype(BF16), y0_g[gi][:, gl], pp, q)
        yield

    def sweep(sb):
        for c in range(nb):
            rows = slice(c * C, (c + 1) * C)
            s_ps = [s_scr[p] for p in range(npair)]
            s_bs = [s_p.astype(BF16) for s_p in s_ps]
            for p in range(npair):
                sl = slice(p * LANES, (p + 1) * LANES)
                rp, y0, pp, q = local[sb, c, p]
                y_s[sb, rows, sl] = _dot_nt(rp, s_bs[p]) + y0
                s_scr[p] = s_ps[p] * ga_s[sb, c * C:c * C + 1, sl] + _dot(s_bs[p], pp) + q
            yield
        y = y_s[sb]
        inv_n = 1.0 / HEAD_DIM
        mean = _seg_sum(y, bd) * inv_n
        d = y - mean
        yield
        var = _seg_sum(d * d, bd) * inv_n
        yn = d * lax.rsqrt(var + GN_EPS) * lng_ref[...] + lnb_ref[...]
        oa_ref[0, sb * ts:(sb + 1) * ts, :] = (yn + bn_s[sb]).astype(oa_ref.dtype)
        yield

    _alternate(prepare(0))
    for sb in range(nsub):
        _alternate(chunk_terms(sb),
                   prepare(sb + 1) if sb + 1 < nsub else None,
                   sweep(sb - 1) if sb > 0 else None)
    _alternate(sweep(nsub - 1))
    prev_scr[...] = us_ref[0, tt - 1:tt, :]

    @pl.when(c_idx == nc - 1)
    def _():
        sout_ref[0] = s_scr[...]


def _wkv(us, prev0, s0_bd, prm, tt):
    b, t, _ = us.shape
    npair = H_A // 2
    nsub = max(1, tt // WKV_SUB)
    ts = tt // nsub
    vec = lambda w: pl.BlockSpec((1, w), lambda i, j: (0, 0))
    mat = lambda: pl.BlockSpec((LANES, W_A), lambda i, j: (0, 0))
    blk = lambda dt: pltpu.VMEM((nsub, ts, W_A), dt)
    return pl.pallas_call(
        functools.partial(_wkv_kernel, nsub=nsub),
        grid=(b, t // tt),
        in_specs=[pl.BlockSpec((1, tt, SHIFT_W), lambda i, j: (i, j, 0)),
                  pl.BlockSpec((1, 1, SHIFT_W), lambda i, j: (i, 0, 0)),
                  pl.BlockSpec((1, npair, LANES, LANES), lambda i, j: (i, 0, 0, 0)),
                  vec(SHIFT_W), vec(W_A), mat(), vec(W_A), mat(),
                  vec(W_A), vec(W_A), vec(W_A), vec(W_A), vec(W_A)],
        out_specs=[pl.BlockSpec((1, tt, W_A), lambda i, j: (i, j, 0)),
                   pl.BlockSpec((1, npair, LANES, LANES), lambda i, j: (i, 0, 0, 0))],
        out_shape=[jax.ShapeDtypeStruct((b, t, W_A), BF16),
                   jax.ShapeDtypeStruct((b, npair, LANES, LANES), F32)],
        scratch_shapes=[pltpu.VMEM((npair, LANES, LANES), F32),
                        pltpu.VMEM((1, SHIFT_W), F32),
                        blk(F32), blk(F32), blk(BF16), blk(BF16), blk(BF16), blk(BF16),
                        blk(F32), blk(F32), blk(F32), blk(F32)],
        compiler_params=_cparams(("parallel", "arbitrary")),
    )(us, prev0, s0_bd, prm["mu"], prm["w0"], prm["wd"], prm["a0"], prm["wa"],
      prm["kkp"], prm["ka"], prm["rk"], prm["lnx_g"], prm["lnx_b"])


_AUG = 8
_PREP_ROWS = 256


def _bias_lanes(cum, key_side):
    ntile = H_B // 2 if key_side else H_B
    width = ntile * LANES
    r = lax.broadcasted_iota(jnp.int32, (LANES, width), 0)
    c = lax.broadcasted_iota(jnp.int32, (LANES, width), 1)
    lane = lax.broadcasted_iota(jnp.int32, (1, width), 1)
    f_off, one_off = (0, 3) if key_side else (3, 0)
    sign = -1.0 if key_side else 1.0
    head_of_tile = (c // LANES) * 2 + (c % LANES) // _AUG if key_side else c // LANES
    slot = c % _AUG
    in_head_lanes = (c % LANES) // _AUG == r % 2
    one_head = (lane % LANES) // _AUG < 2 if key_side else (lane % LANES) // _AUG == (lane // LANES) % 2
    out = jnp.where(one_head & (lane % _AUG >= one_off) & (lane % _AUG < one_off + 3), 1.0, 0.0)
    for t, part in enumerate(_split3(cum)):
        sel = jnp.where((head_of_tile == r) & in_head_lanes & (slot == f_off + t), sign, 0.0).astype(BF16)
        out = out + _dot(part, sel)
    return [out[:, i * LANES:(i + 1) * LANES] for i in range(ntile)]


def _stage_rows(total):
    return [(s, min(_PREP_ROWS, total - s)) for s in range(0, total, _PREP_ROWS)]


def _transpose_rows(x):
    n = x.shape[0]
    n_pad = -n % LANES
    if n_pad:
        x = jnp.concatenate([x, jnp.zeros((n_pad, x.shape[1]), x.dtype)], axis=0)
    return x.T[:, :n]


def _fox_kernel(*refs, tq, tk, past, single, new_time_on_lanes):
    if past:
        q_ref, kc_ref, vc_ref, k_ref, v_ref, cq_ref, ck_ref, o_ref, ka_scr, vt_scr, acc_scr = refs
    else:
        q_ref, k_ref, v_ref, cq_ref, ck_ref, o_ref, ka_scr, vt_scr, acc_scr = refs
    npair = H_B // 2
    lane1 = lax.broadcasted_iota(jnp.int32, (1, LANES), 1)
    head_of_lane = lane1 // HEAD_DIM

    def stage_bias(s, n):
        cum = ck_ref[0, s:s + n, :] * LOG2E
        for p, aug in enumerate(_bias_lanes(cum, True)):
            ka_scr[p, s:s + n, LANES:2 * LANES] = aug.astype(BF16)

    def stage_kv():
        for s, n in _stage_rows(past):
            for p in range(npair):
                ka_scr[p, s:s + n, 0:LANES] = kc_ref[0, p * LANES:(p + 1) * LANES, s:s + n].T.astype(BF16)
            vt_scr[:, s:s + n] = vc_ref[0, :, s:s + n].astype(BF16)
            stage_bias(s, n)
        if new_time_on_lanes:
            for s, n in _stage_rows(k_ref.shape[2]):
                for p in range(npair):
                    ka_scr[p, past + s:past + s + n, 0:LANES] = (
                        k_ref[0, p * LANES:(p + 1) * LANES, s:s + n].T.astype(BF16))
                vt_scr[:, past + s:past + s + n] = v_ref[0, :, s:s + n].astype(BF16)
                stage_bias(past + s, n)
            return
        for s, n in _stage_rows(k_ref.shape[1]):
            kb = k_ref[0, s:s + n, :].astype(BF16)
            for p in range(npair):
                ka_scr[p, past + s:past + s + n, 0:LANES] = kb[:, p * LANES:(p + 1) * LANES]
            vt_scr[:, past + s:past + s + n] = _transpose_rows(v_ref[0, s:s + n, :]).astype(BF16)
            stage_bias(past + s, n)

    if single:
        i = 0
        stage_kv()
    else:
        i = pl.program_id(1)
        pl.when(i == 0)(stage_kv)

    q = q_ref[0].astype(F32) * (HEAD_DIM ** -0.5 * LOG2E)
    cq = cq_ref[0] * LOG2E
    rhs = []
    q_bias = _bias_lanes(cq, False)
    for p in range(npair):
        q_p = q[:, p * LANES:(p + 1) * LANES]
        halves = [jnp.concatenate([jnp.where(head_of_lane == hh, q_p, 0.0), q_bias[2 * p + hh]], axis=1)
                  for hh in range(2)]
        rhs.append(jnp.concatenate(halves, axis=0).astype(BF16))

    n_full = past // tk + i * (tq // tk)
    diag0 = past if single else pl.multiple_of(past + i * tq, tq)
    krow = lax.broadcasted_iota(jnp.int32, (tq, 2 * tq), 0)
    qcol = lax.broadcasted_iota(jnp.int32, (tq, 2 * tq), 1)
    causal_t = krow <= jnp.where(qcol >= tq, qcol - tq, qcol)

    acc_scr[...] = jnp.zeros(acc_scr.shape, F32)

    def update(carry, start, size, mask):
        ms, ls = carry
        st = [_dot_nt(ka_scr[p, pl.ds(start, size), :], rhs[p]) for p in range(npair)]
        if mask is not None:
            st = [jnp.where(mask, s, NEG_INF) for s in st]
        m_new = [jnp.maximum(m, jnp.max(s, axis=0, keepdims=True)) for m, s in zip(ms, st)]
        alpha = [jnp.exp2(m - mn) for m, mn in zip(ms, m_new)]
        pt = [jnp.exp2(s - mn) for s, mn in zip(st, m_new)]
        l_new = [a * l + jnp.sum(x, axis=0, keepdims=True) for a, l, x in zip(alpha, ls, pt)]
        pv = [_dot(vt_scr[p * LANES:(p + 1) * LANES, pl.ds(start, size)], pt[p].astype(BF16))
              for p in range(npair)]
        for p in range(npair):
            acc_scr[p] = acc_scr[p] * alpha[p] + pv[p]
        return tuple(m_new), tuple(l_new)

    def body(j, carry):
        return update(carry, pl.multiple_of(j * tk, tk), tk, None)

    init = (tuple(jnp.full((1, 2 * tq), NEG_INF, F32) for _ in range(npair)),
            tuple(jnp.zeros((1, 2 * tq), F32) for _ in range(npair)))
    carry = lax.fori_loop(0, n_full, body, init)
    _, ls = update(carry, diag0, tq, causal_t)
    for p in range(npair):
        o_t = (acc_scr[p] / ls[p]).T
        o_ref[0, :, p * LANES:(p + 1) * LANES] = jnp.where(
            head_of_lane == 0, o_t[:tq], o_t[tq:]).astype(o_ref.dtype)


def _kv_specs(k_new, cache, layer, width, t_new, new_time_on_lanes=False):
    specs, args = [], []
    if cache is not None:
        past = cache[0].shape[3]
        for c in cache:
            specs.append(pl.BlockSpec((None, 1, width, past), lambda i, j: (layer, i, 0, 0)))
            args.append(c)
    for a in k_new:
        if new_time_on_lanes:
            specs.append(pl.BlockSpec((None, 1, width, t_new), lambda i, j: (layer, i, 0, 0)))
        else:
            specs.append(pl.BlockSpec((None, 1, t_new, width), lambda i, j: (layer, i, 0, 0)))
        args.append(a)
    return specs, args


def _fox(q, kv_new, cache, cum_q, cum_k, *, layer, tq, tk, past, new_time_on_lanes):
    b, t, _ = q.shape
    tkk = past + t
    npair = H_B // 2
    kv_specs, kv_args = _kv_specs(kv_new, cache, layer, W_B, t, new_time_on_lanes)
    return pl.pallas_call(
        functools.partial(_fox_kernel, tq=tq, tk=tk, past=past, single=(t == tq),
                          new_time_on_lanes=new_time_on_lanes),
        grid=(b, t // tq),
        in_specs=[pl.BlockSpec((1, tq, W_B), lambda i, j: (i, j, 0))] + kv_specs + [
                  pl.BlockSpec((1, tq, LANES), lambda i, j: (i, j, 0)),
                  pl.BlockSpec((1, tkk, LANES), lambda i, j: (i, 0, 0))],
        out_specs=pl.BlockSpec((1, tq, W_B), lambda i, j: (i, j, 0)),
        out_shape=jax.ShapeDtypeStruct((b, t, W_B), BF16),
        scratch_shapes=[pltpu.VMEM((npair, tkk, 2 * LANES), BF16), pltpu.VMEM((W_B, tkk), BF16),
                        pltpu.VMEM((npair, LANES, 2 * tq), F32)],
        compiler_params=_cparams(("parallel", "arbitrary")),
    )(q, *kv_args, cum_q, cum_k)


def _sb_kernel(*refs, tq, tk, past, single):
    if past:
        q_ref, kc_ref, vc_ref, k_ref, v_ref, o_ref, kb_scr, vt_scr, acc_scr = refs
    else:
        q_ref, k_ref, v_ref, o_ref, kb_scr, vt_scr, acc_scr = refs
    npair = H_C // 2
    lane1 = lax.broadcasted_iota(jnp.int32, (1, LANES), 1)
    head_of_lane = lane1 // HEAD_DIM

    def stage_kv():
        for s, n in _stage_rows(past):
            kb_scr[s:s + n, :] = kc_ref[0, :, s:s + n].T.astype(BF16)
            vt_scr[:, s:s + n] = vc_ref[0, :, s:s + n].astype(BF16)
        for s, n in _stage_rows(k_ref.shape[1]):
            kb_scr[past + s:past + s + n, :] = k_ref[0, s:s + n, :].astype(BF16)
            vt_scr[:, past + s:past + s + n] = _transpose_rows(v_ref[0, s:s + n, :]).astype(BF16)

    if single:
        i = 0
        stage_kv()
    else:
        i = pl.program_id(1)
        pl.when(i == 0)(stage_kv)

    q = q_ref[0].astype(F32) * (HEAD_DIM ** -0.5 * LOG2E)
    rhs = []
    for p in range(npair):
        q_p = q[:, p * LANES:(p + 1) * LANES]
        rhs.append(jnp.concatenate([jnp.where(head_of_lane == hh, q_p, 0.0) for hh in range(2)],
                                   axis=0).astype(BF16))

    def earlier_matrix(n):
        rr = lax.broadcasted_iota(jnp.int32, (n, n), 0)
        cc = lax.broadcasted_iota(jnp.int32, (n, n), 1)
        return (cc > rr).astype(BF16)

    n_full = past // tk + i * (tq // tk)
    diag0 = past if single else pl.multiple_of(past + i * tq, tq)
    krow = lax.broadcasted_iota(jnp.int32, (tq, 2 * tq), 0)
    qcol = lax.broadcasted_iota(jnp.int32, (tq, 2 * tq), 1)
    strict_t = krow < jnp.where(qcol >= tq, qcol - tq, qcol)

    def update(runs, start, size, mask, first):
        zt = [_dot_nt(kb_scr[pl.ds(start, size), p * LANES:(p + 1) * LANES], rhs[p])
              for p in range(npair)]
        mz = [jnp.minimum(z, 0.0) for z in zt]
        tail = [jnp.log(1.0 + jnp.exp2(m + m - z)) * LOG2E for m, z in zip(mz, zt)]
        lsig = [m - t for m, t in zip(mz, tail)]
        l1m = [s - z for s, z in zip(lsig, zt)]
        if mask is not None:
            l1m = [jnp.where(mask, x, 0.0) for x in l1m]
        later = earlier_matrix(size)
        after = [_dot(later, x.astype(BF16)) for x in l1m]
        wt = [jnp.exp2(s + a + r) for s, a, r in zip(lsig, after, runs)]
        if mask is not None:
            wt = [jnp.where(mask, w, 0.0) for w in wt]
        pv = [_dot(vt_scr[p * LANES:(p + 1) * LANES, pl.ds(start, size)], wt[p].astype(BF16))
              for p in range(npair)]
        for p in range(npair):
            acc_scr[p] = pv[p] if first else acc_scr[p] + pv[p]
        return tuple(r + jnp.sum(x, axis=0, keepdims=True) for r, x in zip(runs, l1m))

    runs = tuple(jnp.zeros((1, 2 * tq), F32) for _ in range(npair))
    runs = update(runs, diag0, tq, strict_t, True)

    def body(jj, runs):
        j = n_full - 1 - jj
        return update(runs, pl.multiple_of(j * tk, tk), tk, None, False)

    lax.fori_loop(0, n_full, body, runs)
    for p in range(npair):
        o_t = acc_scr[p].T
        o_ref[0, :, p * LANES:(p + 1) * LANES] = jnp.where(
            head_of_lane == 0, o_t[:tq], o_t[tq:]).astype(o_ref.dtype)


def _sb(q, kv_new, cache, *, layer, tq, tk, past):
    b, t, _ = q.shape
    tkk = past + kv_new[0].shape[2]
    npair = H_C // 2
    kv_specs, kv_args = _kv_specs(kv_new, cache, layer, W_C, t)
    return pl.pallas_call(
        functools.partial(_sb_kernel, tq=tq, tk=tk, past=past, single=(t == tq)),
        grid=(b, t // tq),
        in_specs=[pl.BlockSpec((1, tq, W_C), lambda i, j: (i, j, 0))] + kv_specs,
        out_specs=pl.BlockSpec((1, tq, W_C), lambda i, j: (i, j, 0)),
        out_shape=jax.ShapeDtypeStruct((b, t, W_C), BF16),
        scratch_shapes=[pltpu.VMEM((tkk, W_C), BF16), pltpu.VMEM((W_C, tkk), BF16),
                        pltpu.VMEM((npair, LANES, 2 * tq), F32)],
        compiler_params=_cparams(("parallel", "arbitrary")),
    )(q, *kv_args)


def _out_kernel(oa_ref, ob_ref, oc_ref, g_ref, x_ref, w_ref, lng_ref, lnb_ref, y_ref, *, alpha):
    g = g_ref[...].astype(F32)
    gate = g * _sigmoid(g)
    gated = lambda o_ref, lo, hi: (o_ref[...].astype(F32) * gate[:, lo:hi]).astype(BF16)
    acc = _dot(gated(oa_ref, 0, W_A), w_ref[0:W_A, :])
    acc = acc + _dot(gated(ob_ref, W_A, W_A + W_B), w_ref[W_A:W_A + W_B, :])
    acc = acc + _dot(gated(oc_ref, W_A + W_B, W_MIX), w_ref[W_A + W_B:, :])
    z = alpha * x_ref[...] + acc
    mu = jnp.mean(z, axis=-1, keepdims=True)
    d = z - mu
    var = jnp.mean(d * d, axis=-1, keepdims=True)
    y_ref[...] = d * lax.rsqrt(var + LN_EPS) * lng_ref[...] + lnb_ref[...]


def _out(oa, ob, oc, g, x2d, w_out, ln_g, ln_b, alpha, tm):
    n, d = x2d.shape
    row = lambda w: pl.BlockSpec((tm, w), lambda i: (i, 0))
    return pl.pallas_call(
        functools.partial(_out_kernel, alpha=alpha),
        grid=(n // tm,),
        in_specs=[row(W_A), row(W_B), row(W_C), row(W_MIX), row(d),
                  pl.BlockSpec((W_MIX, d), lambda i: (0, 0)),
                  pl.BlockSpec((1, d), lambda i: (0, 0)),
                  pl.BlockSpec((1, d), lambda i: (0, 0))],
        out_specs=row(d),
        out_shape=jax.ShapeDtypeStruct((n, d), F32),
        compiler_params=_cparams(("parallel",)),
    )(oa, ob, oc, g, x2d, w_out, ln_g, ln_b)


def _pad_lanes(a, width=LANES):
    return jnp.pad(a, [(0, 0)] * (a.ndim - 1) + [(0, width - a.shape[-1])])


def _state_to_pairs(s):
    b = s.shape[0]
    odd_head = (jnp.arange(H_A) % 2 == 1).reshape(1, H_A, 1, 1)
    upper_half = (jnp.arange(LANES) >= HEAD_DIM).reshape(1, 1, 1, LANES)
    wide = jnp.where(odd_head == upper_half, jnp.concatenate([s, s], axis=-1), 0.0)
    return wide.reshape(b, H_A // 2, LANES, LANES)


def _pairs_to_state(sp):
    b = sp.shape[0]
    wide = sp.reshape(b, H_A, HEAD_DIM, LANES)
    odd_head = (jnp.arange(H_A) % 2 == 1).reshape(1, H_A, 1, 1)
    return jnp.where(odd_head, wide[..., HEAD_DIM:], wide[..., :HEAD_DIM])


_IN_SIZES = (SHIFT_W, W_A, W_B, W_B, W_B, H_B, W_B, W_C, W_C, W_C, W_C)
_IN_OFFS = tuple(int(v) for v in np.concatenate([[0], np.cumsum(_IN_SIZES)]))
_SRC_SHIFT, _SRC_GA, _SRC_QB, _SRC_KB, _SRC_VB, _SRC_F, _SRC_GB, _SRC_QC, _SRC_KC, _SRC_VC, _SRC_GC = (
    _IN_OFFS[:-1])
_W_MOVES = ((_SRC_SHIFT, _OFF_SHIFT, SHIFT_W), (_SRC_GA, _OFF_G, W_A), (_SRC_GB, _OFF_G + W_A, W_B),
            (_SRC_GC, _OFF_G + W_A + W_B, W_C), (_SRC_QB, _OFF_QB, W_B), (_SRC_KB, _OFF_KB, W_B),
            (_SRC_VB, _OFF_VB, W_B), (_SRC_QC, _OFF_QC, W_C), (_SRC_KC, _OFF_KC, W_C),
            (_SRC_VC, _OFF_VC, W_C))
_W_STAGE_ROWS = 128


def _stage_w_kernel(w_ref, o_ref):
    for src, dst, width in _W_MOVES:
        o_ref[:, dst:dst + width] = w_ref[:, src:src + width].astype(BF16)
    lane = lax.broadcasted_iota(jnp.int32, (1, LANES), 1)
    o_ref[:, _OFF_F:_OFF_F + LANES] = jnp.where(
        lane < H_B, w_ref[:, _SRC_F:_SRC_F + LANES], 0.0).astype(BF16)


def _stage_w(w_in, layer):
    _, d, cols = w_in.shape
    return pl.pallas_call(
        _stage_w_kernel,
        grid=(d // _W_STAGE_ROWS,),
        in_specs=[pl.BlockSpec((None, _W_STAGE_ROWS, cols), lambda i: (layer, i, 0))],
        out_specs=pl.BlockSpec((_W_STAGE_ROWS, _PROJ_COLS), lambda i: (i, 0)),
        out_shape=jax.ShapeDtypeStruct((d, _PROJ_COLS), BF16),
        compiler_params=_cparams(("parallel",)),
    )(w_in)


def _layer_params(l, w_in, mu_shift, w0_decay, w_decay, a0, w_aaa, k_k, k_a, r_k,
                  lnx_g, lnx_b, fox_fb, w_out, ln_g, ln_b):
    w_cat = _stage_w(w_in, l)
    zeros = jnp.zeros((D_LORA, W_A), F32)
    row = lambda a: a.reshape(1, -1).astype(F32)
    return dict(
        w_cat=w_cat, fb=_pad_lanes(row(fox_fb[l])),
        mu=row(mu_shift[l]), w0=row(w0_decay[l]),
        wd=jnp.concatenate([w_decay[l], zeros], axis=0).astype(BF16),
        a0=row(a0[l]), wa=jnp.concatenate([zeros, w_aaa[l]], axis=0).astype(BF16),
        kkp=row(k_k[l]), ka=row(k_a[l]), rk=row(r_k[l]), lnx_g=row(lnx_g[l]), lnx_b=row(lnx_b[l]),
        w_out=w_out[l].astype(BF16), ln_g=row(ln_g[l]), ln_b=row(ln_b[l]))


def _run_layer(x, hist, prm, alpha, layer, depth, stacks):
    b, t, d = x.shape
    n = b * t
    x2d = x.reshape(n, d)
    tm = min(ROW_BLOCK, n)
    fox_t = hist is None
    us, g, qb, kb_st, vb_st, qc, kc_st, vc_st, lf = _proj(
        x2d, prm["w_cat"], prm["fb"], tm, layer, depth, stacks, t, fox_t)
    r3 = lambda a: a.reshape(b, t, a.shape[-1])
    us, qb, qc, lf = (r3(a) for a in (us, qb, qc, lf))
    r4 = lambda a: a.reshape(depth, b, t, a.shape[-1])

    if hist is None:
        past = 0
        prev0 = jnp.zeros((b, 1, SHIFT_W), F32)
        s0 = jnp.zeros((b, H_A // 2, LANES, LANES), F32)
        cache_b = cache_c = None
        lf_all = lf
        fox_blk = (min(FOX_BLOCK, t),) * 2
        sb_blk = (min(SB_BLOCK, t),) * 2
    else:
        fk_t, fv_t, h_lf, sk_t, sv_t, h_wkv, h_shift = hist
        past = fk_t.shape[3]
        prev0 = h_shift
        s0 = _state_to_pairs(h_wkv)
        cache_b, cache_c = (fk_t, fv_t), (sk_t, sv_t)
        lf_all = jnp.concatenate([_pad_lanes(h_lf), lf], axis=1)
        fox_blk = sb_blk = (t, min(SB_BLOCK, past))

    oa, s_fin = _wkv(us, prev0, s0, prm, min(WKV_BLOCK, t))
    cum = _cumsum(lf_all)
    fox_kv = (kb_st, vb_st) if fox_t else (r4(kb_st), r4(vb_st))
    ob = _fox(qb, fox_kv, cache_b, cum[:, past:], cum, layer=layer,
              tq=fox_blk[0], tk=fox_blk[1], past=past, new_time_on_lanes=fox_t)
    oc = _sb(qc, (r4(kc_st), r4(vc_st)), cache_c, layer=layer, tq=sb_blk[0], tk=sb_blk[1], past=past)
    y = _out(oa.reshape(n, W_A), ob.reshape(n, W_B), oc.reshape(n, W_C), g, x2d,
             prm["w_out"], prm["ln_g"], prm["ln_b"], alpha, tm)
    small = (lf[:, :, :H_B], _pairs_to_state(s_fin), us[:, -1:, :])
    return y.reshape(b, t, d), (kb_st, vb_st, kc_st, vc_st), small


def kernel(x_prompt, x_sample, cache_fox_k, cache_fox_v, cache_fox_logf, cache_sb_k, cache_sb_v, state_wkv, state_shift, w_in, mu_shift, w0_decay, w_decay, a0, w_aaa, k_k, k_a, r_k, lnx_g, lnx_b, fox_fb, w_out, ln_g, ln_b):
    depth = w_in.shape[0]
    alpha = (2 * depth) ** 0.25
    yp, ys = x_prompt, x_sample

    def time_on_lanes(cache):
        nl, nb, past, nh, hd = cache.shape
        return jnp.transpose(cache, (0, 1, 3, 4, 2)).reshape(nl, nb, nh * hd, past)

    fk_t, fv_t, sk_t, sv_t = (time_on_lanes(c) for c in (cache_fox_k, cache_fox_v, cache_sb_k, cache_sb_v))
    stacks_p, stacks_s = None, None
    small_p, small_s = [], []
    for l in range(depth):
        prm = _layer_params(l, w_in, mu_shift, w0_decay, w_decay, a0, w_aaa, k_k, k_a, r_k,
                            lnx_g, lnx_b, fox_fb, w_out, ln_g, ln_b)
        yp, stacks_p, sm_p = _run_layer(yp, None, prm, alpha, l, depth, stacks_p)
        hist = (fk_t, fv_t, cache_fox_logf[l], sk_t, sv_t, state_wkv[l], state_shift[l])
        ys, stacks_s, sm_s = _run_layer(ys, hist, prm, alpha, l, depth, stacks_s)
        small_p.append(sm_p)
        small_s.append(sm_s)

    def group_outputs(x, stacks, small, fox_t):
        b, t, _ = x.shape
        kb, vb, kc, vc = stacks
        logf, wkv, shift = (jnp.stack([sm[i] for sm in small]) for i in range(3))
        if fox_t:
            kb, vb = (jnp.transpose(a.reshape(depth, b, H_B, HEAD_DIM, t), (0, 1, 4, 2, 3)) for a in (kb, vb))
        return (kb.reshape(depth, b, t, H_B, HEAD_DIM), vb.reshape(depth, b, t, H_B, HEAD_DIM), logf,
                kc.reshape(depth, b, t, H_C, HEAD_DIM), vc.reshape(depth, b, t, H_C, HEAD_DIM),
                wkv, shift)

    return ((yp, ys) + group_outputs(x_prompt, stacks_p, small_p, True)
            + group_outputs(x_sample, stacks_s, small_s, False))
```

```python
import functools

import numpy as np
import jax
import jax.numpy as jnp
from jax import lax
from jax.experimental import pallas as pl
from jax.experimental.pallas import tpu as pltpu

F32 = jnp.float32
BF16 = jnp.bfloat16

HEAD_DIM = 64
H_A, H_B, H_C = 6, 6, 4
W_A, W_B, W_C = H_A * HEAD_DIM, H_B * HEAD_DIM, H_C * HEAD_DIM
W_MIX = W_A + W_B + W_C
D_LORA = 64
SHIFT_W = 3 * W_A + 2 * D_LORA
LANES = 128
WKV_CHUNK = 64
WKV_BLOCK = 1024
WKV_SUB = 256
ROW_BLOCK = 512
FOX_BLOCK = 512
SB_BLOCK = 512
GN_EPS = 64e-5
LN_EPS = 1e-5
NEG_INF = -1e30
LOG2E = 1.4426950408889634
VMEM_LIMIT = 56 * 1024 * 1024


def _cparams(sem):
    return pltpu.CompilerParams(dimension_semantics=sem, vmem_limit_bytes=VMEM_LIMIT)


def _dot(a, b):
    return jnp.dot(a, b, preferred_element_type=F32)


def _dot_nt(a, b):
    return lax.dot_general(a, b, (((1,), (1,)), ((), ())), preferred_element_type=F32)


def _dot_tn(a, b):
    return lax.dot_general(a, b, (((0,), (0,)), ((), ())), preferred_element_type=F32)


def _split3(x):
    hi = x.astype(BF16)
    r1 = x - hi.astype(F32)
    mid = r1.astype(BF16)
    lo = (r1 - mid.astype(F32)).astype(BF16)
    return hi, mid, lo


def _softplus(x):
    return jnp.maximum(x, 0.0) + jnp.log1p(jnp.exp(-jnp.abs(x)))


def _sigmoid(x):
    return 1.0 / (1.0 + jnp.exp(-x))


_OFF_SHIFT = 0
_OFF_G = _OFF_SHIFT + SHIFT_W
_OFF_QB = _OFF_G + W_MIX
_OFF_KB = _OFF_QB + W_B
_OFF_VB = _OFF_KB + W_B
_OFF_QC = _OFF_VB + W_B
_OFF_KC = _OFF_QC + W_C
_OFF_VC = _OFF_KC + W_C
_OFF_F = _OFF_VC + W_C
_PROJ_COLS = _OFF_F + LANES


def _proj_kernel(x_ref, w_ref, fb_ref, *refs):
    (us_ref, g_ref, qb_ref, kb_ref, vb_ref, qc_ref, kc_ref, vc_ref, lf_ref,
     kc_rows_ref, vc_rows_ref) = refs[-11:]
    xb = x_ref[...].astype(BF16)
    tm = xb.shape[0]
    for ref, off in ((us_ref, _OFF_SHIFT), (g_ref, _OFF_G), (qb_ref, _OFF_QB),
                     (kb_ref, _OFF_KB), (vb_ref, _OFF_VB), (qc_ref, _OFF_QC)):
        time_on_lanes = ref.shape[-1] == tm and ref.shape[0] != tm
        width = ref.shape[0] if time_on_lanes else ref.shape[-1]
        res = _dot(xb, w_ref[:, off:off + width])
        ref[...] = (res.T if time_on_lanes else res).astype(ref.dtype)
    for ref, rows_ref, off in ((kc_ref, kc_rows_ref, _OFF_KC), (vc_ref, vc_rows_ref, _OFF_VC)):
        res = _dot(xb, w_ref[:, off:off + W_C])
        rows_ref[...] = res.astype(rows_ref.dtype)
        for h in range(H_C):
            ref[pl.ds(h, tm, stride=H_C), :] = res[:, h * HEAD_DIM:(h + 1) * HEAD_DIM]
    f = _dot(xb, w_ref[:, _OFF_F:_OFF_F + LANES]) + fb_ref[...]
    lf_ref[...] = -_softplus(-f)


_PROJ_WIDTHS = (SHIFT_W, W_MIX, W_B, W_B, W_B, W_C, W_C, W_C, LANES, W_C, W_C)
_PROJ_BF16 = (1, 2, 5, 9, 10)
_PROJ_SB_KV = (6, 7)
_PROJ_STACKED = (3, 4, 6, 7)
_PROJ_FOX_KV = (3, 4)


def _proj(x2d, w_cat, fb_pad, tm, layer, depth, stacks, seq_len, fox_time_on_lanes):
    n, d = x2d.shape
    assert tm != W_B
    per_row = seq_len // tm
    out_specs, out_shape = [], []
    for idx, w in enumerate(_PROJ_WIDTHS):
        if idx in _PROJ_FOX_KV and fox_time_on_lanes:
            out_specs.append(pl.BlockSpec((None, None, w, tm),
                                          lambda i: (layer, i // per_row, 0, i % per_row)))
            out_shape.append(jax.ShapeDtypeStruct((depth, n // seq_len, w, seq_len), F32))
        elif idx in _PROJ_SB_KV:
            out_specs.append(pl.BlockSpec((None, tm * H_C, HEAD_DIM), lambda i: (layer, i, 0)))
            out_shape.append(jax.ShapeDtypeStruct((depth, n * H_C, HEAD_DIM), F32))
        elif idx in _PROJ_STACKED:
            out_specs.append(pl.BlockSpec((None, tm, w), lambda i: (layer, i, 0)))
            out_shape.append(jax.ShapeDtypeStruct((depth, n, w), F32))
        else:
            out_specs.append(pl.BlockSpec((tm, w), lambda i: (i, 0)))
            out_shape.append(jax.ShapeDtypeStruct((n, w), BF16 if idx in _PROJ_BF16 else F32))
    in_specs = [pl.BlockSpec((tm, d), lambda i: (i, 0)),
                pl.BlockSpec((d, _PROJ_COLS), lambda i: (0, 0)),
                pl.BlockSpec((1, LANES), lambda i: (0, 0))]
    args = [x2d, w_cat, fb_pad]
    aliases = {}
    if stacks is not None:
        for j, (idx, st) in enumerate(zip(_PROJ_STACKED, stacks)):
            in_specs.append(pl.BlockSpec(memory_space=pl.ANY))
            args.append(st)
            aliases[3 + j] = idx
    return pl.pallas_call(
        _proj_kernel,
        grid=(n // tm,),
        in_specs=in_specs,
        out_specs=out_specs,
        out_shape=out_shape,
        input_output_aliases=aliases,
        compiler_params=_cparams(("parallel",)),
    )(*args)


def _cumsum_kernel(lf_ref, cum_ref, *, blk):
    tk = lf_ref.shape[1]
    row = lax.broadcasted_iota(jnp.int32, (blk, blk), 0)
    col = lax.broadcasted_iota(jnp.int32, (blk, blk), 1)
    tri = (col <= row).astype(BF16)
    carry = jnp.zeros((1, LANES), F32)
    for s in range(0, tk, blk):
        x = lf_ref[0, s:s + blk, :]
        hi, mid, lo = _split3(x)
        c = _dot(tri, hi) + _dot(tri, mid) + _dot(tri, lo) + carry
        cum_ref[0, s:s + blk, :] = c
        carry = c[blk - 1:blk, :]


def _cumsum(lf, blk=64):
    b, tk, _ = lf.shape
    return pl.pallas_call(
        functools.partial(_cumsum_kernel, blk=blk),
        grid=(b,),
        in_specs=[pl.BlockSpec((1, tk, LANES), lambda i: (i, 0, 0))],
        out_specs=pl.BlockSpec((1, tk, LANES), lambda i: (i, 0, 0)),
        out_shape=jax.ShapeDtypeStruct((b, tk, LANES), F32),
        compiler_params=_cparams(("parallel",)),
    )(lf)


def _seg_sum(x, bd):
    xb = x.astype(BF16)
    return jnp.concatenate([_dot(xb[:, s:s + LANES], bd) for s in range(0, x.shape[1], LANES)], axis=1)


def _alternate(*stages):
    gens = [g for g in stages if g is not None]
    while gens:
        for g in list(gens):
            try:
                next(g)
            except StopIteration:
                gens.remove(g)


def _wkv_kernel(us_ref, prev0_ref, s0_ref, mu_ref, w0_ref, wd_ref, a0_ref, wa_ref,
                kkp_ref, ka_ref, rk_ref, lng_ref, lnb_ref,
                oa_ref, sout_ref,
                s_scr, prev_scr, at_s, rt_s, bh_s, kh_s, be_s, ke_s, v_s, ga_s, bn_s, y_s, *, nsub):
    c_idx = pl.program_id(1)
    nc = pl.num_programs(1)
    C = WKV_CHUNK
    tt = us_ref.shape[1]
    ts = tt // nsub
    nb = ts // C
    npair = H_A // 2
    G = 4 * HEAD_DIM

    @pl.when(c_idx == 0)
    def _():
        s_scr[...] = s0_ref[0]
        prev_scr[...] = prev0_ref[0]

    lane1 = lax.broadcasted_iota(jnp.int32, (1, LANES), 1)
    lane_w = lax.broadcasted_iota(jnp.int32, (LANES, LANES), 0) // HEAD_DIM
    lane_c = lax.broadcasted_iota(jnp.int32, (LANES, LANES), 1) // HEAD_DIM
    bd_pair = lane_w == lane_c
    bd = bd_pair.astype(BF16)
    row_g = lax.broadcasted_iota(jnp.int32, (C, G), 0)
    lane_g = lax.broadcasted_iota(jnp.int32, (C, G), 1) % C
    tri_incl = lane_g <= row_g
    tri_strict = lane_g < row_g
    eye = (lane_g == row_g).astype(F32)
    diag4 = (lax.broadcasted_iota(jnp.int32, (G, G), 0) // HEAD_DIM
             == lax.broadcasted_iota(jnp.int32, (G, G), 1) // HEAD_DIM).astype(BF16)
    rowt = lax.broadcasted_iota(jnp.int32, (ts, ts), 0)
    colt = lax.broadcasted_iota(jnp.int32, (ts, ts), 1)
    tri_b = (((rowt // C) == (colt // C)) & (colt <= rowt)).astype(BF16)
    row1 = lax.broadcasted_iota(jnp.int32, (ts, 1), 0)

    def stack4(x):
        xb = x.astype(BF16)
        return jnp.concatenate([xb, xb, xb, xb], axis=0) * diag4

    def mul4(xs, ys):
        return [_dot(x.astype(BF16), stack4(y)) for x, y in zip(xs, ys)]

    def prepare(sb):
        base = sb * ts
        u = us_ref[0, base:base + ts, :]
        before = prev_scr[...] if sb == 0 else us_ref[0, base - 1:base, :]
        zprev = jnp.where(row1 == 0, before, pltpu.roll(u, 1, axis=0))
        zs = u + (zprev - u) * mu_ref[...]
        r = zs[:, 0:W_A]
        k = zs[:, W_A:2 * W_A]
        v = zs[:, 2 * W_A:3 * W_A]
        lora_in = zs[:, 3 * W_A:3 * W_A + LANES]
        lora_t = jnp.where(lane1 < D_LORA, jnp.tanh(lora_in), lora_in).astype(BF16)
        yield
        wl = w0_ref[...] + _dot(lora_t, wd_ref[...])
        a = _sigmoid(a0_ref[...] + _dot(lora_t, wa_ref[...]))
        w_log = -_softplus(-wl) - 0.5
        lw = -jnp.exp(w_log)
        yield
        kk = k * kkp_ref[...]
        kk = kk * lax.rsqrt(_seg_sum(kk * kk, bd) + 1e-12)
        k2 = k * (1.0 + (a - 1.0) * ka_ref[...])
        bn_s[sb] = _seg_sum(r * k2 * rk_ref[...], bd) * v
        v_s[sb] = v
        yield
        h1, h2, h3 = _split3(lw)
        cs = _dot(tri_b, h1) + _dot(tri_b, h2) + _dot(tri_b, h3)
        tot = jnp.concatenate([jnp.broadcast_to(cs[c * C + C - 1:c * C + C, :], (C, W_A))
                               for c in range(nb)], axis=0)
        yield
        g_inv = jnp.exp(-cs)
        g_end = jnp.exp(tot - cs)
        kka = kk * a
        at_s[sb] = kk * jnp.exp(cs - lw)
        rt_s[sb] = r * jnp.exp(cs)
        yield
        bh_s[sb] = (-kka * g_inv).astype(BF16)
        kh_s[sb] = (k2 * g_inv).astype(BF16)
        be_s[sb] = (-kka * g_end).astype(BF16)
        ke_s[sb] = (k2 * g_end).astype(BF16)
        ga_s[sb] = jnp.exp(tot)
        yield

    local = {}

    def chunk_terms(sb):
        groups = []
        for c in range(nb):
            groups.append([(c, 0), (c, LANES)])
        for c in range(0, nb, 2):
            groups.append([(c, 2 * LANES)] + ([(c + 1, 2 * LANES)] if c + 1 < nb else [None]))

        def gather(ref, grp):
            parts = [jnp.zeros((C, LANES), ref.dtype) if u is None
                     else ref[sb, u[0] * C:(u[0] + 1) * C, u[1]:u[1] + LANES] for u in grp]
            return jnp.concatenate(parts, axis=1)

        at_g = [gather(at_s, g) for g in groups]
        rt_g = [gather(rt_s, g) for g in groups]
        v_g = [gather(v_s, g) for g in groups]
        lhs = [jnp.concatenate([a, r], axis=0).astype(BF16) for a, r in zip(at_g, rt_g)]
        rhs = [jnp.concatenate([stack4(gather(bh_s, g)), stack4(gather(kh_s, g))], axis=0) for g in groups]
        a4 = [_dot_nt(l, r) for l, r in zip(lhs, rhs)]
        yield
        n1 = [jnp.where(tri_strict, a[:C, :G], 0.0) for a in a4]
        a_ak = [jnp.where(tri_strict, a[:C, G:], 0.0) for a in a4]
        a_rb = [jnp.where(tri_incl, a[C:, :G], 0.0) for a in a4]
        a_rk = [jnp.where(tri_incl, a[C:, G:], 0.0) for a in a4]
        vst = [stack4(v) for v in v_g]
        avy = [_dot(jnp.concatenate([ak, rk], axis=0).astype(BF16), vs)
               for ak, rk, vs in zip(a_ak, a_rk, vst)]
        yield
        pair4 = lambda xs, ys: [eye + x + y + m for x, y, m in zip(xs, ys, mul4(xs, ys))]
        n2 = mul4(n1, n1)
        yield
        n4 = mul4(n2, n2)
        p1 = pair4(n1, n2)
        yield
        n8 = mul4(n4, n4)
        yield
        n16 = mul4(n8, n8)
        p2 = pair4(n4, n8)
        yield
        n32 = mul4(n16, n16)
        p12 = mul4(p1, p2)
        yield
        p3 = pair4(n16, n32)
        yield
        t_inv = mul4(p12, p3)
        yield
        xs = [_dot(t.astype(BF16), jnp.concatenate([stack4(a), stack4(y[:C])], axis=1))
              for t, a, y in zip(t_inv, at_g, avy)]
        yield
        rys = [_dot(rb.astype(BF16), jnp.concatenate([stack4(x[:, :G]), stack4(x[:, G:])], axis=1))
               for rb, x in zip(a_rb, xs)]
        rp_g = [r + ry[:, :G] for r, ry in zip(rt_g, rys)]
        y0_g = [ry[:, G:] + y[C:] for ry, y in zip(rys, avy)]
        yield
        for gi, grp in enumerate(groups):
            for kpos, u in enumerate(grp):
                if u is None:
                    continue
                c, p = u[0], u[1] // LANES
                rows = slice(c * C, (c + 1) * C)
                sl = slice(p * LANES, (p + 1) * LANES)
                gl = slice(kpos * LANES, (kpos + 1) * LANES)
                be_u, ke_u = be_s[sb, rows, sl], ke_s[sb, rows, sl]
                atp = xs[gi][:, gl].astype(BF16)
                wv = jnp.concatenate([xs[gi][:, G + kpos * LANES:G + (kpos + 1) * LANES],
                                      v_s[sb, rows, sl]], axis=0).astype(BF16)
                pp = jnp.where(bd_pair, _dot_tn(atp, be_u), 0.0).astype(BF16)
                q = jnp.where(bd_pair, _dot_tn(wv, jnp.concatenate([be_u, ke_u], axis=0)), 0.0)
                local[sb, c, p] = (rp_g[gi][:, gl].astype(BF16), y0_g[gi][:, gl], pp, q)
        yield

    def sweep(sb):
        for c in range(nb):
            rows = slice(c * C, (c + 1) * C)
            s_ps = [s_scr[p] for p in range(npair)]
            s_bs = [s_p.astype(BF16) for s_p in s_ps]
            for p in range(npair):
                sl = slice(p * LANES, (p + 1) * LANES)
                rp, y0, pp, q = local[sb, c, p]
                y_s[sb, rows, sl] = _dot_nt(rp, s_bs[p]) + y0
                s_scr[p] = s_ps[p] * ga_s[sb, c * C:c * C + 1, sl] + _dot(s_bs[p], pp) + q
            yield
        y = y_s[sb]
        inv_n = 1.0 / HEAD_DIM
        mean = _seg_sum(y, bd) * inv_n
        d = y - mean
        yield
        var = _seg_sum(d * d, bd) * inv_n
        yn = d * lax.rsqrt(var + GN_EPS) * lng_ref[...] + lnb_ref[...]
        oa_ref[0, sb * ts:(sb + 1) * ts, :] = (yn + bn_s[sb]).astype(oa_ref.dtype)
        yield

    _alternate(prepare(0))
    for sb in range(nsub):
        _alternate(chunk_terms(sb),
                   prepare(sb + 1) if sb + 1 < nsub else None,
                   sweep(sb - 1) if sb > 0 else None)
    _alternate(sweep(nsub - 1))
    prev_scr[...] = us_ref[0, tt - 1:tt, :]

    @pl.when(c_idx == nc - 1)
    def _():
        sout_ref[0] = s_scr[...]


def _wkv(us, prev0, s0_bd, prm, tt):
    b, t, _ = us.shape
    npair = H_A // 2
    nsub = max(1, tt // WKV_SUB)
    ts = tt // nsub
    vec = lambda w: pl.BlockSpec((1, w), lambda i, j: (0, 0))
    mat = lambda: pl.BlockSpec((LANES, W_A), lambda i, j: (0, 0))
    blk = lambda dt: pltpu.VMEM((nsub, ts, W_A), dt)
    return pl.pallas_call(
        functools.partial(_wkv_kernel, nsub=nsub),
        grid=(b, t // tt),
        in_specs=[pl.BlockSpec((1, tt, SHIFT_W), lambda i, j: (i, j, 0)),
                  pl.BlockSpec((1, 1, SHIFT_W), lambda i, j: (i, 0, 0)),
                  pl.BlockSpec((1, npair, LANES, LANES), lambda i, j: (i, 0, 0, 0)),
                  vec(SHIFT_W), vec(W_A), mat(), vec(W_A), mat(),
                  vec(W_A), vec(W_A), vec(W_A), vec(W_A), vec(W_A)],
        out_specs=[pl.BlockSpec((1, tt, W_A), lambda i, j: (i, j, 0)),
                   pl.BlockSpec((1, npair, LANES, LANES), lambda i, j: (i, 0, 0, 0))],
        out_shape=[jax.ShapeDtypeStruct((b, t, W_A), BF16),
                   jax.ShapeDtypeStruct((b, npair, LANES, LANES), F32)],
        scratch_shapes=[pltpu.VMEM((npair, LANES, LANES), F32),
                        pltpu.VMEM((1, SHIFT_W), F32),
                        blk(F32), blk(F32), blk(BF16), blk(BF16), blk(BF16), blk(BF16),
                        blk(F32), blk(F32), blk(F32), blk(F32)],
        compiler_params=_cparams(("parallel", "arbitrary")),
    )(us, prev0, s0_bd, prm["mu"], prm["w0"], prm["wd"], prm["a0"], prm["wa"],
      prm["kkp"], prm["ka"], prm["rk"], prm["lnx_g"], prm["lnx_b"])


_AUG = 8
_PREP_ROWS = 256


def _bias_lanes(cum, key_side):
    ntile = H_B // 2 if key_side else H_B
    width = ntile * LANES
    r = lax.broadcasted_iota(jnp.int32, (LANES, width), 0)
    c = lax.broadcasted_iota(jnp.int32, (LANES, width), 1)
    lane = lax.broadcasted_iota(jnp.int32, (1, width), 1)
    f_off, one_off = (0, 3) if key_side else (3, 0)
    sign = -1.0 if key_side else 1.0
    head_of_tile = (c // LANES) * 2 + (c % LANES) // _AUG if key_side else c // LANES
    slot = c % _AUG
    in_head_lanes = (c % LANES) // _AUG == r % 2
    one_head = (lane % LANES) // _AUG < 2 if key_side else (lane % LANES) // _AUG == (lane // LANES) % 2
    out = jnp.where(one_head & (lane % _AUG >= one_off) & (lane % _AUG < one_off + 3), 1.0, 0.0)
    for t, part in enumerate(_split3(cum)):
        sel = jnp.where((head_of_tile == r) & in_head_lanes & (slot == f_off + t), sign, 0.0).astype(BF16)
        out = out + _dot(part, sel)
    return [out[:, i * LANES:(i + 1) * LANES] for i in range(ntile)]


def _stage_rows(total):
    return [(s, min(_PREP_ROWS, total - s)) for s in range(0, total, _PREP_ROWS)]


def _transpose_rows(x):
    n = x.shape[0]
    n_pad = -n % LANES
    if n_pad:
        x = jnp.concatenate([x, jnp.zeros((n_pad, x.shape[1]), x.dtype)], axis=0)
    return x.T[:, :n]


def _fox_kernel(*refs, tq, tk, past, single, new_time_on_lanes):
    if past:
        q_ref, kc_ref, vc_ref, k_ref, v_ref, cq_ref, ck_ref, o_ref, ka_scr, vt_scr, acc_scr = refs
    else:
        q_ref, k_ref, v_ref, cq_ref, ck_ref, o_ref, ka_scr, vt_scr, acc_scr = refs
    npair = H_B // 2
    lane1 = lax.broadcasted_iota(jnp.int32, (1, LANES), 1)
    head_of_lane = lane1 // HEAD_DIM

    def stage_bias(s, n):
        cum = ck_ref[0, s:s + n, :] * LOG2E
        for p, aug in enumerate(_bias_lanes(cum, True)):
            ka_scr[p, s:s + n, LANES:2 * LANES] = aug.astype(BF16)

    def stage_kv():
        for s, n in _stage_rows(past):
            for p in range(npair):
                ka_scr[p, s:s + n, 0:LANES] = kc_ref[0, p * LANES:(p + 1) * LANES, s:s + n].T.astype(BF16)
            vt_scr[:, s:s + n] = vc_ref[0, :, s:s + n].astype(BF16)
            stage_bias(s, n)
        if new_time_on_lanes:
            for s, n in _stage_rows(k_ref.shape[2]):
                for p in range(npair):
                    ka_scr[p, past + s:past + s + n, 0:LANES] = (
                        k_ref[0, p * LANES:(p + 1) * LANES, s:s + n].T.astype(BF16))
                vt_scr[:, past + s:past + s + n] = v_ref[0, :, s:s + n].astype(BF16)
                stage_bias(past + s, n)
            return
        for s, n in _stage_rows(k_ref.shape[1]):
            kb = k_ref[0, s:s + n, :].astype(BF16)
            for p in range(npair):
                ka_scr[p, past + s:past + s + n, 0:LANES] = kb[:, p * LANES:(p + 1) * LANES]
            vt_scr[:, past + s:past + s + n] = _transpose_rows(v_ref[0, s:s + n, :]).astype(BF16)
            stage_bias(past + s, n)

    if single:
        i = 0
        stage_kv()
    else:
        i = pl.program_id(1)
        pl.when(i == 0)(stage_kv)

    q = q_ref[0].astype(F32) * (HEAD_DIM ** -0.5 * LOG2E)
    cq = cq_ref[0] * LOG2E
    rhs = []
    q_bias = _bias_lanes(cq, False)
    for p in range(npair):
        q_p = q[:, p * LANES:(p + 1) * LANES]
        halves = [jnp.concatenate([jnp.where(head_of_lane == hh, q_p, 0.0), q_bias[2 * p + hh]], axis=1)
                  for hh in range(2)]
        rhs.append(jnp.concatenate(halves, axis=0).astype(BF16))

    n_full = past // tk + i * (tq // tk)
    diag0 = past if single else pl.multiple_of(past + i * tq, tq)
    krow = lax.broadcasted_iota(jnp.int32, (tq, 2 * tq), 0)
    qcol = lax.broadcasted_iota(jnp.int32, (tq, 2 * tq), 1)
    causal_t = krow <= jnp.where(qcol >= tq, qcol - tq, qcol)

    acc_scr[...] = jnp.zeros(acc_scr.shape, F32)

    def update(carry, start, size, mask):
        ms, ls = carry
        st = [_dot_nt(ka_scr[p, pl.ds(start, size), :], rhs[p]) for p in range(npair)]
        if mask is not None:
            st = [jnp.where(mask, s, NEG_INF) for s in st]
        m_new = [jnp.maximum(m, jnp.max(s, axis=0, keepdims=True)) for m, s in zip(ms, st)]
        alpha = [jnp.exp2(m - mn) for m, mn in zip(ms, m_new)]
        pt = [jnp.exp2(s - mn) for s, mn in zip(st, m_new)]
        l_new = [a * l + jnp.sum(x, axis=0, keepdims=True) for a, l, x in zip(alpha, ls, pt)]
        pv = [_dot(vt_scr[p * LANES:(p + 1) * LANES, pl.ds(start, size)], pt[p].astype(BF16))
              for p in range(npair)]
        for p in range(npair):
            acc_scr[p] = acc_scr[p] * alpha[p] + pv[p]
        return tuple(m_new), tuple(l_new)

    def body(j, carry):
        return update(carry, pl.multiple_of(j * tk, tk), tk, None)

    init = (tuple(jnp.full((1, 2 * tq), NEG_INF, F32) for _ in range(npair)),
            tuple(jnp.zeros((1, 2 * tq), F32) for _ in range(npair)))
    carry = lax.fori_loop(0, n_full, body, init)
    _, ls = update(carry, diag0, tq, causal_t)
    for p in range(npair):
        o_t = (acc_scr[p] / ls[p]).T
        o_ref[0, :, p * LANES:(p + 1) * LANES] = jnp.where(
            head_of_lane == 0, o_t[:tq], o_t[tq:]).astype(o_ref.dtype)


def _kv_specs(k_new, cache, layer, width, t_new, new_time_on_lanes=False, new_layer=None):
    specs, args = [], []
    new_layer = layer if new_layer is None else new_layer
    if cache is not None:
        past = cache[0].shape[3]
        for c in cache:
            specs.append(pl.BlockSpec((None, 1, width, past), lambda i, j: (layer, i, 0, 0)))
            args.append(c)
    for a in k_new:
        if new_time_on_lanes:
            specs.append(pl.BlockSpec((None, 1, width, t_new), lambda i, j: (new_layer, i, 0, 0)))
        else:
            specs.append(pl.BlockSpec((None, 1, t_new, width), lambda i, j: (new_layer, i, 0, 0)))
        args.append(a)
    return specs, args


def _fox(q, kv_new, cache, cum_q, cum_k, *, layer, tq, tk, past, new_time_on_lanes):
    b, t, _ = q.shape
    tkk = past + t
    npair = H_B // 2
    kv_specs, kv_args = _kv_specs(kv_new, cache, layer, W_B, t, new_time_on_lanes)
    return pl.pallas_call(
        functools.partial(_fox_kernel, tq=tq, tk=tk, past=past, single=(t == tq),
                          new_time_on_lanes=new_time_on_lanes),
        grid=(b, t // tq),
        in_specs=[pl.BlockSpec((1, tq, W_B), lambda i, j: (i, j, 0))] + kv_specs + [
                  pl.BlockSpec((1, tq, LANES), lambda i, j: (i, j, 0)),
                  pl.BlockSpec((1, tkk, LANES), lambda i, j: (i, 0, 0))],
        out_specs=pl.BlockSpec((1, tq, W_B), lambda i, j: (i, j, 0)),
        out_shape=jax.ShapeDtypeStruct((b, t, W_B), BF16),
        scratch_shapes=[pltpu.VMEM((npair, tkk, 2 * LANES), BF16), pltpu.VMEM((W_B, tkk), BF16),
                        pltpu.VMEM((npair, LANES, 2 * tq), F32)],
        compiler_params=_cparams(("parallel", "arbitrary")),
    )(q, *kv_args, cum_q, cum_k)


def _sb_kernel(*refs, tq, tk, past, single):
    if past:
        q_ref, kc_ref, vc_ref, k_ref, v_ref, o_ref, kb_scr, vt_scr, acc_scr = refs
    else:
        q_ref, k_ref, v_ref, o_ref, kb_scr, vt_scr, acc_scr = refs
    npair = H_C // 2
    lane1 = lax.broadcasted_iota(jnp.int32, (1, LANES), 1)
    head_of_lane = lane1 // HEAD_DIM

    def stage_kv():
        for s, n in _stage_rows(past):
            kb_scr[s:s + n, :] = kc_ref[0, :, s:s + n].T.astype(BF16)
            vt_scr[:, s:s + n] = vc_ref[0, :, s:s + n].astype(BF16)
        for s, n in _stage_rows(k_ref.shape[1]):
            kb_scr[past + s:past + s + n, :] = k_ref[0, s:s + n, :].astype(BF16)
            vt_scr[:, past + s:past + s + n] = _transpose_rows(v_ref[0, s:s + n, :]).astype(BF16)

    if single:
        i = 0
        stage_kv()
    else:
        i = pl.program_id(1)
        pl.when(i == 0)(stage_kv)

    q = q_ref[0].astype(F32) * (HEAD_DIM ** -0.5 * LOG2E)
    rhs = []
    for p in range(npair):
        q_p = q[:, p * LANES:(p + 1) * LANES]
        rhs.append(jnp.concatenate([jnp.where(head_of_lane == hh, q_p, 0.0) for hh in range(2)],
                                   axis=0).astype(BF16))

    def earlier_matrix(n):
        rr = lax.broadcasted_iota(jnp.int32, (n, n), 0)
        cc = lax.broadcasted_iota(jnp.int32, (n, n), 1)
        return (cc > rr).astype(BF16)

    n_full = past // tk + i * (tq // tk)
    diag0 = past if single else pl.multiple_of(past + i * tq, tq)
    krow = lax.broadcasted_iota(jnp.int32, (tq, 2 * tq), 0)
    qcol = lax.broadcasted_iota(jnp.int32, (tq, 2 * tq), 1)
    strict_t = krow < jnp.where(qcol >= tq, qcol - tq, qcol)

    def update(runs, start, size, mask, first):
        zt = [_dot_nt(kb_scr[pl.ds(start, size), p * LANES:(p + 1) * LANES], rhs[p])
              for p in range(npair)]
        mz = [jnp.minimum(z, 0.0) for z in zt]
        tail = [jnp.log(1.0 + jnp.exp2(m + m - z)) * LOG2E for m, z in zip(mz, zt)]
        lsig = [m - t for m, t in zip(mz, tail)]
        l1m = [s - z for s, z in zip(lsig, zt)]
        if mask is not None:
            l1m = [jnp.where(mask, x, 0.0) for x in l1m]
        later = earlier_matrix(size)
        after = [_dot(later, x.astype(BF16)) for x in l1m]
        wt = [jnp.exp2(s + a + r) for s, a, r in zip(lsig, after, runs)]
        if mask is not None:
            wt = [jnp.where(mask, w, 0.0) for w in wt]
        pv = [_dot(vt_scr[p * LANES:(p + 1) * LANES, pl.ds(start, size)], wt[p].astype(BF16))
              for p in range(npair)]
        for p in range(npair):
            acc_scr[p] = pv[p] if first else acc_scr[p] + pv[p]
        return tuple(r + jnp.sum(x, axis=0, keepdims=True) for r, x in zip(runs, l1m))

    runs = tuple(jnp.zeros((1, 2 * tq), F32) for _ in range(npair))
    runs = update(runs, diag0, tq, strict_t, True)

    def body(jj, runs):
        j = n_full - 1 - jj
        return update(runs, pl.multiple_of(j * tk, tk), tk, None, False)

    lax.fori_loop(0, n_full, body, runs)
    for p in range(npair):
        o_t = acc_scr[p].T
        o_ref[0, :, p * LANES:(p + 1) * LANES] = jnp.where(
            head_of_lane == 0, o_t[:tq], o_t[tq:]).astype(o_ref.dtype)


def _sb(q, kv_new, cache, *, layer, tq, tk, past):
    b, t, _ = q.shape
    tkk = past + kv_new[0].shape[2]
    npair = H_C // 2
    kv_specs, kv_args = _kv_specs(kv_new, cache, layer, W_C, t, new_layer=0)
    return pl.pallas_call(
        functools.partial(_sb_kernel, tq=tq, tk=tk, past=past, single=(t == tq)),
        grid=(b, t // tq),
        in_specs=[pl.BlockSpec((1, tq, W_C), lambda i, j: (i, j, 0))] + kv_specs,
        out_specs=pl.BlockSpec((1, tq, W_C), lambda i, j: (i, j, 0)),
        out_shape=jax.ShapeDtypeStruct((b, t, W_C), BF16),
        scratch_shapes=[pltpu.VMEM((tkk, W_C), BF16), pltpu.VMEM((W_C, tkk), BF16),
                        pltpu.VMEM((npair, LANES, 2 * tq), F32)],
        compiler_params=_cparams(("parallel", "arbitrary")),
    )(q, *kv_args)


def _out_kernel(oa_ref, ob_ref, oc_ref, g_ref, x_ref, w_ref, lng_ref, lnb_ref, y_ref, *, alpha):
    g = g_ref[...].astype(F32)
    gate = g * _sigmoid(g)
    gated = lambda o_ref, lo, hi: (o_ref[...].astype(F32) * gate[:, lo:hi]).astype(BF16)
    acc = _dot(gated(oa_ref, 0, W_A), w_ref[0:W_A, :])
    acc = acc + _dot(gated(ob_ref, W_A, W_A + W_B), w_ref[W_A:W_A + W_B, :])
    acc = acc + _dot(gated(oc_ref, W_A + W_B, W_MIX), w_ref[W_A + W_B:, :])
    z = alpha * x_ref[...] + acc
    mu = jnp.mean(z, axis=-1, keepdims=True)
    d = z - mu
    var = jnp.mean(d * d, axis=-1, keepdims=True)
    y_ref[...] = d * lax.rsqrt(var + LN_EPS) * lng_ref[...] + lnb_ref[...]


def _out(oa, ob, oc, g, x2d, w_out, ln_g, ln_b, alpha, tm):
    n, d = x2d.shape
    row = lambda w: pl.BlockSpec((tm, w), lambda i: (i, 0))
    return pl.pallas_call(
        functools.partial(_out_kernel, alpha=alpha),
        grid=(n // tm,),
        in_specs=[row(W_A), row(W_B), row(W_C), row(W_MIX), row(d),
                  pl.BlockSpec((W_MIX, d), lambda i: (0, 0)),
                  pl.BlockSpec((1, d), lambda i: (0, 0)),
                  pl.BlockSpec((1, d), lambda i: (0, 0))],
        out_specs=row(d),
        out_shape=jax.ShapeDtypeStruct((n, d), F32),
        compiler_params=_cparams(("parallel",)),
    )(oa, ob, oc, g, x2d, w_out, ln_g, ln_b)


def _pad_lanes(a, width=LANES):
    return jnp.pad(a, [(0, 0)] * (a.ndim - 1) + [(0, width - a.shape[-1])])


def _state_to_pairs(s):
    b = s.shape[0]
    odd_head = (jnp.arange(H_A) % 2 == 1).reshape(1, H_A, 1, 1)
    upper_half = (jnp.arange(LANES) >= HEAD_DIM).reshape(1, 1, 1, LANES)
    wide = jnp.where(odd_head == upper_half, jnp.concatenate([s, s], axis=-1), 0.0)
    return wide.reshape(b, H_A // 2, LANES, LANES)


def _pairs_to_state(sp):
    b = sp.shape[0]
    wide = sp.reshape(b, H_A, HEAD_DIM, LANES)
    odd_head = (jnp.arange(H_A) % 2 == 1).reshape(1, H_A, 1, 1)
    return jnp.where(odd_head, wide[..., HEAD_DIM:], wide[..., :HEAD_DIM])


_IN_SIZES = (SHIFT_W, W_A, W_B, W_B, W_B, H_B, W_B, W_C, W_C, W_C, W_C)
_IN_OFFS = tuple(int(v) for v in np.concatenate([[0], np.cumsum(_IN_SIZES)]))
_SRC_SHIFT, _SRC_GA, _SRC_QB, _SRC_KB, _SRC_VB, _SRC_F, _SRC_GB, _SRC_QC, _SRC_KC, _SRC_VC, _SRC_GC = (
    _IN_OFFS[:-1])
_W_MOVES = ((_SRC_SHIFT, _OFF_SHIFT, SHIFT_W), (_SRC_GA, _OFF_G, W_A), (_SRC_GB, _OFF_G + W_A, W_B),
            (_SRC_GC, _OFF_G + W_A + W_B, W_C), (_SRC_QB, _OFF_QB, W_B), (_SRC_KB, _OFF_KB, W_B),
            (_SRC_VB, _OFF_VB, W_B), (_SRC_QC, _OFF_QC, W_C), (_SRC_KC, _OFF_KC, W_C),
            (_SRC_VC, _OFF_VC, W_C))
_W_STAGE_ROWS = 128


def _stage_w_kernel(w_ref, o_ref):
    for src, dst, width in _W_MOVES:
        o_ref[:, dst:dst + width] = w_ref[:, src:src + width].astype(BF16)
    lane = lax.broadcasted_iota(jnp.int32, (1, LANES), 1)
    o_ref[:, _OFF_F:_OFF_F + LANES] = jnp.where(
        lane < H_B, w_ref[:, _SRC_F:_SRC_F + LANES], 0.0).astype(BF16)


def _stage_w(w_in, layer):
    _, d, cols = w_in.shape
    return pl.pallas_call(
        _stage_w_kernel,
        grid=(d // _W_STAGE_ROWS,),
        in_specs=[pl.BlockSpec((None, _W_STAGE_ROWS, cols), lambda i: (layer, i, 0))],
        out_specs=pl.BlockSpec((_W_STAGE_ROWS, _PROJ_COLS), lambda i: (i, 0)),
        out_shape=jax.ShapeDtypeStruct((d, _PROJ_COLS), BF16),
        compiler_params=_cparams(("parallel",)),
    )(w_in)


def _layer_params(l, w_in, mu_shift, w0_decay, w_decay, a0, w_aaa, k_k, k_a, r_k,
                  lnx_g, lnx_b, fox_fb, w_out, ln_g, ln_b):
    w_cat = _stage_w(w_in, l)
    zeros = jnp.zeros((D_LORA, W_A), F32)
    row = lambda a: a.reshape(1, -1).astype(F32)
    return dict(
        w_cat=w_cat, fb=_pad_lanes(row(fox_fb[l])),
        mu=row(mu_shift[l]), w0=row(w0_decay[l]),
        wd=jnp.concatenate([w_decay[l], zeros], axis=0).astype(BF16),
        a0=row(a0[l]), wa=jnp.concatenate([zeros, w_aaa[l]], axis=0).astype(BF16),
        kkp=row(k_k[l]), ka=row(k_a[l]), rk=row(r_k[l]), lnx_g=row(lnx_g[l]), lnx_b=row(lnx_b[l]),
        w_out=w_out[l].astype(BF16), ln_g=row(ln_g[l]), ln_b=row(ln_b[l]))


def _run_layer(x, hist, prm, alpha, layer, depth, stacks):
    b, t, d = x.shape
    n = b * t
    x2d = x.reshape(n, d)
    tm = min(ROW_BLOCK, n)
    fox_t = hist is None
    us, g, qb, kb_st, vb_st, qc, kc_st, vc_st, lf, kc_rows, vc_rows = _proj(
        x2d, prm["w_cat"], prm["fb"], tm, layer, depth, stacks, t, fox_t)
    r3 = lambda a: a.reshape(b, t, a.shape[-1])
    us, qb, qc, lf = (r3(a) for a in (us, qb, qc, lf))
    r4 = lambda a: a.reshape(depth, b, t, a.shape[-1])

    if hist is None:
        past = 0
        prev0 = jnp.zeros((b, 1, SHIFT_W), F32)
        s0 = jnp.zeros((b, H_A // 2, LANES, LANES), F32)
        cache_b = cache_c = None
        lf_all = lf
        fox_blk = (min(FOX_BLOCK, t),) * 2
        sb_blk = (min(SB_BLOCK, t),) * 2
    else:
        fk_t, fv_t, h_lf, sk_t, sv_t, h_wkv, h_shift = hist
        past = fk_t.shape[3]
        prev0 = h_shift
        s0 = _state_to_pairs(h_wkv)
        cache_b, cache_c = (fk_t, fv_t), (sk_t, sv_t)
        lf_all = jnp.concatenate([_pad_lanes(h_lf), lf], axis=1)
        fox_blk = sb_blk = (t, min(SB_BLOCK, past))

    oa, s_fin = _wkv(us, prev0, s0, prm, min(WKV_BLOCK, t))
    cum = _cumsum(lf_all)
    fox_kv = (kb_st, vb_st) if fox_t else (r4(kb_st), r4(vb_st))
    ob = _fox(qb, fox_kv, cache_b, cum[:, past:], cum, layer=layer,
              tq=fox_blk[0], tk=fox_blk[1], past=past, new_time_on_lanes=fox_t)
    sb_kv = (kc_rows.reshape(1, b, t, W_C), vc_rows.reshape(1, b, t, W_C))
    oc = _sb(qc, sb_kv, cache_c, layer=layer, tq=sb_blk[0], tk=sb_blk[1], past=past)
    y = _out(oa.reshape(n, W_A), ob.reshape(n, W_B), oc.reshape(n, W_C), g, x2d,
             prm["w_out"], prm["ln_g"], prm["ln_b"], alpha, tm)
    small = (lf[:, :, :H_B], _pairs_to_state(s_fin), us[:, -1:, :])
    return y.reshape(b, t, d), (kb_st, vb_st, kc_st, vc_st), small


def kernel(x_prompt, x_sample, cache_fox_k, cache_fox_v, cache_fox_logf, cache_sb_k, cache_sb_v, state_wkv, state_shift, w_in, mu_shift, w0_decay, w_decay, a0, w_aaa, k_k, k_a, r_k, lnx_g, lnx_b, fox_fb, w_out, ln_g, ln_b):
    depth = w_in.shape[0]
    alpha = (2 * depth) ** 0.25
    yp, ys = x_prompt, x_sample

    def time_on_lanes(cache):
        nl, nb, past, nh, hd = cache.shape
        return jnp.transpose(cache, (0, 1, 3, 4, 2)).reshape(nl, nb, nh * hd, past)

    fk_t, fv_t, sk_t, sv_t = (time_on_lanes(c) for c in (cache_fox_k, cache_fox_v, cache_sb_k, cache_sb_v))
    stacks_p, stacks_s = None, None
    small_p, small_s = [], []
    for l in range(depth):
        prm = _layer_params(l, w_in, mu_shift, w0_decay, w_decay, a0, w_aaa, k_k, k_a, r_k,
                            lnx_g, lnx_b, fox_fb, w_out, ln_g, ln_b)
        yp, stacks_p, sm_p = _run_layer(yp, None, prm, alpha, l, depth, stacks_p)
        hist = (fk_t, fv_t, cache_fox_logf[l], sk_t, sv_t, state_wkv[l], state_shift[l])
        ys, stacks_s, sm_s = _run_layer(ys, hist, prm, alpha, l, depth, stacks_s)
        small_p.append(sm_p)
        small_s.append(sm_s)

    def group_outputs(x, stacks, small, fox_t):
        b, t, _ = x.shape
        kb, vb, kc, vc = stacks
        logf, wkv, shift = (jnp.stack([sm[i] for sm in small]) for i in range(3))
        if fox_t:
            kb, vb = (jnp.transpose(a.reshape(depth, b, H_B, HEAD_DIM, t), (0, 1, 4, 2, 3)) for a in (kb, vb))
        return (kb.reshape(depth, b, t, H_B, HEAD_DIM), vb.reshape(depth, b, t, H_B, HEAD_DIM), logf,
                kc.reshape(depth, b, t, H_C, HEAD_DIM), vc.reshape(depth, b, t, H_C, HEAD_DIM),
                wkv, shift)

    return ((yp, ys) + group_outputs(x_prompt, stacks_p, small_p, True)
            + group_outputs(x_sample, stacks_s, small_s, False))
```

```python
import functools

import numpy as np
import jax
import jax.numpy as jnp
from jax import lax
from jax.experimental import pallas as pl
from jax.experimental.pallas import tpu as pltpu

F32 = jnp.float32
BF16 = jnp.bfloat16

HEAD_DIM = 64
H_A, H_B, H_C = 6, 6, 4
W_A, W_B, W_C = H_A * HEAD_DIM, H_B * HEAD_DIM, H_C * HEAD_DIM
W_MIX = W_A + W_B + W_C
D_LORA = 64
SHIFT_W = 3 * W_A + 2 * D_LORA
LANES = 128
WKV_CHUNK = 64
WKV_BLOCK = 1024
WKV_SUB = 256
ROW_BLOCK = 512
OUT_BLOCK = 1024
FOX_BLOCK = 512
SB_BLOCK = 512
GN_EPS = 64e-5
LN_EPS = 1e-5
NEG_INF = -1e30
LOG2E = 1.4426950408889634
VMEM_LIMIT = 56 * 1024 * 1024


def _cparams(sem):
    return pltpu.CompilerParams(dimension_semantics=sem, vmem_limit_bytes=VMEM_LIMIT)


def _dot(a, b):
    return jnp.dot(a, b, preferred_element_type=F32)


def _dot_nt(a, b):
    return lax.dot_general(a, b, (((1,), (1,)), ((), ())), preferred_element_type=F32)


def _dot_tn(a, b):
    return lax.dot_general(a, b, (((0,), (0,)), ((), ())), preferred_element_type=F32)


def _split3(x):
    hi = x.astype(BF16)
    r1 = x - hi.astype(F32)
    mid = r1.astype(BF16)
    lo = (r1 - mid.astype(F32)).astype(BF16)
    return hi, mid, lo


def _softplus(x):
    return jnp.maximum(x, 0.0) + jnp.log1p(jnp.exp(-jnp.abs(x)))


def _sigmoid(x):
    return 1.0 / (1.0 + jnp.exp(-x))


_OFF_SHIFT = 0
_OFF_G = _OFF_SHIFT + SHIFT_W
_OFF_QB = _OFF_G + W_MIX
_OFF_KB = _OFF_QB + W_B
_OFF_VB = _OFF_KB + W_B
_OFF_QC = _OFF_VB + W_B
_OFF_KC = _OFF_QC + W_C
_OFF_VC = _OFF_KC + W_C
_OFF_F = _OFF_VC + W_C
_PROJ_COLS = _OFF_F + LANES


def _proj_kernel(x_ref, w_ref, fb_ref, *refs):
    (us_ref, g_ref, qb_ref, kb_ref, vb_ref, qc_ref, kc_ref, vc_ref, lf_ref,
     kc_rows_ref, vc_rows_ref) = refs[-11:]
    xb = x_ref[...].astype(BF16)
    tm = xb.shape[0]
    for ref, off in ((us_ref, _OFF_SHIFT), (g_ref, _OFF_G), (qb_ref, _OFF_QB),
                     (kb_ref, _OFF_KB), (vb_ref, _OFF_VB), (qc_ref, _OFF_QC)):
        time_on_lanes = ref.shape[-1] == tm and ref.shape[0] != tm
        width = ref.shape[0] if time_on_lanes else ref.shape[-1]
        res = _dot(xb, w_ref[:, off:off + width])
        ref[...] = (res.T if time_on_lanes else res).astype(ref.dtype)
    for ref, rows_ref, off in ((kc_ref, kc_rows_ref, _OFF_KC), (vc_ref, vc_rows_ref, _OFF_VC)):
        res = _dot(xb, w_ref[:, off:off + W_C])
        rows_ref[...] = res.astype(rows_ref.dtype)
        for h in range(H_C):
            ref[pl.ds(h, tm, stride=H_C), :] = res[:, h * HEAD_DIM:(h + 1) * HEAD_DIM]
    f = _dot(xb, w_ref[:, _OFF_F:_OFF_F + LANES]) + fb_ref[...]
    lf_ref[...] = -_softplus(-f)


_PROJ_WIDTHS = (SHIFT_W, W_MIX, W_B, W_B, W_B, W_C, W_C, W_C, LANES, W_C, W_C)
_PROJ_BF16 = (1, 2, 5, 9, 10)
_PROJ_SB_KV = (6, 7)
_PROJ_STACKED = (3, 4, 6, 7)
_PROJ_FOX_KV = (3, 4)


def _proj(x2d, w_cat, fb_pad, tm, layer, depth, stacks, seq_len, fox_time_on_lanes):
    n, d = x2d.shape
    assert tm != W_B
    per_row = seq_len // tm
    out_specs, out_shape = [], []
    for idx, w in enumerate(_PROJ_WIDTHS):
        if idx in _PROJ_FOX_KV and fox_time_on_lanes:
            out_specs.append(pl.BlockSpec((None, None, w, tm),
                                          lambda i: (layer, i // per_row, 0, i % per_row)))
            out_shape.append(jax.ShapeDtypeStruct((depth, n // seq_len, w, seq_len), F32))
        elif idx in _PROJ_SB_KV:
            out_specs.append(pl.BlockSpec((None, tm * H_C, HEAD_DIM), lambda i: (layer, i, 0)))
            out_shape.append(jax.ShapeDtypeStruct((depth, n * H_C, HEAD_DIM), F32))
        elif idx in _PROJ_STACKED:
            out_specs.append(pl.BlockSpec((None, tm, w), lambda i: (layer, i, 0)))
            out_shape.append(jax.ShapeDtypeStruct((depth, n, w), F32))
        else:
            out_specs.append(pl.BlockSpec((tm, w), lambda i: (i, 0)))
            out_shape.append(jax.ShapeDtypeStruct((n, w), BF16 if idx in _PROJ_BF16 else F32))
    in_specs = [pl.BlockSpec((tm, d), lambda i: (i, 0)),
                pl.BlockSpec((d, _PROJ_COLS), lambda i: (0, 0)),
                pl.BlockSpec((1, LANES), lambda i: (0, 0))]
    args = [x2d, w_cat, fb_pad]
    aliases = {}
    if stacks is not None:
        for j, (idx, st) in enumerate(zip(_PROJ_STACKED, stacks)):
            in_specs.append(pl.BlockSpec(memory_space=pl.ANY))
            args.append(st)
            aliases[3 + j] = idx
    return pl.pallas_call(
        _proj_kernel,
        grid=(n // tm,),
        in_specs=in_specs,
        out_specs=out_specs,
        out_shape=out_shape,
        input_output_aliases=aliases,
        compiler_params=_cparams(("parallel",)),
    )(*args)


def _cumsum_kernel(lf_ref, cum_ref, *, blk):
    tk = lf_ref.shape[1]
    row = lax.broadcasted_iota(jnp.int32, (blk, blk), 0)
    col = lax.broadcasted_iota(jnp.int32, (blk, blk), 1)
    tri = (col <= row).astype(BF16)
    carry = jnp.zeros((1, LANES), F32)
    for s in range(0, tk, blk):
        x = lf_ref[0, s:s + blk, :]
        hi, mid, lo = _split3(x)
        c = _dot(tri, hi) + _dot(tri, mid) + _dot(tri, lo) + carry
        cum_ref[0, s:s + blk, :] = c
        carry = c[blk - 1:blk, :]


def _cumsum(lf, blk=64):
    b, tk, _ = lf.shape
    return pl.pallas_call(
        functools.partial(_cumsum_kernel, blk=blk),
        grid=(b,),
        in_specs=[pl.BlockSpec((1, tk, LANES), lambda i: (i, 0, 0))],
        out_specs=pl.BlockSpec((1, tk, LANES), lambda i: (i, 0, 0)),
        out_shape=jax.ShapeDtypeStruct((b, tk, LANES), F32),
        compiler_params=_cparams(("parallel",)),
    )(lf)


def _seg_sum(x, bd):
    xb = x.astype(BF16)
    return jnp.concatenate([_dot(xb[:, s:s + LANES], bd) for s in range(0, x.shape[1], LANES)], axis=1)


def _alternate(*stages):
    gens = [g for g in stages if g is not None]
    while gens:
        for g in list(gens):
            try:
                next(g)
            except StopIteration:
                gens.remove(g)


def _wkv_kernel(us_ref, prev0_ref, s0_ref, mu_ref, w0_ref, wd_ref, a0_ref, wa_ref,
                kkp_ref, ka_ref, rk_ref, lng_ref, lnb_ref,
                oa_ref, sout_ref,
                s_scr, prev_scr, at_s, rt_s, bh_s, kh_s, be_s, ke_s, v_s, ga_s, bn_s, y_s, *, nsub):
    c_idx = pl.program_id(1)
    nc = pl.num_programs(1)
    C = WKV_CHUNK
    tt = us_ref.shape[1]
    ts = tt // nsub
    nb = ts // C
    npair = H_A // 2
    G = 4 * HEAD_DIM

    @pl.when(c_idx == 0)
    def _():
        s_scr[...] = s0_ref[0]
        prev_scr[...] = prev0_ref[0]

    lane1 = lax.broadcasted_iota(jnp.int32, (1, LANES), 1)
    lane_w = lax.broadcasted_iota(jnp.int32, (LANES, LANES), 0) // HEAD_DIM
    lane_c = lax.broadcasted_iota(jnp.int32, (LANES, LANES), 1) // HEAD_DIM
    bd_pair = lane_w == lane_c
    bd = bd_pair.astype(BF16)
    row_g = lax.broadcasted_iota(jnp.int32, (C, G), 0)
    lane_g = lax.broadcasted_iota(jnp.int32, (C, G), 1) % C
    tri_incl = lane_g <= row_g
    tri_strict = lane_g < row_g
    eye = (lane_g == row_g).astype(F32)
    diag4 = (lax.broadcasted_iota(jnp.int32, (G, G), 0) // HEAD_DIM
             == lax.broadcasted_iota(jnp.int32, (G, G), 1) // HEAD_DIM).astype(BF16)
    rowt = lax.broadcasted_iota(jnp.int32, (ts, ts), 0)
    colt = lax.broadcasted_iota(jnp.int32, (ts, ts), 1)
    tri_b = (((rowt // C) == (colt // C)) & (colt <= rowt)).astype(BF16)
    row1 = lax.broadcasted_iota(jnp.int32, (ts, 1), 0)

    def stack4(x):
        xb = x.astype(BF16)
        return jnp.concatenate([xb, xb, xb, xb], axis=0) * diag4

    def mul4(xs, ys):
        return [_dot(x.astype(BF16), stack4(y)) for x, y in zip(xs, ys)]

    def prepare(sb):
        base = sb * ts
        u = us_ref[0, base:base + ts, :]
        before = prev_scr[...] if sb == 0 else us_ref[0, base - 1:base, :]
        zprev = jnp.where(row1 == 0, before, pltpu.roll(u, 1, axis=0))
        zs = u + (zprev - u) * mu_ref[...]
        r = zs[:, 0:W_A]
        k = zs[:, W_A:2 * W_A]
        v = zs[:, 2 * W_A:3 * W_A]
        lora_in = zs[:, 3 * W_A:3 * W_A + LANES]
        lora_t = jnp.where(lane1 < D_LORA, jnp.tanh(lora_in), lora_in).astype(BF16)
        yield
        wl = w0_ref[...] + _dot(lora_t, wd_ref[...])
        a = _sigmoid(a0_ref[...] + _dot(lora_t, wa_ref[...]))
        w_log = -_softplus(-wl) - 0.5
        lw = -jnp.exp(w_log)
        yield
        kk = k * kkp_ref[...]
        kk = kk * lax.rsqrt(_seg_sum(kk * kk, bd) + 1e-12)
        k2 = k * (1.0 + (a - 1.0) * ka_ref[...])
        bn_s[sb] = _seg_sum(r * k2 * rk_ref[...], bd) * v
        v_s[sb] = v
        yield
        h1, h2, h3 = _split3(lw)
        cs = _dot(tri_b, h1) + _dot(tri_b, h2) + _dot(tri_b, h3)
        tot = jnp.concatenate([jnp.broadcast_to(cs[c * C + C - 1:c * C + C, :], (C, W_A))
                               for c in range(nb)], axis=0)
        yield
        g_inv = jnp.exp(-cs)
        g_end = jnp.exp(tot - cs)
        kka = kk * a
        at_s[sb] = kk * jnp.exp(cs - lw)
        rt_s[sb] = r * jnp.exp(cs)
        yield
        bh_s[sb] = (-kka * g_inv).astype(BF16)
        kh_s[sb] = (k2 * g_inv).astype(BF16)
        be_s[sb] = (-kka * g_end).astype(BF16)
        ke_s[sb] = (k2 * g_end).astype(BF16)
        ga_s[sb] = jnp.exp(tot)
        yield

    local = {}

    def chunk_terms(sb):
        groups = []
        for c in range(nb):
            groups.append([(c, 0), (c, LANES)])
        for c in range(0, nb, 2):
            groups.append([(c, 2 * LANES)] + ([(c + 1, 2 * LANES)] if c + 1 < nb else [None]))

        def gather(ref, grp):
            parts = [jnp.zeros((C, LANES), ref.dtype) if u is None
                     else ref[sb, u[0] * C:(u[0] + 1) * C, u[1]:u[1] + LANES] for u in grp]
            return jnp.concatenate(parts, axis=1)

        at_g = [gather(at_s, g) for g in groups]
        rt_g = [gather(rt_s, g) for g in groups]
        v_g = [gather(v_s, g) for g in groups]
        lhs = [jnp.concatenate([a, r], axis=0).astype(BF16) for a, r in zip(at_g, rt_g)]
        rhs = [jnp.concatenate([stack4(gather(bh_s, g)), stack4(gather(kh_s, g))], axis=0) for g in groups]
        a4 = [_dot_nt(l, r) for l, r in zip(lhs, rhs)]
        yield
        n1 = [jnp.where(tri_strict, a[:C, :G], 0.0) for a in a4]
        a_ak = [jnp.where(tri_strict, a[:C, G:], 0.0) for a in a4]
        a_rb = [jnp.where(tri_incl, a[C:, :G], 0.0) for a in a4]
        a_rk = [jnp.where(tri_incl, a[C:, G:], 0.0) for a in a4]
        vst = [stack4(v) for v in v_g]
        avy = [_dot(jnp.concatenate([ak, rk], axis=0).astype(BF16), vs)
               for ak, rk, vs in zip(a_ak, a_rk, vst)]
        yield
        pair4 = lambda xs, ys: [eye + x + y + m for x, y, m in zip(xs, ys, mul4(xs, ys))]
        n2 = mul4(n1, n1)
        yield
        n4 = mul4(n2, n2)
        p1 = pair4(n1, n2)
        yield
        n8 = mul4(n4, n4)
        yield
        n16 = mul4(n8, n8)
        p2 = pair4(n4, n8)
        yield
        n32 = mul4(n16, n16)
        p12 = mul4(p1, p2)
        yield
        p3 = pair4(n16, n32)
        yield
        t_inv = mul4(p12, p3)
        yield
        xs = [_dot(t.astype(BF16), jnp.concatenate([stack4(a), stack4(y[:C])], axis=1))
              for t, a, y in zip(t_inv, at_g, avy)]
        yield
        rys = [_dot(rb.astype(BF16), jnp.concatenate([stack4(x[:, :G]), stack4(x[:, G:])], axis=1))
               for rb, x in zip(a_rb, xs)]
        rp_g = [r + ry[:, :G] for r, ry in zip(rt_g, rys)]
        y0_g = [ry[:, G:] + y[C:] for ry, y in zip(rys, avy)]
        yield
        for gi, grp in enumerate(groups):
            for kpos, u in enumerate(grp):
                if u is None:
                    continue
                c, p = u[0], u[1] // LANES
                rows = slice(c * C, (c + 1) * C)
                sl = slice(p * LANES, (p + 1) * LANES)
                gl = slice(kpos * LANES, (kpos + 1) * LANES)
                be_u, ke_u = be_s[sb, rows, sl], ke_s[sb, rows, sl]
                atp = xs[gi][:, gl].astype(BF16)
                wv = jnp.concatenate([xs[gi][:, G + kpos * LANES:G + (kpos + 1) * LANES],
                                      v_s[sb, rows, sl]], axis=0).astype(BF16)
                pp = jnp.where(bd_pair, _dot_tn(atp, be_u), 0.0).astype(BF16)
                q = jnp.where(bd_pair, _dot_tn(wv, jnp.concatenate([be_u, ke_u], axis=0)), 0.0)
                local[sb, c, p] = (rp_g[gi][:, gl].astype(BF16), y0_g[gi][:, gl], pp, q)
        yield

    def sweep(sb):
        for c in range(nb):
            rows = slice(c * C, (c + 1) * C)
            s_ps = [s_scr[p] for p in range(npair)]
            s_bs = [s_p.astype(BF16) for s_p in s_ps]
            for p in range(npair):
                sl = slice(p * LANES, (p + 1) * LANES)
                rp, y0, pp, q = local[sb, c, p]
                y_s[sb, rows, sl] = _dot_nt(rp, s_bs[p]) + y0
                s_scr[p] = s_ps[p] * ga_s[sb, c * C:c * C + 1, sl] + _dot(s_bs[p], pp) + q
            yield
        y = y_s[sb]
        inv_n = 1.0 / HEAD_DIM
        mean = _seg_sum(y, bd) * inv_n
        d = y - mean
        yield
        var = _seg_sum(d * d, bd) * inv_n
        yn = d * lax.rsqrt(var + GN_EPS) * lng_ref[...] + lnb_ref[...]
        oa_ref[0, sb * ts:(sb + 1) * ts, :] = (yn + bn_s[sb]).astype(oa_ref.dtype)
        yield

    _alternate(prepare(0))
    for sb in range(nsub):
        _alternate(chunk_terms(sb),
                   prepare(sb + 1) if sb + 1 < nsub else None,
                   sweep(sb - 1) if sb > 0 else None)
    _alternate(sweep(nsub - 1))
    prev_scr[...] = us_ref[0, tt - 1:tt, :]

    @pl.when(c_idx == nc - 1)
    def _():
        sout_ref[0] = s_scr[...]


def _wkv(us, prev0, s0_bd, prm, tt):
    b, t, _ = us.shape
    npair = H_A // 2
    nsub = max(1, tt // WKV_SUB)
    ts = tt // nsub
    vec = lambda w: pl.BlockSpec((1, w), lambda i, j: (0, 0))
    mat = lambda: pl.BlockSpec((LANES, W_A), lambda i, j: (0, 0))
    blk = lambda dt: pltpu.VMEM((nsub, ts, W_A), dt)
    return pl.pallas_call(
        functools.partial(_wkv_kernel, nsub=nsub),
        grid=(b, t // tt),
        in_specs=[pl.BlockSpec((1, tt, SHIFT_W), lambda i, j: (i, j, 0)),
                  pl.BlockSpec((1, 1, SHIFT_W), lambda i, j: (i, 0, 0)),
                  pl.BlockSpec((1, npair, LANES, LANES), lambda i, j: (i, 0, 0, 0)),
                  vec(SHIFT_W), vec(W_A), mat(), vec(W_A), mat(),
                  vec(W_A), vec(W_A), vec(W_A), vec(W_A), vec(W_A)],
        out_specs=[pl.BlockSpec((1, tt, W_A), lambda i, j: (i, j, 0)),
                   pl.BlockSpec((1, npair, LANES, LANES), lambda i, j: (i, 0, 0, 0))],
        out_shape=[jax.ShapeDtypeStruct((b, t, W_A), BF16),
                   jax.ShapeDtypeStruct((b, npair, LANES, LANES), F32)],
        scratch_shapes=[pltpu.VMEM((npair, LANES, LANES), F32),
                        pltpu.VMEM((1, SHIFT_W), F32),
                        blk(F32), blk(F32), blk(BF16), blk(BF16), blk(BF16), blk(BF16),
                        blk(F32), blk(F32), blk(F32), blk(F32)],
        compiler_params=_cparams(("parallel", "arbitrary")),
    )(us, prev0, s0_bd, prm["mu"], prm["w0"], prm["wd"], prm["a0"], prm["wa"],
      prm["kkp"], prm["ka"], prm["rk"], prm["lnx_g"], prm["lnx_b"])


_AUG = 8
_PREP_ROWS = 256


def _bias_lanes(cum, key_side):
    ntile = H_B // 2 if key_side else H_B
    width = ntile * LANES
    r = lax.broadcasted_iota(jnp.int32, (LANES, width), 0)
    c = lax.broadcasted_iota(jnp.int32, (LANES, width), 1)
    lane = lax.broadcasted_iota(jnp.int32, (1, width), 1)
    f_off, one_off = (0, 3) if key_side else (3, 0)
    sign = -1.0 if key_side else 1.0
    head_of_tile = (c // LANES) * 2 + (c % LANES) // _AUG if key_side else c // LANES
    slot = c % _AUG
    in_head_lanes = (c % LANES) // _AUG == r % 2
    one_head = (lane % LANES) // _AUG < 2 if key_side else (lane % LANES) // _AUG == (lane // LANES) % 2
    out = jnp.where(one_head & (lane % _AUG >= one_off) & (lane % _AUG < one_off + 3), 1.0, 0.0)
    for t, part in enumerate(_split3(cum)):
        sel = jnp.where((head_of_tile == r) & in_head_lanes & (slot == f_off + t), sign, 0.0).astype(BF16)
        out = out + _dot(part, sel)
    return [out[:, i * LANES:(i + 1) * LANES] for i in range(ntile)]


def _stage_rows(total):
    return [(s, min(_PREP_ROWS, total - s)) for s in range(0, total, _PREP_ROWS)]


def _transpose_rows(x):
    n = x.shape[0]
    n_pad = -n % LANES
    if n_pad:
        x = jnp.concatenate([x, jnp.zeros((n_pad, x.shape[1]), x.dtype)], axis=0)
    return x.T[:, :n]


def _fox_kernel(*refs, tq, tk, past, single, new_time_on_lanes):
    if past:
        q_ref, kc_ref, vc_ref, k_ref, v_ref, cq_ref, ck_ref, o_ref, ka_scr, vt_scr, acc_scr = refs
    else:
        q_ref, k_ref, v_ref, cq_ref, ck_ref, o_ref, ka_scr, vt_scr, acc_scr = refs
    npair = H_B // 2
    lane1 = lax.broadcasted_iota(jnp.int32, (1, LANES), 1)
    head_of_lane = lane1 // HEAD_DIM

    def stage_bias(s, n):
        cum = ck_ref[0, s:s + n, :] * LOG2E
        for p, aug in enumerate(_bias_lanes(cum, True)):
            ka_scr[p, s:s + n, LANES:2 * LANES] = aug.astype(BF16)

    def stage_kv():
        for s, n in _stage_rows(past):
            for p in range(npair):
                ka_scr[p, s:s + n, 0:LANES] = kc_ref[0, p * LANES:(p + 1) * LANES, s:s + n].T.astype(BF16)
            vt_scr[:, s:s + n] = vc_ref[0, :, s:s + n].astype(BF16)
            stage_bias(s, n)
        if new_time_on_lanes:
            for s, n in _stage_rows(k_ref.shape[2]):
                for p in range(npair):
                    ka_scr[p, past + s:past + s + n, 0:LANES] = (
                        k_ref[0, p * LANES:(p + 1) * LANES, s:s + n].T.astype(BF16))
                vt_scr[:, past + s:past + s + n] = v_ref[0, :, s:s + n].astype(BF16)
                stage_bias(past + s, n)
            return
        for s, n in _stage_rows(k_ref.shape[1]):
            kb = k_ref[0, s:s + n, :].astype(BF16)
            for p in range(npair):
                ka_scr[p, past + s:past + s + n, 0:LANES] = kb[:, p * LANES:(p + 1) * LANES]
            vt_scr[:, past + s:past + s + n] = _transpose_rows(v_ref[0, s:s + n, :]).astype(BF16)
            stage_bias(past + s, n)

    if single:
        i = 0
        stage_kv()
    else:
        i = pl.program_id(1)
        pl.when(i == 0)(stage_kv)

    q = q_ref[0].astype(F32) * (HEAD_DIM ** -0.5 * LOG2E)
    cq = cq_ref[0] * LOG2E
    rhs = []
    q_bias = _bias_lanes(cq, False)
    for p in range(npair):
        q_p = q[:, p * LANES:(p + 1) * LANES]
        halves = [jnp.concatenate([jnp.where(head_of_lane == hh, q_p, 0.0), q_bias[2 * p + hh]], axis=1)
                  for hh in range(2)]
        rhs.append(jnp.concatenate(halves, axis=0).astype(BF16))

    n_full = past // tk + i * (tq // tk)
    diag0 = past if single else pl.multiple_of(past + i * tq, tq)
    krow = lax.broadcasted_iota(jnp.int32, (tq, 2 * tq), 0)
    qcol = lax.broadcasted_iota(jnp.int32, (tq, 2 * tq), 1)
    causal_t = krow <= jnp.where(qcol >= tq, qcol - tq, qcol)

    acc_scr[...] = jnp.zeros(acc_scr.shape, F32)

    def update(carry, start, size, mask):
        ms, ls = carry
        st = [_dot_nt(ka_scr[p, pl.ds(start, size), :], rhs[p]) for p in range(npair)]
        if mask is not None:
            st = [jnp.where(mask, s, NEG_INF) for s in st]
        m_new = [jnp.maximum(m, jnp.max(s, axis=0, keepdims=True)) for m, s in zip(ms, st)]
        alpha = [jnp.exp2(m - mn) for m, mn in zip(ms, m_new)]
        pt = [jnp.exp2(s - mn) for s, mn in zip(st, m_new)]
        l_new = [a * l + jnp.sum(x, axis=0, keepdims=True) for a, l, x in zip(alpha, ls, pt)]
        pv = [_dot(vt_scr[p * LANES:(p + 1) * LANES, pl.ds(start, size)], pt[p].astype(BF16))
              for p in range(npair)]
        for p in range(npair):
            acc_scr[p] = acc_scr[p] * alpha[p] + pv[p]
        return tuple(m_new), tuple(l_new)

    def body(j, carry):
        return update(carry, pl.multiple_of(j * tk, tk), tk, None)

    init = (tuple(jnp.full((1, 2 * tq), NEG_INF, F32) for _ in range(npair)),
            tuple(jnp.zeros((1, 2 * tq), F32) for _ in range(npair)))
    carry = lax.fori_loop(0, n_full, body, init)
    _, ls = update(carry, diag0, tq, causal_t)
    for p in range(npair):
        o_t = (acc_scr[p] / ls[p]).T
        o_ref[0, :, p * LANES:(p + 1) * LANES] = jnp.where(
            head_of_lane == 0, o_t[:tq], o_t[tq:]).astype(o_ref.dtype)


def _kv_specs(k_new, cache, layer, width, t_new, new_time_on_lanes=False, new_layer=None):
    specs, args = [], []
    new_layer = layer if new_layer is None else new_layer
    if cache is not None:
        past = cache[0].shape[3]
        for c in cache:
            specs.append(pl.BlockSpec((None, 1, width, past), lambda i, j: (layer, i, 0, 0)))
            args.append(c)
    for a in k_new:
        if new_time_on_lanes:
            specs.append(pl.BlockSpec((None, 1, width, t_new), lambda i, j: (new_layer, i, 0, 0)))
        else:
            specs.append(pl.BlockSpec((None, 1, t_new, width), lambda i, j: (new_layer, i, 0, 0)))
        args.append(a)
    return specs, args


def _fox(q, kv_new, cache, cum_q, cum_k, *, layer, tq, tk, past, new_time_on_lanes):
    b, t, _ = q.shape
    tkk = past + t
    npair = H_B // 2
    kv_specs, kv_args = _kv_specs(kv_new, cache, layer, W_B, t, new_time_on_lanes)
    return pl.pallas_call(
        functools.partial(_fox_kernel, tq=tq, tk=tk, past=past, single=(t == tq),
                          new_time_on_lanes=new_time_on_lanes),
        grid=(b, t // tq),
        in_specs=[pl.BlockSpec((1, tq, W_B), lambda i, j: (i, j, 0))] + kv_specs + [
                  pl.BlockSpec((1, tq, LANES), lambda i, j: (i, j, 0)),
                  pl.BlockSpec((1, tkk, LANES), lambda i, j: (i, 0, 0))],
        out_specs=pl.BlockSpec((1, tq, W_B), lambda i, j: (i, j, 0)),
        out_shape=jax.ShapeDtypeStruct((b, t, W_B), BF16),
        scratch_shapes=[pltpu.VMEM((npair, tkk, 2 * LANES), BF16), pltpu.VMEM((W_B, tkk), BF16),
                        pltpu.VMEM((npair, LANES, 2 * tq), F32)],
        compiler_params=_cparams(("parallel", "arbitrary")),
    )(q, *kv_args, cum_q, cum_k)


def _sb_kernel(*refs, tq, tk, past, single):
    if past:
        q_ref, kc_ref, vc_ref, k_ref, v_ref, o_ref, kb_scr, vt_scr, acc_scr = refs
    else:
        q_ref, k_ref, v_ref, o_ref, kb_scr, vt_scr, acc_scr = refs
    npair = H_C // 2
    lane1 = lax.broadcasted_iota(jnp.int32, (1, LANES), 1)
    head_of_lane = lane1 // HEAD_DIM

    def stage_kv():
        for s, n in _stage_rows(past):
            kb_scr[s:s + n, :] = kc_ref[0, :, s:s + n].T.astype(BF16)
            vt_scr[:, s:s + n] = vc_ref[0, :, s:s + n].astype(BF16)
        for s, n in _stage_rows(k_ref.shape[1]):
            kb_scr[past + s:past + s + n, :] = k_ref[0, s:s + n, :].astype(BF16)
            vt_scr[:, past + s:past + s + n] = _transpose_rows(v_ref[0, s:s + n, :]).astype(BF16)

    if single:
        i = 0
        stage_kv()
    else:
        i = pl.program_id(1)
        pl.when(i == 0)(stage_kv)

    q = q_ref[0].astype(F32) * (HEAD_DIM ** -0.5 * LOG2E)
    rhs = []
    for p in range(npair):
        q_p = q[:, p * LANES:(p + 1) * LANES]
        rhs.append(jnp.concatenate([jnp.where(head_of_lane == hh, q_p, 0.0) for hh in range(2)],
                                   axis=0).astype(BF16))

    def earlier_matrix(n):
        rr = lax.broadcasted_iota(jnp.int32, (n, n), 0)
        cc = lax.broadcasted_iota(jnp.int32, (n, n), 1)
        return (cc > rr).astype(BF16)

    n_full = past // tk + i * (tq // tk)
    diag0 = past if single else pl.multiple_of(past + i * tq, tq)
    krow = lax.broadcasted_iota(jnp.int32, (tq, 2 * tq), 0)
    qcol = lax.broadcasted_iota(jnp.int32, (tq, 2 * tq), 1)
    strict_t = krow < jnp.where(qcol >= tq, qcol - tq, qcol)

    def update(runs, start, size, mask, first):
        zt = [_dot_nt(kb_scr[pl.ds(start, size), p * LANES:(p + 1) * LANES], rhs[p])
              for p in range(npair)]
        mz = [jnp.minimum(z, 0.0) for z in zt]
        tail = [jnp.log(1.0 + jnp.exp2(m + m - z)) * LOG2E for m, z in zip(mz, zt)]
        lsig = [m - t for m, t in zip(mz, tail)]
        l1m = [s - z for s, z in zip(lsig, zt)]
        if mask is not None:
            l1m = [jnp.where(mask, x, 0.0) for x in l1m]
        later = earlier_matrix(size)
        after = [_dot(later, x.astype(BF16)) for x in l1m]
        wt = [jnp.exp2(s + a + r) for s, a, r in zip(lsig, after, runs)]
        if mask is not None:
            wt = [jnp.where(mask, w, 0.0) for w in wt]
        pv = [_dot(vt_scr[p * LANES:(p + 1) * LANES, pl.ds(start, size)], wt[p].astype(BF16))
              for p in range(npair)]
        for p in range(npair):
            acc_scr[p] = pv[p] if first else acc_scr[p] + pv[p]
        return tuple(r + jnp.sum(x, axis=0, keepdims=True) for r, x in zip(runs, l1m))

    runs = tuple(jnp.zeros((1, 2 * tq), F32) for _ in range(npair))
    runs = update(runs, diag0, tq, strict_t, True)

    def body(jj, runs):
        j = n_full - 1 - jj
        return update(runs, pl.multiple_of(j * tk, tk), tk, None, False)

    lax.fori_loop(0, n_full, body, runs)
    for p in range(npair):
        o_t = acc_scr[p].T
        o_ref[0, :, p * LANES:(p + 1) * LANES] = jnp.where(
            head_of_lane == 0, o_t[:tq], o_t[tq:]).astype(o_ref.dtype)


def _sb(q, kv_new, cache, *, layer, tq, tk, past):
    b, t, _ = q.shape
    tkk = past + kv_new[0].shape[2]
    npair = H_C // 2
    kv_specs, kv_args = _kv_specs(kv_new, cache, layer, W_C, t, new_layer=0)
    return pl.pallas_call(
        functools.partial(_sb_kernel, tq=tq, tk=tk, past=past, single=(t == tq)),
        grid=(b, t // tq),
        in_specs=[pl.BlockSpec((1, tq, W_C), lambda i, j: (i, j, 0))] + kv_specs,
        out_specs=pl.BlockSpec((1, tq, W_C), lambda i, j: (i, j, 0)),
        out_shape=jax.ShapeDtypeStruct((b, t, W_C), BF16),
        scratch_shapes=[pltpu.VMEM((tkk, W_C), BF16), pltpu.VMEM((W_C, tkk), BF16),
                        pltpu.VMEM((npair, LANES, 2 * tq), F32)],
        compiler_params=_cparams(("parallel", "arbitrary")),
    )(q, *kv_args)


def _out_kernel(oa_ref, ob_ref, oc_ref, g_ref, x_ref, w_ref, lng_ref, lnb_ref, y_ref, *, alpha):
    g = g_ref[...].astype(F32)
    gate = g * _sigmoid(g)
    gated = lambda o_ref, lo, hi: (o_ref[...].astype(F32) * gate[:, lo:hi]).astype(BF16)
    acc = _dot(gated(oa_ref, 0, W_A), w_ref[0:W_A, :])
    acc = acc + _dot(gated(ob_ref, W_A, W_A + W_B), w_ref[W_A:W_A + W_B, :])
    acc = acc + _dot(gated(oc_ref, W_A + W_B, W_MIX), w_ref[W_A + W_B:, :])
    z = alpha * x_ref[...] + acc
    mu = jnp.mean(z, axis=-1, keepdims=True)
    d = z - mu
    var = jnp.mean(d * d, axis=-1, keepdims=True)
    y_ref[...] = d * lax.rsqrt(var + LN_EPS) * lng_ref[...] + lnb_ref[...]


def _out(oa, ob, oc, g, x2d, w_out, ln_g, ln_b, alpha, tm):
    n, d = x2d.shape
    row = lambda w: pl.BlockSpec((tm, w), lambda i: (i, 0))
    return pl.pallas_call(
        functools.partial(_out_kernel, alpha=alpha),
        grid=(n // tm,),
        in_specs=[row(W_A), row(W_B), row(W_C), row(W_MIX), row(d),
                  pl.BlockSpec((W_MIX, d), lambda i: (0, 0)),
                  pl.BlockSpec((1, d), lambda i: (0, 0)),
                  pl.BlockSpec((1, d), lambda i: (0, 0))],
        out_specs=row(d),
        out_shape=jax.ShapeDtypeStruct((n, d), F32),
        compiler_params=_cparams(("parallel",)),
    )(oa, ob, oc, g, x2d, w_out, ln_g, ln_b)


def _pad_lanes(a, width=LANES):
    return jnp.pad(a, [(0, 0)] * (a.ndim - 1) + [(0, width - a.shape[-1])])


def _state_to_pairs(s):
    b = s.shape[0]
    odd_head = (jnp.arange(H_A) % 2 == 1).reshape(1, H_A, 1, 1)
    upper_half = (jnp.arange(LANES) >= HEAD_DIM).reshape(1, 1, 1, LANES)
    wide = jnp.where(odd_head == upper_half, jnp.concatenate([s, s], axis=-1), 0.0)
    return wide.reshape(b, H_A // 2, LANES, LANES)


def _pairs_to_state(sp):
    b = sp.shape[0]
    wide = sp.reshape(b, H_A, HEAD_DIM, LANES)
    odd_head = (jnp.arange(H_A) % 2 == 1).reshape(1, H_A, 1, 1)
    return jnp.where(odd_head, wide[..., HEAD_DIM:], wide[..., :HEAD_DIM])


_IN_SIZES = (SHIFT_W, W_A, W_B, W_B, W_B, H_B, W_B, W_C, W_C, W_C, W_C)
_IN_OFFS = tuple(int(v) for v in np.concatenate([[0], np.cumsum(_IN_SIZES)]))
_SRC_SHIFT, _SRC_GA, _SRC_QB, _SRC_KB, _SRC_VB, _SRC_F, _SRC_GB, _SRC_QC, _SRC_KC, _SRC_VC, _SRC_GC = (
    _IN_OFFS[:-1])
_W_MOVES = ((_SRC_SHIFT, _OFF_SHIFT, SHIFT_W), (_SRC_GA, _OFF_G, W_A), (_SRC_GB, _OFF_G + W_A, W_B),
            (_SRC_GC, _OFF_G + W_A + W_B, W_C), (_SRC_QB, _OFF_QB, W_B), (_SRC_KB, _OFF_KB, W_B),
            (_SRC_VB, _OFF_VB, W_B), (_SRC_QC, _OFF_QC, W_C), (_SRC_KC, _OFF_KC, W_C),
            (_SRC_VC, _OFF_VC, W_C))
_W_STAGE_ROWS = 128


def _stage_w_kernel(w_ref, o_ref):
    for src, dst, width in _W_MOVES:
        o_ref[:, dst:dst + width] = w_ref[:, src:src + width].astype(BF16)
    lane = lax.broadcasted_iota(jnp.int32, (1, LANES), 1)
    o_ref[:, _OFF_F:_OFF_F + LANES] = jnp.where(
        lane < H_B, w_ref[:, _SRC_F:_SRC_F + LANES], 0.0).astype(BF16)


def _stage_w(w_in, layer):
    _, d, cols = w_in.shape
    return pl.pallas_call(
        _stage_w_kernel,
        grid=(d // _W_STAGE_ROWS,),
        in_specs=[pl.BlockSpec((None, _W_STAGE_ROWS, cols), lambda i: (layer, i, 0))],
        out_specs=pl.BlockSpec((_W_STAGE_ROWS, _PROJ_COLS), lambda i: (i, 0)),
        out_shape=jax.ShapeDtypeStruct((d, _PROJ_COLS), BF16),
        compiler_params=_cparams(("parallel",)),
    )(w_in)


def _layer_params(l, w_in, mu_shift, w0_decay, w_decay, a0, w_aaa, k_k, k_a, r_k,
                  lnx_g, lnx_b, fox_fb, w_out, ln_g, ln_b):
    w_cat = _stage_w(w_in, l)
    zeros = jnp.zeros((D_LORA, W_A), F32)
    row = lambda a: a.reshape(1, -1).astype(F32)
    return dict(
        w_cat=w_cat, fb=_pad_lanes(row(fox_fb[l])),
        mu=row(mu_shift[l]), w0=row(w0_decay[l]),
        wd=jnp.concatenate([w_decay[l], zeros], axis=0).astype(BF16),
        a0=row(a0[l]), wa=jnp.concatenate([zeros, w_aaa[l]], axis=0).astype(BF16),
        kkp=row(k_k[l]), ka=row(k_a[l]), rk=row(r_k[l]), lnx_g=row(lnx_g[l]), lnx_b=row(lnx_b[l]),
        w_out=w_out[l].astype(BF16), ln_g=row(ln_g[l]), ln_b=row(ln_b[l]))


def _run_layer(x, hist, prm, alpha, layer, depth, stacks):
    b, t, d = x.shape
    n = b * t
    x2d = x.reshape(n, d)
    tm = min(ROW_BLOCK, n)
    fox_t = hist is None
    us, g, qb, kb_st, vb_st, qc, kc_st, vc_st, lf, kc_rows, vc_rows = _proj(
        x2d, prm["w_cat"], prm["fb"], tm, layer, depth, stacks, t, fox_t)
    r3 = lambda a: a.reshape(b, t, a.shape[-1])
    us, qb, qc, lf = (r3(a) for a in (us, qb, qc, lf))
    r4 = lambda a: a.reshape(depth, b, t, a.shape[-1])

    if hist is None:
        past = 0
        prev0 = jnp.zeros((b, 1, SHIFT_W), F32)
        s0 = jnp.zeros((b, H_A // 2, LANES, LANES), F32)
        cache_b = cache_c = None
        lf_all = lf
        fox_blk = (min(FOX_BLOCK, t),) * 2
        sb_blk = (min(SB_BLOCK, t),) * 2
    else:
        fk_t, fv_t, h_lf, sk_t, sv_t, h_wkv, h_shift = hist
        past = fk_t.shape[3]
        prev0 = h_shift
        s0 = _state_to_pairs(h_wkv)
        cache_b, cache_c = (fk_t, fv_t), (sk_t, sv_t)
        lf_all = jnp.concatenate([_pad_lanes(h_lf), lf], axis=1)
        fox_blk = sb_blk = (t, min(SB_BLOCK, past))

    oa, s_fin = _wkv(us, prev0, s0, prm, min(WKV_BLOCK, t))
    cum = _cumsum(lf_all)
    fox_kv = (kb_st, vb_st) if fox_t else (r4(kb_st), r4(vb_st))
    ob = _fox(qb, fox_kv, cache_b, cum[:, past:], cum, layer=layer,
              tq=fox_blk[0], tk=fox_blk[1], past=past, new_time_on_lanes=fox_t)
    sb_kv = (kc_rows.reshape(1, b, t, W_C), vc_rows.reshape(1, b, t, W_C))
    oc = _sb(qc, sb_kv, cache_c, layer=layer, tq=sb_blk[0], tk=sb_blk[1], past=past)
    y = _out(oa.reshape(n, W_A), ob.reshape(n, W_B), oc.reshape(n, W_C), g, x2d,
             prm["w_out"], prm["ln_g"], prm["ln_b"], alpha, min(OUT_BLOCK, n))
    small = (lf[:, :, :H_B], _pairs_to_state(s_fin), us[:, -1:, :])
    return y.reshape(b, t, d), (kb_st, vb_st, kc_st, vc_st), small


def kernel(x_prompt, x_sample, cache_fox_k, cache_fox_v, cache_fox_logf, cache_sb_k, cache_sb_v, state_wkv, state_shift, w_in, mu_shift, w0_decay, w_decay, a0, w_aaa, k_k, k_a, r_k, lnx_g, lnx_b, fox_fb, w_out, ln_g, ln_b):
    depth = w_in.shape[0]
    alpha = (2 * depth) ** 0.25
    yp, ys = x_prompt, x_sample

    def time_on_lanes(cache):
        nl, nb, past, nh, hd = cache.shape
        return jnp.transpose(cache, (0, 1, 3, 4, 2)).reshape(nl, nb, nh * hd, past)

    fk_t, fv_t, sk_t, sv_t = (time_on_lanes(c) for c in (cache_fox_k, cache_fox_v, cache_sb_k, cache_sb_v))
    stacks_p, stacks_s = None, None
    small_p, small_s = [], []
    for l in range(depth):
        prm = _layer_params(l, w_in, mu_shift, w0_decay, w_decay, a0, w_aaa, k_k, k_a, r_k,
                            lnx_g, lnx_b, fox_fb, w_out, ln_g, ln_b)
        yp, stacks_p, sm_p = _run_layer(yp, None, prm, alpha, l, depth, stacks_p)
        hist = (fk_t, fv_t, cache_fox_logf[l], sk_t, sv_t, state_wkv[l], state_shift[l])
        ys, stacks_s, sm_s = _run_layer(ys, hist, prm, alpha, l, depth, stacks_s)
        small_p.append(sm_p)
        small_s.append(sm_s)

    def group_outputs(x, stacks, small, fox_t):
        b, t, _ = x.shape
        kb, vb, kc, vc = stacks
        logf, wkv, shift = (jnp.stack([sm[i] for sm in small]) for i in range(3))
        if fox_t:
            kb, vb = (jnp.transpose(a.reshape(depth, b, H_B, HEAD_DIM, t), (0, 1, 4, 2, 3)) for a in (kb, vb))
        return (kb.reshape(depth, b, t, H_B, HEAD_DIM), vb.reshape(depth, b, t, H_B, HEAD_DIM), logf,
                kc.reshape(depth, b, t, H_C, HEAD_DIM), vc.reshape(depth, b, t, H_C, HEAD_DIM),
                wkv, shift)

    return ((yp, ys) + group_outputs(x_prompt, stacks_p, small_p, True)
            + group_outputs(x_sample, stacks_s, small_s, False))
```

```python
import functools

import numpy as np
import jax
import jax.numpy as jnp
from jax import lax
from jax.experimental import pallas as pl
from jax.experimental.pallas import tpu as pltpu

F32 = jnp.float32
BF16 = jnp.bfloat16

HEAD_DIM = 64
H_A, H_B, H_C = 6, 6, 4
W_A, W_B, W_C = H_A * HEAD_DIM, H_B * HEAD_DIM, H_C * HEAD_DIM
W_MIX = W_A + W_B + W_C
D_LORA = 64
SHIFT_W = 3 * W_A + 2 * D_LORA
LANES = 128
WKV_CHUNK = 64
WKV_BLOCK = 1024
WKV_SUB = 256
ROW_BLOCK = 512
OUT_BLOCK = 1024
FOX_BLOCK = 512
SB_BLOCK = 512
GN_EPS = 64e-5
LN_EPS = 1e-5
NEG_INF = -1e30
LOG2E = 1.4426950408889634
VMEM_LIMIT = 56 * 1024 * 1024


def _cparams(sem):
    return pltpu.CompilerParams(dimension_semantics=sem, vmem_limit_bytes=VMEM_LIMIT)


def _dot(a, b):
    return jnp.dot(a, b, preferred_element_type=F32)


def _dot_nt(a, b):
    return lax.dot_general(a, b, (((1,), (1,)), ((), ())), preferred_element_type=F32)


def _dot_tn(a, b):
    return lax.dot_general(a, b, (((0,), (0,)), ((), ())), preferred_element_type=F32)


def _split3(x):
    hi = x.astype(BF16)
    r1 = x - hi.astype(F32)
    mid = r1.astype(BF16)
    lo = (r1 - mid.astype(F32)).astype(BF16)
    return hi, mid, lo


def _softplus(x):
    return jnp.maximum(x, 0.0) + jnp.log1p(jnp.exp(-jnp.abs(x)))


def _sigmoid(x):
    return 1.0 / (1.0 + jnp.exp(-x))


_OFF_SHIFT = 0
_OFF_G = _OFF_SHIFT + SHIFT_W
_OFF_QB = _OFF_G + W_MIX
_OFF_KB = _OFF_QB + W_B
_OFF_VB = _OFF_KB + W_B
_OFF_QC = _OFF_VB + W_B
_OFF_KC = _OFF_QC + W_C
_OFF_VC = _OFF_KC + W_C
_OFF_F = _OFF_VC + W_C
_PROJ_COLS = _OFF_F + LANES


def _proj_kernel(x_ref, w_ref, fb_ref, *refs):
    (us_ref, g_ref, qb_ref, kb_ref, vb_ref, qc_ref, kc_ref, vc_ref, lf_ref,
     kc_rows_ref, vc_rows_ref, lft_ref) = refs[-12:]
    xb = x_ref[...].astype(BF16)
    tm = xb.shape[0]
    for ref, off in ((us_ref, _OFF_SHIFT), (g_ref, _OFF_G), (qb_ref, _OFF_QB),
                     (kb_ref, _OFF_KB), (vb_ref, _OFF_VB), (qc_ref, _OFF_QC)):
        time_on_lanes = ref.shape[-1] == tm and ref.shape[0] != tm
        width = ref.shape[0] if time_on_lanes else ref.shape[-1]
        res = _dot(xb, w_ref[:, off:off + width])
        ref[...] = (res.T if time_on_lanes else res).astype(ref.dtype)
    for ref, rows_ref, off in ((kc_ref, kc_rows_ref, _OFF_KC), (vc_ref, vc_rows_ref, _OFF_VC)):
        res = _dot(xb, w_ref[:, off:off + W_C])
        rows_ref[...] = res.astype(rows_ref.dtype)
        for h in range(H_C):
            ref[pl.ds(h, tm, stride=H_C), :] = res[:, h * HEAD_DIM:(h + 1) * HEAD_DIM]
    f = _dot(xb, w_ref[:, _OFF_F:_OFF_F + LANES]) + fb_ref[...]
    lf = -_softplus(-f)
    lf_ref[...] = lf
    lft_ref[...] = lf.T[:lft_ref.shape[0], :]


_PROJ_WIDTHS = (SHIFT_W, W_MIX, W_B, W_B, W_B, W_C, W_C, W_C, LANES, W_C, W_C, 8)
_PROJ_LOGF_T = 11
_PROJ_BF16 = (1, 2, 5, 9, 10)
_PROJ_SB_KV = (6, 7)
_PROJ_STACKED = (3, 4, 6, 7)
_PROJ_FOX_KV = (3, 4)


def _proj(x2d, w_cat, fb_pad, tm, layer, depth, stacks, seq_len, fox_time_on_lanes):
    n, d = x2d.shape
    assert tm != W_B
    per_row = seq_len // tm
    out_specs, out_shape = [], []
    for idx, w in enumerate(_PROJ_WIDTHS):
        if idx in _PROJ_FOX_KV and fox_time_on_lanes:
            out_specs.append(pl.BlockSpec((None, None, w, tm),
                                          lambda i: (layer, i // per_row, 0, i % per_row)))
            out_shape.append(jax.ShapeDtypeStruct((depth, n // seq_len, w, seq_len), F32))
        elif idx == _PROJ_LOGF_T:
            out_specs.append(pl.BlockSpec((w, tm), lambda i: (0, i)))
            out_shape.append(jax.ShapeDtypeStruct((w, n), F32))
        elif idx in _PROJ_SB_KV:
            out_specs.append(pl.BlockSpec((None, tm * H_C, HEAD_DIM), lambda i: (layer, i, 0)))
            out_shape.append(jax.ShapeDtypeStruct((depth, n * H_C, HEAD_DIM), F32))
        elif idx in _PROJ_STACKED:
            out_specs.append(pl.BlockSpec((None, tm, w), lambda i: (layer, i, 0)))
            out_shape.append(jax.ShapeDtypeStruct((depth, n, w), F32))
        else:
            out_specs.append(pl.BlockSpec((tm, w), lambda i: (i, 0)))
            out_shape.append(jax.ShapeDtypeStruct((n, w), BF16 if idx in _PROJ_BF16 else F32))
    in_specs = [pl.BlockSpec((tm, d), lambda i: (i, 0)),
                pl.BlockSpec((d, _PROJ_COLS), lambda i: (0, 0)),
                pl.BlockSpec((1, LANES), lambda i: (0, 0))]
    args = [x2d, w_cat, fb_pad]
    aliases = {}
    if stacks is not None:
        for j, (idx, st) in enumerate(zip(_PROJ_STACKED, stacks)):
            in_specs.append(pl.BlockSpec(memory_space=pl.ANY))
            args.append(st)
            aliases[3 + j] = idx
    return pl.pallas_call(
        _proj_kernel,
        grid=(n // tm,),
        in_specs=in_specs,
        out_specs=out_specs,
        out_shape=out_shape,
        input_output_aliases=aliases,
        compiler_params=_cparams(("parallel",)),
    )(*args)


def _cumsum_kernel(lf_ref, cum_ref, *, blk):
    tk = lf_ref.shape[1]
    row = lax.broadcasted_iota(jnp.int32, (blk, blk), 0)
    col = lax.broadcasted_iota(jnp.int32, (blk, blk), 1)
    tri = (col <= row).astype(BF16)
    carry = jnp.zeros((1, LANES), F32)
    for s in range(0, tk, blk):
        x = lf_ref[0, s:s + blk, :]
        hi, mid, lo = _split3(x)
        c = _dot(tri, hi) + _dot(tri, mid) + _dot(tri, lo) + carry
        cum_ref[0, s:s + blk, :] = c
        carry = c[blk - 1:blk, :]


def _cumsum(lf, blk=64):
    b, tk, _ = lf.shape
    return pl.pallas_call(
        functools.partial(_cumsum_kernel, blk=blk),
        grid=(b,),
        in_specs=[pl.BlockSpec((1, tk, LANES), lambda i: (i, 0, 0))],
        out_specs=pl.BlockSpec((1, tk, LANES), lambda i: (i, 0, 0)),
        out_shape=jax.ShapeDtypeStruct((b, tk, LANES), F32),
        compiler_params=_cparams(("parallel",)),
    )(lf)


def _seg_sum(x, bd):
    xb = x.astype(BF16)
    return jnp.concatenate([_dot(xb[:, s:s + LANES], bd) for s in range(0, x.shape[1], LANES)], axis=1)


def _alternate(*stages):
    gens = [g for g in stages if g is not None]
    while gens:
        for g in list(gens):
            try:
                next(g)
            except StopIteration:
                gens.remove(g)


def _wkv_kernel(us_ref, prev0_ref, s0_ref, mu_ref, w0_ref, wd_ref, a0_ref, wa_ref,
                kkp_ref, ka_ref, rk_ref, lng_ref, lnb_ref,
                oa_ref, sout_ref,
                s_scr, prev_scr, at_s, rt_s, bh_s, kh_s, be_s, ke_s, v_s, ga_s, bn_s, y_s, *, nsub):
    c_idx = pl.program_id(1)
    nc = pl.num_programs(1)
    C = WKV_CHUNK
    tt = us_ref.shape[1]
    ts = tt // nsub
    nb = ts // C
    npair = H_A // 2
    G = 4 * HEAD_DIM

    @pl.when(c_idx == 0)
    def _():
        s_scr[...] = s0_ref[0]
        prev_scr[...] = prev0_ref[0]

    lane1 = lax.broadcasted_iota(jnp.int32, (1, LANES), 1)
    lane_w = lax.broadcasted_iota(jnp.int32, (LANES, LANES), 0) // HEAD_DIM
    lane_c = lax.broadcasted_iota(jnp.int32, (LANES, LANES), 1) // HEAD_DIM
    bd_pair = lane_w == lane_c
    bd = bd_pair.astype(BF16)
    row_g = lax.broadcasted_iota(jnp.int32, (C, G), 0)
    lane_g = lax.broadcasted_iota(jnp.int32, (C, G), 1) % C
    tri_incl = lane_g <= row_g
    tri_strict = lane_g < row_g
    eye = (lane_g == row_g).astype(F32)
    diag4 = (lax.broadcasted_iota(jnp.int32, (G, G), 0) // HEAD_DIM
             == lax.broadcasted_iota(jnp.int32, (G, G), 1) // HEAD_DIM).astype(BF16)
    rowt = lax.broadcasted_iota(jnp.int32, (ts, ts), 0)
    colt = lax.broadcasted_iota(jnp.int32, (ts, ts), 1)
    tri_b = (((rowt // C) == (colt // C)) & (colt <= rowt)).astype(BF16)
    row1 = lax.broadcasted_iota(jnp.int32, (ts, 1), 0)

    def stack4(x):
        xb = x.astype(BF16)
        return jnp.concatenate([xb, xb, xb, xb], axis=0) * diag4

    def mul4(xs, ys):
        return [_dot(x.astype(BF16), stack4(y)) for x, y in zip(xs, ys)]

    def prepare(sb):
        base = sb * ts
        u = us_ref[0, base:base + ts, :]
        before = prev_scr[...] if sb == 0 else us_ref[0, base - 1:base, :]
        zprev = jnp.where(row1 == 0, before, pltpu.roll(u, 1, axis=0))
        zs = u + (zprev - u) * mu_ref[...]
        r = zs[:, 0:W_A]
        k = zs[:, W_A:2 * W_A]
        v = zs[:, 2 * W_A:3 * W_A]
        lora_in = zs[:, 3 * W_A:3 * W_A + LANES]
        lora_t = jnp.where(lane1 < D_LORA, jnp.tanh(lora_in), lora_in).astype(BF16)
        yield
        wl = w0_ref[...] + _dot(lora_t, wd_ref[...])
        a = _sigmoid(a0_ref[...] + _dot(lora_t, wa_ref[...]))
        w_log = -_softplus(-wl) - 0.5
        lw = -jnp.exp(w_log)
        yield
        kk = k * kkp_ref[...]
        kk = kk * lax.rsqrt(_seg_sum(kk * kk, bd) + 1e-12)
        k2 = k * (1.0 + (a - 1.0) * ka_ref[...])
        bn_s[sb] = _seg_sum(r * k2 * rk_ref[...], bd) * v
        v_s[sb] = v
        yield
        h1, h2, h3 = _split3(lw)
        cs = _dot(tri_b, h1) + _dot(tri_b, h2) + _dot(tri_b, h3)
        tot = jnp.concatenate([jnp.broadcast_to(cs[c * C + C - 1:c * C + C, :], (C, W_A))
                               for c in range(nb)], axis=0)
        yield
        g_inv = jnp.exp(-cs)
        g_end = jnp.exp(tot - cs)
        kka = kk * a
        at_s[sb] = kk * jnp.exp(cs - lw)
        rt_s[sb] = r * jnp.exp(cs)
        yield
        bh_s[sb] = (-kka * g_inv).astype(BF16)
        kh_s[sb] = (k2 * g_inv).astype(BF16)
        be_s[sb] = (-kka * g_end).astype(BF16)
        ke_s[sb] = (k2 * g_end).astype(BF16)
        ga_s[sb] = jnp.exp(tot)
        yield

    local = {}

    def chunk_terms(sb):
        groups = []
        for c in range(nb):
            groups.append([(c, 0), (c, LANES)])
        for c in range(0, nb, 2):
            groups.append([(c, 2 * LANES)] + ([(c + 1, 2 * LANES)] if c + 1 < nb else [None]))

        def gather(ref, grp):
            parts = [jnp.zeros((C, LANES), ref.dtype) if u is None
                     else ref[sb, u[0] * C:(u[0] + 1) * C, u[1]:u[1] + LANES] for u in grp]
            return jnp.concatenate(parts, axis=1)

        at_g = [gather(at_s, g) for g in groups]
        rt_g = [gather(rt_s, g) for g in groups]
        v_g = [gather(v_s, g) for g in groups]
        lhs = [jnp.concatenate([a, r], axis=0).astype(BF16) for a, r in zip(at_g, rt_g)]
        rhs = [jnp.concatenate([stack4(gather(bh_s, g)), stack4(gather(kh_s, g))], axis=0) for g in groups]
        a4 = [_dot_nt(l, r) for l, r in zip(lhs, rhs)]
        yield
        n1 = [jnp.where(tri_strict, a[:C, :G], 0.0) for a in a4]
        a_ak = [jnp.where(tri_strict, a[:C, G:], 0.0) for a in a4]
        a_rb = [jnp.where(tri_incl, a[C:, :G], 0.0) for a in a4]
        a_rk = [jnp.where(tri_incl, a[C:, G:], 0.0) for a in a4]
        vst = [stack4(v) for v in v_g]
        avy = [_dot(jnp.concatenate([ak, rk], axis=0).astype(BF16), vs)
               for ak, rk, vs in zip(a_ak, a_rk, vst)]
        yield
        pair4 = lambda xs, ys: [eye + x + y + m for x, y, m in zip(xs, ys, mul4(xs, ys))]
        n2 = mul4(n1, n1)
        yield
        n4 = mul4(n2, n2)
        p1 = pair4(n1, n2)
        yield
        n8 = mul4(n4, n4)
        yield
        n16 = mul4(n8, n8)
        p2 = pair4(n4, n8)
        yield
        n32 = mul4(n16, n16)
        p12 = mul4(p1, p2)
        yield
        p3 = pair4(n16, n32)
        yield
        t_inv = mul4(p12, p3)
        yield
        xs = [_dot(t.astype(BF16), jnp.concatenate([stack4(a), stack4(y[:C])], axis=1))
              for t, a, y in zip(t_inv, at_g, avy)]
        yield
        rys = [_dot(rb.astype(BF16), jnp.concatenate([stack4(x[:, :G]), stack4(x[:, G:])], axis=1))
               for rb, x in zip(a_rb, xs)]
        rp_g = [r + ry[:, :G] for r, ry in zip(rt_g, rys)]
        y0_g = [ry[:, G:] + y[C:] for ry, y in zip(rys, avy)]
        yield
        for gi, grp in enumerate(groups):
            for kpos, u in enumerate(grp):
                if u is None:
                    continue
                c, p = u[0], u[1] // LANES
                rows = slice(c * C, (c + 1) * C)
                sl = slice(p * LANES, (p + 1) * LANES)
                gl = slice(kpos * LANES, (kpos + 1) * LANES)
                be_u, ke_u = be_s[sb, rows, sl], ke_s[sb, rows, sl]
                atp = xs[gi][:, gl].astype(BF16)
                wv = jnp.concatenate([xs[gi][:, G + kpos * LANES:G + (kpos + 1) * LANES],
                                      v_s[sb, rows, sl]], axis=0).astype(BF16)
                pp = jnp.where(bd_pair, _dot_tn(atp, be_u), 0.0).astype(BF16)
                q = jnp.where(bd_pair, _dot_tn(wv, jnp.concatenate([be_u, ke_u], axis=0)), 0.0)
                local[sb, c, p] = (rp_g[gi][:, gl].astype(BF16), y0_g[gi][:, gl], pp, q)
        yield

    def sweep(sb):
        for c in range(nb):
            rows = slice(c * C, (c + 1) * C)
            s_ps = [s_scr[p] for p in range(npair)]
            s_bs = [s_p.astype(BF16) for s_p in s_ps]
            for p in range(npair):
                sl = slice(p * LANES, (p + 1) * LANES)
                rp, y0, pp, q = local[sb, c, p]
                y_s[sb, rows, sl] = _dot_nt(rp, s_bs[p]) + y0
                s_scr[p] = s_ps[p] * ga_s[sb, c * C:c * C + 1, sl] + _dot(s_bs[p], pp) + q
            yield
        y = y_s[sb]
        inv_n = 1.0 / HEAD_DIM
        mean = _seg_sum(y, bd) * inv_n
        d = y - mean
        yield
        var = _seg_sum(d * d, bd) * inv_n
        yn = d * lax.rsqrt(var + GN_EPS) * lng_ref[...] + lnb_ref[...]
        oa_ref[0, sb * ts:(sb + 1) * ts, :] = (yn + bn_s[sb]).astype(oa_ref.dtype)
        yield

    _alternate(prepare(0))
    for sb in range(nsub):
        _alternate(chunk_terms(sb),
                   prepare(sb + 1) if sb + 1 < nsub else None,
                   sweep(sb - 1) if sb > 0 else None)
    _alternate(sweep(nsub - 1))
    prev_scr[...] = us_ref[0, tt - 1:tt, :]

    @pl.when(c_idx == nc - 1)
    def _():
        sout_ref[0] = s_scr[...]


def _wkv(us, prev0, s0_bd, prm, tt):
    b, t, _ = us.shape
    npair = H_A // 2
    nsub = max(1, tt // WKV_SUB)
    ts = tt // nsub
    vec = lambda w: pl.BlockSpec((1, w), lambda i, j: (0, 0))
    mat = lambda: pl.BlockSpec((LANES, W_A), lambda i, j: (0, 0))
    blk = lambda dt: pltpu.VMEM((nsub, ts, W_A), dt)
    return pl.pallas_call(
        functools.partial(_wkv_kernel, nsub=nsub),
        grid=(b, t // tt),
        in_specs=[pl.BlockSpec((1, tt, SHIFT_W), lambda i, j: (i, j, 0)),
                  pl.BlockSpec((1, 1, SHIFT_W), lambda i, j: (i, 0, 0)),
                  pl.BlockSpec((1, npair, LANES, LANES), lambda i, j: (i, 0, 0, 0)),
                  vec(SHIFT_W), vec(W_A), mat(), vec(W_A), mat(),
                  vec(W_A), vec(W_A), vec(W_A), vec(W_A), vec(W_A)],
        out_specs=[pl.BlockSpec((1, tt, W_A), lambda i, j: (i, j, 0)),
                   pl.BlockSpec((1, npair, LANES, LANES), lambda i, j: (i, 0, 0, 0))],
        out_shape=[jax.ShapeDtypeStruct((b, t, W_A), BF16),
                   jax.ShapeDtypeStruct((b, npair, LANES, LANES), F32)],
        scratch_shapes=[pltpu.VMEM((npair, LANES, LANES), F32),
                        pltpu.VMEM((1, SHIFT_W), F32),
                        blk(F32), blk(F32), blk(BF16), blk(BF16), blk(BF16), blk(BF16),
                        blk(F32), blk(F32), blk(F32), blk(F32)],
        compiler_params=_cparams(("parallel", "arbitrary")),
    )(us, prev0, s0_bd, prm["mu"], prm["w0"], prm["wd"], prm["a0"], prm["wa"],
      prm["kkp"], prm["ka"], prm["rk"], prm["lnx_g"], prm["lnx_b"])


_AUG = 8
_PREP_ROWS = 256


def _bias_lanes(cum, key_side):
    ntile = H_B // 2 if key_side else H_B
    width = ntile * LANES
    r = lax.broadcasted_iota(jnp.int32, (LANES, width), 0)
    c = lax.broadcasted_iota(jnp.int32, (LANES, width), 1)
    lane = lax.broadcasted_iota(jnp.int32, (1, width), 1)
    f_off, one_off = (0, 3) if key_side else (3, 0)
    sign = -1.0 if key_side else 1.0
    head_of_tile = (c // LANES) * 2 + (c % LANES) // _AUG if key_side else c // LANES
    slot = c % _AUG
    in_head_lanes = (c % LANES) // _AUG == r % 2
    one_head = (lane % LANES) // _AUG < 2 if key_side else (lane % LANES) // _AUG == (lane // LANES) % 2
    out = jnp.where(one_head & (lane % _AUG >= one_off) & (lane % _AUG < one_off + 3), 1.0, 0.0)
    for t, part in enumerate(_split3(cum)):
        sel = jnp.where((head_of_tile == r) & in_head_lanes & (slot == f_off + t), sign, 0.0).astype(BF16)
        out = out + _dot(part, sel)
    return [out[:, i * LANES:(i + 1) * LANES] for i in range(ntile)]


def _stage_rows(total):
    return [(s, min(_PREP_ROWS, total - s)) for s in range(0, total, _PREP_ROWS)]


def _transpose_rows(x):
    n = x.shape[0]
    n_pad = -n % LANES
    if n_pad:
        x = jnp.concatenate([x, jnp.zeros((n_pad, x.shape[1]), x.dtype)], axis=0)
    return x.T[:, :n]


def _fox_kernel(*refs, tq, tk, past, single, new_time_on_lanes):
    if past:
        q_ref, kc_ref, vc_ref, k_ref, v_ref, cq_ref, ck_ref, o_ref, ka_scr, vt_scr, acc_scr = refs
    else:
        q_ref, k_ref, v_ref, cq_ref, ck_ref, o_ref, ka_scr, vt_scr, acc_scr = refs
    npair = H_B // 2
    lane1 = lax.broadcasted_iota(jnp.int32, (1, LANES), 1)
    head_of_lane = lane1 // HEAD_DIM

    def stage_bias(s, n):
        cum = ck_ref[0, s:s + n, :] * LOG2E
        for p, aug in enumerate(_bias_lanes(cum, True)):
            ka_scr[p, s:s + n, LANES:2 * LANES] = aug.astype(BF16)

    def stage_kv():
        for s, n in _stage_rows(past):
            for p in range(npair):
                ka_scr[p, s:s + n, 0:LANES] = kc_ref[0, p * LANES:(p + 1) * LANES, s:s + n].T.astype(BF16)
            vt_scr[:, s:s + n] = vc_ref[0, :, s:s + n].astype(BF16)
            stage_bias(s, n)
        if new_time_on_lanes:
            for s, n in _stage_rows(k_ref.shape[2]):
                for p in range(npair):
                    ka_scr[p, past + s:past + s + n, 0:LANES] = (
                        k_ref[0, p * LANES:(p + 1) * LANES, s:s + n].T.astype(BF16))
                vt_scr[:, past + s:past + s + n] = v_ref[0, :, s:s + n].astype(BF16)
                stage_bias(past + s, n)
            return
        for s, n in _stage_rows(k_ref.shape[1]):
            kb = k_ref[0, s:s + n, :].astype(BF16)
            for p in range(npair):
                ka_scr[p, past + s:past + s + n, 0:LANES] = kb[:, p * LANES:(p + 1) * LANES]
            vt_scr[:, past + s:past + s + n] = _transpose_rows(v_ref[0, s:s + n, :]).astype(BF16)
            stage_bias(past + s, n)

    if single:
        i = 0
        stage_kv()
    else:
        i = pl.program_id(1)
        pl.when(i == 0)(stage_kv)

    q = q_ref[0].astype(F32) * (HEAD_DIM ** -0.5 * LOG2E)
    cq = cq_ref[0] * LOG2E
    rhs = []
    q_bias = _bias_lanes(cq, False)
    for p in range(npair):
        q_p = q[:, p * LANES:(p + 1) * LANES]
        halves = [jnp.concatenate([jnp.where(head_of_lane == hh, q_p, 0.0), q_bias[2 * p + hh]], axis=1)
                  for hh in range(2)]
        rhs.append(jnp.concatenate(halves, axis=0).astype(BF16))

    n_full = past // tk + i * (tq // tk)
    diag0 = past if single else pl.multiple_of(past + i * tq, tq)
    krow = lax.broadcasted_iota(jnp.int32, (tq, 2 * tq), 0)
    qcol = lax.broadcasted_iota(jnp.int32, (tq, 2 * tq), 1)
    causal_t = krow <= jnp.where(qcol >= tq, qcol - tq, qcol)

    acc_scr[...] = jnp.zeros(acc_scr.shape, F32)

    def update(carry, start, size, mask):
        ms, ls = carry
        st = [_dot_nt(ka_scr[p, pl.ds(start, size), :], rhs[p]) for p in range(npair)]
        if mask is not None:
            st = [jnp.where(mask, s, NEG_INF) for s in st]
        m_new = [jnp.maximum(m, jnp.max(s, axis=0, keepdims=True)) for m, s in zip(ms, st)]
        alpha = [jnp.exp2(m - mn) for m, mn in zip(ms, m_new)]
        pt = [jnp.exp2(s - mn) for s, mn in zip(st, m_new)]
        l_new = [a * l + jnp.sum(x, axis=0, keepdims=True) for a, l, x in zip(alpha, ls, pt)]
        pv = [_dot(vt_scr[p * LANES:(p + 1) * LANES, pl.ds(start, size)], pt[p].astype(BF16))
              for p in range(npair)]
        for p in range(npair):
            acc_scr[p] = acc_scr[p] * alpha[p] + pv[p]
        return tuple(m_new), tuple(l_new)

    def body(j, carry):
        return update(carry, pl.multiple_of(j * tk, tk), tk, None)

    init = (tuple(jnp.full((1, 2 * tq), NEG_INF, F32) for _ in range(npair)),
            tuple(jnp.zeros((1, 2 * tq), F32) for _ in range(npair)))
    carry = lax.fori_loop(0, n_full, body, init)
    _, ls = update(carry, diag0, tq, causal_t)
    for p in range(npair):
        o_t = (acc_scr[p] / ls[p]).T
        o_ref[0, :, p * LANES:(p + 1) * LANES] = jnp.where(
            head_of_lane == 0, o_t[:tq], o_t[tq:]).astype(o_ref.dtype)


def _kv_specs(k_new, cache, layer, width, t_new, new_time_on_lanes=False, new_layer=None):
    specs, args = [], []
    new_layer = layer if new_layer is None else new_layer
    if cache is not None:
        past = cache[0].shape[3]
        for c in cache:
            specs.append(pl.BlockSpec((None, 1, width, past), lambda i, j: (layer, i, 0, 0)))
            args.append(c)
    for a in k_new:
        if new_time_on_lanes:
            specs.append(pl.BlockSpec((None, 1, width, t_new), lambda i, j: (new_layer, i, 0, 0)))
        else:
            specs.append(pl.BlockSpec((None, 1, t_new, width), lambda i, j: (new_layer, i, 0, 0)))
        args.append(a)
    return specs, args


def _fox(q, kv_new, cache, cum_q, cum_k, *, layer, tq, tk, past, new_time_on_lanes):
    b, t, _ = q.shape
    tkk = past + t
    npair = H_B // 2
    kv_specs, kv_args = _kv_specs(kv_new, cache, layer, W_B, t, new_time_on_lanes)
    return pl.pallas_call(
        functools.partial(_fox_kernel, tq=tq, tk=tk, past=past, single=(t == tq),
                          new_time_on_lanes=new_time_on_lanes),
        grid=(b, t // tq),
        in_specs=[pl.BlockSpec((1, tq, W_B), lambda i, j: (i, j, 0))] + kv_specs + [
                  pl.BlockSpec((1, tq, LANES), lambda i, j: (i, j, 0)),
                  pl.BlockSpec((1, tkk, LANES), lambda i, j: (i, 0, 0))],
        out_specs=pl.BlockSpec((1, tq, W_B), lambda i, j: (i, j, 0)),
        out_shape=jax.ShapeDtypeStruct((b, t, W_B), BF16),
        scratch_shapes=[pltpu.VMEM((npair, tkk, 2 * LANES), BF16), pltpu.VMEM((W_B, tkk), BF16),
                        pltpu.VMEM((npair, LANES, 2 * tq), F32)],
        compiler_params=_cparams(("parallel", "arbitrary")),
    )(q, *kv_args, cum_q, cum_k)


def _sb_kernel(*refs, tq, tk, past, single):
    if past:
        q_ref, kc_ref, vc_ref, k_ref, v_ref, o_ref, kb_scr, vt_scr, acc_scr = refs
    else:
        q_ref, k_ref, v_ref, o_ref, kb_scr, vt_scr, acc_scr = refs
    npair = H_C // 2
    lane1 = lax.broadcasted_iota(jnp.int32, (1, LANES), 1)
    head_of_lane = lane1 // HEAD_DIM

    def stage_kv():
        for s, n in _stage_rows(past):
            kb_scr[s:s + n, :] = kc_ref[0, :, s:s + n].T.astype(BF16)
            vt_scr[:, s:s + n] = vc_ref[0, :, s:s + n].astype(BF16)
        for s, n in _stage_rows(k_ref.shape[1]):
            kb_scr[past + s:past + s + n, :] = k_ref[0, s:s + n, :].astype(BF16)
            vt_scr[:, past + s:past + s + n] = _transpose_rows(v_ref[0, s:s + n, :]).astype(BF16)

    if single:
        i = 0
        stage_kv()
    else:
        i = pl.program_id(1)
        pl.when(i == 0)(stage_kv)

    q = q_ref[0].astype(F32) * (HEAD_DIM ** -0.5 * LOG2E)
    rhs = []
    for p in range(npair):
        q_p = q[:, p * LANES:(p + 1) * LANES]
        rhs.append(jnp.concatenate([jnp.where(head_of_lane == hh, q_p, 0.0) for hh in range(2)],
                                   axis=0).astype(BF16))

    def earlier_matrix(n):
        rr = lax.broadcasted_iota(jnp.int32, (n, n), 0)
        cc = lax.broadcasted_iota(jnp.int32, (n, n), 1)
        return (cc > rr).astype(BF16)

    n_full = past // tk + i * (tq // tk)
    diag0 = past if single else pl.multiple_of(past + i * tq, tq)
    krow = lax.broadcasted_iota(jnp.int32, (tq, 2 * tq), 0)
    qcol = lax.broadcasted_iota(jnp.int32, (tq, 2 * tq), 1)
    strict_t = krow < jnp.where(qcol >= tq, qcol - tq, qcol)

    def update(runs, start, size, mask, first):
        zt = [_dot_nt(kb_scr[pl.ds(start, size), p * LANES:(p + 1) * LANES], rhs[p])
              for p in range(npair)]
        mz = [jnp.minimum(z, 0.0) for z in zt]
        tail = [jnp.log(1.0 + jnp.exp2(m + m - z)) * LOG2E for m, z in zip(mz, zt)]
        lsig = [m - t for m, t in zip(mz, tail)]
        l1m = [s - z for s, z in zip(lsig, zt)]
        if mask is not None:
            l1m = [jnp.where(mask, x, 0.0) for x in l1m]
        later = earlier_matrix(size)
        after = [_dot(later, x.astype(BF16)) for x in l1m]
        wt = [jnp.exp2(s + a + r) for s, a, r in zip(lsig, after, runs)]
        if mask is not None:
            wt = [jnp.where(mask, w, 0.0) for w in wt]
        pv = [_dot(vt_scr[p * LANES:(p + 1) * LANES, pl.ds(start, size)], wt[p].astype(BF16))
              for p in range(npair)]
        for p in range(npair):
            acc_scr[p] = pv[p] if first else acc_scr[p] + pv[p]
        return tuple(r + jnp.sum(x, axis=0, keepdims=True) for r, x in zip(runs, l1m))

    runs = tuple(jnp.zeros((1, 2 * tq), F32) for _ in range(npair))
    runs = update(runs, diag0, tq, strict_t, True)

    def body(jj, runs):
        j = n_full - 1 - jj
        return update(runs, pl.multiple_of(j * tk, tk), tk, None, False)

    lax.fori_loop(0, n_full, body, runs)
    for p in range(npair):
        o_t = acc_scr[p].T
        o_ref[0, :, p * LANES:(p + 1) * LANES] = jnp.where(
            head_of_lane == 0, o_t[:tq], o_t[tq:]).astype(o_ref.dtype)


def _sb(q, kv_new, cache, *, layer, tq, tk, past):
    b, t, _ = q.shape
    tkk = past + kv_new[0].shape[2]
    npair = H_C // 2
    kv_specs, kv_args = _kv_specs(kv_new, cache, layer, W_C, t, new_layer=0)
    return pl.pallas_call(
        functools.partial(_sb_kernel, tq=tq, tk=tk, past=past, single=(t == tq)),
        grid=(b, t // tq),
        in_specs=[pl.BlockSpec((1, tq, W_C), lambda i, j: (i, j, 0))] + kv_specs,
        out_specs=pl.BlockSpec((1, tq, W_C), lambda i, j: (i, j, 0)),
        out_shape=jax.ShapeDtypeStruct((b, t, W_C), BF16),
        scratch_shapes=[pltpu.VMEM((tkk, W_C), BF16), pltpu.VMEM((W_C, tkk), BF16),
                        pltpu.VMEM((npair, LANES, 2 * tq), F32)],
        compiler_params=_cparams(("parallel", "arbitrary")),
    )(q, *kv_args)


def _out_kernel(oa_ref, ob_ref, oc_ref, g_ref, x_ref, w_ref, lng_ref, lnb_ref, y_ref, *, alpha):
    g = g_ref[...].astype(F32)
    gate = g * _sigmoid(g)
    gated = lambda o_ref, lo, hi: (o_ref[...].astype(F32) * gate[:, lo:hi]).astype(BF16)
    acc = _dot(gated(oa_ref, 0, W_A), w_ref[0:W_A, :])
    acc = acc + _dot(gated(ob_ref, W_A, W_A + W_B), w_ref[W_A:W_A + W_B, :])
    acc = acc + _dot(gated(oc_ref, W_A + W_B, W_MIX), w_ref[W_A + W_B:, :])
    z = alpha * x_ref[...] + acc
    mu = jnp.mean(z, axis=-1, keepdims=True)
    d = z - mu
    var = jnp.mean(d * d, axis=-1, keepdims=True)
    y_ref[...] = d * lax.rsqrt(var + LN_EPS) * lng_ref[...] + lnb_ref[...]


def _out(oa, ob, oc, g, x2d, w_out, ln_g, ln_b, alpha, tm):
    n, d = x2d.shape
    row = lambda w: pl.BlockSpec((tm, w), lambda i: (i, 0))
    return pl.pallas_call(
        functools.partial(_out_kernel, alpha=alpha),
        grid=(n // tm,),
        in_specs=[row(W_A), row(W_B), row(W_C), row(W_MIX), row(d),
                  pl.BlockSpec((W_MIX, d), lambda i: (0, 0)),
                  pl.BlockSpec((1, d), lambda i: (0, 0)),
                  pl.BlockSpec((1, d), lambda i: (0, 0))],
        out_specs=row(d),
        out_shape=jax.ShapeDtypeStruct((n, d), F32),
        compiler_params=_cparams(("parallel",)),
    )(oa, ob, oc, g, x2d, w_out, ln_g, ln_b)


def _pad_lanes(a, width=LANES):
    return jnp.pad(a, [(0, 0)] * (a.ndim - 1) + [(0, width - a.shape[-1])])


def _state_to_pairs(s):
    b = s.shape[0]
    odd_head = (jnp.arange(H_A) % 2 == 1).reshape(1, H_A, 1, 1)
    upper_half = (jnp.arange(LANES) >= HEAD_DIM).reshape(1, 1, 1, LANES)
    wide = jnp.where(odd_head == upper_half, jnp.concatenate([s, s], axis=-1), 0.0)
    return wide.reshape(b, H_A // 2, LANES, LANES)


def _pairs_to_state(sp):
    b = sp.shape[0]
    wide = sp.reshape(b, H_A, HEAD_DIM, LANES)
    odd_head = (jnp.arange(H_A) % 2 == 1).reshape(1, H_A, 1, 1)
    return jnp.where(odd_head, wide[..., HEAD_DIM:], wide[..., :HEAD_DIM])


_IN_SIZES = (SHIFT_W, W_A, W_B, W_B, W_B, H_B, W_B, W_C, W_C, W_C, W_C)
_IN_OFFS = tuple(int(v) for v in np.concatenate([[0], np.cumsum(_IN_SIZES)]))
_SRC_SHIFT, _SRC_GA, _SRC_QB, _SRC_KB, _SRC_VB, _SRC_F, _SRC_GB, _SRC_QC, _SRC_KC, _SRC_VC, _SRC_GC = (
    _IN_OFFS[:-1])
_W_MOVES = ((_SRC_SHIFT, _OFF_SHIFT, SHIFT_W), (_SRC_GA, _OFF_G, W_A), (_SRC_GB, _OFF_G + W_A, W_B),
            (_SRC_GC, _OFF_G + W_A + W_B, W_C), (_SRC_QB, _OFF_QB, W_B), (_SRC_KB, _OFF_KB, W_B),
            (_SRC_VB, _OFF_VB, W_B), (_SRC_QC, _OFF_QC, W_C), (_SRC_KC, _OFF_KC, W_C),
            (_SRC_VC, _OFF_VC, W_C))
_W_STAGE_ROWS = 128


def _stage_w_kernel(w_ref, o_ref):
    for src, dst, width in _W_MOVES:
        o_ref[:, dst:dst + width] = w_ref[:, src:src + width].astype(BF16)
    lane = lax.broadcasted_iota(jnp.int32, (1, LANES), 1)
    o_ref[:, _OFF_F:_OFF_F + LANES] = jnp.where(
        lane < H_B, w_ref[:, _SRC_F:_SRC_F + LANES], 0.0).astype(BF16)


def _stage_w(w_in, layer):
    _, d, cols = w_in.shape
    return pl.pallas_call(
        _stage_w_kernel,
        grid=(d // _W_STAGE_ROWS,),
        in_specs=[pl.BlockSpec((None, _W_STAGE_ROWS, cols), lambda i: (layer, i, 0))],
        out_specs=pl.BlockSpec((_W_STAGE_ROWS, _PROJ_COLS), lambda i: (i, 0)),
        out_shape=jax.ShapeDtypeStruct((d, _PROJ_COLS), BF16),
        compiler_params=_cparams(("parallel",)),
    )(w_in)


def _layer_params(l, w_in, mu_shift, w0_decay, w_decay, a0, w_aaa, k_k, k_a, r_k,
                  lnx_g, lnx_b, fox_fb, w_out, ln_g, ln_b):
    w_cat = _stage_w(w_in, l)
    zeros = jnp.zeros((D_LORA, W_A), F32)
    row = lambda a: a.reshape(1, -1).astype(F32)
    return dict(
        w_cat=w_cat, fb=_pad_lanes(row(fox_fb[l])),
        mu=row(mu_shift[l]), w0=row(w0_decay[l]),
        wd=jnp.concatenate([w_decay[l], zeros], axis=0).astype(BF16),
        a0=row(a0[l]), wa=jnp.concatenate([zeros, w_aaa[l]], axis=0).astype(BF16),
        kkp=row(k_k[l]), ka=row(k_a[l]), rk=row(r_k[l]), lnx_g=row(lnx_g[l]), lnx_b=row(lnx_b[l]),
        w_out=w_out[l].astype(BF16), ln_g=row(ln_g[l]), ln_b=row(ln_b[l]))


def _run_layer(x, hist, prm, alpha, layer, depth, stacks):
    b, t, d = x.shape
    n = b * t
    x2d = x.reshape(n, d)
    tm = min(ROW_BLOCK, n)
    fox_t = hist is None
    us, g, qb, kb_st, vb_st, qc, kc_st, vc_st, lf, kc_rows, vc_rows, lf_t = _proj(
        x2d, prm["w_cat"], prm["fb"], tm, layer, depth, stacks, t, fox_t)
    r3 = lambda a: a.reshape(b, t, a.shape[-1])
    us, qb, qc, lf = (r3(a) for a in (us, qb, qc, lf))
    r4 = lambda a: a.reshape(depth, b, t, a.shape[-1])

    if hist is None:
        past = 0
        prev0 = jnp.zeros((b, 1, SHIFT_W), F32)
        s0 = jnp.zeros((b, H_A // 2, LANES, LANES), F32)
        cache_b = cache_c = None
        lf_all = lf
        fox_blk = (min(FOX_BLOCK, t),) * 2
        sb_blk = (min(SB_BLOCK, t),) * 2
    else:
        fk_t, fv_t, h_lf, sk_t, sv_t, h_wkv, h_shift = hist
        past = fk_t.shape[3]
        prev0 = h_shift
        s0 = _state_to_pairs(h_wkv)
        cache_b, cache_c = (fk_t, fv_t), (sk_t, sv_t)
        lf_all = jnp.concatenate([_pad_lanes(h_lf), lf], axis=1)
        fox_blk = sb_blk = (t, min(SB_BLOCK, past))

    oa, s_fin = _wkv(us, prev0, s0, prm, min(WKV_BLOCK, t))
    cum = _cumsum(lf_all)
    fox_kv = (kb_st, vb_st) if fox_t else (r4(kb_st), r4(vb_st))
    ob = _fox(qb, fox_kv, cache_b, cum[:, past:], cum, layer=layer,
              tq=fox_blk[0], tk=fox_blk[1], past=past, new_time_on_lanes=fox_t)
    sb_kv = (kc_rows.reshape(1, b, t, W_C), vc_rows.reshape(1, b, t, W_C))
    oc = _sb(qc, sb_kv, cache_c, layer=layer, tq=sb_blk[0], tk=sb_blk[1], past=past)
    y = _out(oa.reshape(n, W_A), ob.reshape(n, W_B), oc.reshape(n, W_C), g, x2d,
             prm["w_out"], prm["ln_g"], prm["ln_b"], alpha, min(OUT_BLOCK, n))
    small = (lf_t[:H_B].reshape(H_B, b, t), _pairs_to_state(s_fin), us[:, -1:, :])
    return y.reshape(b, t, d), (kb_st, vb_st, kc_st, vc_st), small


def kernel(x_prompt, x_sample, cache_fox_k, cache_fox_v, cache_fox_logf, cache_sb_k, cache_sb_v, state_wkv, state_shift, w_in, mu_shift, w0_decay, w_decay, a0, w_aaa, k_k, k_a, r_k, lnx_g, lnx_b, fox_fb, w_out, ln_g, ln_b):
    depth = w_in.shape[0]
    alpha = (2 * depth) ** 0.25
    yp, ys = x_prompt, x_sample

    def time_on_lanes(cache):
        nl, nb, past, nh, hd = cache.shape
        return jnp.transpose(cache, (0, 1, 3, 4, 2)).reshape(nl, nb, nh * hd, past)

    fk_t, fv_t, sk_t, sv_t = (time_on_lanes(c) for c in (cache_fox_k, cache_fox_v, cache_sb_k, cache_sb_v))
    stacks_p, stacks_s = None, None
    small_p, small_s = [], []
    for l in range(depth):
        prm = _layer_params(l, w_in, mu_shift, w0_decay, w_decay, a0, w_aaa, k_k, k_a, r_k,
                            lnx_g, lnx_b, fox_fb, w_out, ln_g, ln_b)
        yp, stacks_p, sm_p = _run_layer(yp, None, prm, alpha, l, depth, stacks_p)
        hist = (fk_t, fv_t, cache_fox_logf[l], sk_t, sv_t, state_wkv[l], state_shift[l])
        ys, stacks_s, sm_s = _run_layer(ys, hist, prm, alpha, l, depth, stacks_s)
        small_p.append(sm_p)
        small_s.append(sm_s)

    def group_outputs(x, stacks, small, fox_t):
        b, t, _ = x.shape
        kb, vb, kc, vc = stacks
        logf, wkv, shift = (jnp.stack([sm[i] for sm in small]) for i in range(3))
        logf = jnp.transpose(logf, (0, 2, 3, 1))
        if fox_t:
            kb, vb = (jnp.transpose(a.reshape(depth, b, H_B, HEAD_DIM, t), (0, 1, 4, 2, 3)) for a in (kb, vb))
        return (kb.reshape(depth, b, t, H_B, HEAD_DIM), vb.reshape(depth, b, t, H_B, HEAD_DIM), logf,
                kc.reshape(depth, b, t, H_C, HEAD_DIM), vc.reshape(depth, b, t, H_C, HEAD_DIM),
                wkv, shift)

    return ((yp, ys) + group_outputs(x_prompt, stacks_p, small_p, True)
            + group_outputs(x_sample, stacks_s, small_s, False))
```

```python
import functools

import numpy as np
import jax
import jax.numpy as jnp
from jax import lax
from jax.experimental import pallas as pl
from jax.experimental.pallas import tpu as pltpu

F32 = jnp.float32
BF16 = jnp.bfloat16

HEAD_DIM = 64
H_A, H_B, H_C = 6, 6, 4
W_A, W_B, W_C = H_A * HEAD_DIM, H_B * HEAD_DIM, H_C * HEAD_DIM
W_MIX = W_A + W_B + W_C
D_LORA = 64
SHIFT_W = 3 * W_A + 2 * D_LORA
LANES = 128
WKV_CHUNK = 64
WKV_BLOCK = 1024
WKV_SUB = 256
ROW_BLOCK = 1024
OUT_BLOCK = 1024
FOX_BLOCK = 512
SB_BLOCK = 512
GN_EPS = 64e-5
LN_EPS = 1e-5
NEG_INF = -1e30
LOG2E = 1.4426950408889634
VMEM_LIMIT = 56 * 1024 * 1024


def _cparams(sem):
    return pltpu.CompilerParams(dimension_semantics=sem, vmem_limit_bytes=VMEM_LIMIT)


def _dot(a, b):
    return jnp.dot(a, b, preferred_element_type=F32)


def _dot_nt(a, b):
    return lax.dot_general(a, b, (((1,), (1,)), ((), ())), preferred_element_type=F32)


def _dot_tn(a, b):
    return lax.dot_general(a, b, (((0,), (0,)), ((), ())), preferred_element_type=F32)


def _split3(x):
    hi = x.astype(BF16)
    r1 = x - hi.astype(F32)
    mid = r1.astype(BF16)
    lo = (r1 - mid.astype(F32)).astype(BF16)
    return hi, mid, lo


def _softplus(x):
    return jnp.maximum(x, 0.0) + jnp.log1p(jnp.exp(-jnp.abs(x)))


def _sigmoid(x):
    return 1.0 / (1.0 + jnp.exp(-x))


_OFF_SHIFT = 0
_OFF_G = _OFF_SHIFT + SHIFT_W
_OFF_QB = _OFF_G + W_MIX
_OFF_KB = _OFF_QB + W_B
_OFF_VB = _OFF_KB + W_B
_OFF_QC = _OFF_VB + W_B
_OFF_KC = _OFF_QC + W_C
_OFF_VC = _OFF_KC + W_C
_OFF_F = _OFF_VC + W_C
_PROJ_COLS = _OFF_F + LANES


def _proj_kernel(x_ref, w_ref, fb_ref, *refs):
    (us_ref, g_ref, qb_ref, kb_ref, vb_ref, qc_ref, kc_ref, vc_ref, lf_ref,
     kc_rows_ref, vc_rows_ref, lft_ref) = refs[-12:]
    xb = x_ref[...].astype(BF16)
    tm = xb.shape[0]
    for ref, off in ((us_ref, _OFF_SHIFT), (g_ref, _OFF_G), (qb_ref, _OFF_QB),
                     (kb_ref, _OFF_KB), (vb_ref, _OFF_VB), (qc_ref, _OFF_QC)):
        time_on_lanes = ref.shape[-1] == tm and ref.shape[0] != tm
        width = ref.shape[0] if time_on_lanes else ref.shape[-1]
        res = _dot(xb, w_ref[:, off:off + width])
        ref[...] = (res.T if time_on_lanes else res).astype(ref.dtype)
    for ref, rows_ref, off in ((kc_ref, kc_rows_ref, _OFF_KC), (vc_ref, vc_rows_ref, _OFF_VC)):
        res = _dot(xb, w_ref[:, off:off + W_C])
        rows_ref[...] = res.astype(rows_ref.dtype)
        for h in range(H_C):
            ref[pl.ds(h, tm, stride=H_C), :] = res[:, h * HEAD_DIM:(h + 1) * HEAD_DIM]
    f = _dot(xb, w_ref[:, _OFF_F:_OFF_F + LANES]) + fb_ref[...]
    lf = -_softplus(-f)
    lf_ref[...] = lf
    lft_ref[...] = lf.T[:lft_ref.shape[0], :]


_PROJ_WIDTHS = (SHIFT_W, W_MIX, W_B, W_B, W_B, W_C, W_C, W_C, LANES, W_C, W_C, 8)
_PROJ_LOGF_T = 11
_PROJ_BF16 = (1, 2, 5, 9, 10)
_PROJ_SB_KV = (6, 7)
_PROJ_STACKED = (3, 4, 6, 7)
_PROJ_FOX_KV = (3, 4)


def _proj(x2d, w_cat, fb_pad, tm, layer, depth, stacks, seq_len, fox_time_on_lanes):
    n, d = x2d.shape
    assert tm != W_B
    per_row = seq_len // tm
    out_specs, out_shape = [], []
    for idx, w in enumerate(_PROJ_WIDTHS):
        if idx in _PROJ_FOX_KV and fox_time_on_lanes:
            out_specs.append(pl.BlockSpec((None, None, w, tm),
                                          lambda i: (layer, i // per_row, 0, i % per_row)))
            out_shape.append(jax.ShapeDtypeStruct((depth, n // seq_len, w, seq_len), F32))
        elif idx == _PROJ_LOGF_T:
            out_specs.append(pl.BlockSpec((w, tm), lambda i: (0, i)))
            out_shape.append(jax.ShapeDtypeStruct((w, n), F32))
        elif idx in _PROJ_SB_KV:
            out_specs.append(pl.BlockSpec((None, tm * H_C, HEAD_DIM), lambda i: (layer, i, 0)))
            out_shape.append(jax.ShapeDtypeStruct((depth, n * H_C, HEAD_DIM), F32))
        elif idx in _PROJ_STACKED:
            out_specs.append(pl.BlockSpec((None, tm, w), lambda i: (layer, i, 0)))
            out_shape.append(jax.ShapeDtypeStruct((depth, n, w), F32))
        else:
            out_specs.append(pl.BlockSpec((tm, w), lambda i: (i, 0)))
            out_shape.append(jax.ShapeDtypeStruct((n, w), BF16 if idx in _PROJ_BF16 else F32))
    in_specs = [pl.BlockSpec((tm, d), lambda i: (i, 0)),
                pl.BlockSpec((d, _PROJ_COLS), lambda i: (0, 0), pipeline_mode=pl.Buffered(1)),
                pl.BlockSpec((1, LANES), lambda i: (0, 0))]
    args = [x2d, w_cat, fb_pad]
    aliases = {}
    if stacks is not None:
        for j, (idx, st) in enumerate(zip(_PROJ_STACKED, stacks)):
            in_specs.append(pl.BlockSpec(memory_space=pl.ANY))
            args.append(st)
            aliases[3 + j] = idx
    return pl.pallas_call(
        _proj_kernel,
        grid=(n // tm,),
        in_specs=in_specs,
        out_specs=out_specs,
        out_shape=out_shape,
        input_output_aliases=aliases,
        compiler_params=_cparams(("parallel",)),
    )(*args)


def _cumsum_kernel(lf_ref, cum_ref, *, blk):
    tk = lf_ref.shape[1]
    row = lax.broadcasted_iota(jnp.int32, (blk, blk), 0)
    col = lax.broadcasted_iota(jnp.int32, (blk, blk), 1)
    tri = (col <= row).astype(BF16)
    carry = jnp.zeros((1, LANES), F32)
    for s in range(0, tk, blk):
        x = lf_ref[0, s:s + blk, :]
        hi, mid, lo = _split3(x)
        c = _dot(tri, hi) + _dot(tri, mid) + _dot(tri, lo) + carry
        cum_ref[0, s:s + blk, :] = c
        carry = c[blk - 1:blk, :]


def _cumsum(lf, blk=64):
    b, tk, _ = lf.shape
    return pl.pallas_call(
        functools.partial(_cumsum_kernel, blk=blk),
        grid=(b,),
        in_specs=[pl.BlockSpec((1, tk, LANES), lambda i: (i, 0, 0))],
        out_specs=pl.BlockSpec((1, tk, LANES), lambda i: (i, 0, 0)),
        out_shape=jax.ShapeDtypeStruct((b, tk, LANES), F32),
        compiler_params=_cparams(("parallel",)),
    )(lf)


def _seg_sum(x, bd):
    xb = x.astype(BF16)
    return jnp.concatenate([_dot(xb[:, s:s + LANES], bd) for s in range(0, x.shape[1], LANES)], axis=1)


def _alternate(*stages):
    gens = [g for g in stages if g is not None]
    while gens:
        for g in list(gens):
            try:
                next(g)
            except StopIteration:
                gens.remove(g)


def _wkv_kernel(us_ref, prev0_ref, s0_ref, mu_ref, w0_ref, wd_ref, a0_ref, wa_ref,
                kkp_ref, ka_ref, rk_ref, lng_ref, lnb_ref,
                oa_ref, sout_ref,
                s_scr, prev_scr, at_s, rt_s, bh_s, kh_s, be_s, ke_s, v_s, ga_s, bn_s, y_s, *, nsub):
    c_idx = pl.program_id(1)
    nc = pl.num_programs(1)
    C = WKV_CHUNK
    tt = us_ref.shape[1]
    ts = tt // nsub
    nb = ts // C
    npair = H_A // 2
    G = 4 * HEAD_DIM

    @pl.when(c_idx == 0)
    def _():
        s_scr[...] = s0_ref[0]
        prev_scr[...] = prev0_ref[0]

    lane1 = lax.broadcasted_iota(jnp.int32, (1, LANES), 1)
    lane_w = lax.broadcasted_iota(jnp.int32, (LANES, LANES), 0) // HEAD_DIM
    lane_c = lax.broadcasted_iota(jnp.int32, (LANES, LANES), 1) // HEAD_DIM
    bd_pair = lane_w == lane_c
    bd = bd_pair.astype(BF16)
    row_g = lax.broadcasted_iota(jnp.int32, (C, G), 0)
    lane_g = lax.broadcasted_iota(jnp.int32, (C, G), 1) % C
    tri_incl = lane_g <= row_g
    tri_strict = lane_g < row_g
    eye = (lane_g == row_g).astype(F32)
    diag4 = (lax.broadcasted_iota(jnp.int32, (G, G), 0) // HEAD_DIM
             == lax.broadcasted_iota(jnp.int32, (G, G), 1) // HEAD_DIM).astype(BF16)
    rowt = lax.broadcasted_iota(jnp.int32, (ts, ts), 0)
    colt = lax.broadcasted_iota(jnp.int32, (ts, ts), 1)
    tri_b = (((rowt // C) == (colt // C)) & (colt <= rowt)).astype(BF16)
    row1 = lax.broadcasted_iota(jnp.int32, (ts, 1), 0)

    def stack4(x):
        xb = x.astype(BF16)
        return jnp.concatenate([xb, xb, xb, xb], axis=0) * diag4

    def mul4(xs, ys):
        return [_dot(x.astype(BF16), stack4(y)) for x, y in zip(xs, ys)]

    def prepare(sb):
        base = sb * ts
        u = us_ref[0, base:base + ts, :]
        before = prev_scr[...] if sb == 0 else us_ref[0, base - 1:base, :]
        zprev = jnp.where(row1 == 0, before, pltpu.roll(u, 1, axis=0))
        zs = u + (zprev - u) * mu_ref[...]
        r = zs[:, 0:W_A]
        k = zs[:, W_A:2 * W_A]
        v = zs[:, 2 * W_A:3 * W_A]
        lora_in = zs[:, 3 * W_A:3 * W_A + LANES]
        lora_t = jnp.where(lane1 < D_LORA, jnp.tanh(lora_in), lora_in).astype(BF16)
        yield
        wl = w0_ref[...] + _dot(lora_t, wd_ref[...])
        a = _sigmoid(a0_ref[...] + _dot(lora_t, wa_ref[...]))
        w_log = -_softplus(-wl) - 0.5
        lw = -jnp.exp(w_log)
        yield
        kk = k * kkp_ref[...]
        kk = kk * lax.rsqrt(_seg_sum(kk * kk, bd) + 1e-12)
        k2 = k * (1.0 + (a - 1.0) * ka_ref[...])
        bn_s[sb] = _seg_sum(r * k2 * rk_ref[...], bd) * v
        v_s[sb] = v
        yield
        h1, h2, h3 = _split3(lw)
        cs = _dot(tri_b, h1) + _dot(tri_b, h2) + _dot(tri_b, h3)
        tot = jnp.concatenate([jnp.broadcast_to(cs[c * C + C - 1:c * C + C, :], (C, W_A))
                               for c in range(nb)], axis=0)
        yield
        g_inv = jnp.exp(-cs)
        g_end = jnp.exp(tot - cs)
        kka = kk * a
        at_s[sb] = kk * jnp.exp(cs - lw)
        rt_s[sb] = r * jnp.exp(cs)
        yield
        bh_s[sb] = (-kka * g_inv).astype(BF16)
        kh_s[sb] = (k2 * g_inv).astype(BF16)
        be_s[sb] = (-kka * g_end).astype(BF16)
        ke_s[sb] = (k2 * g_end).astype(BF16)
        ga_s[sb] = jnp.exp(tot)
        yield

    local = {}

    def chunk_terms(sb):
        groups = []
        for c in range(nb):
            groups.append([(c, 0), (c, LANES)])
        for c in range(0, nb, 2):
            groups.append([(c, 2 * LANES)] + ([(c + 1, 2 * LANES)] if c + 1 < nb else [None]))

        def gather(ref, grp):
            parts = [jnp.zeros((C, LANES), ref.dtype) if u is None
                     else ref[sb, u[0] * C:(u[0] + 1) * C, u[1]:u[1] + LANES] for u in grp]
            return jnp.concatenate(parts, axis=1)

        at_g = [gather(at_s, g) for g in groups]
        rt_g = [gather(rt_s, g) for g in groups]
        v_g = [gather(v_s, g) for g in groups]
        lhs = [jnp.concatenate([a, r], axis=0).astype(BF16) for a, r in zip(at_g, rt_g)]
        rhs = [jnp.concatenate([stack4(gather(bh_s, g)), stack4(gather(kh_s, g))], axis=0) for g in groups]
        a4 = [_dot_nt(l, r) for l, r in zip(lhs, rhs)]
        yield
        n1 = [jnp.where(tri_strict, a[:C, :G], 0.0) for a in a4]
        a_ak = [jnp.where(tri_strict, a[:C, G:], 0.0) for a in a4]
        a_rb = [jnp.where(tri_incl, a[C:, :G], 0.0) for a in a4]
        a_rk = [jnp.where(tri_incl, a[C:, G:], 0.0) for a in a4]
        vst = [stack4(v) for v in v_g]
        avy = [_dot(jnp.concatenate([ak, rk], axis=0).astype(BF16), vs)
               for ak, rk, vs in zip(a_ak, a_rk, vst)]
        yield
        pair4 = lambda xs, ys: [eye + x + y + m for x, y, m in zip(xs, ys, mul4(xs, ys))]
        n2 = mul4(n1, n1)
        yield
        n4 = mul4(n2, n2)
        p1 = pair4(n1, n2)
        yield
        n8 = mul4(n4, n4)
        yield
        n16 = mul4(n8, n8)
        p2 = pair4(n4, n8)
        yield
        n32 = mul4(n16, n16)
        p12 = mul4(p1, p2)
        yield
        p3 = pair4(n16, n32)
        yield
        t_inv = mul4(p12, p3)
        yield
        xs = [_dot(t.astype(BF16), jnp.concatenate([stack4(a), stack4(y[:C])], axis=1))
              for t, a, y in zip(t_inv, at_g, avy)]
        yield
        rys = [_dot(rb.astype(BF16), jnp.concatenate([stack4(x[:, :G]), stack4(x[:, G:])], axis=1))
               for rb, x in zip(a_rb, xs)]
        rp_g = [r + ry[:, :G] for r, ry in zip(rt_g, rys)]
        y0_g = [ry[:, G:] + y[C:] for ry, y in zip(rys, avy)]
        yield
        for gi, grp in enumerate(groups):
            for kpos, u in enumerate(grp):
                if u is None:
                    continue
                c, p = u[0], u[1] // LANES
                rows = slice(c * C, (c + 1) * C)
                sl = slice(p * LANES, (p + 1) * LANES)
                gl = slice(kpos * LANES, (kpos + 1) * LANES)
                be_u, ke_u = be_s[sb, rows, sl], ke_s[sb, rows, sl]
                atp = xs[gi][:, gl].astype(BF16)
                wv = jnp.concatenate([xs[gi][:, G + kpos * LANES:G + (kpos + 1) * LANES],
                                      v_s[sb, rows, sl]], axis=0).astype(BF16)
                pp = jnp.where(bd_pair, _dot_tn(atp, be_u), 0.0).astype(BF16)
                q = jnp.where(bd_pair, _dot_tn(wv, jnp.concatenate([be_u, ke_u], axis=0)), 0.0)
                local[sb, c, p] = (rp_g[gi][:, gl].astype(BF16), y0_g[gi][:, gl], pp, q)
        yield

    def sweep(sb):
        for c in range(nb):
            rows = slice(c * C, (c + 1) * C)
            s_ps = [s_scr[p] for p in range(npair)]
            s_bs = [s_p.astype(BF16) for s_p in s_ps]
            for p in range(npair):
                sl = slice(p * LANES, (p + 1) * LANES)
                rp, y0, pp, q = local[sb, c, p]
                y_s[sb, rows, sl] = _dot_nt(rp, s_bs[p]) + y0
                s_scr[p] = s_ps[p] * ga_s[sb, c * C:c * C + 1, sl] + _dot(s_bs[p], pp) + q
            yield
        y = y_s[sb]
        inv_n = 1.0 / HEAD_DIM
        mean = _seg_sum(y, bd) * inv_n
        d = y - mean
        yield
        var = _seg_sum(d * d, bd) * inv_n
        yn = d * lax.rsqrt(var + GN_EPS) * lng_ref[...] + lnb_ref[...]
        oa_ref[0, sb * ts:(sb + 1) * ts, :] = (yn + bn_s[sb]).astype(oa_ref.dtype)
        yield

    _alternate(prepare(0))
    for sb in range(nsub):
        _alternate(chunk_terms(sb),
                   prepare(sb + 1) if sb + 1 < nsub else None,
                   sweep(sb - 1) if sb > 0 else None)
    _alternate(sweep(nsub - 1))
    prev_scr[...] = us_ref[0, tt - 1:tt, :]

    @pl.when(c_idx == nc - 1)
    def _():
        sout_ref[0] = s_scr[...]


def _wkv(us, prev0, s0_bd, prm, tt):
    b, t, _ = us.shape
    npair = H_A // 2
    nsub = max(1, tt // WKV_SUB)
    ts = tt // nsub
    vec = lambda w: pl.BlockSpec((1, w), lambda i, j: (0, 0))
    mat = lambda: pl.BlockSpec((LANES, W_A), lambda i, j: (0, 0))
    blk = lambda dt: pltpu.VMEM((nsub, ts, W_A), dt)
    return pl.pallas_call(
        functools.partial(_wkv_kernel, nsub=nsub),
        grid=(b, t // tt),
        in_specs=[pl.BlockSpec((1, tt, SHIFT_W), lambda i, j: (i, j, 0)),
                  pl.BlockSpec((1, 1, SHIFT_W), lambda i, j: (i, 0, 0)),
                  pl.BlockSpec((1, npair, LANES, LANES), lambda i, j: (i, 0, 0, 0)),
                  vec(SHIFT_W), vec(W_A), mat(), vec(W_A), mat(),
                  vec(W_A), vec(W_A), vec(W_A), vec(W_A), vec(W_A)],
        out_specs=[pl.BlockSpec((1, tt, W_A), lambda i, j: (i, j, 0)),
                   pl.BlockSpec((1, npair, LANES, LANES), lambda i, j: (i, 0, 0, 0))],
        out_shape=[jax.ShapeDtypeStruct((b, t, W_A), BF16),
                   jax.ShapeDtypeStruct((b, npair, LANES, LANES), F32)],
        scratch_shapes=[pltpu.VMEM((npair, LANES, LANES), F32),
                        pltpu.VMEM((1, SHIFT_W), F32),
                        blk(F32), blk(F32), blk(BF16), blk(BF16), blk(BF16), blk(BF16),
                        blk(F32), blk(F32), blk(F32), blk(F32)],
        compiler_params=_cparams(("parallel", "arbitrary")),
    )(us, prev0, s0_bd, prm["mu"], prm["w0"], prm["wd"], prm["a0"], prm["wa"],
      prm["kkp"], prm["ka"], prm["rk"], prm["lnx_g"], prm["lnx_b"])


_AUG = 8
_PREP_ROWS = 256


def _bias_lanes(cum, key_side):
    ntile = H_B // 2 if key_side else H_B
    width = ntile * LANES
    r = lax.broadcasted_iota(jnp.int32, (LANES, width), 0)
    c = lax.broadcasted_iota(jnp.int32, (LANES, width), 1)
    lane = lax.broadcasted_iota(jnp.int32, (1, width), 1)
    f_off, one_off = (0, 3) if key_side else (3, 0)
    sign = -1.0 if key_side else 1.0
    head_of_tile = (c // LANES) * 2 + (c % LANES) // _AUG if key_side else c // LANES
    slot = c % _AUG
    in_head_lanes = (c % LANES) // _AUG == r % 2
    one_head = (lane % LANES) // _AUG < 2 if key_side else (lane % LANES) // _AUG == (lane // LANES) % 2
    out = jnp.where(one_head & (lane % _AUG >= one_off) & (lane % _AUG < one_off + 3), 1.0, 0.0)
    for t, part in enumerate(_split3(cum)):
        sel = jnp.where((head_of_tile == r) & in_head_lanes & (slot == f_off + t), sign, 0.0).astype(BF16)
        out = out + _dot(part, sel)
    return [out[:, i * LANES:(i + 1) * LANES] for i in range(ntile)]


def _stage_rows(total):
    return [(s, min(_PREP_ROWS, total - s)) for s in range(0, total, _PREP_ROWS)]


def _transpose_rows(x):
    n = x.shape[0]
    n_pad = -n % LANES
    if n_pad:
        x = jnp.concatenate([x, jnp.zeros((n_pad, x.shape[1]), x.dtype)], axis=0)
    return x.T[:, :n]


def _fox_kernel(*refs, tq, tk, past, single, new_time_on_lanes):
    if past:
        q_ref, kc_ref, vc_ref, k_ref, v_ref, cq_ref, ck_ref, o_ref, ka_scr, vt_scr, acc_scr = refs
    else:
        q_ref, k_ref, v_ref, cq_ref, ck_ref, o_ref, ka_scr, vt_scr, acc_scr = refs
    npair = H_B // 2
    lane1 = lax.broadcasted_iota(jnp.int32, (1, LANES), 1)
    head_of_lane = lane1 // HEAD_DIM

    def stage_bias(s, n):
        cum = ck_ref[0, s:s + n, :] * LOG2E
        for p, aug in enumerate(_bias_lanes(cum, True)):
            ka_scr[p, s:s + n, LANES:2 * LANES] = aug.astype(BF16)

    def stage_kv():
        for s, n in _stage_rows(past):
            for p in range(npair):
                ka_scr[p, s:s + n, 0:LANES] = kc_ref[0, p * LANES:(p + 1) * LANES, s:s + n].T.astype(BF16)
            vt_scr[:, s:s + n] = vc_ref[0, :, s:s + n].astype(BF16)
            stage_bias(s, n)
        if new_time_on_lanes:
            for s, n in _stage_rows(k_ref.shape[2]):
                for p in range(npair):
                    ka_scr[p, past + s:past + s + n, 0:LANES] = (
                        k_ref[0, p * LANES:(p + 1) * LANES, s:s + n].T.astype(BF16))
                vt_scr[:, past + s:past + s + n] = v_ref[0, :, s:s + n].astype(BF16)
                stage_bias(past + s, n)
            return
        for s, n in _stage_rows(k_ref.shape[1]):
            kb = k_ref[0, s:s + n, :].astype(BF16)
            for p in range(npair):
                ka_scr[p, past + s:past + s + n, 0:LANES] = kb[:, p * LANES:(p + 1) * LANES]
            vt_scr[:, past + s:past + s + n] = _transpose_rows(v_ref[0, s:s + n, :]).astype(BF16)
            stage_bias(past + s, n)

    if single:
        i = 0
        stage_kv()
    else:
        i = pl.program_id(1)
        pl.when(i == 0)(stage_kv)

    q = q_ref[0].astype(F32) * (HEAD_DIM ** -0.5 * LOG2E)
    cq = cq_ref[0] * LOG2E
    rhs = []
    q_bias = _bias_lanes(cq, False)
    for p in range(npair):
        q_p = q[:, p * LANES:(p + 1) * LANES]
        halves = [jnp.concatenate([jnp.where(head_of_lane == hh, q_p, 0.0), q_bias[2 * p + hh]], axis=1)
                  for hh in range(2)]
        rhs.append(jnp.concatenate(halves, axis=0).astype(BF16))

    n_full = past // tk + i * (tq // tk)
    diag0 = past if single else pl.multiple_of(past + i * tq, tq)
    krow = lax.broadcasted_iota(jnp.int32, (tq, 2 * tq), 0)
    qcol = lax.broadcasted_iota(jnp.int32, (tq, 2 * tq), 1)
    causal_t = krow <= jnp.where(qcol >= tq, qcol - tq, qcol)

    acc_scr[...] = jnp.zeros(acc_scr.shape, F32)

    def update(carry, start, size, mask):
        ms, ls = carry
        st = [_dot_nt(ka_scr[p, pl.ds(start, size), :], rhs[p]) for p in range(npair)]
        if mask is not None:
            st = [jnp.where(mask, s, NEG_INF) for s in st]
        m_new = [jnp.maximum(m, jnp.max(s, axis=0, keepdims=True)) for m, s in zip(ms, st)]
        alpha = [jnp.exp2(m - mn) for m, mn in zip(ms, m_new)]
        pt = [jnp.exp2(s - mn) for s, mn in zip(st, m_new)]
        l_new = [a * l + jnp.sum(x, axis=0, keepdims=True) for a, l, x in zip(alpha, ls, pt)]
        pv = [_dot(vt_scr[p * LANES:(p + 1) * LANES, pl.ds(start, size)], pt[p].astype(BF16))
              for p in range(npair)]
        for p in range(npair):
            acc_scr[p] = acc_scr[p] * alpha[p] + pv[p]
        return tuple(m_new), tuple(l_new)

    def body(j, carry):
        return update(carry, pl.multiple_of(j * tk, tk), tk, None)

    init = (tuple(jnp.full((1, 2 * tq), NEG_INF, F32) for _ in range(npair)),
            tuple(jnp.zeros((1, 2 * tq), F32) for _ in range(npair)))
    carry = lax.fori_loop(0, n_full, body, init)
    _, ls = update(carry, diag0, tq, causal_t)
    for p in range(npair):
        o_t = (acc_scr[p] / ls[p]).T
        o_ref[0, :, p * LANES:(p + 1) * LANES] = jnp.where(
            head_of_lane == 0, o_t[:tq], o_t[tq:]).astype(o_ref.dtype)


def _kv_specs(k_new, cache, layer, width, t_new, new_time_on_lanes=False, new_layer=None):
    specs, args = [], []
    new_layer = layer if new_layer is None else new_layer
    if cache is not None:
        past = cache[0].shape[3]
        for c in cache:
            specs.append(pl.BlockSpec((None, 1, width, past), lambda i, j: (layer, i, 0, 0)))
            args.append(c)
    for a in k_new:
        if new_time_on_lanes:
            specs.append(pl.BlockSpec((None, 1, width, t_new), lambda i, j: (new_layer, i, 0, 0)))
        else:
            specs.append(pl.BlockSpec((None, 1, t_new, width), lambda i, j: (new_layer, i, 0, 0)))
        args.append(a)
    return specs, args


def _fox(q, kv_new, cache, cum_q, cum_k, *, layer, tq, tk, past, new_time_on_lanes):
    b, t, _ = q.shape
    tkk = past + t
    npair = H_B // 2
    kv_specs, kv_args = _kv_specs(kv_new, cache, layer, W_B, t, new_time_on_lanes)
    return pl.pallas_call(
        functools.partial(_fox_kernel, tq=tq, tk=tk, past=past, single=(t == tq),
                          new_time_on_lanes=new_time_on_lanes),
        grid=(b, t // tq),
        in_specs=[pl.BlockSpec((1, tq, W_B), lambda i, j: (i, j, 0))] + kv_specs + [
                  pl.BlockSpec((1, tq, LANES), lambda i, j: (i, j, 0)),
                  pl.BlockSpec((1, tkk, LANES), lambda i, j: (i, 0, 0))],
        out_specs=pl.BlockSpec((1, tq, W_B), lambda i, j: (i, j, 0)),
        out_shape=jax.ShapeDtypeStruct((b, t, W_B), BF16),
        scratch_shapes=[pltpu.VMEM((npair, tkk, 2 * LANES), BF16), pltpu.VMEM((W_B, tkk), BF16),
                        pltpu.VMEM((npair, LANES, 2 * tq), F32)],
        compiler_params=_cparams(("parallel", "arbitrary")),
    )(q, *kv_args, cum_q, cum_k)


def _sb_kernel(*refs, tq, tk, past, single):
    if past:
        q_ref, kc_ref, vc_ref, k_ref, v_ref, o_ref, kb_scr, vt_scr, acc_scr = refs
    else:
        q_ref, k_ref, v_ref, o_ref, kb_scr, vt_scr, acc_scr = refs
    npair = H_C // 2
    lane1 = lax.broadcasted_iota(jnp.int32, (1, LANES), 1)
    head_of_lane = lane1 // HEAD_DIM

    def stage_kv():
        for s, n in _stage_rows(past):
            kb_scr[s:s + n, :] = kc_ref[0, :, s:s + n].T.astype(BF16)
            vt_scr[:, s:s + n] = vc_ref[0, :, s:s + n].astype(BF16)
        for s, n in _stage_rows(k_ref.shape[1]):
            kb_scr[past + s:past + s + n, :] = k_ref[0, s:s + n, :].astype(BF16)
            vt_scr[:, past + s:past + s + n] = _transpose_rows(v_ref[0, s:s + n, :]).astype(BF16)

    if single:
        i = 0
        stage_kv()
    else:
        i = pl.program_id(1)
        pl.when(i == 0)(stage_kv)

    q = q_ref[0].astype(F32) * (HEAD_DIM ** -0.5 * LOG2E)
    rhs = []
    for p in range(npair):
        q_p = q[:, p * LANES:(p + 1) * LANES]
        rhs.append(jnp.concatenate([jnp.where(head_of_lane == hh, q_p, 0.0) for hh in range(2)],
                                   axis=0).astype(BF16))

    def earlier_matrix(n):
        rr = lax.broadcasted_iota(jnp.int32, (n, n), 0)
        cc = lax.broadcasted_iota(jnp.int32, (n, n), 1)
        return (cc > rr).astype(BF16)

    n_full = past // tk + i * (tq // tk)
    diag0 = past if single else pl.multiple_of(past + i * tq, tq)
    krow = lax.broadcasted_iota(jnp.int32, (tq, 2 * tq), 0)
    qcol = lax.broadcasted_iota(jnp.int32, (tq, 2 * tq), 1)
    strict_t = krow < jnp.where(qcol >= tq, qcol - tq, qcol)

    def update(runs, start, size, mask, first):
        zt = [_dot_nt(kb_scr[pl.ds(start, size), p * LANES:(p + 1) * LANES], rhs[p])
              for p in range(npair)]
        mz = [jnp.minimum(z, 0.0) for z in zt]
        tail = [jnp.log(1.0 + jnp.exp2(m + m - z)) * LOG2E for m, z in zip(mz, zt)]
        lsig = [m - t for m, t in zip(mz, tail)]
        l1m = [s - z for s, z in zip(lsig, zt)]
        if mask is not None:
            l1m = [jnp.where(mask, x, 0.0) for x in l1m]
        later = earlier_matrix(size)
        after = [_dot(later, x.astype(BF16)) for x in l1m]
        wt = [jnp.exp2(s + a + r) for s, a, r in zip(lsig, after, runs)]
        if mask is not None:
            wt = [jnp.where(mask, w, 0.0) for w in wt]
        pv = [_dot(vt_scr[p * LANES:(p + 1) * LANES, pl.ds(start, size)], wt[p].astype(BF16))
              for p in range(npair)]
        for p in range(npair):
            acc_scr[p] = pv[p] if first else acc_scr[p] + pv[p]
        return tuple(r + jnp.sum(x, axis=0, keepdims=True) for r, x in zip(runs, l1m))

    runs = tuple(jnp.zeros((1, 2 * tq), F32) for _ in range(npair))
    runs = update(runs, diag0, tq, strict_t, True)

    def body(jj, runs):
        j = n_full - 1 - jj
        return update(runs, pl.multiple_of(j * tk, tk), tk, None, False)

    lax.fori_loop(0, n_full, body, runs)
    for p in range(npair):
        o_t = acc_scr[p].T
        o_ref[0, :, p * LANES:(p + 1) * LANES] = jnp.where(
            head_of_lane == 0, o_t[:tq], o_t[tq:]).astype(o_ref.dtype)


def _sb(q, kv_new, cache, *, layer, tq, tk, past):
    b, t, _ = q.shape
    tkk = past + kv_new[0].shape[2]
    npair = H_C // 2
    kv_specs, kv_args = _kv_specs(kv_new, cache, layer, W_C, t, new_layer=0)
    return pl.pallas_call(
        functools.partial(_sb_kernel, tq=tq, tk=tk, past=past, single=(t == tq)),
        grid=(b, t // tq),
        in_specs=[pl.BlockSpec((1, tq, W_C), lambda i, j: (i, j, 0))] + kv_specs,
        out_specs=pl.BlockSpec((1, tq, W_C), lambda i, j: (i, j, 0)),
        out_shape=jax.ShapeDtypeStruct((b, t, W_C), BF16),
        scratch_shapes=[pltpu.VMEM((tkk, W_C), BF16), pltpu.VMEM((W_C, tkk), BF16),
                        pltpu.VMEM((npair, LANES, 2 * tq), F32)],
        compiler_params=_cparams(("parallel", "arbitrary")),
    )(q, *kv_args)


def _out_kernel(oa_ref, ob_ref, oc_ref, g_ref, x_ref, w_ref, lng_ref, lnb_ref, y_ref, *, alpha):
    g = g_ref[...].astype(F32)
    gate = g * _sigmoid(g)
    gated = lambda o_ref, lo, hi: (o_ref[...].astype(F32) * gate[:, lo:hi]).astype(BF16)
    acc = _dot(gated(oa_ref, 0, W_A), w_ref[0:W_A, :])
    acc = acc + _dot(gated(ob_ref, W_A, W_A + W_B), w_ref[W_A:W_A + W_B, :])
    acc = acc + _dot(gated(oc_ref, W_A + W_B, W_MIX), w_ref[W_A + W_B:, :])
    z = alpha * x_ref[...] + acc
    mu = jnp.mean(z, axis=-1, keepdims=True)
    d = z - mu
    var = jnp.mean(d * d, axis=-1, keepdims=True)
    y_ref[...] = d * lax.rsqrt(var + LN_EPS) * lng_ref[...] + lnb_ref[...]


def _out(oa, ob, oc, g, x2d, w_out, ln_g, ln_b, alpha, tm):
    n, d = x2d.shape
    row = lambda w: pl.BlockSpec((tm, w), lambda i: (i, 0))
    return pl.pallas_call(
        functools.partial(_out_kernel, alpha=alpha),
        grid=(n // tm,),
        in_specs=[row(W_A), row(W_B), row(W_C), row(W_MIX), row(d),
                  pl.BlockSpec((W_MIX, d), lambda i: (0, 0)),
                  pl.BlockSpec((1, d), lambda i: (0, 0)),
                  pl.BlockSpec((1, d), lambda i: (0, 0))],
        out_specs=row(d),
        out_shape=jax.ShapeDtypeStruct((n, d), F32),
        compiler_params=_cparams(("parallel",)),
    )(oa, ob, oc, g, x2d, w_out, ln_g, ln_b)


def _pad_lanes(a, width=LANES):
    return jnp.pad(a, [(0, 0)] * (a.ndim - 1) + [(0, width - a.shape[-1])])


def _state_to_pairs(s):
    b = s.shape[0]
    odd_head = (jnp.arange(H_A) % 2 == 1).reshape(1, H_A, 1, 1)
    upper_half = (jnp.arange(LANES) >= HEAD_DIM).reshape(1, 1, 1, LANES)
    wide = jnp.where(odd_head == upper_half, jnp.concatenate([s, s], axis=-1), 0.0)
    return wide.reshape(b, H_A // 2, LANES, LANES)


def _pairs_to_state(sp):
    b = sp.shape[0]
    wide = sp.reshape(b, H_A, HEAD_DIM, LANES)
    odd_head = (jnp.arange(H_A) % 2 == 1).reshape(1, H_A, 1, 1)
    return jnp.where(odd_head, wide[..., HEAD_DIM:], wide[..., :HEAD_DIM])


_IN_SIZES = (SHIFT_W, W_A, W_B, W_B, W_B, H_B, W_B, W_C, W_C, W_C, W_C)
_IN_OFFS = tuple(int(v) for v in np.concatenate([[0], np.cumsum(_IN_SIZES)]))
_SRC_SHIFT, _SRC_GA, _SRC_QB, _SRC_KB, _SRC_VB, _SRC_F, _SRC_GB, _SRC_QC, _SRC_KC, _SRC_VC, _SRC_GC = (
    _IN_OFFS[:-1])
_W_MOVES = ((_SRC_SHIFT, _OFF_SHIFT, SHIFT_W), (_SRC_GA, _OFF_G, W_A), (_SRC_GB, _OFF_G + W_A, W_B),
            (_SRC_GC, _OFF_G + W_A + W_B, W_C), (_SRC_QB, _OFF_QB, W_B), (_SRC_KB, _OFF_KB, W_B),
            (_SRC_VB, _OFF_VB, W_B), (_SRC_QC, _OFF_QC, W_C), (_SRC_KC, _OFF_KC, W_C),
            (_SRC_VC, _OFF_VC, W_C))
_W_STAGE_ROWS = 128


def _stage_w_kernel(w_ref, o_ref):
    for src, dst, width in _W_MOVES:
        o_ref[:, dst:dst + width] = w_ref[:, src:src + width].astype(BF16)
    lane = lax.broadcasted_iota(jnp.int32, (1, LANES), 1)
    o_ref[:, _OFF_F:_OFF_F + LANES] = jnp.where(
        lane < H_B, w_ref[:, _SRC_F:_SRC_F + LANES], 0.0).astype(BF16)


def _stage_w(w_in, layer):
    _, d, cols = w_in.shape
    return pl.pallas_call(
        _stage_w_kernel,
        grid=(d // _W_STAGE_ROWS,),
        in_specs=[pl.BlockSpec((None, _W_STAGE_ROWS, cols), lambda i: (layer, i, 0))],
        out_specs=pl.BlockSpec((_W_STAGE_ROWS, _PROJ_COLS), lambda i: (i, 0)),
        out_shape=jax.ShapeDtypeStruct((d, _PROJ_COLS), BF16),
        compiler_params=_cparams(("parallel",)),
    )(w_in)


def _layer_params(l, w_in, mu_shift, w0_decay, w_decay, a0, w_aaa, k_k, k_a, r_k,
                  lnx_g, lnx_b, fox_fb, w_out, ln_g, ln_b):
    w_cat = _stage_w(w_in, l)
    zeros = jnp.zeros((D_LORA, W_A), F32)
    row = lambda a: a.reshape(1, -1).astype(F32)
    return dict(
        w_cat=w_cat, fb=_pad_lanes(row(fox_fb[l])),
        mu=row(mu_shift[l]), w0=row(w0_decay[l]),
        wd=jnp.concatenate([w_decay[l], zeros], axis=0).astype(BF16),
        a0=row(a0[l]), wa=jnp.concatenate([zeros, w_aaa[l]], axis=0).astype(BF16),
        kkp=row(k_k[l]), ka=row(k_a[l]), rk=row(r_k[l]), lnx_g=row(lnx_g[l]), lnx_b=row(lnx_b[l]),
        w_out=w_out[l].astype(BF16), ln_g=row(ln_g[l]), ln_b=row(ln_b[l]))


def _run_layer(x, hist, prm, alpha, layer, depth, stacks):
    b, t, d = x.shape
    n = b * t
    x2d = x.reshape(n, d)
    tm = min(ROW_BLOCK, n)
    fox_t = hist is None
    us, g, qb, kb_st, vb_st, qc, kc_st, vc_st, lf, kc_rows, vc_rows, lf_t = _proj(
        x2d, prm["w_cat"], prm["fb"], tm, layer, depth, stacks, t, fox_t)
    r3 = lambda a: a.reshape(b, t, a.shape[-1])
    us, qb, qc, lf = (r3(a) for a in (us, qb, qc, lf))
    r4 = lambda a: a.reshape(depth, b, t, a.shape[-1])

    if hist is None:
        past = 0
        prev0 = jnp.zeros((b, 1, SHIFT_W), F32)
        s0 = jnp.zeros((b, H_A // 2, LANES, LANES), F32)
        cache_b = cache_c = None
        lf_all = lf
        fox_blk = (min(FOX_BLOCK, t),) * 2
        sb_blk = (min(SB_BLOCK, t),) * 2
    else:
        fk_t, fv_t, h_lf, sk_t, sv_t, h_wkv, h_shift = hist
        past = fk_t.shape[3]
        prev0 = h_shift
        s0 = _state_to_pairs(h_wkv)
        cache_b, cache_c = (fk_t, fv_t), (sk_t, sv_t)
        lf_all = jnp.concatenate([_pad_lanes(h_lf), lf], axis=1)
        fox_blk = sb_blk = (t, min(SB_BLOCK, past))

    oa, s_fin = _wkv(us, prev0, s0, prm, min(WKV_BLOCK, t))
    cum = _cumsum(lf_all)
    fox_kv = (kb_st, vb_st) if fox_t else (r4(kb_st), r4(vb_st))
    ob = _fox(qb, fox_kv, cache_b, cum[:, past:], cum, layer=layer,
              tq=fox_blk[0], tk=fox_blk[1], past=past, new_time_on_lanes=fox_t)
    sb_kv = (kc_rows.reshape(1, b, t, W_C), vc_rows.reshape(1, b, t, W_C))
    oc = _sb(qc, sb_kv, cache_c, layer=layer, tq=sb_blk[0], tk=sb_blk[1], past=past)
    y = _out(oa.reshape(n, W_A), ob.reshape(n, W_B), oc.reshape(n, W_C), g, x2d,
             prm["w_out"], prm["ln_g"], prm["ln_b"], alpha, min(OUT_BLOCK, n))
    small = (lf_t[:H_B].reshape(H_B, b, t), _pairs_to_state(s_fin), us[:, -1:, :])
    return y.reshape(b, t, d), (kb_st, vb_st, kc_st, vc_st), small


def kernel(x_prompt, x_sample, cache_fox_k, cache_fox_v, cache_fox_logf, cache_sb_k, cache_sb_v, state_wkv, state_shift, w_in, mu_shift, w0_decay, w_decay, a0, w_aaa, k_k, k_a, r_k, lnx_g, lnx_b, fox_fb, w_out, ln_g, ln_b):
    depth = w_in.shape[0]
    alpha = (2 * depth) ** 0.25
    yp, ys = x_prompt, x_sample

    def time_on_lanes(cache):
        nl, nb, past, nh, hd = cache.shape
        return jnp.transpose(cache, (0, 1, 3, 4, 2)).reshape(nl, nb, nh * hd, past)

    fk_t, fv_t, sk_t, sv_t = (time_on_lanes(c) for c in (cache_fox_k, cache_fox_v, cache_sb_k, cache_sb_v))
    stacks_p, stacks_s = None, None
    small_p, small_s = [], []
    for l in range(depth):
        prm = _layer_params(l, w_in, mu_shift, w0_decay, w_decay, a0, w_aaa, k_k, k_a, r_k,
                            lnx_g, lnx_b, fox_fb, w_out, ln_g, ln_b)
        yp, stacks_p, sm_p = _run_layer(yp, None, prm, alpha, l, depth, stacks_p)
        hist = (fk_t, fv_t, cache_fox_logf[l], sk_t, sv_t, state_wkv[l], state_shift[l])
        ys, stacks_s, sm_s = _run_layer(ys, hist, prm, alpha, l, depth, stacks_s)
        small_p.append(sm_p)
        small_s.append(sm_s)

    def group_outputs(x, stacks, small, fox_t):
        b, t, _ = x.shape
        kb, vb, kc, vc = stacks
        logf, wkv, shift = (jnp.stack([sm[i] for sm in small]) for i in range(3))
        logf = jnp.transpose(logf, (0, 2, 3, 1))
        if fox_t:
            kb, vb = (jnp.transpose(a.reshape(depth, b, H_B, HEAD_DIM, t), (0, 1, 4, 2, 3)) for a in (kb, vb))
        return (kb.reshape(depth, b, t, H_B, HEAD_DIM), vb.reshape(depth, b, t, H_B, HEAD_DIM), logf,
                kc.reshape(depth, b, t, H_C, HEAD_DIM), vc.reshape(depth, b, t, H_C, HEAD_DIM),
                wkv, shift)

    return ((yp, ys) + group_outputs(x_prompt, stacks_p, small_p, True)
            + group_outputs(x_sample, stacks_s, small_s, False))
```
